```python
import math
import jax, jax.numpy as jnp
from jax import lax
import numpy as np

D_MODEL = 1024
BATCH = 8
SEQ = 2048
DEPTH = 1
DEC_BATCH = 1
DEC_SEQ = 16384
PAST_LEN = 128

HYENA_WIDTH = 512
HYENA_ORDER = 2
SHORT_CONV = 3
FILTER_BANDS = 16
FILTER_EMB = 1 + 2 * FILTER_BANDS
FILTER_HIDDEN = 64
N_DIRECTIONS = 2
DECAY_FAST_PCT = 0.3
DECAY_SLOW_PCT = 1.5
DECAY_TARGET = 1e-2
DECAY_SHIFT = 0.05
N_HEADS = 8
N_KV_HEADS = 2
HEAD_DIM = 64
ATTN_WIDTH = N_HEADS * HEAD_DIM
KV_WIDTH = N_KV_HEADS * HEAD_DIM
WINDOW = 128
BLOCK = 128
ROPE_THETA = 10000.0
N_BRANCHES = 2
IN_WIDTH = 3 * HYENA_WIDTH + ATTN_WIDTH + 2 * KV_WIDTH + N_BRANCHES * D_MODEL
FFN_HIDDEN = -(-8 * D_MODEL // (3 * 256)) * 256
RMS_EPS = 1e-6
NEG_INF = -1e30

kernel_name = "hybrid_hyena_swa_sink_encoder"


def rmsnorm(x, w):
    xf = x.astype(jnp.float32)
    y = xf * lax.rsqrt(jnp.mean(xf * xf, axis=-1, keepdims=True) + RMS_EPS) * w.astype(jnp.float32)
    return y.astype(x.dtype)


def short_conv3(u, w, b):
    up = jnp.pad(u, ((0, 0), (1, 1), (0, 0)))
    return w[0] * up[:, :-2] + w[1] * up[:, 1:-1] + w[2] * up[:, 2:] + b


def hyena_filter_spectrum(L, w1, b1, w2, b2, w3, freq):
    f32 = jnp.float32
    pos = jnp.arange(L, dtype=f32)
    t = pos / (L - 1)
    bands = jnp.linspace(1e-4, FILTER_BANDS - 1, FILTER_BANDS, dtype=f32)
    ang = (2.0 * math.pi / L) * pos[:, None] * bands[None, :]
    z = jnp.concatenate([t[:, None], jnp.cos(ang), -jnp.sin(ang)], axis=-1)
    fr = freq.astype(f32)
    h = jnp.sin(fr * (z @ w1.astype(f32) + b1.astype(f32)))
    h = jnp.sin(fr * (h @ w2.astype(f32) + b2.astype(f32)))
    h = (h @ w3.astype(f32)).reshape(L, HYENA_ORDER, N_DIRECTIONS, HYENA_WIDTH)
    max_decay = math.log(DECAY_TARGET) / DECAY_FAST_PCT
    min_decay = math.log(DECAY_TARGET) / DECAY_SLOW_PCT
    deltas = jnp.linspace(min_decay, max_decay, HYENA_WIDTH, dtype=f32)
    window = jnp.exp(-t[:, None] * jnp.abs(deltas)[None, :]) + DECAY_SHIFT
    h = h * window[:, None, None, :]
    fwd, bwd = h[:, :, 0], h[:, :, 1]
    k = jnp.concatenate([fwd, jnp.zeros((1, HYENA_ORDER, HYENA_WIDTH), f32), jnp.flip(bwd[1:], axis=0)], axis=0)
    k = k / jnp.sum(jnp.abs(k), axis=0, keepdims=True)
    return jnp.fft.rfft(k, axis=0)


def fft_long_conv(u, spec):
    L = u.shape[1]
    U = jnp.fft.rfft(u, n=2 * L, axis=1)
    return jnp.fft.irfft(U * spec[None], n=2 * L, axis=1)[:, :L]


def hyena_mixer(u, conv_w, conv_b, spec, skip_bias):
    uc = short_conv3(u.astype(jnp.float32), conv_w.astype(jnp.float32), conv_b.astype(jnp.float32))
    x1, x2, v = jnp.split(uc, 3, axis=-1)
    sb = skip_bias.astype(jnp.float32)
    z = v
    for o, gate in enumerate((x1, x2)):
        z = gate * (fft_long_conv(z, spec[:, o]) + sb[o] * z)
    return z.astype(u.dtype)


def apply_rope(x):
    L = x.shape[1]
    inv = ROPE_THETA ** (-jnp.arange(0, HEAD_DIM, 2, dtype=jnp.float32) / HEAD_DIM)
    ang = jnp.arange(L, dtype=jnp.float32)[:, None] * inv[None, :]
    cos = jnp.cos(ang)[None, :, None, :]
    sin = jnp.sin(ang)[None, :, None, :]
    xf = x.astype(jnp.float32)
    a, b = xf[..., :HEAD_DIM // 2], xf[..., HEAD_DIM // 2:]
    return jnp.concatenate([a * cos - b * sin, b * cos + a * sin], axis=-1).astype(x.dtype)


def windowed_attention(q, k, v, sink):
    B, L = q.shape[:2]
    nb = L // BLOCK
    G = N_HEADS // N_KV_HEADS
    qb = q.reshape(B, nb, BLOCK, N_KV_HEADS, G, HEAD_DIM)

    def band(t):
        tp = jnp.pad(t, ((0, 0), (BLOCK, BLOCK), (0, 0), (0, 0))).reshape(B, nb + 2, BLOCK, N_KV_HEADS, HEAD_DIM)
        return jnp.concatenate([tp[:, :-2], tp[:, 1:-1], tp[:, 2:]], axis=2)

    kb, vb = band(k), band(v)
    s = jnp.einsum('bnqkgd,bnskd->bnkgqs', qb, kb, preferred_element_type=jnp.float32) * (HEAD_DIM ** -0.5)
    qi = jnp.arange(BLOCK)[:, None]
    si = jnp.arange(3 * BLOCK)[None, :]
    rel = si - BLOCK - qi
    kpos = (jnp.arange(nb)[:, None, None] - 1) * BLOCK + si[None]
    mask = (jnp.abs(rel) <= WINDOW)[None] & (kpos >= 0) & (kpos < L)
    s = jnp.where(mask[None, :, None, None], s, NEG_INF)
    sink_logit = jnp.broadcast_to(sink.astype(jnp.float32).reshape(1, 1, N_KV_HEADS, G, 1, 1), s.shape[:-1] + (1,))
    p = jax.nn.softmax(jnp.concatenate([s, sink_logit], axis=-1), axis=-1)[..., :-1]
    o = jnp.einsum('bnkgqs,bnskd->bnqkgd', p.astype(v.dtype), vb)
    return o.reshape(B, L, ATTN_WIDTH)


def encoder_layer(x, attn_norm_w, w_in, hyena_conv_w, hyena_conv_b, filt_w1, filt_b1, filt_w2, filt_b2,
                  filt_w3, filt_freq, hyena_bias, q_norm_w, k_norm_w, attn_sink, w_hy_out, w_at_out, w_o,
                  ffn_norm_w, w_gate, w_up, w_down):
    B, L, _ = x.shape
    h = rmsnorm(x, attn_norm_w)
    proj = h @ w_in
    c0 = 3 * HYENA_WIDTH
    c1 = c0 + ATTN_WIDTH
    c2 = c1 + KV_WIDTH
    c3 = c2 + KV_WIDTH
    u_hy = proj[..., :c0]
    q = proj[..., c0:c1].reshape(B, L, N_HEADS, HEAD_DIM)
    k = proj[..., c1:c2].reshape(B, L, N_KV_HEADS, HEAD_DIM)
    v = proj[..., c2:c3].reshape(B, L, N_KV_HEADS, HEAD_DIM)
    g_hy = proj[..., c3:c3 + D_MODEL]
    g_at = proj[..., c3 + D_MODEL:]
    spec = hyena_filter_spectrum(L, filt_w1, filt_b1, filt_w2, filt_b2, filt_w3, filt_freq)
    y_hy = hyena_mixer(u_hy, hyena_conv_w, hyena_conv_b, spec, hyena_bias)
    q = apply_rope(rmsnorm(q, q_norm_w))
    k = apply_rope(rmsnorm(k, k_norm_w))
    y_at = windowed_attention(q, k, v, attn_sink)
    merged = jax.nn.sigmoid(g_hy) * (y_hy @ w_hy_out) + jax.nn.sigmoid(g_at) * (y_at @ w_at_out)
    x = x + merged @ w_o
    f = rmsnorm(x, ffn_norm_w)
    x = x + (jax.nn.silu(f @ w_gate) * (f @ w_up)) @ w_down
    return x


def trunk(x, attn_norm_w, w_in, hyena_conv_w, hyena_conv_b, filt_w1, filt_b1, filt_w2, filt_b2, filt_w3,
          filt_freq, hyena_bias, q_norm_w, k_norm_w, attn_sink, w_hy_out, w_at_out, w_o, ffn_norm_w,
          w_gate, w_up, w_down):
    for l in range(DEPTH):
        x = encoder_layer(x, attn_norm_w[l], w_in[l], hyena_conv_w[l], hyena_conv_b[l], filt_w1[l], filt_b1[l],
                          filt_w2[l], filt_b2[l], filt_w3[l], filt_freq[l], hyena_bias[l], q_norm_w[l],
                          k_norm_w[l], attn_sink[l], w_hy_out[l], w_at_out[l], w_o[l], ffn_norm_w[l],
                          w_gate[l], w_up[l], w_down[l])
    return x


def setup_inputs(seed: int = 0) -> dict:
    key = jax.random.key(seed)
    ks = jax.random.split(key, 24)
    f32 = jnp.float32

    def nrm(k, shape, scale):
        return jax.random.normal(k, shape, f32) * scale

    return {
        'x_prompt': nrm(ks[0], (BATCH, SEQ, D_MODEL), 1.0),
        'x_sample': nrm(ks[1], (DEC_BATCH, DEC_SEQ, D_MODEL), 1.0),
        'attn_norm_w': 1.0 + nrm(ks[2], (DEPTH, D_MODEL), 0.01),
        'w_in': nrm(ks[3], (DEPTH, D_MODEL, IN_WIDTH), D_MODEL ** -0.5),
        'hyena_conv_w': nrm(ks[4], (DEPTH, SHORT_CONV, 3 * HYENA_WIDTH), SHORT_CONV ** -0.5),
        'hyena_conv_b': nrm(ks[5], (DEPTH, 3 * HYENA_WIDTH), 0.01),
        'filt_w1': nrm(ks[6], (DEPTH, FILTER_EMB, FILTER_HIDDEN), FILTER_EMB ** -0.5),
        'filt_b1': nrm(ks[7], (DEPTH, FILTER_HIDDEN), 0.1),
        'filt_w2': nrm(ks[8], (DEPTH, FILTER_HIDDEN, FILTER_HIDDEN), FILTER_HIDDEN ** -0.5),
        'filt_b2': nrm(ks[9], (DEPTH, FILTER_HIDDEN), 0.1),
        'filt_w3': nrm(ks[10], (DEPTH, FILTER_HIDDEN, HYENA_ORDER * N_DIRECTIONS * HYENA_WIDTH), FILTER_HIDDEN ** -0.5),
        'filt_freq': 1.0 + nrm(ks[11], (DEPTH, FILTER_HIDDEN), 0.01),
        'hyena_bias': nrm(ks[12], (DEPTH, HYENA_ORDER, HYENA_WIDTH), 1.0),
        'q_norm_w': 1.0 + nrm(ks[13], (DEPTH, HEAD_DIM), 0.01),
        'k_norm_w': 1.0 + nrm(ks[14], (DEPTH, HEAD_DIM), 0.01),
        'attn_sink': nrm(ks[15], (DEPTH, N_HEADS), 1.0),
        'w_hy_out': nrm(ks[16], (DEPTH, HYENA_WIDTH, D_MODEL), HYENA_WIDTH ** -0.5),
        'w_at_out': nrm(ks[17], (DEPTH, ATTN_WIDTH, D_MODEL), ATTN_WIDTH ** -0.5),
        'w_o': nrm(ks[18], (DEPTH, D_MODEL, D_MODEL), D_MODEL ** -0.5),
        'ffn_norm_w': 1.0 + nrm(ks[19], (DEPTH, D_MODEL), 0.01),
        'w_gate': nrm(ks[20], (DEPTH, D_MODEL, FFN_HIDDEN), D_MODEL ** -0.5),
        'w_up': nrm(ks[21], (DEPTH, D_MODEL, FFN_HIDDEN), D_MODEL ** -0.5),
        'w_down': nrm(ks[22], (DEPTH, FFN_HIDDEN, D_MODEL), FFN_HIDDEN ** -0.5),
    }


def reference(x_prompt, x_sample, attn_norm_w, w_in, hyena_conv_w, hyena_conv_b, filt_w1, filt_b1, filt_w2,
              filt_b2, filt_w3, filt_freq, hyena_bias, q_norm_w, k_norm_w, attn_sink, w_hy_out, w_at_out, w_o,
              ffn_norm_w, w_gate, w_up, w_down):
    y_prompt = trunk(x_prompt, attn_norm_w, w_in, hyena_conv_w, hyena_conv_b, filt_w1, filt_b1, filt_w2, filt_b2,
                     filt_w3, filt_freq, hyena_bias, q_norm_w, k_norm_w, attn_sink, w_hy_out, w_at_out, w_o,
                     ffn_norm_w, w_gate, w_up, w_down)
    y_sample = trunk(x_sample, attn_norm_w, w_in, hyena_conv_w, hyena_conv_b, filt_w1, filt_b1, filt_w2, filt_b2,
                     filt_w3, filt_freq, hyena_bias, q_norm_w, k_norm_w, attn_sink, w_hy_out, w_at_out, w_o,
                     ffn_norm_w, w_gate, w_up, w_down)
    return (y_prompt, y_sample)
```

```python
import functools
import math

import numpy as np
import jax
import jax.numpy as jnp
from jax import lax
from jax.experimental import pallas as pl
from jax.experimental.pallas import tpu as pltpu

F32 = jnp.float32
BF16 = jnp.bfloat16

D_MODEL = 1024
HYENA_WIDTH = 512
HYENA_ORDER = 2
FILTER_BANDS = 16
FILTER_EMB = 1 + 2 * FILTER_BANDS
FILTER_HIDDEN = 64
DECAY_FAST_PCT = 0.3
DECAY_SLOW_PCT = 1.5
DECAY_TARGET = 1e-2
DECAY_SHIFT = 0.05
N_HEADS = 8
N_KV_HEADS = 2
HEAD_DIM = 64
ATTN_WIDTH = N_HEADS * HEAD_DIM
KV_WIDTH = N_KV_HEADS * HEAD_DIM
WINDOW = 128
ROPE_THETA = 10000.0
FFN_HIDDEN = 2816
RMS_EPS = 1e-6
NEG_INF = -1e30

C_HY = 3 * HYENA_WIDTH
C_Q = C_HY + ATTN_WIDTH
C_K = C_Q + KV_WIDTH
C_V = C_K + KV_WIDTH
IN_WIDTH = C_V + 2 * D_MODEL

LANES = 128
SUBLANES = 8
ATTN_BLOCK = 128
VMEM_LIMIT = 56 * 1024 * 1024

TM_INPROJ = 512
TM_FINAL = 256
TN_FILT = 512
FFN_CHUNK = FFN_HIDDEN // 2


def _cparams(*sem):
    return pltpu.CompilerParams(dimension_semantics=sem, vmem_limit_bytes=VMEM_LIMIT)


def _const_spec(shape):
    nd = len(shape)
    return pl.BlockSpec(shape, lambda *_: (0,) * nd, pipeline_mode=pl.Buffered(1))


def _dot(a, b):
    return jnp.dot(a, b, preferred_element_type=F32)


def _split(a):
    hi = a.astype(BF16)
    lo = (a - hi.astype(F32)).astype(BF16)
    return hi, lo


def _dot3(a, w):
    ah, al = _split(a)
    wh, wl = _split(w)
    return _dot(ah, wh) + _dot(al, wh) + _dot(ah, wl)


def _inproj_kernel(x_ref, xp_ref, xn_ref, nw_ref, w_ref, cw_ref, cb_ref, qw_ref, kw_ref,
                   cos_ref, sa_ref, sb_ref, ones_ref,
                   uc_ref, q_ref, k4_ref, v4_ref, g_ref, pad_ref, *, tm, tiles_per_seq):
    pos = pl.program_id(0) % tiles_per_seq
    nw = nw_ref[...]

    def norm(xv):
        ms = jnp.mean(xv * xv, axis=-1, keepdims=True)
        return (xv * lax.rsqrt(ms + RMS_EPS) * nw).astype(BF16)

    h = norm(x_ref[...])
    hh = norm(jnp.concatenate([xp_ref[...], xn_ref[...]], axis=0))

    w_hy = w_ref[:, :C_HY]
    hy = _dot(h, w_hy)
    hyh = _dot(hh, w_hy)
    prev_ok = (pos > 0).astype(F32)
    next_ok = (pos < tiles_per_seq - 1).astype(F32)
    pad_ref[0:SUBLANES] = hyh[0:SUBLANES] * prev_ok
    pad_ref[SUBLANES:SUBLANES + tm] = hy
    pad_ref[SUBLANES + tm:2 * SUBLANES + tm] = hyh[SUBLANES:] * next_ok
    up = pad_ref[SUBLANES - 1:SUBLANES - 1 + tm]
    un = pad_ref[SUBLANES + 1:SUBLANES + 1 + tm]
    cw = cw_ref[...]
    uc_ref[...] = cw[0:1] * up + cw[1:2] * hy + cw[2:3] * un + cb_ref[...]

    def head_norm_rope(t, wrow, ones, reps):
        hi, lo = _split(t * t)
        ms = _dot(hi, ones) + _dot(lo, ones)
        tn = t * lax.rsqrt(ms + RMS_EPS) * wrow
        width = t.shape[1]
        cos = jnp.concatenate([cos_ref[...]] * reps, axis=1)
        sa = jnp.concatenate([sa_ref[...]] * reps, axis=1)
        sb = jnp.concatenate([sb_ref[...]] * reps, axis=1)
        half = HEAD_DIM // 2
        return tn * cos + pltpu.roll(tn, width - half, 1) * sa + pltpu.roll(tn, half, 1) * sb

    q = _dot(h, w_ref[:, C_HY:C_Q])
    q = head_norm_rope(q, qw_ref[...], ones_ref[...], ATTN_WIDTH // LANES)
    q_ref[...] = (q * (HEAD_DIM ** -0.5)).astype(BF16)

    k = _dot(h, w_ref[:, C_Q:C_K])
    k = head_norm_rope(k, kw_ref[...], ones_ref[:KV_WIDTH, :KV_WIDTH], 1)
    v = _dot(h, w_ref[:, C_K:C_V])

    lo_half = lax.broadcasted_iota(jnp.int32, (tm, KV_WIDTH), 1) < HEAD_DIM

    def spread(t):
        tr = pltpu.roll(t, HEAD_DIM, 1)
        zero = jnp.zeros_like(t)
        return jnp.concatenate([jnp.where(lo_half, t, zero), jnp.where(lo_half, zero, tr),
                                jnp.where(lo_half, tr, zero), jnp.where(lo_half, zero, t)],
                               axis=1).astype(BF16)

    k4_ref[...] = spread(k)
    v4_ref[...] = spread(v)
    g_ref[...] = jax.nn.sigmoid(_dot(h, w_ref[:, C_V:]))


def _inproj(xf, seq_len, nw, w_in_bf, cw, cb, qw, kw, cos, sa, sb, ones_bd):
    n = xf.shape[0]
    tm = TM_INPROJ
    tiles_per_seq = seq_len // tm
    nblk8 = n // SUBLANES
    r8 = tm // SUBLANES
    kern = functools.partial(_inproj_kernel, tm=tm, tiles_per_seq=tiles_per_seq)
    row = lambda i: (i, 0)
    return pl.pallas_call(
        kern,
        grid=(n // tm,),
        in_specs=[
            pl.BlockSpec((tm, D_MODEL), row),
            pl.BlockSpec((SUBLANES, D_MODEL), lambda i: (jnp.maximum(i * r8 - 1, 0), 0)),
            pl.BlockSpec((SUBLANES, D_MODEL), lambda i: (jnp.minimum((i + 1) * r8, nblk8 - 1), 0)),
            _const_spec((1, D_MODEL)),
            _const_spec((D_MODEL, IN_WIDTH)),
            _const_spec((3, C_HY)),
            _const_spec((1, C_HY)),
            _const_spec((1, ATTN_WIDTH)),
            _const_spec((1, KV_WIDTH)),
            pl.BlockSpec((tm, LANES), lambda i: (i % tiles_per_seq, 0)),
            pl.BlockSpec((tm, LANES), lambda i: (i % tiles_per_seq, 0)),
            pl.BlockSpec((tm, LANES), lambda i: (i % tiles_per_seq, 0)),
            _const_spec((ATTN_WIDTH, ATTN_WIDTH)),
        ],
        out_specs=[
            pl.BlockSpec((tm, C_HY), row),
            pl.BlockSpec((tm, ATTN_WIDTH), row),
            pl.BlockSpec((tm, 4 * LANES), row),
            pl.BlockSpec((tm, 4 * LANES), row),
            pl.BlockSpec((tm, 2 * D_MODEL), row),
        ],
        out_shape=[
            jax.ShapeDtypeStruct((n, C_HY), F32),
            jax.ShapeDtypeStruct((n, ATTN_WIDTH), BF16),
            jax.ShapeDtypeStruct((n, 4 * LANES), BF16),
            jax.ShapeDtypeStruct((n, 4 * LANES), BF16),
            jax.ShapeDtypeStruct((n, 2 * D_MODEL), F32),
        ],
        scratch_shapes=[pltpu.VMEM((tm + 2 * SUBLANES, C_HY), F32)],
        compiler_params=_cparams("parallel"),
        name="inproj",
    )(xf, xf, xf, nw, w_in_bf, cw, cb, qw, kw, cos, sa, sb, ones_bd)


def _filt_kernel(z_ref, w1_ref, b1_ref, w2_ref, b2_ref, w3_ref, fr_ref, dl_ref, k_ref, s_ref):
    i = pl.program_id(0)
    z = z_ref[...]
    fr = fr_ref[...]
    h = jnp.sin(fr * (_dot3(z, w1_ref[...]) + b1_ref[...]))
    h = jnp.sin(fr * (_dot3(h, w2_ref[...]) + b2_ref[...]))
    h3 = _dot3(h, w3_ref[...])
    t = z[:, 0:1]
    sgn = z[:, FILTER_EMB:FILTER_EMB + 1]
    win = jnp.exp(-t * dl_ref[...]) + DECAY_SHIFT
    kk = h3 * win * sgn
    k_ref[...] = kk
    part = jnp.sum(jnp.abs(kk), axis=0, keepdims=True)

    @pl.when(i == 0)
    def _():
        s_ref[...] = part

    @pl.when(i > 0)
    def _():
        s_ref[...] += part


def _filter_taps(zf, w1p, b1, w2, b2, w3d, fr, dl2, seq_len):
    m = zf.shape[0]
    tn = TN_FILT
    cw = HYENA_ORDER * HYENA_WIDTH
    fwd_tiles = seq_len // tn
    return pl.pallas_call(
        _filt_kernel,
        grid=(m // tn,),
        in_specs=[
            pl.BlockSpec((tn, LANES), lambda i: (i, 0)),
            _const_spec((LANES, FILTER_HIDDEN)),
            _const_spec((1, FILTER_HIDDEN)),
            _const_spec((FILTER_HIDDEN, FILTER_HIDDEN)),
            _const_spec((1, FILTER_HIDDEN)),
            pl.BlockSpec((None, FILTER_HIDDEN, cw), lambda i: (i // fwd_tiles, 0, 0)),
            _const_spec((1, FILTER_HIDDEN)),
            _const_spec((1, cw)),
        ],
        out_specs=[
            pl.BlockSpec((tn, cw), lambda i: (i, 0)),
            pl.BlockSpec((1, cw), lambda i: (0, 0)),
        ],
        out_shape=[
            jax.ShapeDtypeStruct((m, cw), F32),
            jax.ShapeDtypeStruct((1, cw), F32),
        ],
        compiler_params=_cparams("arbitrary"),
        name="filter_taps",
    )(zf, w1p, b1, w2, b2, w3d, fr, dl2)


def _fwda_kernel(g_ref, u_ref, o_ref, *, tj, transposed):
    gm = g_ref[...].astype(BF16)
    cols = []
    for j in range(tj):
        uj = u_ref[j] if transposed else u_ref[:, j, :]
        cols.append(uj.astype(BF16))
    r = _dot(gm, jnp.concatenate(cols, axis=1))
    cb = o_ref.shape[-1]
    for j in range(tj):
        o_ref[j] = r[:, j * cb:(j + 1) * cb]


def _fwd_a(u4, gmat, ch_blk, n_ch_blk, tj, transposed):
    r, k = gmat.shape
    b = u4.shape[0]
    cb = HYENA_WIDTH
    if transposed:
        u_spec = pl.BlockSpec((None, tj, k, cb), lambda bi, ci, ji: (bi, ji, 0, ch_blk + ci))
    else:
        u_spec = pl.BlockSpec((None, k, tj, cb), lambda bi, ci, ji: (bi, 0, ji, ch_blk + ci))
    return pl.pallas_call(
        functools.partial(_fwda_kernel, tj=tj, transposed=transposed),
        grid=(b, n_ch_blk, LANES // tj),
        in_specs=[_const_spec((r, k)), u_spec],
        out_specs=pl.BlockSpec((None, tj, r, cb), lambda bi, ci, ji: (bi, ji, 0, ci)),
        out_shape=jax.ShapeDtypeStruct((b, LANES, r, n_ch_blk * cb), F32),
        compiler_params=_cparams("parallel", "parallel", "parallel"),
        name="dft_stage_a",
    )(gmat, u4)


def _twiddled(a_ref, tw_ref, k, reps):
    ar = a_ref[:, 0, k, :]
    ai = a_ref[:, 1, k, :]
    tr = jnp.concatenate([tw_ref[k, 0]] * reps, axis=1)
    ti = jnp.concatenate([tw_ref[k, 1]] * reps, axis=1)
    col = jnp.concatenate([ar * tr - ai * ti, ar * ti + ai * tr], axis=0)
    return col, tr, ti


def _midf_kernel(f_ref, tw_ref, s_ref, a_ref, h_ref, *, kb):
    fm = f_ref[...].astype(BF16)
    inv = 1.0 / s_ref[...]
    reps = a_ref.shape[-1] // LANES
    for k in range(kb):
        col, _, _ = _twiddled(a_ref, tw_ref, k, reps)
        h_ref[k] = _dot(fm, col.astype(BF16)) * inv


def _filter_spectrum(af5, fblk, tw, asum, kb):
    n1h = af5.shape[3]
    ctot = af5.shape[4]
    cb = HYENA_WIDTH
    return pl.pallas_call(
        functools.partial(_midf_kernel, kb=kb),
        grid=(n1h // kb, ctot // cb),
        in_specs=[
            _const_spec((2 * LANES, 2 * LANES)),
            pl.BlockSpec((kb, 2, LANES, LANES), lambda ki, ci: (ki, 0, 0, 0)),
            pl.BlockSpec((1, cb), lambda ki, ci: (0, ci)),
            pl.BlockSpec((None, LANES, 2, kb, cb), lambda ki, ci: (0, 0, 0, ki, ci)),
        ],
        out_specs=pl.BlockSpec((kb, 2 * LANES, cb), lambda ki, ci: (ki, 0, ci)),
        out_shape=jax.ShapeDtypeStruct((n1h, 2 * LANES, ctot), F32),
        compiler_params=_cparams("parallel", "parallel"),
        name="filter_spectrum",
    )(fblk, tw, asum, af5)


def _mid_kernel(f_ref, fi_ref, tw_ref, h_ref, a_ref, o_ref, *, kb):
    fm = f_ref[...].astype(BF16)
    fim = fi_ref[...].astype(BF16)
    reps = a_ref.shape[-1] // LANES
    for k in range(kb):
        col, tr, ti = _twiddled(a_ref, tw_ref, k, reps)
        x = _dot(fm, col.astype(BF16))
        xr, xi = x[:LANES], x[LANES:]
        hr, hi = h_ref[k, :LANES], h_ref[k, LANES:]
        y = jnp.concatenate([xr * hr - xi * hi, xr * hi + xi * hr], axis=0)
        bc = _dot(fim, y.astype(BF16))
        br, bi = bc[:LANES], bc[LANES:]
        o_ref[k] = jnp.concatenate([br * tr + bi * ti, bi * tr - br * ti], axis=0)


def _mid(a5, fblk, fblk_inv, tw, hspec, order, kb):
    b, _, _, n1h, cb = a5.shape
    return pl.pallas_call(
        functools.partial(_mid_kernel, kb=kb),
        grid=(n1h // kb, b),
        in_specs=[
            _const_spec((2 * LANES, 2 * LANES)),
            _const_spec((2 * LANES, 2 * LANES)),
            pl.BlockSpec((kb, 2, LANES, LANES), lambda ki, bi: (ki, 0, 0, 0)),
            pl.BlockSpec((kb, 2 * LANES, cb), lambda ki, bi: (ki, 0, order)),
            pl.BlockSpec((None, LANES, 2, kb, cb), lambda ki, bi: (bi, 0, 0, ki, 0)),
        ],
        out_specs=pl.BlockSpec((None, kb, 2 * LANES, cb), lambda ki, bi: (bi, ki, 0, 0)),
        out_shape=jax.ShapeDtypeStruct((b, n1h, 2 * LANES, cb), F32),
        compiler_params=_cparams("parallel", "parallel"),
        name="dft_stage_c",
    )(fblk, fblk_inv, tw, hspec, a5)


def _inva_kernel(gi_ref, b_ref, xg_ref, v_ref, sb_ref, o_ref, *, tj, v_transposed, out_transposed):
    gm = gi_ref[...].astype(BF16)
    cols = []
    for j in range(tj):
        cols.append(jnp.concatenate([b_ref[:, 0, j, :], b_ref[:, 1, j, :]], axis=0).astype(BF16))
    y = _dot(gm, jnp.concatenate(cols, axis=1))
    sb = sb_ref[...]
    cb = HYENA_WIDTH
    for j in range(tj):
        vj = v_ref[j] if v_transposed else v_ref[:, j, :]
        zj = xg_ref[:, j, :] * (y[:, j * cb:(j + 1) * cb] + sb * vj)
        if out_transposed:
            o_ref[j] = zj
        else:
            o_ref[:, j, :] = zj


def _inv_a(b5, ginv, uc4, gate_blk, v4, v_blk, v_transposed, sb_row, tj, out_transposed):
    b, n1h = b5.shape[0], b5.shape[1]
    cb = HYENA_WIDTH
    if v_transposed:
        v_spec = pl.BlockSpec((None, tj, n1h, cb), lambda bi, ji: (bi, ji, 0, v_blk))
    else:
        v_spec = pl.BlockSpec((None, n1h, tj, cb), lambda bi, ji: (bi, 0, ji, v_blk))
    if out_transposed:
        o_spec = pl.BlockSpec((None, tj, n1h, cb), lambda bi, ji: (bi, ji, 0, 0))
        o_shape = jax.ShapeDtypeStruct((b, LANES, n1h, cb), F32)
    else:
        o_spec = pl.BlockSpec((None, n1h, tj, cb), lambda bi, ji: (bi, 0, ji, 0))
        o_shape = jax.ShapeDtypeStruct((b, n1h, LANES, cb), F32)
    return pl.pallas_call(
        functools.partial(_inva_kernel, tj=tj, v_transposed=v_transposed, out_transposed=out_transposed),
        grid=(b, LANES // tj),
        in_specs=[
            _const_spec((n1h, 2 * n1h)),
            pl.BlockSpec((None, n1h, 2, tj, cb), lambda bi, ji: (bi, 0, 0, ji, 0)),
            pl.BlockSpec((None, n1h, tj, cb), lambda bi, ji: (bi, 0, ji, gate_blk)),
            v_spec,
            _const_spec((1, cb)),
        ],
        out_specs=o_spec,
        out_shape=o_shape,
        compiler_params=_cparams("parallel", "parallel"),
        name="idft_stage_a_gate",
    )(ginv, b5, uc4, v4, sb_row)


def _attn_kernel(sink_ref, q_ref, kp_ref, kc_ref, kn_ref, vp_ref, vc_ref, vn_ref, o_ref, *, nblk):
    i = pl.program_id(1)
    blk = ATTN_BLOCK
    qi = lax.broadcasted_iota(jnp.int32, (blk, 3 * blk), 0)
    si = lax.broadcasted_iota(jnp.int32, (blk, 3 * blk), 1)
    valid = (((si < blk) & (si >= qi) & (i > 0))
             | ((si >= blk) & (si < 2 * blk))
             | ((si >= 2 * blk) & (si - 2 * blk <= qi) & (i < nblk - 1)))
    kcat = jnp.concatenate([kp_ref[...], kc_ref[...], kn_ref[...]], axis=0)
    vcat = jnp.concatenate([vp_ref[...], vc_ref[...], vn_ref[...]], axis=0)
    outs = []
    for p in range(N_HEADS // 2):
        qp = q_ref[:, p * LANES:(p + 1) * LANES]
        acc = jnp.zeros((blk, LANES), F32)
        for half in range(2):
            h = 2 * p + half
            c = 2 * (h // (N_HEADS // N_KV_HEADS)) + half
            s = lax.dot_general(qp, kcat[:, c * LANES:(c + 1) * LANES], (((1,), (1,)), ((), ())),
                                preferred_element_type=F32)
            s = jnp.where(valid, s, NEG_INF)
            sk = sink_ref[h]
            m = jnp.maximum(jnp.max(s, axis=1, keepdims=True), sk)
            e = jnp.exp(s - m)
            den = jnp.sum(e, axis=1, keepdims=True) + jnp.exp(sk - m)
            acc = acc + _dot(e.astype(BF16), vcat[:, c * LANES:(c + 1) * LANES]) / den
        outs.append(acc)
    o_ref[...] = jnp.concatenate(outs, axis=1).astype(BF16)


def _attention(q3, k3, v3, sink):
    b, seq_len, _ = q3.shape
    blk = ATTN_BLOCK
    nblk = seq_len // blk
    wide = 4 * LANES
    prev = lambda bi, i: (bi, jnp.maximum(i - 1, 0), 0)
    cur = lambda bi, i: (bi, i, 0)
    nxt = lambda bi, i: (bi, jnp.minimum(i + 1, nblk - 1), 0)
    kv = lambda f: pl.BlockSpec((None, blk, wide), f)
    return pl.pallas_call(
        functools.partial(_attn_kernel, nblk=nblk),
        grid=(b, nblk),
        in_specs=[pl.BlockSpec(memory_space=pltpu.SMEM),
                  pl.BlockSpec((None, blk, ATTN_WIDTH), cur),
                  kv(prev), kv(cur), kv(nxt), kv(prev), kv(cur), kv(nxt)],
        out_specs=pl.BlockSpec((None, blk, ATTN_WIDTH), cur),
        out_shape=jax.ShapeDtypeStruct((b, seq_len, ATTN_WIDTH), BF16),
        compiler_params=_cparams("parallel", "parallel"),
        name="banded_attention",
    )(sink, q3, k3, k3, k3, v3, v3, v3)


def _final_kernel(x_ref, yh_ref, ya_ref, g_ref, why_ref, wat_ref, wo_ref, nw_ref, wg_ref, wu_ref, wd_ref,
                  o_ref):
    a = _dot(yh_ref[...].astype(BF16), why_ref[...])
    b = _dot(ya_ref[...], wat_ref[...])
    merged = g_ref[:, :D_MODEL] * a + g_ref[:, D_MODEL:] * b
    x1 = x_ref[...] + _dot(merged.astype(BF16), wo_ref[...])
    ms = jnp.mean(x1 * x1, axis=-1, keepdims=True)
    f = (x1 * lax.rsqrt(ms + RMS_EPS) * nw_ref[...]).astype(BF16)
    acc = x1
    for c in range(FFN_HIDDEN // FFN_CHUNK):
        lo, hi = c * FFN_CHUNK, (c + 1) * FFN_CHUNK
        gt = _dot(f, wg_ref[:, lo:hi])
        upv = _dot(f, wu_ref[:, lo:hi])
        hid = gt * jax.nn.sigmoid(gt) * upv
        acc = acc + _dot(hid.astype(BF16), wd_ref[lo:hi, :])
    o_ref[...] = acc


def _final(xf, yh, ya, g, why, wat, wo, nw, wg, wu, wd):
    n = xf.shape[0]
    tm = TM_FINAL
    row = lambda i: (i, 0)
    return pl.pallas_call(
        _final_kernel,
        grid=(n // tm,),
        in_specs=[
            pl.BlockSpec((tm, D_MODEL), row),
            pl.BlockSpec((tm, HYENA_WIDTH), row),
            pl.BlockSpec((tm, ATTN_WIDTH), row),
            pl.BlockSpec((tm, 2 * D_MODEL), row),
            _const_spec((HYENA_WIDTH, D_MODEL)),
            _const_spec((ATTN_WIDTH, D_MODEL)),
            _const_spec((D_MODEL, D_MODEL)),
            _const_spec((1, D_MODEL)),
            _const_spec((D_MODEL, FFN_HIDDEN)),
            _const_spec((D_MODEL, FFN_HIDDEN)),
            _const_spec((FFN_HIDDEN, D_MODEL)),
        ],
        out_specs=pl.BlockSpec((tm, D_MODEL), row),
        out_shape=jax.ShapeDtypeStruct((n, D_MODEL), F32),
        compiler_params=_cparams("parallel"),
        name="merge_ffn",
    )(xf, yh, ya, g, why, wat, wo, nw, wg, wu, wd)


@functools.lru_cache(maxsize=None)
def _dft_constants(n1h):
    n1_len = 2 * n1h
    m = n1_len * LANES
    k1 = np.arange(n1h, dtype=np.float64)[:, None] + 0.5
    n1 = np.arange(n1_len, dtype=np.float64)[None, :]
    gc = np.exp(-2j * np.pi * k1 * n1 / n1_len)
    g_full = np.concatenate([gc.real, gc.imag], axis=0)
    g_inv = (2.0 / m) * np.concatenate([gc.real[:, :n1h].T, gc.imag[:, :n1h].T], axis=1)
    n2 = np.arange(LANES, dtype=np.float64)
    tw = np.exp(-2j * np.pi * k1 * n2[None, :] / m)
    tw4 = np.stack([tw.real, tw.imag], axis=1)[:, :, :, None] * np.ones((1, 1, 1, LANES))
    fc = np.exp(-2j * np.pi * np.outer(n2, n2) / LANES)
    fblk = np.block([[fc.real, -fc.imag], [fc.imag, fc.real]])
    fblk_inv = np.block([[fc.real, fc.imag], [-fc.imag, fc.real]])
    f32 = lambda a: np.ascontiguousarray(a, dtype=np.float32)
    return f32(g_full), f32(g_full[:, :n1h]), f32(g_inv), f32(tw4), f32(fblk), f32(fblk_inv)


def _filter_features(seq_len):
    pos = jnp.arange(seq_len, dtype=F32)
    t = pos / (seq_len - 1)
    bands = jnp.linspace(1e-4, FILTER_BANDS - 1, FILTER_BANDS, dtype=F32)
    ang = (2.0 * math.pi / seq_len) * pos[:, None] * bands[None, :]
    z = jnp.concatenate([t[:, None], jnp.cos(ang), -jnp.sin(ang)], axis=-1)
    zall = jnp.concatenate([z, z[:1], jnp.flip(z[1:], axis=0)], axis=0)
    sgn = jnp.concatenate([jnp.ones((seq_len,), F32), jnp.zeros((1,), F32), -jnp.ones((seq_len - 1,), F32)])
    pad = jnp.zeros((2 * seq_len, LANES - FILTER_EMB - 1), F32)
    return jnp.concatenate([zall, sgn[:, None], pad], axis=-1)


def _rope_tables(seq_len):
    inv = ROPE_THETA ** (-jnp.arange(0, HEAD_DIM, 2, dtype=F32) / HEAD_DIM)
    ang = jnp.arange(seq_len, dtype=F32)[:, None] * inv[None, :]
    cos, sin = jnp.cos(ang), jnp.sin(ang)
    zero = jnp.zeros_like(sin)
    reps = LANES // HEAD_DIM
    cos_t = jnp.tile(jnp.concatenate([cos, cos], axis=1), (1, reps))
    sa = jnp.tile(jnp.concatenate([-sin, zero], axis=1), (1, reps))
    sb = jnp.tile(jnp.concatenate([zero, sin], axis=1), (1, reps))
    return cos_t, sa, sb


def _layer(x, p):
    b, seq_len, _ = x.shape
    n = b * seq_len
    n1h = seq_len // LANES
    xf = x.reshape(n, D_MODEL)
    g_full, g_half, g_inv, tw, fblk, fblk_inv = (jnp.asarray(c) for c in _dft_constants(n1h))
    tj = SUBLANES if n1h >= LANES else 4 * SUBLANES
    kb = SUBLANES

    cos, sa, sb = _rope_tables(seq_len)
    uc, q, k4, v4, g = _inproj(xf, seq_len, p["attn_norm_w"], p["w_in"], p["conv_w"], p["conv_b"],
                               p["q_norm_w"], p["k_norm_w"], cos, sa, sb, p["ones_bd"])

    taps, asum = _filter_taps(_filter_features(seq_len), p["filt_w1"], p["filt_b1"], p["filt_w2"],
                              p["filt_b2"], p["filt_w3"], p["filt_freq"], p["decay"], seq_len)
    cw = HYENA_ORDER * HYENA_WIDTH
    af = _fwd_a(taps.reshape(1, 2 * n1h, LANES, cw), g_full, 0, HYENA_ORDER, SUBLANES, False)
    hspec = _filter_spectrum(af.reshape(1, LANES, 2, n1h, cw), fblk, tw, asum, kb)

    uc4 = uc.reshape(b, n1h, LANES, C_HY)
    a1 = _fwd_a(uc4, g_half, 2, 1, tj, False)
    b1 = _mid(a1.reshape(b, LANES, 2, n1h, HYENA_WIDTH), fblk, fblk_inv, tw, hspec, 0, kb)
    z1t = _inv_a(b1.reshape(b, n1h, 2, LANES, HYENA_WIDTH), g_inv, uc4, 0, uc4, 2, False,
                 p["hyena_bias"][0:1], tj, True)
    a2 = _fwd_a(z1t, g_half, 0, 1, tj, True)
    b2 = _mid(a2.reshape(b, LANES, 2, n1h, HYENA_WIDTH), fblk, fblk_inv, tw, hspec, 1, kb)
    yh = _inv_a(b2.reshape(b, n1h, 2, LANES, HYENA_WIDTH), g_inv, uc4, 1, z1t, 0, True,
                p["hyena_bias"][1:2], tj, False)

    ya = _attention(q.reshape(b, seq_len, ATTN_WIDTH), k4.reshape(b, seq_len, 4 * LANES),
                    v4.reshape(b, seq_len, 4 * LANES), p["attn_sink"])

    out = _final(xf, yh.reshape(n, HYENA_WIDTH), ya.reshape(n, ATTN_WIDTH), g, p["w_hy_out"], p["w_at_out"],
                 p["w_o"], p["ffn_norm_w"], p["w_gate"], p["w_up"], p["w_down"])
    return out.reshape(b, seq_len, D_MODEL)


def kernel(x_prompt, x_sample, attn_norm_w, w_in, hyena_conv_w, hyena_conv_b, filt_w1, filt_b1, filt_w2, filt_b2,
           filt_w3, filt_freq, hyena_bias, q_norm_w, k_norm_w, attn_sink, w_hy_out, w_at_out, w_o, ffn_norm_w,
           w_gate, w_up, w_down):
    cw = HYENA_ORDER * HYENA_WIDTH
    max_decay = math.log(DECAY_TARGET) / DECAY_FAST_PCT
    min_decay = math.log(DECAY_TARGET) / DECAY_SLOW_PCT
    deltas = jnp.abs(jnp.linspace(min_decay, max_decay, HYENA_WIDTH, dtype=F32))
    head = np.arange(ATTN_WIDTH) // HEAD_DIM
    ones_bd = jnp.asarray((head[:, None] == head[None, :]).astype(np.float32) / HEAD_DIM).astype(BF16)
    w3 = filt_w3[0].reshape(FILTER_HIDDEN, HYENA_ORDER, 2, HYENA_WIDTH).transpose(2, 0, 1, 3)
    p = dict(
        attn_norm_w=attn_norm_w[0][None, :],
        w_in=w_in[0].astype(BF16),
        conv_w=hyena_conv_w[0],
        conv_b=hyena_conv_b[0][None, :],
        filt_w1=jnp.pad(filt_w1[0], ((0, LANES - FILTER_EMB), (0, 0))),
        filt_b1=filt_b1[0][None, :],
        filt_w2=filt_w2[0],
        filt_b2=filt_b2[0][None, :],
        filt_w3=w3.reshape(2, FILTER_HIDDEN, cw),
        filt_freq=filt_freq[0][None, :],
        decay=jnp.tile(deltas, HYENA_ORDER)[None, :],
        hyena_bias=hyena_bias[0],
        q_norm_w=jnp.tile(q_norm_w[0], N_HEADS)[None, :],
        k_norm_w=jnp.tile(k_norm_w[0], N_KV_HEADS)[None, :],
        attn_sink=attn_sink[0],
        ones_bd=ones_bd,
        w_hy_out=w_hy_out[0].astype(BF16),
        w_at_out=w_at_out[0].astype(BF16),
        w_o=w_o[0].astype(BF16),
        ffn_norm_w=ffn_norm_w[0][None, :],
        w_gate=w_gate[0].astype(BF16),
        w_up=w_up[0].astype(BF16),
        w_down=w_down[0].astype(BF16),
    )
    return (_layer(x_prompt, p), _layer(x_sample, p))
```

```python
import functools
import math

import numpy as np
import jax
import jax.numpy as jnp
from jax import lax
from jax.experimental import pallas as pl
from jax.experimental.pallas import tpu as pltpu

F32 = jnp.float32
BF16 = jnp.bfloat16

D_MODEL = 1024
HYENA_WIDTH = 512
HYENA_ORDER = 2
FILTER_BANDS = 16
FILTER_EMB = 1 + 2 * FILTER_BANDS
FILTER_HIDDEN = 64
DECAY_FAST_PCT = 0.3
DECAY_SLOW_PCT = 1.5
DECAY_TARGET = 1e-2
DECAY_SHIFT = 0.05
N_HEADS = 8
N_KV_HEADS = 2
HEAD_DIM = 64
ATTN_WIDTH = N_HEADS * HEAD_DIM
KV_WIDTH = N_KV_HEADS * HEAD_DIM
WINDOW = 128
ROPE_THETA = 10000.0
FFN_HIDDEN = 2816
RMS_EPS = 1e-6
NEG_INF = -1e30

C_HY = 3 * HYENA_WIDTH
C_Q = C_HY + ATTN_WIDTH
C_K = C_Q + KV_WIDTH
C_V = C_K + KV_WIDTH
IN_WIDTH = C_V + 2 * D_MODEL

LANES = 128
SUBLANES = 8
ATTN_BLOCK = 128
VMEM_LIMIT = 56 * 1024 * 1024

TM_INPROJ = 512
TM_FINAL = 256
FFN_CHUNK = FFN_HIDDEN // 2


def _cparams(*sem):
    return pltpu.CompilerParams(dimension_semantics=sem, vmem_limit_bytes=VMEM_LIMIT)


def _const_spec(shape):
    nd = len(shape)
    return pl.BlockSpec(shape, lambda *_: (0,) * nd, pipeline_mode=pl.Buffered(1))


def _dot(a, b):
    return jnp.dot(a, b, preferred_element_type=F32)


def _split(a):
    hi = a.astype(BF16)
    lo = (a - hi.astype(F32)).astype(BF16)
    return hi, lo


def _dot3(a, w):
    ah, al = _split(a)
    wh, wl = _split(w)
    return _dot(ah, wh) + _dot(al, wh) + _dot(ah, wl)


def _inproj_kernel(x_ref, xp_ref, xn_ref, nw_ref, w_ref, cw_ref, cb_ref, qw_ref, kw_ref,
                   cos_ref, sa_ref, sb_ref, ones_ref,
                   x1_ref, x2_ref, v_ref, q_ref, k4_ref, v4_ref, g_ref, pad_ref, *, tm, tiles_per_seq):
    pos = pl.program_id(0) % tiles_per_seq
    nw = nw_ref[...]

    def norm(xv):
        ms = jnp.mean(xv * xv, axis=-1, keepdims=True)
        return (xv * lax.rsqrt(ms + RMS_EPS) * nw).astype(BF16)

    h = norm(x_ref[...])
    hh = norm(jnp.concatenate([xp_ref[...], xn_ref[...]], axis=0))

    w_hy = w_ref[:, :C_HY]
    hy = _dot(h, w_hy)
    hyh = _dot(hh, w_hy)
    prev_ok = (pos > 0).astype(F32)
    next_ok = (pos < tiles_per_seq - 1).astype(F32)
    pad_ref[0:SUBLANES] = hyh[0:SUBLANES] * prev_ok
    pad_ref[SUBLANES:SUBLANES + tm] = hy
    pad_ref[SUBLANES + tm:2 * SUBLANES + tm] = hyh[SUBLANES:] * next_ok
    up = pad_ref[SUBLANES - 1:SUBLANES - 1 + tm]
    un = pad_ref[SUBLANES + 1:SUBLANES + 1 + tm]
    cw = cw_ref[...]
    uc = cw[0:1] * up + cw[1:2] * hy + cw[2:3] * un + cb_ref[...]
    x1_ref[...] = uc[:, :HYENA_WIDTH]
    x2_ref[...] = uc[:, HYENA_WIDTH:2 * HYENA_WIDTH]
    v_ref[...] = uc[:, 2 * HYENA_WIDTH:]

    def head_norm_rope(t, wrow, ones, reps):
        hi, lo = _split(t * t)
        ms = _dot(hi, ones) + _dot(lo, ones)
        tn = t * lax.rsqrt(ms + RMS_EPS) * wrow
        width = t.shape[1]
        cos = jnp.concatenate([cos_ref[...]] * reps, axis=1)
        sa = jnp.concatenate([sa_ref[...]] * reps, axis=1)
        sb = jnp.concatenate([sb_ref[...]] * reps, axis=1)
        half = HEAD_DIM // 2
        return tn * cos + pltpu.roll(tn, width - half, 1) * sa + pltpu.roll(tn, half, 1) * sb

    q = _dot(h, w_ref[:, C_HY:C_Q])
    q = head_norm_rope(q, qw_ref[...], ones_ref[...], ATTN_WIDTH // LANES)
    q_ref[...] = (q * (HEAD_DIM ** -0.5)).astype(BF16)

    k = _dot(h, w_ref[:, C_Q:C_K])
    k = head_norm_rope(k, kw_ref[...], ones_ref[:KV_WIDTH, :KV_WIDTH], 1)
    v = _dot(h, w_ref[:, C_K:C_V])

    lo_half = lax.broadcasted_iota(jnp.int32, (tm, KV_WIDTH), 1) < HEAD_DIM

    def spread(t):
        tr = pltpu.roll(t, HEAD_DIM, 1)
        zero = jnp.zeros_like(t)
        return jnp.concatenate([jnp.where(lo_half, t, zero), jnp.where(lo_half, zero, tr),
                                jnp.where(lo_half, tr, zero), jnp.where(lo_half, zero, t)],
                               axis=1).astype(BF16)

    k4_ref[...] = spread(k)
    v4_ref[...] = spread(v)
    g_ref[...] = jax.nn.sigmoid(_dot(h, w_ref[:, C_V:]))


def _inproj(xf, seq_len, nw, w_in_bf, cw, cb, qw, kw, cos, sa, sb, ones_bd):
    n = xf.shape[0]
    tm = TM_INPROJ
    tiles_per_seq = seq_len // tm
    nblk8 = n // SUBLANES
    r8 = tm // SUBLANES
    kern = functools.partial(_inproj_kernel, tm=tm, tiles_per_seq=tiles_per_seq)
    row = lambda i: (i, 0)
    return pl.pallas_call(
        kern,
        grid=(n // tm,),
        in_specs=[
            pl.BlockSpec((tm, D_MODEL), row),
            pl.BlockSpec((SUBLANES, D_MODEL), lambda i: (jnp.maximum(i * r8 - 1, 0), 0)),
            pl.BlockSpec((SUBLANES, D_MODEL), lambda i: (jnp.minimum((i + 1) * r8, nblk8 - 1), 0)),
            _const_spec((1, D_MODEL)),
            _const_spec((D_MODEL, IN_WIDTH)),
            _const_spec((3, C_HY)),
            _const_spec((1, C_HY)),
            _const_spec((1, ATTN_WIDTH)),
            _const_spec((1, KV_WIDTH)),
            pl.BlockSpec((tm, LANES), lambda i: (i % tiles_per_seq, 0)),
            pl.BlockSpec((tm, LANES), lambda i: (i % tiles_per_seq, 0)),
            pl.BlockSpec((tm, LANES), lambda i: (i % tiles_per_seq, 0)),
            _const_spec((ATTN_WIDTH, ATTN_WIDTH)),
        ],
        out_specs=[
            pl.BlockSpec((tm, HYENA_WIDTH), row),
            pl.BlockSpec((tm, HYENA_WIDTH), row),
            pl.BlockSpec((tm, HYENA_WIDTH), row),
            pl.BlockSpec((tm, ATTN_WIDTH), row),
            pl.BlockSpec((tm, 4 * LANES), row),
            pl.BlockSpec((tm, 4 * LANES), row),
            pl.BlockSpec((tm, 2 * D_MODEL), row),
        ],
        out_shape=[
            jax.ShapeDtypeStruct((n, HYENA_WIDTH), F32),
            jax.ShapeDtypeStruct((n, HYENA_WIDTH), F32),
            jax.ShapeDtypeStruct((n, HYENA_WIDTH), F32),
            jax.ShapeDtypeStruct((n, ATTN_WIDTH), BF16),
            jax.ShapeDtypeStruct((n, 4 * LANES), BF16),
            jax.ShapeDtypeStruct((n, 4 * LANES), BF16),
            jax.ShapeDtypeStruct((n, 2 * D_MODEL), F32),
        ],
        scratch_shapes=[pltpu.VMEM((tm + 2 * SUBLANES, C_HY), F32)],
        compiler_params=_cparams("parallel"),
        name="inproj",
    )(xf, xf, xf, nw, w_in_bf, cw, cb, qw, kw, cos, sa, sb, ones_bd)


def _twiddle_store(r, twr_ref, twi_ref, ore_ref, oim_ref, n1h, tj, cb):
    reps = cb // LANES
    for j in range(tj):
        ar = r[:n1h, j * cb:(j + 1) * cb]
        ai = r[n1h:, j * cb:(j + 1) * cb]
        tr = jnp.concatenate([twr_ref[:, j * LANES:(j + 1) * LANES]] * reps, axis=1)
        ti = jnp.concatenate([twi_ref[:, j * LANES:(j + 1) * LANES]] * reps, axis=1)
        ore_ref[:, j * cb:(j + 1) * cb] = (ar * tr - ai * ti).astype(BF16)
        oim_ref[:, j * cb:(j + 1) * cb] = (ar * ti + ai * tr).astype(BF16)


def _filt_kernel(z_ref, w1_ref, b1_ref, w2_ref, b2_ref, w3f_ref, w3b_ref, fr_ref, dl_ref, g_ref, twr_ref, twi_ref,
                 are_ref, aim_ref, s_ref, *, n1h, tj):
    i = pl.program_id(0)
    z = z_ref[...]
    fr = fr_ref[...]
    h = jnp.sin(fr * (_dot3(z, w1_ref[...]) + b1_ref[...]))
    h = jnp.sin(fr * (_dot3(h, w2_ref[...]) + b2_ref[...]))
    dl = dl_ref[...]
    kf = _dot3(h, w3f_ref[...]) * (jnp.exp(-z[:, 0:1] * dl) + DECAY_SHIFT)
    sgn = z[:, LANES + FILTER_EMB:LANES + FILTER_EMB + 1]
    kb = _dot3(h, w3b_ref[...]) * (jnp.exp(-z[:, LANES:LANES + 1] * dl) + DECAY_SHIFT) * sgn
    part = jnp.sum(jnp.abs(kf), axis=0, keepdims=True) + jnp.sum(jnp.abs(kb), axis=0, keepdims=True)

    @pl.when(i == 0)
    def _():
        s_ref[...] = part

    @pl.when(i > 0)
    def _():
        s_ref[...] += part

    cols = [jnp.concatenate([kf[j * n1h:(j + 1) * n1h], kb[j * n1h:(j + 1) * n1h]], axis=0).astype(BF16)
            for j in range(tj)]
    r = _dot(g_ref[...].astype(BF16), jnp.concatenate(cols, axis=1))
    _twiddle_store(r, twr_ref, twi_ref, are_ref, aim_ref, n1h, tj, kf.shape[1])


def _filter_stage_a(zf, w1bd, b1, w2bd, b2, w3f, w3b, fr, dl2, g_full, twr, twi, n1h, tj):
    cw = HYENA_ORDER * HYENA_WIDTH
    hid2 = 2 * FILTER_HIDDEN
    a_shape = jax.ShapeDtypeStruct((n1h, LANES * cw), BF16)
    a_spec = pl.BlockSpec((n1h, tj * cw), lambda i: (0, i))
    t_spec = pl.BlockSpec((n1h, tj * LANES), lambda i: (0, i))
    return pl.pallas_call(
        functools.partial(_filt_kernel, n1h=n1h, tj=tj),
        grid=(LANES // tj,),
        in_specs=[
            pl.BlockSpec((tj * n1h, 2 * LANES), lambda i: (i, 0)),
            _const_spec((2 * LANES, hid2)),
            _const_spec((1, hid2)),
            _const_spec((hid2, hid2)),
            _const_spec((1, hid2)),
            _const_spec((hid2, cw)),
            _const_spec((hid2, cw)),
            _const_spec((1, hid2)),
            _const_spec((1, cw)),
            _const_spec((2 * n1h, 2 * n1h)),
            t_spec, t_spec,
        ],
        out_specs=[a_spec, a_spec, pl.BlockSpec((1, cw), lambda i: (0, 0))],
        out_shape=[a_shape, a_shape, jax.ShapeDtypeStruct((1, cw), F32)],
        compiler_params=_cparams("arbitrary"),
        name="filter_taps_stage_a",
    )(zf, w1bd, b1, w2bd, b2, w3f, w3b, fr, dl2, g_full, twr, twi)


def _midf_kernel(f_ref, s_ref, are_ref, aim_ref, h_ref, *, kb):
    fm = f_ref[...].astype(BF16)
    inv = 1.0 / s_ref[...]
    for k in range(kb):
        h_ref[k] = _dot(fm, jnp.concatenate([are_ref[k], aim_ref[k]], axis=0)) * inv


def _filter_spectrum(afre, afim, fblk, asum, kb):
    n1h, _, ctot = afre.shape
    cb = HYENA_WIDTH
    a_spec = pl.BlockSpec((kb, LANES, cb), lambda ki, ci: (ki, 0, ci))
    return pl.pallas_call(
        functools.partial(_midf_kernel, kb=kb),
        grid=(n1h // kb, ctot // cb),
        in_specs=[_const_spec((2 * LANES, 2 * LANES)), pl.BlockSpec((1, cb), lambda ki, ci: (0, ci)), a_spec, a_spec],
        out_specs=pl.BlockSpec((kb, 2 * LANES, cb), lambda ki, ci: (ki, 0, ci)),
        out_shape=jax.ShapeDtypeStruct((n1h, 2 * LANES, ctot), F32),
        compiler_params=_cparams("parallel", "parallel"),
        name="filter_spectrum",
    )(fblk, asum, afre, afim)


def _fwda_kernel(g_ref, twr_ref, twi_ref, u_ref, are_ref, aim_ref, *, n1h, tj):
    r = _dot(g_ref[...].astype(BF16), u_ref[...].astype(BF16))
    _twiddle_store(r, twr_ref, twi_ref, are_ref, aim_ref, n1h, tj, HYENA_WIDTH)


def _fwd_a(u3, g_half, twr, twi, tj):
    b, n1h, wide = u3.shape
    cb = HYENA_WIDTH
    blk = pl.BlockSpec((None, n1h, tj * cb), lambda bi, ji: (bi, 0, ji))
    t_spec = pl.BlockSpec((n1h, tj * LANES), lambda bi, ji: (0, ji))
    a_shape = jax.ShapeDtypeStruct((b, n1h, wide), BF16)
    return pl.pallas_call(
        functools.partial(_fwda_kernel, n1h=n1h, tj=tj),
        grid=(b, LANES // tj),
        in_specs=[_const_spec((2 * n1h, n1h)), t_spec, t_spec, blk],
        out_specs=[blk, blk],
        out_shape=[a_shape, a_shape],
        compiler_params=_cparams("parallel", "parallel"),
        name="dft_stage_a",
    )(g_half, twr, twi, u3)


def _mid_kernel(f_ref, fi_ref, tw_ref, h_ref, are_ref, aim_ref, bre_ref, bim_ref, *, kb):
    fm = f_ref[...].astype(BF16)
    fim = fi_ref[...].astype(BF16)
    reps = are_ref.shape[-1] // LANES
    for k in range(kb):
        x = _dot(fm, jnp.concatenate([are_ref[k], aim_ref[k]], axis=0))
        xr, xi = x[:LANES], x[LANES:]
        hr, hi = h_ref[k, :LANES], h_ref[k, LANES:]
        y = jnp.concatenate([xr * hr - xi * hi, xr * hi + xi * hr], axis=0)
        bc = _dot(fim, y.astype(BF16))
        br, bi = bc[:LANES], bc[LANES:]
        tr = jnp.concatenate([tw_ref[k, 0]] * reps, axis=1)
        ti = jnp.concatenate([tw_ref[k, 1]] * reps, axis=1)
        bre_ref[k] = (br * tr + bi * ti).astype(BF16)
        bim_ref[k] = (bi * tr - br * ti).astype(BF16)


def _mid(are, aim, fblk, fblk_inv, tw, hspec, order, kb):
    b, n1h, _, cb = are.shape
    a_spec = pl.BlockSpec((None, kb, LANES, cb), lambda ki, bi: (bi, ki, 0, 0))
    shape = jax.ShapeDtypeStruct((b, n1h, LANES, cb), BF16)
    return pl.pallas_call(
        functools.partial(_mid_kernel, kb=kb),
        grid=(n1h // kb, b),
        in_specs=[
            _const_spec((2 * LANES, 2 * LANES)),
            _const_spec((2 * LANES, 2 * LANES)),
            pl.BlockSpec((kb, 2, LANES, LANES), lambda ki, bi: (ki, 0, 0, 0)),
            pl.BlockSpec((kb, 2 * LANES, cb), lambda ki, bi: (ki, 0, order)),
            a_spec, a_spec,
        ],
        out_specs=[a_spec, a_spec],
        out_shape=[shape, shape],
        compiler_params=_cparams("parallel", "parallel"),
        name="dft_stage_c",
    )(fblk, fblk_inv, tw, hspec, are, aim)


def _inva_kernel(gi_ref, bre_ref, bim_ref, xg_ref, v_ref, sb_ref, *rest, n1h, tj, fuse_next):
    y = _dot(gi_ref[...].astype(BF16), jnp.concatenate([bre_ref[...], bim_ref[...]], axis=0))
    z = xg_ref[...] * (y + sb_ref[...] * v_ref[...])
    if fuse_next:
        g_ref, twr_ref, twi_ref, z_ref, are_ref, aim_ref = rest
        z_ref[...] = z
        r = _dot(g_ref[...].astype(BF16), z.astype(BF16))
        _twiddle_store(r, twr_ref, twi_ref, are_ref, aim_ref, n1h, tj, HYENA_WIDTH)
    else:
        (z_ref,) = rest
        z_ref[...] = z


def _inv_a(bre, bim, g_inv, xg, v, sb_row, tj, nxt=None):
    b, n1h, wide = bre.shape
    cb = HYENA_WIDTH
    blk = pl.BlockSpec((None, n1h, tj * cb), lambda bi, ji: (bi, 0, ji))
    in_specs = [_const_spec((n1h, 2 * n1h)), blk, blk, blk, blk, _const_spec((1, tj * cb))]
    args = [g_inv, bre, bim, xg, v, sb_row]
    out_specs = [blk]
    out_shape = [jax.ShapeDtypeStruct((b, n1h, wide), F32)]
    if nxt is not None:
        t_spec = pl.BlockSpec((n1h, tj * LANES), lambda bi, ji: (0, ji))
        in_specs += [_const_spec((2 * n1h, n1h)), t_spec, t_spec]
        args += list(nxt)
        out_specs += [blk, blk]
        out_shape += [jax.ShapeDtypeStruct((b, n1h, wide), BF16)] * 2
    return pl.pallas_call(
        functools.partial(_inva_kernel, n1h=n1h, tj=tj, fuse_next=nxt is not None),
        grid=(b, LANES // tj),
        in_specs=in_specs,
        out_specs=out_specs,
        out_shape=out_shape,
        compiler_params=_cparams("parallel", "parallel"),
        name="idft_stage_a_gate",
    )(*args)


def _attn_kernel(sink_ref, q_ref, kp_ref, kc_ref, kn_ref, vp_ref, vc_ref, vn_ref, o_ref, *, nblk):
    i = pl.program_id(1)
    blk = ATTN_BLOCK
    qi = lax.broadcasted_iota(jnp.int32, (blk, 3 * blk), 0)
    si = lax.broadcasted_iota(jnp.int32, (blk, 3 * blk), 1)
    valid = (((si < blk) & (si >= qi) & (i > 0))
             | ((si >= blk) & (si < 2 * blk))
             | ((si >= 2 * blk) & (si - 2 * blk <= qi) & (i < nblk - 1)))
    kcat = jnp.concatenate([kp_ref[...], kc_ref[...], kn_ref[...]], axis=0)
    vcat = jnp.concatenate([vp_ref[...], vc_ref[...], vn_ref[...]], axis=0)
    outs = []
    for p in range(N_HEADS // 2):
        qp = q_ref[:, p * LANES:(p + 1) * LANES]
        acc = jnp.zeros((blk, LANES), F32)
        for half in range(2):
            h = 2 * p + half
            c = 2 * (h // (N_HEADS // N_KV_HEADS)) + half
            s = lax.dot_general(qp, kcat[:, c * LANES:(c + 1) * LANES], (((1,), (1,)), ((), ())),
                                preferred_element_type=F32)
            s = jnp.where(valid, s, NEG_INF)
            sk = sink_ref[h]
            m = jnp.maximum(jnp.max(s, axis=1, keepdims=True), sk)
            e = jnp.exp(s - m)
            den = jnp.sum(e, axis=1, keepdims=True) + jnp.exp(sk - m)
            acc = acc + _dot(e.astype(BF16), vcat[:, c * LANES:(c + 1) * LANES]) / den
        outs.append(acc)
    o_ref[...] = jnp.concatenate(outs, axis=1).astype(BF16)


def _attention(q3, k3, v3, sink):
    b, seq_len, _ = q3.shape
    blk = ATTN_BLOCK
    nblk = seq_len // blk
    wide = 4 * LANES
    prev = lambda bi, i: (bi, jnp.maximum(i - 1, 0), 0)
    cur = lambda bi, i: (bi, i, 0)
    nxt = lambda bi, i: (bi, jnp.minimum(i + 1, nblk - 1), 0)
    kv = lambda f: pl.BlockSpec((None, blk, wide), f)
    return pl.pallas_call(
        functools.partial(_attn_kernel, nblk=nblk),
        grid=(b, nblk),
        in_specs=[pl.BlockSpec(memory_space=pltpu.SMEM),
                  pl.BlockSpec((None, blk, ATTN_WIDTH), cur),
                  kv(prev), kv(cur), kv(nxt), kv(prev), kv(cur), kv(nxt)],
        out_specs=pl.BlockSpec((None, blk, ATTN_WIDTH), cur),
        out_shape=jax.ShapeDtypeStruct((b, seq_len, ATTN_WIDTH), BF16),
        compiler_params=_cparams("parallel", "parallel"),
        name="banded_attention",
    )(sink, q3, k3, k3, k3, v3, v3, v3)


def _final_kernel(x_ref, yh_ref, ya_ref, g_ref, why_ref, wat_ref, wo_ref, nw_ref, wg_ref, wu_ref, wd_ref,
                  o_ref):
    a = _dot(yh_ref[...].astype(BF16), why_ref[...])
    b = _dot(ya_ref[...], wat_ref[...])
    merged = g_ref[:, :D_MODEL] * a + g_ref[:, D_MODEL:] * b
    x1 = x_ref[...] + _dot(merged.astype(BF16), wo_ref[...])
    ms = jnp.mean(x1 * x1, axis=-1, keepdims=True)
    f = (x1 * lax.rsqrt(ms + RMS_EPS) * nw_ref[...]).astype(BF16)
    acc = x1
    for c in range(FFN_HIDDEN // FFN_CHUNK):
        lo, hi = c * FFN_CHUNK, (c + 1) * FFN_CHUNK
        gt = _dot(f, wg_ref[:, lo:hi])
        upv = _dot(f, wu_ref[:, lo:hi])
        hid = gt * jax.nn.sigmoid(gt) * upv
        acc = acc + _dot(hid.astype(BF16), wd_ref[lo:hi, :])
    o_ref[...] = acc


def _final(xf, yh, ya, g, why, wat, wo, nw, wg, wu, wd):
    n = xf.shape[0]
    tm = TM_FINAL
    row = lambda i: (i, 0)
    return pl.pallas_call(
        _final_kernel,
        grid=(n // tm,),
        in_specs=[
            pl.BlockSpec((tm, D_MODEL), row),
            pl.BlockSpec((tm, HYENA_WIDTH), row),
            pl.BlockSpec((tm, ATTN_WIDTH), row),
            pl.BlockSpec((tm, 2 * D_MODEL), row),
            _const_spec((HYENA_WIDTH, D_MODEL)),
            _const_spec((ATTN_WIDTH, D_MODEL)),
            _const_spec((D_MODEL, D_MODEL)),
            _const_spec((1, D_MODEL)),
            _const_spec((D_MODEL, FFN_HIDDEN)),
            _const_spec((D_MODEL, FFN_HIDDEN)),
            _const_spec((FFN_HIDDEN, D_MODEL)),
        ],
        out_specs=pl.BlockSpec((tm, D_MODEL), row),
        out_shape=jax.ShapeDtypeStruct((n, D_MODEL), F32),
        compiler_params=_cparams("parallel"),
        name="merge_ffn",
    )(xf, yh, ya, g, why, wat, wo, nw, wg, wu, wd)


@functools.lru_cache(maxsize=None)
def _dft_constants(n1h):
    n1_len = 2 * n1h
    m = n1_len * LANES
    k1 = np.arange(n1h, dtype=np.float64)[:, None] + 0.5
    n1 = np.arange(n1_len, dtype=np.float64)[None, :]
    gc = np.exp(-2j * np.pi * k1 * n1 / n1_len)
    g_full = np.concatenate([gc.real, gc.imag], axis=0)
    g_inv = (2.0 / m) * np.concatenate([gc.real[:, :n1h].T, gc.imag[:, :n1h].T], axis=1)
    n2 = np.arange(LANES, dtype=np.float64)
    tw = np.exp(-2j * np.pi * k1 * n2[None, :] / m)
    tw4 = np.stack([tw.real, tw.imag], axis=1)[:, :, :, None] * np.ones((1, 1, 1, LANES))
    fc = np.exp(-2j * np.pi * np.outer(n2, n2) / LANES)
    fblk = np.block([[fc.real, -fc.imag], [fc.imag, fc.real]])
    fblk_inv = np.block([[fc.real, fc.imag], [-fc.imag, fc.real]])
    f32 = lambda a: np.ascontiguousarray(a, dtype=np.float32)
    return dict(g_full=f32(g_full), g_half=f32(g_full[:, :n1h]), g_inv=f32(g_inv), tw4=f32(tw4),
                twr=f32(tw4[:, 0].reshape(n1h, LANES * LANES)), twi=f32(tw4[:, 1].reshape(n1h, LANES * LANES)),
                fblk=f32(fblk), fblk_inv=f32(fblk_inv))


def _filter_features(seq_len):
    n1h = seq_len // LANES
    n2 = jnp.arange(LANES, dtype=jnp.int32)[:, None]
    n1 = jnp.arange(n1h, dtype=jnp.int32)[None, :]
    slot_f = (LANES * n1 + n2).reshape(-1)
    slot_b = slot_f + seq_len
    neg_lag = slot_b > seq_len
    pos_b = jnp.where(neg_lag, 2 * seq_len - slot_b, 0).astype(F32)
    sgn_b = jnp.where(neg_lag, -1.0, 0.0).astype(F32)
    bands = jnp.linspace(1e-4, FILTER_BANDS - 1, FILTER_BANDS, dtype=F32)

    def feats(pos, sgn):
        t = pos / (seq_len - 1)
        ang = (2.0 * math.pi / seq_len) * pos[:, None] * bands[None, :]
        pad = jnp.zeros((pos.shape[0], LANES - FILTER_EMB - 1), F32)
        return jnp.concatenate([t[:, None], jnp.cos(ang), -jnp.sin(ang), sgn[:, None], pad], axis=-1)

    return jnp.concatenate([feats(slot_f.astype(F32), jnp.ones_like(sgn_b)), feats(pos_b, sgn_b)], axis=-1)


def _rope_tables(seq_len):
    inv = ROPE_THETA ** (-jnp.arange(0, HEAD_DIM, 2, dtype=F32) / HEAD_DIM)
    ang = jnp.arange(seq_len, dtype=F32)[:, None] * inv[None, :]
    cos, sin = jnp.cos(ang), jnp.sin(ang)
    zero = jnp.zeros_like(sin)
    reps = LANES // HEAD_DIM
    cos_t = jnp.tile(jnp.concatenate([cos, cos], axis=1), (1, reps))
    sa = jnp.tile(jnp.concatenate([-sin, zero], axis=1), (1, reps))
    sb = jnp.tile(jnp.concatenate([zero, sin], axis=1), (1, reps))
    return cos_t, sa, sb


def _layer(x, p):
    b, seq_len, _ = x.shape
    n = b * seq_len
    n1h = seq_len // LANES
    xf = x.reshape(n, D_MODEL)
    c = {k: jnp.asarray(v) for k, v in _dft_constants(n1h).items()}
    long_seq = n1h >= LANES
    tj = SUBLANES if long_seq else 4 * SUBLANES
    tjf = SUBLANES // 2 if long_seq else 4 * SUBLANES
    kb = SUBLANES
    cw = HYENA_ORDER * HYENA_WIDTH
    wide = LANES * HYENA_WIDTH
    r3 = lambda a: a.reshape(b, n1h, wide)
    r4 = lambda a: a.reshape(b, n1h, LANES, HYENA_WIDTH)

    cos, sa, sb = _rope_tables(seq_len)
    x1, x2, v, q, k4, v4, g = _inproj(xf, seq_len, p["attn_norm_w"], p["w_in"], p["conv_w"], p["conv_b"],
                                      p["q_norm_w"], p["k_norm_w"], cos, sa, sb, p["ones_bd"])

    afre, afim, asum = _filter_stage_a(_filter_features(seq_len), p["filt_w1"], p["filt_b1"], p["filt_w2"],
                                       p["filt_b2"], p["filt_w3f"], p["filt_w3b"], p["filt_freq"], p["decay"],
                                       c["g_full"], c["twr"], c["twi"], n1h, tjf)
    hspec = _filter_spectrum(afre.reshape(n1h, LANES, cw), afim.reshape(n1h, LANES, cw), c["fblk"], asum, kb)

    bias = lambda o: jnp.tile(p["hyena_bias"][o], tj)[None, :]
    are, aim = _fwd_a(r3(v), c["g_half"], c["twr"], c["twi"], tj)
    bre, bim = _mid(r4(are), r4(aim), c["fblk"], c["fblk_inv"], c["tw4"], hspec, 0, kb)
    z1, are, aim = _inv_a(r3(bre), r3(bim), c["g_inv"], r3(x1), r3(v), bias(0), tj,
                          nxt=(c["g_half"], c["twr"], c["twi"]))
    bre, bim = _mid(r4(are), r4(aim), c["fblk"], c["fblk_inv"], c["tw4"], hspec, 1, kb)
    (yh,) = _inv_a(r3(bre), r3(bim), c["g_inv"], r3(x2), z1, bias(1), tj)

    ya = _attention(q.reshape(b, seq_len, ATTN_WIDTH), k4.reshape(b, seq_len, 4 * LANES),
                    v4.reshape(b, seq_len, 4 * LANES), p["attn_sink"])

    out = _final(xf, yh.reshape(n, HYENA_WIDTH), ya.reshape(n, ATTN_WIDTH), g, p["w_hy_out"], p["w_at_out"],
                 p["w_o"], p["ffn_norm_w"], p["w_gate"], p["w_up"], p["w_down"])
    return out.reshape(b, seq_len, D_MODEL)


def _block_diag2(w):
    z = jnp.zeros_like(w)
    return jnp.concatenate([jnp.concatenate([w, z], axis=1), jnp.concatenate([z, w], axis=1)], axis=0)


def kernel(x_prompt, x_sample, attn_norm_w, w_in, hyena_conv_w, hyena_conv_b, filt_w1, filt_b1, filt_w2, filt_b2,
           filt_w3, filt_freq, hyena_bias, q_norm_w, k_norm_w, attn_sink, w_hy_out, w_at_out, w_o, ffn_norm_w,
           w_gate, w_up, w_down):
    cw = HYENA_ORDER * HYENA_WIDTH
    max_decay = math.log(DECAY_TARGET) / DECAY_FAST_PCT
    min_decay = math.log(DECAY_TARGET) / DECAY_SLOW_PCT
    deltas = jnp.abs(jnp.linspace(min_decay, max_decay, HYENA_WIDTH, dtype=F32))
    head = np.arange(ATTN_WIDTH) // HEAD_DIM
    ones_bd = jnp.asarray((head[:, None] == head[None, :]).astype(np.float32) / HEAD_DIM).astype(BF16)
    w3 = filt_w3[0].reshape(FILTER_HIDDEN, HYENA_ORDER, 2, HYENA_WIDTH).transpose(2, 0, 1, 3)
    w3 = w3.reshape(2, FILTER_HIDDEN, cw)
    w3_zero = jnp.zeros((FILTER_HIDDEN, cw), F32)
    twice = lambda a: jnp.tile(a, 2)[None, :]
    p = dict(
        attn_norm_w=attn_norm_w[0][None, :],
        w_in=w_in[0].astype(BF16),
        conv_w=hyena_conv_w[0],
        conv_b=hyena_conv_b[0][None, :],
        filt_w1=_block_diag2(jnp.pad(filt_w1[0], ((0, LANES - FILTER_EMB), (0, 0)))),
        filt_b1=twice(filt_b1[0]),
        filt_w2=_block_diag2(filt_w2[0]),
        filt_b2=twice(filt_b2[0]),
        filt_w3f=jnp.concatenate([w3[0], w3_zero], axis=0),
        filt_w3b=jnp.concatenate([w3_zero, w3[1]], axis=0),
        filt_freq=twice(filt_freq[0]),
        decay=jnp.tile(deltas, HYENA_ORDER)[None, :],
        hyena_bias=hyena_bias[0],
        q_norm_w=jnp.tile(q_norm_w[0], N_HEADS)[None, :],
        k_norm_w=jnp.tile(k_norm_w[0], N_KV_HEADS)[None, :],
        attn_sink=attn_sink[0],
        ones_bd=ones_bd,
        w_hy_out=w_hy_out[0].astype(BF16),
        w_at_out=w_at_out[0].astype(BF16),
        w_o=w_o[0].astype(BF16),
        ffn_norm_w=ffn_norm_w[0][None, :],
        w_gate=w_gate[0].astype(BF16),
        w_up=w_up[0].astype(BF16),
        w_down=w_down[0].astype(BF16),
    )
    return (_layer(x_prompt, p), _layer(x_sample, p))
```

```python
import functools
import math

import numpy as np
import jax
import jax.numpy as jnp
from jax import lax
from jax.experimental import pallas as pl
from jax.experimental.pallas import tpu as pltpu

F32 = jnp.float32
BF16 = jnp.bfloat16

D_MODEL = 1024
HYENA_WIDTH = 512
HYENA_ORDER = 2
FILTER_BANDS = 16
FILTER_EMB = 1 + 2 * FILTER_BANDS
FILTER_HIDDEN = 64
DECAY_FAST_PCT = 0.3
DECAY_SLOW_PCT = 1.5
DECAY_TARGET = 1e-2
DECAY_SHIFT = 0.05
N_HEADS = 8
N_KV_HEADS = 2
HEAD_DIM = 64
ATTN_WIDTH = N_HEADS * HEAD_DIM
KV_WIDTH = N_KV_HEADS * HEAD_DIM
WINDOW = 128
ROPE_THETA = 10000.0
FFN_HIDDEN = 2816
RMS_EPS = 1e-6
NEG_INF = -1e30

C_HY = 3 * HYENA_WIDTH
C_Q = C_HY + ATTN_WIDTH
C_K = C_Q + KV_WIDTH
C_V = C_K + KV_WIDTH
IN_WIDTH = C_V + 2 * D_MODEL

LANES = 128
SUBLANES = 8
ATTN_BLOCK = 128
VMEM_LIMIT = 56 * 1024 * 1024

TM_INPROJ = 512
TM_FINAL = 256
FFN_CHUNK = FFN_HIDDEN // 2
FILT_PAD = 64


def _cparams(*sem):
    return pltpu.CompilerParams(dimension_semantics=sem, vmem_limit_bytes=VMEM_LIMIT)


def _const_spec(shape):
    nd = len(shape)
    return pl.BlockSpec(shape, lambda *_: (0,) * nd, pipeline_mode=pl.Buffered(1))


def _dot(a, b):
    return jnp.dot(a, b, preferred_element_type=F32)


def _split(a):
    hi = a.astype(BF16)
    lo = (a - hi.astype(F32)).astype(BF16)
    return hi, lo


def _dot3(a, w):
    ah, al = _split(a)
    wh, wl = _split(w)
    return _dot(ah, wh) + _dot(al, wh) + _dot(ah, wl)


def _inproj_kernel(x_ref, xp_ref, xn_ref, nw_ref, w_ref, cw_ref, cb_ref, qw_ref, kw_ref,
                   cos_ref, sa_ref, sb_ref, ones_ref,
                   x1_ref, x2_ref, v_ref, q_ref, k4_ref, v4_ref, g_ref, pad_ref, *, tm, tiles_per_seq):
    pos = pl.program_id(0) % tiles_per_seq
    nw = nw_ref[...]

    def norm(xv):
        ms = jnp.mean(xv * xv, axis=-1, keepdims=True)
        return (xv * lax.rsqrt(ms + RMS_EPS) * nw).astype(BF16)

    h = norm(x_ref[...])
    hh = norm(jnp.concatenate([xp_ref[...], xn_ref[...]], axis=0))

    w_hy = w_ref[:, :C_HY]
    hy = _dot(h, w_hy)
    hyh = _dot(hh, w_hy)
    prev_ok = (pos > 0).astype(F32)
    next_ok = (pos < tiles_per_seq - 1).astype(F32)
    pad_ref[0:SUBLANES] = hyh[0:SUBLANES] * prev_ok
    pad_ref[SUBLANES:SUBLANES + tm] = hy
    pad_ref[SUBLANES + tm:2 * SUBLANES + tm] = hyh[SUBLANES:] * next_ok
    up = pad_ref[SUBLANES - 1:SUBLANES - 1 + tm]
    un = pad_ref[SUBLANES + 1:SUBLANES + 1 + tm]
    cw = cw_ref[...]
    uc = cw[0:1] * up + cw[1:2] * hy + cw[2:3] * un + cb_ref[...]
    x1_ref[...] = uc[:, :HYENA_WIDTH]
    x2_ref[...] = uc[:, HYENA_WIDTH:2 * HYENA_WIDTH]
    v_ref[...] = uc[:, 2 * HYENA_WIDTH:]

    def head_norm_rope(t, wrow, ones, reps):
        hi, lo = _split(t * t)
        ms = _dot(hi, ones) + _dot(lo, ones)
        tn = t * lax.rsqrt(ms + RMS_EPS) * wrow
        width = t.shape[1]
        cos = jnp.concatenate([cos_ref[...]] * reps, axis=1)
        sa = jnp.concatenate([sa_ref[...]] * reps, axis=1)
        sb = jnp.concatenate([sb_ref[...]] * reps, axis=1)
        half = HEAD_DIM // 2
        return tn * cos + pltpu.roll(tn, width - half, 1) * sa + pltpu.roll(tn, half, 1) * sb

    q = _dot(h, w_ref[:, C_HY:C_Q])
    q = head_norm_rope(q, qw_ref[...], ones_ref[...], ATTN_WIDTH // LANES)
    q_ref[...] = (q * (HEAD_DIM ** -0.5)).astype(BF16)

    k = _dot(h, w_ref[:, C_Q:C_K])
    k = head_norm_rope(k, kw_ref[...], ones_ref[:KV_WIDTH, :KV_WIDTH], 1)
    v = _dot(h, w_ref[:, C_K:C_V])

    lo_half = lax.broadcasted_iota(jnp.int32, (tm, KV_WIDTH), 1) < HEAD_DIM

    def spread(t):
        tr = pltpu.roll(t, HEAD_DIM, 1)
        zero = jnp.zeros_like(t)
        return jnp.concatenate([jnp.where(lo_half, t, zero), jnp.where(lo_half, zero, tr),
                                jnp.where(lo_half, tr, zero), jnp.where(lo_half, zero, t)],
                               axis=1).astype(BF16)

    k4_ref[...] = spread(k)
    v4_ref[...] = spread(v)
    g_ref[...] = jax.nn.sigmoid(_dot(h, w_ref[:, C_V:]))


def _inproj(xf, seq_len, nw, w_in_bf, cw, cb, qw, kw, cos, sa, sb, ones_bd):
    n = xf.shape[0]
    tm = TM_INPROJ
    tiles_per_seq = seq_len // tm
    nblk8 = n // SUBLANES
    r8 = tm // SUBLANES
    kern = functools.partial(_inproj_kernel, tm=tm, tiles_per_seq=tiles_per_seq)
    row = lambda i: (i, 0)
    return pl.pallas_call(
        kern,
        grid=(n // tm,),
        in_specs=[
            pl.BlockSpec((tm, D_MODEL), row),
            pl.BlockSpec((SUBLANES, D_MODEL), lambda i: (jnp.maximum(i * r8 - 1, 0), 0)),
            pl.BlockSpec((SUBLANES, D_MODEL), lambda i: (jnp.minimum((i + 1) * r8, nblk8 - 1), 0)),
            _const_spec((1, D_MODEL)),
            _const_spec((D_MODEL, IN_WIDTH)),
            _const_spec((3, C_HY)),
            _const_spec((1, C_HY)),
            _const_spec((1, ATTN_WIDTH)),
            _const_spec((1, KV_WIDTH)),
            pl.BlockSpec((tm, LANES), lambda i: (i % tiles_per_seq, 0)),
            pl.BlockSpec((tm, LANES), lambda i: (i % tiles_per_seq, 0)),
            pl.BlockSpec((tm, LANES), lambda i: (i % tiles_per_seq, 0)),
            _const_spec((ATTN_WIDTH, ATTN_WIDTH)),
        ],
        out_specs=[
            pl.BlockSpec((tm, HYENA_WIDTH), row),
            pl.BlockSpec((tm, HYENA_WIDTH), row),
            pl.BlockSpec((tm, HYENA_WIDTH), row),
            pl.BlockSpec((tm, ATTN_WIDTH), row),
            pl.BlockSpec((tm, 4 * LANES), row),
            pl.BlockSpec((tm, 4 * LANES), row),
            pl.BlockSpec((tm, 2 * D_MODEL), row),
        ],
        out_shape=[
            jax.ShapeDtypeStruct((n, HYENA_WIDTH), F32),
            jax.ShapeDtypeStruct((n, HYENA_WIDTH), F32),
            jax.ShapeDtypeStruct((n, HYENA_WIDTH), F32),
            jax.ShapeDtypeStruct((n, ATTN_WIDTH), BF16),
            jax.ShapeDtypeStruct((n, 4 * LANES), BF16),
            jax.ShapeDtypeStruct((n, 4 * LANES), BF16),
            jax.ShapeDtypeStruct((n, 2 * D_MODEL), F32),
        ],
        scratch_shapes=[pltpu.VMEM((tm + 2 * SUBLANES, C_HY), F32)],
        compiler_params=_cparams("parallel"),
        name="inproj",
    )(xf, xf, xf, nw, w_in_bf, cw, cb, qw, kw, cos, sa, sb, ones_bd)


def _rows_to_tiles(r, tj):
    return r.reshape(r.shape[0], tj, r.shape[1] // tj)


def _tiles_to_rows(t):
    return t.reshape(t.shape[0], t.shape[1] * t.shape[2])


def _twiddle_tiles(r, twr_ref, twi_ref, n1h, tj):
    r3 = _rows_to_tiles(r, tj)
    ar, ai = r3[:n1h], r3[n1h:]
    reps = r3.shape[-1] // LANES
    tr = jnp.concatenate([twr_ref[...]] * reps, axis=-1)
    ti = jnp.concatenate([twi_ref[...]] * reps, axis=-1)
    return (ar * tr - ai * ti).astype(BF16), (ar * ti + ai * tr).astype(BF16)


def _filt_kernel(z_ref, w1_ref, b1_ref, w2_ref, b2_ref, w3f_ref, w3b_ref, fr_ref, dl_ref, g_ref, twr_ref, twi_ref,
                 are_ref, aim_ref, s_ref, *, n1h, tj):
    i = pl.program_id(0)
    z = z_ref[...]
    fr = fr_ref[...]
    h = jnp.sin(fr * (_dot3(z, w1_ref[...]) + b1_ref[...]))
    h = jnp.sin(fr * (_dot3(h, w2_ref[...]) + b2_ref[...]))
    hh, hl = _split(h)
    tf = z[:, 0:1]
    tb = z[:, FILT_PAD:FILT_PAD + 1]
    sgn = z[:, FILT_PAD + FILTER_EMB:FILT_PAD + FILTER_EMB + 1]
    gm = g_ref[...].astype(BF16)
    cb = HYENA_WIDTH
    parts = []
    for o in range(HYENA_ORDER):
        sl = slice(o * cb, (o + 1) * cb)
        dl = dl_ref[...]

        def taps(w_ref, t):
            wh, wl = _split(w_ref[:, sl])
            return (_dot(hh, wh) + _dot(hl, wh) + _dot(hh, wl)) * (jnp.exp(-t * dl) + DECAY_SHIFT)

        kf = taps(w3f_ref, tf)
        kb = taps(w3b_ref, tb) * sgn
        parts.append(jnp.sum(jnp.abs(kf), axis=0, keepdims=True) + jnp.sum(jnp.abs(kb), axis=0, keepdims=True))
        cols = [jnp.concatenate([kf[j * n1h:(j + 1) * n1h], kb[j * n1h:(j + 1) * n1h]], axis=0).astype(BF16)
                for j in range(tj)]
        re, im = _twiddle_tiles(_dot(gm, jnp.concatenate(cols, axis=1)), twr_ref, twi_ref, n1h, tj)
        are_ref[:, :, sl] = re
        aim_ref[:, :, sl] = im
    part = jnp.concatenate(parts, axis=1)

    @pl.when(i == 0)
    def _():
        s_ref[...] = part

    @pl.when(i > 0)
    def _():
        s_ref[...] += part


def _filter_stage_a(zf, w1bd, b1, w2bd, b2, w3f, w3b, fr, dl, g_full, twr, twi, n1h, tj):
    cw = HYENA_ORDER * HYENA_WIDTH
    hid2 = 2 * FILTER_HIDDEN
    a_shape = jax.ShapeDtypeStruct((n1h, LANES, cw), BF16)
    a_spec = pl.BlockSpec((n1h, tj, cw), lambda i: (0, i, 0))
    t_spec = pl.BlockSpec((n1h, tj, LANES), lambda i: (0, i, 0))
    return pl.pallas_call(
        functools.partial(_filt_kernel, n1h=n1h, tj=tj),
        grid=(LANES // tj,),
        in_specs=[
            pl.BlockSpec((tj * n1h, 2 * FILT_PAD), lambda i: (i, 0)),
            _const_spec((2 * FILT_PAD, hid2)),
            _const_spec((1, hid2)),
            _const_spec((hid2, hid2)),
            _const_spec((1, hid2)),
            _const_spec((hid2, cw)),
            _const_spec((hid2, cw)),
            _const_spec((1, hid2)),
            _const_spec((1, HYENA_WIDTH)),
            _const_spec((2 * n1h, 2 * n1h)),
            t_spec, t_spec,
        ],
        out_specs=[a_spec, a_spec, pl.BlockSpec((1, cw), lambda i: (0, 0))],
        out_shape=[a_shape, a_shape, jax.ShapeDtypeStruct((1, cw), F32)],
        compiler_params=_cparams("arbitrary"),
        name="filter_taps_stage_a",
    )(zf, w1bd, b1, w2bd, b2, w3f, w3b, fr, dl, g_full, twr, twi)


def _midf_kernel(f_ref, s_ref, are_ref, aim_ref, h_ref, *, kb):
    fm = f_ref[...].astype(BF16)
    inv = 1.0 / s_ref[...]
    for k in range(kb):
        h_ref[k] = _dot(fm, jnp.concatenate([are_ref[k], aim_ref[k]], axis=0)) * inv


def _filter_spectrum(afre, afim, fblk, asum, kb):
    n1h, _, ctot = afre.shape
    cb = HYENA_WIDTH
    a_spec = pl.BlockSpec((kb, LANES, cb), lambda ki, ci: (ki, 0, ci))
    return pl.pallas_call(
        functools.partial(_midf_kernel, kb=kb),
        grid=(n1h // kb, ctot // cb),
        in_specs=[_const_spec((2 * LANES, 2 * LANES)), pl.BlockSpec((1, cb), lambda ki, ci: (0, ci)), a_spec, a_spec],
        out_specs=pl.BlockSpec((kb, 2 * LANES, cb), lambda ki, ci: (ki, 0, ci)),
        out_shape=jax.ShapeDtypeStruct((n1h, 2 * LANES, ctot), F32),
        compiler_params=_cparams("parallel", "parallel"),
        name="filter_spectrum",
    )(fblk, asum, afre, afim)


def _fwda_kernel(g_ref, twr_ref, twi_ref, u_ref, are_ref, aim_ref, *, n1h, tj):
    u = _tiles_to_rows(u_ref[...].astype(BF16))
    re, im = _twiddle_tiles(_dot(g_ref[...].astype(BF16), u), twr_ref, twi_ref, n1h, tj)
    are_ref[...] = re
    aim_ref[...] = im


def _fwd_a(u4, g_half, twr, twi, tj):
    b, n1h, _, cb = u4.shape
    blk = pl.BlockSpec((None, n1h, tj, cb), lambda bi, ji: (bi, 0, ji, 0))
    t_spec = pl.BlockSpec((n1h, tj, LANES), lambda bi, ji: (0, ji, 0))
    a_shape = jax.ShapeDtypeStruct(u4.shape, BF16)
    return pl.pallas_call(
        functools.partial(_fwda_kernel, n1h=n1h, tj=tj),
        grid=(b, LANES // tj),
        in_specs=[_const_spec((2 * n1h, n1h)), t_spec, t_spec, blk],
        out_specs=[blk, blk],
        out_shape=[a_shape, a_shape],
        compiler_params=_cparams("parallel", "parallel"),
        name="dft_stage_a",
    )(g_half, twr, twi, u4)


def _mid_kernel(f_ref, fi_ref, twr_ref, twi_ref, h_ref, are_ref, aim_ref, bre_ref, bim_ref, *, kb):
    fm = f_ref[...].astype(BF16)
    fim = fi_ref[...].astype(BF16)
    reps = are_ref.shape[-1] // LANES
    for k in range(kb):
        x = _dot(fm, jnp.concatenate([are_ref[k], aim_ref[k]], axis=0))
        xr, xi = x[:LANES], x[LANES:]
        hr, hi = h_ref[k, :LANES], h_ref[k, LANES:]
        y = jnp.concatenate([xr * hr - xi * hi, xr * hi + xi * hr], axis=0)
        bc = _dot(fim, y.astype(BF16))
        br, bi = bc[:LANES], bc[LANES:]
        tr = jnp.concatenate([twr_ref[k]] * reps, axis=1)
        ti = jnp.concatenate([twi_ref[k]] * reps, axis=1)
        bre_ref[k] = (br * tr + bi * ti).astype(BF16)
        bim_ref[k] = (bi * tr - br * ti).astype(BF16)


def _mid(are, aim, fblk, fblk_inv, twr, twi, hspec, order, kb):
    b, n1h, _, cb = are.shape
    a_spec = pl.BlockSpec((None, kb, LANES, cb), lambda ki, bi: (bi, ki, 0, 0))
    t_spec = pl.BlockSpec((kb, LANES, LANES), lambda ki, bi: (ki, 0, 0))
    shape = jax.ShapeDtypeStruct(are.shape, BF16)
    return pl.pallas_call(
        functools.partial(_mid_kernel, kb=kb),
        grid=(n1h // kb, b),
        in_specs=[
            _const_spec((2 * LANES, 2 * LANES)),
            _const_spec((2 * LANES, 2 * LANES)),
            t_spec, t_spec,
            pl.BlockSpec((kb, 2 * LANES, cb), lambda ki, bi: (ki, 0, order)),
            a_spec, a_spec,
        ],
        out_specs=[a_spec, a_spec],
        out_shape=[shape, shape],
        compiler_params=_cparams("parallel", "parallel"),
        name="dft_stage_c",
    )(fblk, fblk_inv, twr, twi, hspec, are, aim)


def _inva_kernel(gi_ref, bre_ref, bim_ref, xg_ref, v_ref, sb_ref, *rest, n1h, tj, fuse_next):
    rhs = jnp.concatenate([_tiles_to_rows(bre_ref[...]), _tiles_to_rows(bim_ref[...])], axis=0)
    y = _rows_to_tiles(_dot(gi_ref[...].astype(BF16), rhs), tj)
    z = xg_ref[...] * (y + sb_ref[...] * v_ref[...])
    if fuse_next:
        g_ref, twr_ref, twi_ref, z_ref, are_ref, aim_ref = rest
        z_ref[...] = z
        r = _dot(g_ref[...].astype(BF16), _tiles_to_rows(z.astype(BF16)))
        re, im = _twiddle_tiles(r, twr_ref, twi_ref, n1h, tj)
        are_ref[...] = re
        aim_ref[...] = im
    else:
        (z_ref,) = rest
        z_ref[...] = z


def _inv_a(bre, bim, g_inv, xg, v, sb_row, tj, nxt=None):
    b, n1h, _, cb = bre.shape
    blk = pl.BlockSpec((None, n1h, tj, cb), lambda bi, ji: (bi, 0, ji, 0))
    in_specs = [_const_spec((n1h, 2 * n1h)), blk, blk, blk, blk, _const_spec((1, cb))]
    args = [g_inv, bre, bim, xg, v, sb_row]
    out_specs = [blk]
    out_shape = [jax.ShapeDtypeStruct(bre.shape, F32)]
    if nxt is not None:
        t_spec = pl.BlockSpec((n1h, tj, LANES), lambda bi, ji: (0, ji, 0))
        in_specs += [_const_spec((2 * n1h, n1h)), t_spec, t_spec]
        args += list(nxt)
        out_specs += [blk, blk]
        out_shape += [jax.ShapeDtypeStruct(bre.shape, BF16)] * 2
    return pl.pallas_call(
        functools.partial(_inva_kernel, n1h=n1h, tj=tj, fuse_next=nxt is not None),
        grid=(b, LANES // tj),
        in_specs=in_specs,
        out_specs=out_specs,
        out_shape=out_shape,
        compiler_params=_cparams("parallel", "parallel"),
        name="idft_stage_a_gate",
    )(*args)


def _attn_kernel(sink_ref, q_ref, kp_ref, kc_ref, kn_ref, vp_ref, vc_ref, vn_ref, o_ref, *, nblk):
    i = pl.program_id(1)
    blk = ATTN_BLOCK
    qi = lax.broadcasted_iota(jnp.int32, (blk, 3 * blk), 0)
    si = lax.broadcasted_iota(jnp.int32, (blk, 3 * blk), 1)
    valid = (((si < blk) & (si >= qi) & (i > 0))
             | ((si >= blk) & (si < 2 * blk))
             | ((si >= 2 * blk) & (si - 2 * blk <= qi) & (i < nblk - 1)))
    kcat = jnp.concatenate([kp_ref[...], kc_ref[...], kn_ref[...]], axis=0)
    vcat = jnp.concatenate([vp_ref[...], vc_ref[...], vn_ref[...]], axis=0)
    outs = []
    for p in range(N_HEADS // 2):
        qp = q_ref[:, p * LANES:(p + 1) * LANES]
        acc = jnp.zeros((blk, LANES), F32)
        for half in range(2):
            h = 2 * p + half
            c = 2 * (h // (N_HEADS // N_KV_HEADS)) + half
            s = lax.dot_general(qp, kcat[:, c * LANES:(c + 1) * LANES], (((1,), (1,)), ((), ())),
                                preferred_element_type=F32)
            s = jnp.where(valid, s, NEG_INF)
            sk = sink_ref[h]
            m = jnp.maximum(jnp.max(s, axis=1, keepdims=True), sk)
            e = jnp.exp(s - m)
            den = jnp.sum(e, axis=1, keepdims=True) + jnp.exp(sk - m)
            acc = acc + _dot(e.astype(BF16), vcat[:, c * LANES:(c + 1) * LANES]) / den
        outs.append(acc)
    o_ref[...] = jnp.concatenate(outs, axis=1).astype(BF16)


def _attention(q3, k3, v3, sink):
    b, seq_len, _ = q3.shape
    blk = ATTN_BLOCK
    nblk = seq_len // blk
    wide = 4 * LANES
    prev = lambda bi, i: (bi, jnp.maximum(i - 1, 0), 0)
    cur = lambda bi, i: (bi, i, 0)
    nxt = lambda bi, i: (bi, jnp.minimum(i + 1, nblk - 1), 0)
    kv = lambda f: pl.BlockSpec((None, blk, wide), f)
    return pl.pallas_call(
        functools.partial(_attn_kernel, nblk=nblk),
        grid=(b, nblk),
        in_specs=[pl.BlockSpec(memory_space=pltpu.SMEM),
                  pl.BlockSpec((None, blk, ATTN_WIDTH), cur),
                  kv(prev), kv(cur), kv(nxt), kv(prev), kv(cur), kv(nxt)],
        out_specs=pl.BlockSpec((None, blk, ATTN_WIDTH), cur),
        out_shape=jax.ShapeDtypeStruct((b, seq_len, ATTN_WIDTH), BF16),
        compiler_params=_cparams("parallel", "parallel"),
        name="banded_attention",
    )(sink, q3, k3, k3, k3, v3, v3, v3)


def _final_kernel(x_ref, yh_ref, ya_ref, g_ref, why_ref, wat_ref, wo_ref, nw_ref, wg_ref, wu_ref, wd_ref,
                  o_ref):
    a = _dot(yh_ref[...].astype(BF16), why_ref[...])
    b = _dot(ya_ref[...], wat_ref[...])
    merged = g_ref[:, :D_MODEL] * a + g_ref[:, D_MODEL:] * b
    x1 = x_ref[...] + _dot(merged.astype(BF16), wo_ref[...])
    ms = jnp.mean(x1 * x1, axis=-1, keepdims=True)
    f = (x1 * lax.rsqrt(ms + RMS_EPS) * nw_ref[...]).astype(BF16)
    acc = x1
    for c in range(FFN_HIDDEN // FFN_CHUNK):
        lo, hi = c * FFN_CHUNK, (c + 1) * FFN_CHUNK
        gt = _dot(f, wg_ref[:, lo:hi])
        upv = _dot(f, wu_ref[:, lo:hi])
        hid = gt * jax.nn.sigmoid(gt) * upv
        acc = acc + _dot(hid.astype(BF16), wd_ref[lo:hi, :])
    o_ref[...] = acc


def _final(xf, yh, ya, g, why, wat, wo, nw, wg, wu, wd):
    n = xf.shape[0]
    tm = TM_FINAL
    row = lambda i: (i, 0)
    return pl.pallas_call(
        _final_kernel,
        grid=(n // tm,),
        in_specs=[
            pl.BlockSpec((tm, D_MODEL), row),
            pl.BlockSpec((tm, HYENA_WIDTH), row),
            pl.BlockSpec((tm, ATTN_WIDTH), row),
            pl.BlockSpec((tm, 2 * D_MODEL), row),
            _const_spec((HYENA_WIDTH, D_MODEL)),
            _const_spec((ATTN_WIDTH, D_MODEL)),
            _const_spec((D_MODEL, D_MODEL)),
            _const_spec((1, D_MODEL)),
            _const_spec((D_MODEL, FFN_HIDDEN)),
            _const_spec((D_MODEL, FFN_HIDDEN)),
            _const_spec((FFN_HIDDEN, D_MODEL)),
        ],
        out_specs=pl.BlockSpec((tm, D_MODEL), row),
        out_shape=jax.ShapeDtypeStruct((n, D_MODEL), F32),
        compiler_params=_cparams("parallel"),
        name="merge_ffn",
    )(xf, yh, ya, g, why, wat, wo, nw, wg, wu, wd)


@functools.lru_cache(maxsize=None)
def _dft_constants(n1h):
    n1_len = 2 * n1h
    m = n1_len * LANES
    k1 = np.arange(n1h, dtype=np.float64)[:, None] + 0.5
    n1 = np.arange(n1_len, dtype=np.float64)[None, :]
    gc = np.exp(-2j * np.pi * k1 * n1 / n1_len)
    g_full = np.concatenate([gc.real, gc.imag], axis=0)
    g_inv = (2.0 / m) * np.concatenate([gc.real[:, :n1h].T, gc.imag[:, :n1h].T], axis=1)
    n2 = np.arange(LANES, dtype=np.float64)
    tw = np.exp(-2j * np.pi * k1 * n2[None, :] / m)[:, :, None] * np.ones((1, 1, LANES))
    fc = np.exp(-2j * np.pi * np.outer(n2, n2) / LANES)
    fblk = np.block([[fc.real, -fc.imag], [fc.imag, fc.real]])
    fblk_inv = np.block([[fc.real, fc.imag], [-fc.imag, fc.real]])
    f32 = lambda a: np.ascontiguousarray(a, dtype=np.float32)
    return dict(g_full=f32(g_full), g_half=f32(g_full[:, :n1h]), g_inv=f32(g_inv), twr=f32(tw.real),
                twi=f32(tw.imag), fblk=f32(fblk), fblk_inv=f32(fblk_inv))


@functools.lru_cache(maxsize=None)
def _filter_features(seq_len):
    n1h = seq_len // LANES
    n2 = np.arange(LANES)[:, None]
    n1 = np.arange(n1h)[None, :]
    slot_f = (LANES * n1 + n2).reshape(-1)
    slot_b = slot_f + seq_len
    neg_lag = slot_b > seq_len
    pos_b = np.where(neg_lag, 2 * seq_len - slot_b, 0)
    bands = np.linspace(1e-4, FILTER_BANDS - 1, FILTER_BANDS)

    def feats(pos, sgn):
        pos = pos.astype(np.float64)
        ang = (2.0 * math.pi / seq_len) * pos[:, None] * bands[None, :]
        pad = np.zeros((pos.shape[0], FILT_PAD - FILTER_EMB - 1))
        return np.concatenate([(pos / (seq_len - 1))[:, None], np.cos(ang), -np.sin(ang), sgn[:, None], pad], axis=-1)

    z = np.concatenate([feats(slot_f, np.ones(slot_f.shape)), feats(pos_b, np.where(neg_lag, -1.0, 0.0))], axis=-1)
    return np.ascontiguousarray(z, dtype=np.float32)


@functools.lru_cache(maxsize=None)
def _rope_tables(seq_len):
    inv = ROPE_THETA ** (-np.arange(0, HEAD_DIM, 2, dtype=np.float64) / HEAD_DIM)
    ang = np.arange(seq_len, dtype=np.float64)[:, None] * inv[None, :]
    cos, sin = np.cos(ang), np.sin(ang)
    zero = np.zeros_like(sin)
    reps = LANES // HEAD_DIM
    f32 = lambda a: np.ascontiguousarray(np.tile(a, (1, reps)), dtype=np.float32)
    return f32(np.concatenate([cos, cos], axis=1)), f32(np.concatenate([-sin, zero], axis=1)), \
        f32(np.concatenate([zero, sin], axis=1))


def _layer(x, p):
    b, seq_len, _ = x.shape
    n = b * seq_len
    n1h = seq_len // LANES
    xf = x.reshape(n, D_MODEL)
    c = {k: jnp.asarray(v) for k, v in _dft_constants(n1h).items()}
    tj = SUBLANES if n1h >= LANES else 4 * SUBLANES
    kb = SUBLANES
    r4 = lambda a: a.reshape(b, n1h, LANES, HYENA_WIDTH)

    cos, sa, sb = (jnp.asarray(t) for t in _rope_tables(seq_len))
    x1, x2, v, q, k4, v4, g = _inproj(xf, seq_len, p["attn_norm_w"], p["w_in"], p["conv_w"], p["conv_b"],
                                      p["q_norm_w"], p["k_norm_w"], cos, sa, sb, p["ones_bd"])

    afre, afim, asum = _filter_stage_a(jnp.asarray(_filter_features(seq_len)), p["filt_w1"], p["filt_b1"],
                                       p["filt_w2"], p["filt_b2"], p["filt_w3f"], p["filt_w3b"], p["filt_freq"],
                                       p["decay"], c["g_full"], c["twr"], c["twi"], n1h, tj)
    hspec = _filter_spectrum(afre, afim, c["fblk"], asum, kb)

    are, aim = _fwd_a(r4(v), c["g_half"], c["twr"], c["twi"], tj)
    bre, bim = _mid(are, aim, c["fblk"], c["fblk_inv"], c["twr"], c["twi"], hspec, 0, kb)
    z1, are, aim = _inv_a(bre, bim, c["g_inv"], r4(x1), r4(v), p["hyena_bias"][0:1], tj,
                          nxt=(c["g_half"], c["twr"], c["twi"]))
    bre, bim = _mid(are, aim, c["fblk"], c["fblk_inv"], c["twr"], c["twi"], hspec, 1, kb)
    (yh,) = _inv_a(bre, bim, c["g_inv"], r4(x2), z1, p["hyena_bias"][1:2], tj)

    ya = _attention(q.reshape(b, seq_len, ATTN_WIDTH), k4.reshape(b, seq_len, 4 * LANES),
                    v4.reshape(b, seq_len, 4 * LANES), p["attn_sink"])

    out = _final(xf, yh.reshape(n, HYENA_WIDTH), ya.reshape(n, ATTN_WIDTH), g, p["w_hy_out"], p["w_at_out"],
                 p["w_o"], p["ffn_norm_w"], p["w_gate"], p["w_up"], p["w_down"])
    return out.reshape(b, seq_len, D_MODEL)


def _block_diag2(w):
    z = jnp.zeros_like(w)
    return jnp.concatenate([jnp.concatenate([w, z], axis=1), jnp.concatenate([z, w], axis=1)], axis=0)


def kernel(x_prompt, x_sample, attn_norm_w, w_in, hyena_conv_w, hyena_conv_b, filt_w1, filt_b1, filt_w2, filt_b2,
           filt_w3, filt_freq, hyena_bias, q_norm_w, k_norm_w, attn_sink, w_hy_out, w_at_out, w_o, ffn_norm_w,
           w_gate, w_up, w_down):
    cw = HYENA_ORDER * HYENA_WIDTH
    max_decay = math.log(DECAY_TARGET) / DECAY_FAST_PCT
    min_decay = math.log(DECAY_TARGET) / DECAY_SLOW_PCT
    deltas = jnp.abs(jnp.linspace(min_decay, max_decay, HYENA_WIDTH, dtype=F32))
    head = np.arange(ATTN_WIDTH) // HEAD_DIM
    ones_bd = jnp.asarray((head[:, None] == head[None, :]).astype(np.float32) / HEAD_DIM).astype(BF16)
    w3 = filt_w3[0].reshape(FILTER_HIDDEN, HYENA_ORDER, 2, HYENA_WIDTH).transpose(2, 0, 1, 3)
    w3 = w3.reshape(2, FILTER_HIDDEN, cw)
    w3_zero = jnp.zeros((FILTER_HIDDEN, cw), F32)
    twice = lambda a: jnp.tile(a, 2)[None, :]
    p = dict(
        attn_norm_w=attn_norm_w[0][None, :],
        w_in=w_in[0].astype(BF16),
        conv_w=hyena_conv_w[0],
        conv_b=hyena_conv_b[0][None, :],
        filt_w1=_block_diag2(jnp.pad(filt_w1[0], ((0, FILT_PAD - FILTER_EMB), (0, 0)))),
        filt_b1=twice(filt_b1[0]),
        filt_w2=_block_diag2(filt_w2[0]),
        filt_b2=twice(filt_b2[0]),
        filt_w3f=jnp.concatenate([w3[0], w3_zero], axis=0),
        filt_w3b=jnp.concatenate([w3_zero, w3[1]], axis=0),
        filt_freq=twice(filt_freq[0]),
        decay=deltas[None, :],
        hyena_bias=hyena_bias[0],
        q_norm_w=jnp.tile(q_norm_w[0], N_HEADS)[None, :],
        k_norm_w=jnp.tile(k_norm_w[0], N_KV_HEADS)[None, :],
        attn_sink=attn_sink[0],
        ones_bd=ones_bd,
        w_hy_out=w_hy_out[0].astype(BF16),
        w_at_out=w_at_out[0].astype(BF16),
        w_o=w_o[0].astype(BF16),
        ffn_norm_w=ffn_norm_w[0][None, :],
        w_gate=w_gate[0].astype(BF16),
        w_up=w_up[0].astype(BF16),
        w_down=w_down[0].astype(BF16),
    )
    return (_layer(x_prompt, p), _layer(x_sample, p))
```

```python
import functools
import math

import numpy as np
import jax
import jax.numpy as jnp
from jax import lax
from jax.experimental import pallas as pl
from jax.experimental.pallas import tpu as pltpu

F32 = jnp.float32
BF16 = jnp.bfloat16

D_MODEL = 1024
HYENA_WIDTH = 512
HYENA_ORDER = 2
FILTER_BANDS = 16
FILTER_EMB = 1 + 2 * FILTER_BANDS
FILTER_HIDDEN = 64
DECAY_FAST_PCT = 0.3
DECAY_SLOW_PCT = 1.5
DECAY_TARGET = 1e-2
DECAY_SHIFT = 0.05
N_HEADS = 8
N_KV_HEADS = 2
HEAD_DIM = 64
ATTN_WIDTH = N_HEADS * HEAD_DIM
KV_WIDTH = N_KV_HEADS * HEAD_DIM
WINDOW = 128
ROPE_THETA = 10000.0
FFN_HIDDEN = 2816
RMS_EPS = 1e-6
NEG_INF = -1e30

C_HY = 3 * HYENA_WIDTH
C_Q = C_HY + ATTN_WIDTH
C_K = C_Q + KV_WIDTH
C_V = C_K + KV_WIDTH
IN_WIDTH = C_V + 2 * D_MODEL

LANES = 128
SUBLANES = 8
ATTN_BLOCK = 128
ATTN_SUB = 2
VMEM_LIMIT = 56 * 1024 * 1024

TM_INPROJ = 512
TM_FINAL = 256
FFN_CHUNK = FFN_HIDDEN // 2
FILT_PAD = 64


def _cparams(*sem):
    return pltpu.CompilerParams(dimension_semantics=sem, vmem_limit_bytes=VMEM_LIMIT)


def _const_spec(shape):
    nd = len(shape)
    return pl.BlockSpec(shape, lambda *_: (0,) * nd, pipeline_mode=pl.Buffered(1))


def _dot(a, b):
    return jnp.dot(a, b, preferred_element_type=F32)


def _split(a):
    hi = a.astype(BF16)
    lo = (a - hi.astype(F32)).astype(BF16)
    return hi, lo


def _dot3(a, w):
    ah, al = _split(a)
    wh, wl = _split(w)
    return _dot(ah, wh) + _dot(al, wh) + _dot(ah, wl)


def _inproj_kernel(x_ref, xp_ref, xn_ref, nw_ref, w_ref, cw_ref, cb_ref, qw_ref, kw_ref,
                   cos_ref, sa_ref, sb_ref, ones_ref,
                   x1_ref, x2_ref, v_ref, q_ref, k4_ref, v4_ref, g_ref, pad_ref, *, tm, tiles_per_seq):
    pos = pl.program_id(0) % tiles_per_seq
    nw = nw_ref[...]

    def norm(xv):
        ms = jnp.mean(xv * xv, axis=-1, keepdims=True)
        return (xv * lax.rsqrt(ms + RMS_EPS) * nw).astype(BF16)

    h = norm(x_ref[...])
    hh = norm(jnp.concatenate([xp_ref[...], xn_ref[...]], axis=0))

    w_hy = w_ref[:, :C_HY]
    hy = _dot(h, w_hy)
    hyh = _dot(hh, w_hy)
    prev_ok = (pos > 0).astype(F32)
    next_ok = (pos < tiles_per_seq - 1).astype(F32)
    pad_ref[0:SUBLANES] = hyh[0:SUBLANES] * prev_ok
    pad_ref[SUBLANES:SUBLANES + tm] = hy
    pad_ref[SUBLANES + tm:2 * SUBLANES + tm] = hyh[SUBLANES:] * next_ok
    up = pad_ref[SUBLANES - 1:SUBLANES - 1 + tm]
    un = pad_ref[SUBLANES + 1:SUBLANES + 1 + tm]
    cw = cw_ref[...]
    uc = cw[0:1] * up + cw[1:2] * hy + cw[2:3] * un + cb_ref[...]
    x1_ref[...] = uc[:, :HYENA_WIDTH]
    x2_ref[...] = uc[:, HYENA_WIDTH:2 * HYENA_WIDTH]
    v_ref[...] = uc[:, 2 * HYENA_WIDTH:]

    def head_norm_rope(t, wrow, ones, reps):
        hi, lo = _split(t * t)
        ms = _dot(hi, ones) + _dot(lo, ones)
        tn = t * lax.rsqrt(ms + RMS_EPS) * wrow
        width = t.shape[1]
        cos = jnp.concatenate([cos_ref[...]] * reps, axis=1)
        sa = jnp.concatenate([sa_ref[...]] * reps, axis=1)
        sb = jnp.concatenate([sb_ref[...]] * reps, axis=1)
        half = HEAD_DIM // 2
        return tn * cos + pltpu.roll(tn, width - half, 1) * sa + pltpu.roll(tn, half, 1) * sb

    q = _dot(h, w_ref[:, C_HY:C_Q])
    q = head_norm_rope(q, qw_ref[...], ones_ref[...], ATTN_WIDTH // LANES) * (HEAD_DIM ** -0.5)
    blocks = []
    for p in range(N_HEADS // 2):
        qp = q[:, p * LANES:(p + 1) * LANES]
        blocks += [qp, pltpu.roll(qp, HEAD_DIM, 1)]
    q_ref[...] = jnp.concatenate(blocks, axis=1).astype(BF16)

    k = _dot(h, w_ref[:, C_Q:C_K])
    k = head_norm_rope(k, kw_ref[...], ones_ref[:KV_WIDTH, :KV_WIDTH], 1)
    v = _dot(h, w_ref[:, C_K:C_V])

    lo_half = lax.broadcasted_iota(jnp.int32, (tm, KV_WIDTH), 1) < HEAD_DIM

    def spread(t):
        zero = jnp.zeros_like(t)
        return jnp.concatenate([jnp.where(lo_half, t, zero), jnp.where(lo_half, pltpu.roll(t, HEAD_DIM, 1), zero)],
                               axis=1).astype(BF16)

    k4_ref[...] = spread(k)
    v4_ref[...] = spread(v)
    g_ref[...] = jax.nn.sigmoid(_dot(h, w_ref[:, C_V:]))


def _inproj(xf, seq_len, nw, w_in_bf, cw, cb, qw, kw, cos, sa, sb, ones_bd):
    n = xf.shape[0]
    tm = TM_INPROJ
    tiles_per_seq = seq_len // tm
    nblk8 = n // SUBLANES
    r8 = tm // SUBLANES
    kern = functools.partial(_inproj_kernel, tm=tm, tiles_per_seq=tiles_per_seq)
    row = lambda i: (i, 0)
    return pl.pallas_call(
        kern,
        grid=(n // tm,),
        in_specs=[
            pl.BlockSpec((tm, D_MODEL), row),
            pl.BlockSpec((SUBLANES, D_MODEL), lambda i: (jnp.maximum(i * r8 - 1, 0), 0)),
            pl.BlockSpec((SUBLANES, D_MODEL), lambda i: (jnp.minimum((i + 1) * r8, nblk8 - 1), 0)),
            _const_spec((1, D_MODEL)),
            _const_spec((D_MODEL, IN_WIDTH)),
            _const_spec((3, C_HY)),
            _const_spec((1, C_HY)),
            _const_spec((1, ATTN_WIDTH)),
            _const_spec((1, KV_WIDTH)),
            pl.BlockSpec((tm, LANES), lambda i: (i % tiles_per_seq, 0)),
            pl.BlockSpec((tm, LANES), lambda i: (i % tiles_per_seq, 0)),
            pl.BlockSpec((tm, LANES), lambda i: (i % tiles_per_seq, 0)),
            _const_spec((ATTN_WIDTH, ATTN_WIDTH)),
        ],
        out_specs=[
            pl.BlockSpec((tm, HYENA_WIDTH), row),
            pl.BlockSpec((tm, HYENA_WIDTH), row),
            pl.BlockSpec((tm, HYENA_WIDTH), row),
            pl.BlockSpec((tm, 2 * ATTN_WIDTH), row),
            pl.BlockSpec((tm, 2 * LANES), row),
            pl.BlockSpec((tm, 2 * LANES), row),
            pl.BlockSpec((tm, 2 * D_MODEL), row),
        ],
        out_shape=[
            jax.ShapeDtypeStruct((n, HYENA_WIDTH), F32),
            jax.ShapeDtypeStruct((n, HYENA_WIDTH), F32),
            jax.ShapeDtypeStruct((n, HYENA_WIDTH), F32),
            jax.ShapeDtypeStruct((n, 2 * ATTN_WIDTH), BF16),
            jax.ShapeDtypeStruct((n, 2 * LANES), BF16),
            jax.ShapeDtypeStruct((n, 2 * LANES), BF16),
            jax.ShapeDtypeStruct((n, 2 * D_MODEL), F32),
        ],
        scratch_shapes=[pltpu.VMEM((tm + 2 * SUBLANES, C_HY), F32)],
        compiler_params=_cparams("parallel"),
        name="inproj",
    )(xf, xf, xf, nw, w_in_bf, cw, cb, qw, kw, cos, sa, sb, ones_bd)


def _rows_to_tiles(r, tj):
    return r.reshape(r.shape[0], tj, r.shape[1] // tj)


def _tiles_to_rows(t):
    return t.reshape(t.shape[0], t.shape[1] * t.shape[2])


def _twiddle_tiles(r, twr_ref, twi_ref, n1h, tj):
    r3 = _rows_to_tiles(r, tj)
    ar, ai = r3[:n1h], r3[n1h:]
    reps = r3.shape[-1] // LANES
    tr = jnp.concatenate([twr_ref[...]] * reps, axis=-1)
    ti = jnp.concatenate([twi_ref[...]] * reps, axis=-1)
    return (ar * tr - ai * ti).astype(BF16), (ar * ti + ai * tr).astype(BF16)


def _filt_kernel(z_ref, w1_ref, b1_ref, w2_ref, b2_ref, w3f_ref, w3b_ref, fr_ref, dl_ref, g_ref, twr_ref, twi_ref,
                 are_ref, aim_ref, s_ref, *, n1h, tj):
    i = pl.program_id(0)
    z = z_ref[...]
    fr = fr_ref[...]
    h = jnp.sin(fr * (_dot3(z, w1_ref[...]) + b1_ref[...]))
    h = jnp.sin(fr * (_dot3(h, w2_ref[...]) + b2_ref[...]))
    hh, hl = _split(h)
    tf = z[:, 0:1]
    tb = z[:, FILT_PAD:FILT_PAD + 1]
    sgn = z[:, FILT_PAD + FILTER_EMB:FILT_PAD + FILTER_EMB + 1]
    gm = g_ref[...].astype(BF16)
    cb = HYENA_WIDTH
    parts = []
    for o in range(HYENA_ORDER):
        sl = slice(o * cb, (o + 1) * cb)
        dl = dl_ref[...]

        def taps(w_ref, t):
            wh, wl = _split(w_ref[:, sl])
            return (_dot(hh, wh) + _dot(hl, wh) + _dot(hh, wl)) * (jnp.exp(-t * dl) + DECAY_SHIFT)

        kf = taps(w3f_ref, tf)
        kb = taps(w3b_ref, tb) * sgn
        parts.append(jnp.sum(jnp.abs(kf), axis=0, keepdims=True) + jnp.sum(jnp.abs(kb), axis=0, keepdims=True))
        cols = [jnp.concatenate([kf[j * n1h:(j + 1) * n1h], kb[j * n1h:(j + 1) * n1h]], axis=0).astype(BF16)
                for j in range(tj)]
        re, im = _twiddle_tiles(_dot(gm, jnp.concatenate(cols, axis=1)), twr_ref, twi_ref, n1h, tj)
        are_ref[:, :, sl] = re
        aim_ref[:, :, sl] = im
    part = jnp.concatenate(parts, axis=1)

    @pl.when(i == 0)
    def _():
        s_ref[...] = part

    @pl.when(i > 0)
    def _():
        s_ref[...] += part


def _filter_stage_a(zf, w1bd, b1, w2bd, b2, w3f, w3b, fr, dl, g_full, twr, twi, n1h, tj):
    cw = HYENA_ORDER * HYENA_WIDTH
    hid2 = 2 * FILTER_HIDDEN
    a_shape = jax.ShapeDtypeStruct((n1h, LANES, cw), BF16)
    a_spec = pl.BlockSpec((n1h, tj, cw), lambda i: (0, i, 0))
    t_spec = pl.BlockSpec((n1h, tj, LANES), lambda i: (0, i, 0))
    return pl.pallas_call(
        functools.partial(_filt_kernel, n1h=n1h, tj=tj),
        grid=(LANES // tj,),
        in_specs=[
            pl.BlockSpec((tj * n1h, 2 * FILT_PAD), lambda i: (i, 0)),
            _const_spec((2 * FILT_PAD, hid2)),
            _const_spec((1, hid2)),
            _const_spec((hid2, hid2)),
            _const_spec((1, hid2)),
            _const_spec((hid2, cw)),
            _const_spec((hid2, cw)),
            _const_spec((1, hid2)),
            _const_spec((1, HYENA_WIDTH)),
            _const_spec((2 * n1h, 2 * n1h)),
            t_spec, t_spec,
        ],
        out_specs=[a_spec, a_spec, pl.BlockSpec((1, cw), lambda i: (0, 0))],
        out_shape=[a_shape, a_shape, jax.ShapeDtypeStruct((1, cw), F32)],
        compiler_params=_cparams("arbitrary"),
        name="filter_taps_stage_a",
    )(zf, w1bd, b1, w2bd, b2, w3f, w3b, fr, dl, g_full, twr, twi)


def _midf_kernel(f_ref, s_ref, are_ref, aim_ref, h_ref, *, kb):
    fm = f_ref[...].astype(BF16)
    inv = 1.0 / s_ref[...]
    for k in range(kb):
        h_ref[k] = _dot(fm, jnp.concatenate([are_ref[k], aim_ref[k]], axis=0)) * inv


def _filter_spectrum(afre, afim, fblk, asum, kb):
    n1h, _, ctot = afre.shape
    cb = HYENA_WIDTH
    a_spec = pl.BlockSpec((kb, LANES, cb), lambda ki, ci: (ki, 0, ci))
    return pl.pallas_call(
        functools.partial(_midf_kernel, kb=kb),
        grid=(n1h // kb, ctot // cb),
        in_specs=[_const_spec((2 * LANES, 2 * LANES)), pl.BlockSpec((1, cb), lambda ki, ci: (0, ci)), a_spec, a_spec],
        out_specs=pl.BlockSpec((kb, 2 * LANES, cb), lambda ki, ci: (ki, 0, ci)),
        out_shape=jax.ShapeDtypeStruct((n1h, 2 * LANES, ctot), F32),
        compiler_params=_cparams("parallel", "parallel"),
        name="filter_spectrum",
    )(fblk, asum, afre, afim)


def _fwda_kernel(g_ref, twr_ref, twi_ref, u_ref, are_ref, aim_ref, *, n1h, tj):
    u = _tiles_to_rows(u_ref[...].astype(BF16))
    re, im = _twiddle_tiles(_dot(g_ref[...].astype(BF16), u), twr_ref, twi_ref, n1h, tj)
    are_ref[...] = re
    aim_ref[...] = im


def _fwd_a(u4, g_half, twr, twi, tj):
    b, n1h, _, cb = u4.shape
    blk = pl.BlockSpec((None, n1h, tj, cb), lambda bi, ji: (bi, 0, ji, 0))
    t_spec = pl.BlockSpec((n1h, tj, LANES), lambda bi, ji: (0, ji, 0))
    a_shape = jax.ShapeDtypeStruct(u4.shape, BF16)
    return pl.pallas_call(
        functools.partial(_fwda_kernel, n1h=n1h, tj=tj),
        grid=(b, LANES // tj),
        in_specs=[_const_spec((2 * n1h, n1h)), t_spec, t_spec, blk],
        out_specs=[blk, blk],
        out_shape=[a_shape, a_shape],
        compiler_params=_cparams("parallel", "parallel"),
        name="dft_stage_a",
    )(g_half, twr, twi, u4)


def _mid_kernel(f_ref, fi_ref, twr_ref, twi_ref, h_ref, are_ref, aim_ref, bre_ref, bim_ref, *, kb):
    fm = f_ref[...].astype(BF16)
    fim = fi_ref[...].astype(BF16)
    reps = are_ref.shape[-1] // LANES
    for k in range(kb):
        x = _dot(fm, jnp.concatenate([are_ref[k], aim_ref[k]], axis=0))
        xr, xi = x[:LANES], x[LANES:]
        hr, hi = h_ref[k, :LANES], h_ref[k, LANES:]
        y = jnp.concatenate([xr * hr - xi * hi, xr * hi + xi * hr], axis=0)
        bc = _dot(fim, y.astype(BF16))
        br, bi = bc[:LANES], bc[LANES:]
        tr = jnp.concatenate([twr_ref[k]] * reps, axis=1)
        ti = jnp.concatenate([twi_ref[k]] * reps, axis=1)
        bre_ref[k] = (br * tr + bi * ti).astype(BF16)
        bim_ref[k] = (bi * tr - br * ti).astype(BF16)


def _mid(are, aim, fblk, fblk_inv, twr, twi, hspec, order, kb):
    b, n1h, _, cb = are.shape
    a_spec = pl.BlockSpec((None, kb, LANES, cb), lambda ki, bi: (bi, ki, 0, 0))
    t_spec = pl.BlockSpec((kb, LANES, LANES), lambda ki, bi: (ki, 0, 0))
    shape = jax.ShapeDtypeStruct(are.shape, BF16)
    return pl.pallas_call(
        functools.partial(_mid_kernel, kb=kb),
        grid=(n1h // kb, b),
        in_specs=[
            _const_spec((2 * LANES, 2 * LANES)),
            _const_spec((2 * LANES, 2 * LANES)),
            t_spec, t_spec,
            pl.BlockSpec((kb, 2 * LANES, cb), lambda ki, bi: (ki, 0, order)),
            a_spec, a_spec,
        ],
        out_specs=[a_spec, a_spec],
        out_shape=[shape, shape],
        compiler_params=_cparams("parallel", "parallel"),
        name="dft_stage_c",
    )(fblk, fblk_inv, twr, twi, hspec, are, aim)


def _inva_kernel(gi_ref, bre_ref, bim_ref, xg_ref, v_ref, sb_ref, *rest, n1h, tj, fuse_next):
    rhs = jnp.concatenate([_tiles_to_rows(bre_ref[...]), _tiles_to_rows(bim_ref[...])], axis=0)
    y = _rows_to_tiles(_dot(gi_ref[...].astype(BF16), rhs), tj)
    z = xg_ref[...] * (y + sb_ref[...] * v_ref[...])
    if fuse_next:
        g_ref, twr_ref, twi_ref, z_ref, are_ref, aim_ref = rest
        z_ref[...] = z
        r = _dot(g_ref[...].astype(BF16), _tiles_to_rows(z.astype(BF16)))
        re, im = _twiddle_tiles(r, twr_ref, twi_ref, n1h, tj)
        are_ref[...] = re
        aim_ref[...] = im
    else:
        (z_ref,) = rest
        z_ref[...] = z


def _inv_a(bre, bim, g_inv, xg, v, sb_row, tj, nxt=None):
    b, n1h, _, cb = bre.shape
    blk = pl.BlockSpec((None, n1h, tj, cb), lambda bi, ji: (bi, 0, ji, 0))
    in_specs = [_const_spec((n1h, 2 * n1h)), blk, blk, blk, blk, _const_spec((1, cb))]
    args = [g_inv, bre, bim, xg, v, sb_row]
    out_specs = [blk]
    out_shape = [jax.ShapeDtypeStruct(bre.shape, F32)]
    if nxt is not None:
        t_spec = pl.BlockSpec((n1h, tj, LANES), lambda bi, ji: (0, ji, 0))
        in_specs += [_const_spec((2 * n1h, n1h)), t_spec, t_spec]
        args += list(nxt)
        out_specs += [blk, blk]
        out_shape += [jax.ShapeDtypeStruct(bre.shape, BF16)] * 2
    return pl.pallas_call(
        functools.partial(_inva_kernel, n1h=n1h, tj=tj, fuse_next=nxt is not None),
        grid=(b, LANES // tj),
        in_specs=in_specs,
        out_specs=out_specs,
        out_shape=out_shape,
        compiler_params=_cparams("parallel", "parallel"),
        name="idft_stage_a_gate",
    )(*args)


def _attn_kernel(sink_ref, q_ref, kp_ref, kc_ref, kn_ref, vp_ref, vc_ref, vn_ref, o_ref, *, nblk, sub):
    i = pl.program_id(1)
    blk = ATTN_BLOCK
    grp = N_HEADS // N_KV_HEADS
    rows = grp * blk
    qi = lax.broadcasted_iota(jnp.int32, (rows, 3 * blk), 0) % blk
    si = lax.broadcasted_iota(jnp.int32, (rows, 3 * blk), 1)
    mid = (si >= blk) & (si < 2 * blk)
    tri_prev = (si < blk) & (si >= qi)
    tri_next = (si >= 2 * blk) & (si - 2 * blk <= qi)
    hrow = lax.broadcasted_iota(jnp.int32, (rows, 1), 0) // blk
    kcat = jnp.concatenate([kp_ref[...], kc_ref[...], kn_ref[...]], axis=0)
    vcat = jnp.concatenate([vp_ref[...], vc_ref[...], vn_ref[...]], axis=0)
    for s in range(sub):
        j = sub * i + s
        valid = mid | (tri_prev & (j > 0)) | (tri_next & (j < nblk - 1))
        outs = []
        for g in range(N_KV_HEADS):
            lhs = jnp.concatenate([q_ref[s * blk:(s + 1) * blk, (grp * g + t) * LANES:(grp * g + t + 1) * LANES]
                                   for t in range(grp)], axis=0)
            sc = lax.dot_general(lhs, kcat[s * blk:(s + 3) * blk, g * LANES:(g + 1) * LANES],
                                 (((1,), (1,)), ((), ())), preferred_element_type=F32)
            sc = jnp.where(valid, sc, NEG_INF)
            sk = jnp.full((rows, 1), sink_ref[grp * g], F32)
            for t in range(1, grp):
                sk = jnp.where(hrow == t, sink_ref[grp * g + t], sk)
            m = jnp.maximum(jnp.max(sc, axis=1, keepdims=True), sk)
            e = jnp.exp(sc - m)
            den = jnp.sum(e, axis=1, keepdims=True) + jnp.exp(sk - m)
            r = _dot(e.astype(BF16), vcat[s * blk:(s + 3) * blk, g * LANES:(g + 1) * LANES]) / den
            for t in range(0, grp, 2):
                outs.append(r[t * blk:(t + 1) * blk] + pltpu.roll(r[(t + 1) * blk:(t + 2) * blk], HEAD_DIM, 1))
        o_ref[s * blk:(s + 1) * blk, :] = jnp.concatenate(outs, axis=1).astype(BF16)


def _attention(q3, k3, v3, sink):
    b, seq_len, qw = q3.shape
    blk = ATTN_BLOCK
    sub = ATTN_SUB
    nblk = seq_len // blk
    wide = k3.shape[-1]
    prev = lambda bi, i: (bi, jnp.maximum(sub * i - 1, 0), 0)
    cur = lambda bi, i: (bi, i, 0)
    nxt = lambda bi, i: (bi, jnp.minimum(sub * (i + 1), nblk - 1), 0)
    edge = lambda f: pl.BlockSpec((None, blk, wide), f)
    body = pl.BlockSpec((None, sub * blk, wide), cur)
    return pl.pallas_call(
        functools.partial(_attn_kernel, nblk=nblk, sub=sub),
        grid=(b, nblk // sub),
        in_specs=[pl.BlockSpec(memory_space=pltpu.SMEM),
                  pl.BlockSpec((None, sub * blk, qw), cur),
                  edge(prev), body, edge(nxt), edge(prev), body, edge(nxt)],
        out_specs=pl.BlockSpec((None, sub * blk, ATTN_WIDTH), cur),
        out_shape=jax.ShapeDtypeStruct((b, seq_len, ATTN_WIDTH), BF16),
        compiler_params=_cparams("parallel", "parallel"),
        name="banded_attention",
    )(sink, q3, k3, k3, k3, v3, v3, v3)


def _final_kernel(x_ref, yh_ref, ya_ref, g_ref, why_ref, wat_ref, wo_ref, nw_ref, wg_ref, wu_ref, wd_ref,
                  o_ref):
    a = _dot(yh_ref[...].astype(BF16), why_ref[...])
    b = _dot(ya_ref[...], wat_ref[...])
    merged = g_ref[:, :D_MODEL] * a + g_ref[:, D_MODEL:] * b
    x1 = x_ref[...] + _dot(merged.astype(BF16), wo_ref[...])
    ms = jnp.mean(x1 * x1, axis=-1, keepdims=True)
    f = (x1 * lax.rsqrt(ms + RMS_EPS) * nw_ref[...]).astype(BF16)
    acc = x1
    for c in range(FFN_HIDDEN // FFN_CHUNK):
        lo, hi = c * FFN_CHUNK, (c + 1) * FFN_CHUNK
        gt = _dot(f, wg_ref[:, lo:hi])
        upv = _dot(f, wu_ref[:, lo:hi])
        hid = gt * jax.nn.sigmoid(gt) * upv
        acc = acc + _dot(hid.astype(BF16), wd_ref[lo:hi, :])
    o_ref[...] = acc


def _final(xf, yh, ya, g, why, wat, wo, nw, wg, wu, wd):
    n = xf.shape[0]
    tm = TM_FINAL
    row = lambda i: (i, 0)
    return pl.pallas_call(
        _final_kernel,
        grid=(n // tm,),
        in_specs=[
            pl.BlockSpec((tm, D_MODEL), row),
            pl.BlockSpec((tm, HYENA_WIDTH), row),
            pl.BlockSpec((tm, ATTN_WIDTH), row),
            pl.BlockSpec((tm, 2 * D_MODEL), row),
            _const_spec((HYENA_WIDTH, D_MODEL)),
            _const_spec((ATTN_WIDTH, D_MODEL)),
            _const_spec((D_MODEL, D_MODEL)),
            _const_spec((1, D_MODEL)),
            _const_spec((D_MODEL, FFN_HIDDEN)),
            _const_spec((D_MODEL, FFN_HIDDEN)),
            _const_spec((FFN_HIDDEN, D_MODEL)),
        ],
        out_specs=pl.BlockSpec((tm, D_MODEL), row),
        out_shape=jax.ShapeDtypeStruct((n, D_MODEL), F32),
        compiler_params=_cparams("parallel"),
        name="merge_ffn",
    )(xf, yh, ya, g, why, wat, wo, nw, wg, wu, wd)


@functools.lru_cache(maxsize=None)
def _dft_constants(n1h):
    n1_len = 2 * n1h
    m = n1_len * LANES
    k1 = np.arange(n1h, dtype=np.float64)[:, None] + 0.5
    n1 = np.arange(n1_len, dtype=np.float64)[None, :]
    gc = np.exp(-2j * np.pi * k1 * n1 / n1_len)
    g_full = np.concatenate([gc.real, gc.imag], axis=0)
    g_inv = (2.0 / m) * np.concatenate([gc.real[:, :n1h].T, gc.imag[:, :n1h].T], axis=1)
    n2 = np.arange(LANES, dtype=np.float64)
    tw = np.exp(-2j * np.pi * k1 * n2[None, :] / m)[:, :, None] * np.ones((1, 1, LANES))
    fc = np.exp(-2j * np.pi * np.outer(n2, n2) / LANES)
    fblk = np.block([[fc.real, -fc.imag], [fc.imag, fc.real]])
    fblk_inv = np.block([[fc.real, fc.imag], [-fc.imag, fc.real]])
    f32 = lambda a: np.ascontiguousarray(a, dtype=np.float32)
    return dict(g_full=f32(g_full), g_half=f32(g_full[:, :n1h]), g_inv=f32(g_inv), twr=f32(tw.real),
                twi=f32(tw.imag), fblk=f32(fblk), fblk_inv=f32(fblk_inv))


@functools.lru_cache(maxsize=None)
def _filter_features(seq_len):
    n1h = seq_len // LANES
    n2 = np.arange(LANES)[:, None]
    n1 = np.arange(n1h)[None, :]
    slot_f = (LANES * n1 + n2).reshape(-1)
    slot_b = slot_f + seq_len
    neg_lag = slot_b > seq_len
    pos_b = np.where(neg_lag, 2 * seq_len - slot_b, 0)
    bands = np.linspace(1e-4, FILTER_BANDS - 1, FILTER_BANDS)

    def feats(pos, sgn):
        pos = pos.astype(np.float64)
        ang = (2.0 * math.pi / seq_len) * pos[:, None] * bands[None, :]
        pad = np.zeros((pos.shape[0], FILT_PAD - FILTER_EMB - 1))
        return np.concatenate([(pos / (seq_len - 1))[:, None], np.cos(ang), -np.sin(ang), sgn[:, None], pad], axis=-1)

    z = np.concatenate([feats(slot_f, np.ones(slot_f.shape)), feats(pos_b, np.where(neg_lag, -1.0, 0.0))], axis=-1)
    return np.ascontiguousarray(z, dtype=np.float32)


@functools.lru_cache(maxsize=None)
def _rope_tables(seq_len):
    inv = ROPE_THETA ** (-np.arange(0, HEAD_DIM, 2, dtype=np.float64) / HEAD_DIM)
    ang = np.arange(seq_len, dtype=np.float64)[:, None] * inv[None, :]
    cos, sin = np.cos(ang), np.sin(ang)
    zero = np.zeros_like(sin)
    reps = LANES // HEAD_DIM
    f32 = lambda a: np.ascontiguousarray(np.tile(a, (1, reps)), dtype=np.float32)
    return f32(np.concatenate([cos, cos], axis=1)), f32(np.concatenate([-sin, zero], axis=1)), \
        f32(np.concatenate([zero, sin], axis=1))


def _layer(x, p):
    b, seq_len, _ = x.shape
    n = b * seq_len
    n1h = seq_len // LANES
    xf = x.reshape(n, D_MODEL)
    c = {k: jnp.asarray(v) for k, v in _dft_constants(n1h).items()}
    tj = SUBLANES if n1h >= LANES else 4 * SUBLANES
    kb = SUBLANES
    r4 = lambda a: a.reshape(b, n1h, LANES, HYENA_WIDTH)

    cos, sa, sb = (jnp.asarray(t) for t in _rope_tables(seq_len))
    x1, x2, v, q, k4, v4, g = _inproj(xf, seq_len, p["attn_norm_w"], p["w_in"], p["conv_w"], p["conv_b"],
                                      p["q_norm_w"], p["k_norm_w"], cos, sa, sb, p["ones_bd"])

    afre, afim, asum = _filter_stage_a(jnp.asarray(_filter_features(seq_len)), p["filt_w1"], p["filt_b1"],
                                       p["filt_w2"], p["filt_b2"], p["filt_w3f"], p["filt_w3b"], p["filt_freq"],
                                       p["decay"], c["g_full"], c["twr"], c["twi"], n1h, tj)
    hspec = _filter_spectrum(afre, afim, c["fblk"], asum, kb)

    are, aim = _fwd_a(r4(v), c["g_half"], c["twr"], c["twi"], tj)
    bre, bim = _mid(are, aim, c["fblk"], c["fblk_inv"], c["twr"], c["twi"], hspec, 0, kb)
    z1, are, aim = _inv_a(bre, bim, c["g_inv"], r4(x1), r4(v), p["hyena_bias"][0:1], tj,
                          nxt=(c["g_half"], c["twr"], c["twi"]))
    bre, bim = _mid(are, aim, c["fblk"], c["fblk_inv"], c["twr"], c["twi"], hspec, 1, kb)
    (yh,) = _inv_a(bre, bim, c["g_inv"], r4(x2), z1, p["hyena_bias"][1:2], tj)

    ya = _attention(q.reshape(b, seq_len, 2 * ATTN_WIDTH), k4.reshape(b, seq_len, 2 * LANES),
                    v4.reshape(b, seq_len, 2 * LANES), p["attn_sink"])

    out = _final(xf, yh.reshape(n, HYENA_WIDTH), ya.reshape(n, ATTN_WIDTH), g, p["w_hy_out"], p["w_at_out"],
                 p["w_o"], p["ffn_norm_w"], p["w_gate"], p["w_up"], p["w_down"])
    return out.reshape(b, seq_len, D_MODEL)


def _block_diag2(w):
    z = jnp.zeros_like(w)
    return jnp.concatenate([jnp.concatenate([w, z], axis=1), jnp.concatenate([z, w], axis=1)], axis=0)


def kernel(x_prompt, x_sample, attn_norm_w, w_in, hyena_conv_w, hyena_conv_b, filt_w1, filt_b1, filt_w2, filt_b2,
           filt_w3, filt_freq, hyena_bias, q_norm_w, k_norm_w, attn_sink, w_hy_out, w_at_out, w_o, ffn_norm_w,
           w_gate, w_up, w_down):
    cw = HYENA_ORDER * HYENA_WIDTH
    max_decay = math.log(DECAY_TARGET) / DECAY_FAST_PCT
    min_decay = math.log(DECAY_TARGET) / DECAY_SLOW_PCT
    deltas = jnp.abs(jnp.linspace(min_decay, max_decay, HYENA_WIDTH, dtype=F32))
    head = np.arange(ATTN_WIDTH) // HEAD_DIM
    ones_bd = jnp.asarray((head[:, None] == head[None, :]).astype(np.float32) / HEAD_DIM).astype(BF16)
    w3 = filt_w3[0].reshape(FILTER_HIDDEN, HYENA_ORDER, 2, HYENA_WIDTH).transpose(2, 0, 1, 3)
    w3 = w3.reshape(2, FILTER_HIDDEN, cw)
    w3_zero = jnp.zeros((FILTER_HIDDEN, cw), F32)
    twice = lambda a: jnp.tile(a, 2)[None, :]
    p = dict(
        attn_norm_w=attn_norm_w[0][None, :],
        w_in=w_in[0].astype(BF16),
        conv_w=hyena_conv_w[0],
        conv_b=hyena_conv_b[0][None, :],
        filt_w1=_block_diag2(jnp.pad(filt_w1[0], ((0, FILT_PAD - FILTER_EMB), (0, 0)))),
        filt_b1=twice(filt_b1[0]),
        filt_w2=_block_diag2(filt_w2[0]),
        filt_b2=twice(filt_b2[0]),
        filt_w3f=jnp.concatenate([w3[0], w3_zero], axis=0),
        filt_w3b=jnp.concatenate([w3_zero, w3[1]], axis=0),
        filt_freq=twice(filt_freq[0]),
        decay=deltas[None, :],
        hyena_bias=hyena_bias[0],
        q_norm_w=jnp.tile(q_norm_w[0], N_HEADS)[None, :],
        k_norm_w=jnp.tile(k_norm_w[0], N_KV_HEADS)[None, :],
        attn_sink=attn_sink[0],
        ones_bd=ones_bd,
        w_hy_out=w_hy_out[0].astype(BF16),
        w_at_out=w_at_out[0].astype(BF16),
        w_o=w_o[0].astype(BF16),
        ffn_norm_w=ffn_norm_w[0][None, :],
        w_gate=w_gate[0].astype(BF16),
        w_up=w_up[0].astype(BF16),
        w_down=w_down[0].astype(BF16),
    )
    return (_layer(x_prompt, p), _layer(x_sample, p))
```

```python
import functools
import math

import numpy as np
import jax
import jax.numpy as jnp
from jax import lax
from jax.experimental import pallas as pl
from jax.experimental.pallas import tpu as pltpu

F32 = jnp.float32
BF16 = jnp.bfloat16

D_MODEL = 1024
HYENA_WIDTH = 512
HYENA_ORDER = 2
FILTER_BANDS = 16
FILTER_EMB = 1 + 2 * FILTER_BANDS
FILTER_HIDDEN = 64
DECAY_FAST_PCT = 0.3
DECAY_SLOW_PCT = 1.5
DECAY_TARGET = 1e-2
DECAY_SHIFT = 0.05
N_HEADS = 8
N_KV_HEADS = 2
HEAD_DIM = 64
ATTN_WIDTH = N_HEADS * HEAD_DIM
KV_WIDTH = N_KV_HEADS * HEAD_DIM
WINDOW = 128
ROPE_THETA = 10000.0
FFN_HIDDEN = 2816
RMS_EPS = 1e-6
NEG_INF = -1e30

C_HY = 3 * HYENA_WIDTH
C_Q = C_HY + ATTN_WIDTH
C_K = C_Q + KV_WIDTH
C_V = C_K + KV_WIDTH
IN_WIDTH = C_V + 2 * D_MODEL

LANES = 128
SUBLANES = 8
ATTN_BLOCK = 128
ATTN_SUB = 2
VMEM_LIMIT = 56 * 1024 * 1024

TM_INPROJ = 512
TM_FINAL = 256
FFN_CHUNK = FFN_HIDDEN // 2
_CH_SPLITS = (slice(0, HYENA_WIDTH // 2), slice(HYENA_WIDTH // 2, HYENA_WIDTH))
FILT_PAD = 64


def _cparams(*sem):
    return pltpu.CompilerParams(dimension_semantics=sem, vmem_limit_bytes=VMEM_LIMIT)


def _const_spec(shape):
    nd = len(shape)
    return pl.BlockSpec(shape, lambda *_: (0,) * nd, pipeline_mode=pl.Buffered(1))


def _dot(a, b):
    return jnp.dot(a, b, preferred_element_type=F32)


def _sigmoid(x):
    return 0.5 * jnp.tanh(0.5 * x) + 0.5


def _split(a):
    hi = a.astype(BF16)
    lo = (a - hi.astype(F32)).astype(BF16)
    return hi, lo


def _dot3(a, w):
    ah, al = _split(a)
    wh, wl = _split(w)
    return _dot(ah, wh) + _dot(al, wh) + _dot(ah, wl)


def _inproj_kernel(x_ref, xp_ref, xn_ref, nw_ref, w_ref, cw_ref, cb_ref, qw_ref, kw_ref,
                   cos_ref, sa_ref, sb_ref, ones_ref,
                   x1_ref, x2_ref, v_ref, q_ref, k4_ref, v4_ref, g_ref, pad_ref, *, tm, tiles_per_seq):
    pos = pl.program_id(0) % tiles_per_seq
    nw = nw_ref[...]

    def norm(xv):
        ms = jnp.mean(xv * xv, axis=-1, keepdims=True)
        return (xv * lax.rsqrt(ms + RMS_EPS) * nw).astype(BF16)

    h = norm(x_ref[...])
    hh = norm(jnp.concatenate([xp_ref[...], xn_ref[...]], axis=0))

    w_hy = w_ref[:, :C_HY]
    hy = _dot(h, w_hy)
    hyh = _dot(hh, w_hy)
    prev_ok = (pos > 0).astype(F32)
    next_ok = (pos < tiles_per_seq - 1).astype(F32)
    pad_ref[0:SUBLANES] = hyh[0:SUBLANES] * prev_ok
    pad_ref[SUBLANES:SUBLANES + tm] = hy
    pad_ref[SUBLANES + tm:2 * SUBLANES + tm] = hyh[SUBLANES:] * next_ok
    up = pad_ref[SUBLANES - 1:SUBLANES - 1 + tm]
    un = pad_ref[SUBLANES + 1:SUBLANES + 1 + tm]
    cw = cw_ref[...]
    uc = cw[0:1] * up + cw[1:2] * hy + cw[2:3] * un + cb_ref[...]
    x1_ref[...] = uc[:, :HYENA_WIDTH]
    x2_ref[...] = uc[:, HYENA_WIDTH:2 * HYENA_WIDTH]
    v_ref[...] = uc[:, 2 * HYENA_WIDTH:]

    def head_norm_rope(t, wrow, ones, reps):
        ms = _dot((t * t).astype(BF16), ones)
        tn = t * lax.rsqrt(ms + RMS_EPS) * wrow
        width = t.shape[1]
        cos = jnp.concatenate([cos_ref[...]] * reps, axis=1)
        sa = jnp.concatenate([sa_ref[...]] * reps, axis=1)
        sb = jnp.concatenate([sb_ref[...]] * reps, axis=1)
        half = HEAD_DIM // 2
        return tn * cos + pltpu.roll(tn, width - half, 1) * sa + pltpu.roll(tn, half, 1) * sb

    q = _dot(h, w_ref[:, C_HY:C_Q])
    q = head_norm_rope(q, qw_ref[...], ones_ref[...], ATTN_WIDTH // LANES) * (HEAD_DIM ** -0.5)
    blocks = []
    for p in range(N_HEADS // 2):
        qp = q[:, p * LANES:(p + 1) * LANES]
        blocks += [qp, pltpu.roll(qp, HEAD_DIM, 1)]
    q_ref[...] = jnp.concatenate(blocks, axis=1).astype(BF16)

    k = _dot(h, w_ref[:, C_Q:C_K])
    k = head_norm_rope(k, kw_ref[...], ones_ref[:KV_WIDTH, :KV_WIDTH], 1)
    v = _dot(h, w_ref[:, C_K:C_V])

    lo_half = lax.broadcasted_iota(jnp.int32, (tm, KV_WIDTH), 1) < HEAD_DIM

    def spread(t):
        zero = jnp.zeros_like(t)
        return jnp.concatenate([jnp.where(lo_half, t, zero), jnp.where(lo_half, pltpu.roll(t, HEAD_DIM, 1), zero)],
                               axis=1).astype(BF16)

    k4_ref[...] = spread(k)
    v4_ref[...] = spread(v)
    g_ref[...] = _sigmoid(_dot(h, w_ref[:, C_V:]))


def _inproj(xf, seq_len, nw, w_in_bf, cw, cb, qw, kw, cos, sa, sb, ones_bd):
    n = xf.shape[0]
    tm = TM_INPROJ
    tiles_per_seq = seq_len // tm
    nblk8 = n // SUBLANES
    r8 = tm // SUBLANES
    kern = functools.partial(_inproj_kernel, tm=tm, tiles_per_seq=tiles_per_seq)
    row = lambda i: (i, 0)
    return pl.pallas_call(
        kern,
        grid=(n // tm,),
        in_specs=[
            pl.BlockSpec((tm, D_MODEL), row),
            pl.BlockSpec((SUBLANES, D_MODEL), lambda i: (jnp.maximum(i * r8 - 1, 0), 0)),
            pl.BlockSpec((SUBLANES, D_MODEL), lambda i: (jnp.minimum((i + 1) * r8, nblk8 - 1), 0)),
            _const_spec((1, D_MODEL)),
            _const_spec((D_MODEL, IN_WIDTH)),
            _const_spec((3, C_HY)),
            _const_spec((1, C_HY)),
            _const_spec((1, ATTN_WIDTH)),
            _const_spec((1, KV_WIDTH)),
            pl.BlockSpec((tm, LANES), lambda i: (i % tiles_per_seq, 0)),
            pl.BlockSpec((tm, LANES), lambda i: (i % tiles_per_seq, 0)),
            pl.BlockSpec((tm, LANES), lambda i: (i % tiles_per_seq, 0)),
            _const_spec((ATTN_WIDTH, ATTN_WIDTH)),
        ],
        out_specs=[
            pl.BlockSpec((tm, HYENA_WIDTH), row),
            pl.BlockSpec((tm, HYENA_WIDTH), row),
            pl.BlockSpec((tm, HYENA_WIDTH), row),
            pl.BlockSpec((tm, 2 * ATTN_WIDTH), row),
            pl.BlockSpec((tm, 2 * LANES), row),
            pl.BlockSpec((tm, 2 * LANES), row),
            pl.BlockSpec((tm, 2 * D_MODEL), row),
        ],
        out_shape=[
            jax.ShapeDtypeStruct((n, HYENA_WIDTH), F32),
            jax.ShapeDtypeStruct((n, HYENA_WIDTH), F32),
            jax.ShapeDtypeStruct((n, HYENA_WIDTH), F32),
            jax.ShapeDtypeStruct((n, 2 * ATTN_WIDTH), BF16),
            jax.ShapeDtypeStruct((n, 2 * LANES), BF16),
            jax.ShapeDtypeStruct((n, 2 * LANES), BF16),
            jax.ShapeDtypeStruct((n, 2 * D_MODEL), F32),
        ],
        scratch_shapes=[pltpu.VMEM((tm + 2 * SUBLANES, C_HY), F32)],
        compiler_params=_cparams("parallel"),
        name="inproj",
    )(xf, xf, xf, nw, w_in_bf, cw, cb, qw, kw, cos, sa, sb, ones_bd)


def _rows_to_tiles(r, tj):
    return r.reshape(r.shape[0], tj, r.shape[1] // tj)


def _tiles_to_rows(t):
    return t.reshape(t.shape[0], t.shape[1] * t.shape[2])


def _twiddle_tiles(r, twr_ref, twi_ref, n1h, tj):
    r3 = _rows_to_tiles(r, tj)
    ar, ai = r3[:n1h], r3[n1h:]
    reps = r3.shape[-1] // LANES
    tr = jnp.concatenate([twr_ref[...]] * reps, axis=-1)
    ti = jnp.concatenate([twi_ref[...]] * reps, axis=-1)
    return (ar * tr - ai * ti).astype(BF16), (ar * ti + ai * tr).astype(BF16)


def _store_twiddled(r, twr_ref, twi_ref, are_ref, aim_ref, sl, n1h, tj):
    re, im = _twiddle_tiles(r, twr_ref, twi_ref, n1h, tj)
    are_ref[:, :, sl] = re
    aim_ref[:, :, sl] = im


def _filt_kernel(z_ref, w1_ref, b1_ref, w2_ref, b2_ref, w3f_ref, w3b_ref, fr_ref, dl_ref, g_ref, twr_ref, twi_ref,
                 are_ref, aim_ref, s_ref, *, n1h, tj):
    i = pl.program_id(0)
    z = z_ref[...]
    fr = fr_ref[...]
    h = jnp.sin(fr * (_dot3(z, w1_ref[...]) + b1_ref[...]))
    h = jnp.sin(fr * (_dot3(h, w2_ref[...]) + b2_ref[...]))
    hb = h.astype(BF16)
    tf = z[:, 0:1]
    tb = z[:, FILT_PAD:FILT_PAD + 1]
    sgn = z[:, FILT_PAD + FILTER_EMB:FILT_PAD + FILTER_EMB + 1]
    gm = g_ref[...].astype(BF16)
    cb = HYENA_WIDTH
    parts = []
    for o in range(HYENA_ORDER):
        sl = slice(o * cb, (o + 1) * cb)
        dl = dl_ref[...]

        def taps(w_ref, t):
            return _dot(hb, w_ref[:, sl].astype(BF16)) * (jnp.exp(-t * dl) + DECAY_SHIFT)

        kf = taps(w3f_ref, tf)
        kb = taps(w3b_ref, tb) * sgn
        parts.append(jnp.sum(jnp.abs(kf), axis=0, keepdims=True) + jnp.sum(jnp.abs(kb), axis=0, keepdims=True))
        cols = [jnp.concatenate([kf[j * n1h:(j + 1) * n1h], kb[j * n1h:(j + 1) * n1h]], axis=0).astype(BF16)
                for j in range(tj)]
        re, im = _twiddle_tiles(_dot(gm, jnp.concatenate(cols, axis=1)), twr_ref, twi_ref, n1h, tj)
        are_ref[:, :, sl] = re
        aim_ref[:, :, sl] = im
    part = jnp.concatenate(parts, axis=1)

    @pl.when(i == 0)
    def _():
        s_ref[...] = part

    @pl.when(i > 0)
    def _():
        s_ref[...] += part


def _filter_stage_a(zf, w1bd, b1, w2bd, b2, w3f, w3b, fr, dl, g_full, twr, twi, n1h, tj):
    cw = HYENA_ORDER * HYENA_WIDTH
    hid2 = 2 * FILTER_HIDDEN
    a_shape = jax.ShapeDtypeStruct((n1h, LANES, cw), BF16)
    a_spec = pl.BlockSpec((n1h, tj, cw), lambda i: (0, i, 0))
    t_spec = pl.BlockSpec((n1h, tj, LANES), lambda i: (0, i, 0))
    return pl.pallas_call(
        functools.partial(_filt_kernel, n1h=n1h, tj=tj),
        grid=(LANES // tj,),
        in_specs=[
            pl.BlockSpec((tj * n1h, 2 * FILT_PAD), lambda i: (i, 0)),
            _const_spec((2 * FILT_PAD, hid2)),
            _const_spec((1, hid2)),
            _const_spec((hid2, hid2)),
            _const_spec((1, hid2)),
            _const_spec((hid2, cw)),
            _const_spec((hid2, cw)),
            _const_spec((1, hid2)),
            _const_spec((1, HYENA_WIDTH)),
            _const_spec((2 * n1h, 2 * n1h)),
            t_spec, t_spec,
        ],
        out_specs=[a_spec, a_spec, pl.BlockSpec((1, cw), lambda i: (0, 0))],
        out_shape=[a_shape, a_shape, jax.ShapeDtypeStruct((1, cw), F32)],
        compiler_params=_cparams("arbitrary"),
        name="filter_taps_stage_a",
    )(zf, w1bd, b1, w2bd, b2, w3f, w3b, fr, dl, g_full, twr, twi)


def _midf_kernel(f_ref, s_ref, are_ref, aim_ref, h_ref, *, kb):
    fm = f_ref[...].astype(BF16)
    inv = 1.0 / s_ref[...]
    for k in range(kb):
        h_ref[k] = (_dot(fm, jnp.concatenate([are_ref[k], aim_ref[k]], axis=0)) * inv).astype(BF16)


def _filter_spectrum(afre, afim, fblk, asum, kb):
    n1h, _, ctot = afre.shape
    cb = HYENA_WIDTH
    a_spec = pl.BlockSpec((kb, LANES, cb), lambda ki, ci: (ki, 0, ci))
    return pl.pallas_call(
        functools.partial(_midf_kernel, kb=kb),
        grid=(n1h // kb, ctot // cb),
        in_specs=[_const_spec((2 * LANES, 2 * LANES)), pl.BlockSpec((1, cb), lambda ki, ci: (0, ci)), a_spec, a_spec],
        out_specs=pl.BlockSpec((kb, 2 * LANES, cb), lambda ki, ci: (ki, 0, ci)),
        out_shape=jax.ShapeDtypeStruct((n1h, 2 * LANES, ctot), BF16),
        compiler_params=_cparams("parallel", "parallel"),
        name="filter_spectrum",
    )(fblk, asum, afre, afim)


def _fwda_kernel(g_ref, twr_ref, twi_ref, u_ref, are_ref, aim_ref, *, n1h, tj):
    gm = g_ref[...].astype(BF16)
    prev = None
    for sl in _CH_SPLITS:
        u = _tiles_to_rows(u_ref[:, :, sl].astype(BF16))
        if prev is not None:
            _store_twiddled(prev[1], twr_ref, twi_ref, are_ref, aim_ref, prev[0], n1h, tj)
        prev = (sl, _dot(gm, u))
    _store_twiddled(prev[1], twr_ref, twi_ref, are_ref, aim_ref, prev[0], n1h, tj)


def _fwd_a(u4, g_half, twr, twi, tj):
    b, n1h, _, cb = u4.shape
    blk = pl.BlockSpec((None, n1h, tj, cb), lambda bi, ji: (bi, 0, ji, 0))
    t_spec = pl.BlockSpec((n1h, tj, LANES), lambda bi, ji: (0, ji, 0))
    a_shape = jax.ShapeDtypeStruct(u4.shape, BF16)
    return pl.pallas_call(
        functools.partial(_fwda_kernel, n1h=n1h, tj=tj),
        grid=(b, LANES // tj),
        in_specs=[_const_spec((2 * n1h, n1h)), t_spec, t_spec, blk],
        out_specs=[blk, blk],
        out_shape=[a_shape, a_shape],
        compiler_params=_cparams("parallel", "parallel"),
        name="dft_stage_a",
    )(g_half, twr, twi, u4)


def _mid_kernel(f_ref, fi_ref, twr_ref, twi_ref, h_ref, are_ref, aim_ref, bre_ref, bim_ref, *, kb):
    fm = f_ref[...].astype(BF16)
    fim = fi_ref[...].astype(BF16)
    reps = are_ref.shape[-1] // LANES
    def fwd(k):
        return _dot(fm, jnp.concatenate([are_ref[k], aim_ref[k]], axis=0))

    def spec(k, x):
        xr, xi = x[:LANES], x[LANES:]
        hr, hi = h_ref[k, :LANES], h_ref[k, LANES:]
        return jnp.concatenate([xr * hr - xi * hi, xr * hi + xi * hr], axis=0).astype(BF16)

    def out(k, bc):
        br, bi = bc[:LANES], bc[LANES:]
        tr = jnp.concatenate([twr_ref[k]] * reps, axis=1)
        ti = jnp.concatenate([twi_ref[k]] * reps, axis=1)
        bre_ref[k] = (br * tr + bi * ti).astype(BF16)
        bim_ref[k] = (bi * tr - br * ti).astype(BF16)

    x = fwd(0)
    bc_prev = None
    for k in range(kb):
        y = spec(k, x)
        if k + 1 < kb:
            x = fwd(k + 1)
        if bc_prev is not None:
            out(k - 1, bc_prev)
        bc_prev = _dot(fim, y)
    out(kb - 1, bc_prev)


def _mid(are, aim, fblk, fblk_inv, twr, twi, hspec, order, kb):
    b, n1h, _, cb = are.shape
    a_spec = pl.BlockSpec((None, kb, LANES, cb), lambda ki, bi: (bi, ki, 0, 0))
    t_spec = pl.BlockSpec((kb, LANES, LANES), lambda ki, bi: (ki, 0, 0))
    shape = jax.ShapeDtypeStruct(are.shape, BF16)
    return pl.pallas_call(
        functools.partial(_mid_kernel, kb=kb),
        grid=(n1h // kb, b),
        in_specs=[
            _const_spec((2 * LANES, 2 * LANES)),
            _const_spec((2 * LANES, 2 * LANES)),
            t_spec, t_spec,
            pl.BlockSpec((kb, 2 * LANES, cb), lambda ki, bi: (ki, 0, order)),
            a_spec, a_spec,
        ],
        out_specs=[a_spec, a_spec],
        out_shape=[shape, shape],
        compiler_params=_cparams("parallel", "parallel"),
        name="dft_stage_c",
    )(fblk, fblk_inv, twr, twi, hspec, are, aim)


def _inva_kernel(gi_ref, bre_ref, bim_ref, xg_ref, v_ref, sb_ref, *rest, n1h, tj, fuse_next):
    gim = gi_ref[...].astype(BF16)
    if fuse_next:
        g_ref, twr_ref, twi_ref, z_ref, are_ref, aim_ref = rest
        gm = g_ref[...].astype(BF16)
    else:
        (z_ref,) = rest

    def conv(sl):
        rhs = jnp.concatenate([_tiles_to_rows(bre_ref[:, :, sl]), _tiles_to_rows(bim_ref[:, :, sl])], axis=0)
        return _dot(gim, rhs)

    def gate(sl, y):
        z = xg_ref[:, :, sl] * (_rows_to_tiles(y, tj) + sb_ref[:, sl] * v_ref[:, :, sl])
        z_ref[:, :, sl] = z
        return z

    s0, s1 = _CH_SPLITS
    y0 = conv(s0)
    y1 = conv(s1)
    z0 = gate(s0, y0)
    if fuse_next:
        r0 = _dot(gm, _tiles_to_rows(z0.astype(BF16)))
    z1 = gate(s1, y1)
    if fuse_next:
        _store_twiddled(r0, twr_ref, twi_ref, are_ref, aim_ref, s0, n1h, tj)
        r1 = _dot(gm, _tiles_to_rows(z1.astype(BF16)))
        _store_twiddled(r1, twr_ref, twi_ref, are_ref, aim_ref, s1, n1h, tj)


def _inv_a(bre, bim, g_inv, xg, v, sb_row, tj, nxt=None):
    b, n1h, _, cb = bre.shape
    blk = pl.BlockSpec((None, n1h, tj, cb), lambda bi, ji: (bi, 0, ji, 0))
    in_specs = [_const_spec((n1h, 2 * n1h)), blk, blk, blk, blk, _const_spec((1, cb))]
    args = [g_inv, bre, bim, xg, v, sb_row]
    out_specs = [blk]
    out_shape = [jax.ShapeDtypeStruct(bre.shape, F32)]
    if nxt is not None:
        t_spec = pl.BlockSpec((n1h, tj, LANES), lambda bi, ji: (0, ji, 0))
        in_specs += [_const_spec((2 * n1h, n1h)), t_spec, t_spec]
        args += list(nxt)
        out_specs += [blk, blk]
        out_shape += [jax.ShapeDtypeStruct(bre.shape, BF16)] * 2
    return pl.pallas_call(
        functools.partial(_inva_kernel, n1h=n1h, tj=tj, fuse_next=nxt is not None),
        grid=(b, LANES // tj),
        in_specs=in_specs,
        out_specs=out_specs,
        out_shape=out_shape,
        compiler_params=_cparams("parallel", "parallel"),
        name="idft_stage_a_gate",
    )(*args)


def _attn_kernel(sink_ref, q_ref, kp_ref, kc_ref, kn_ref, vp_ref, vc_ref, vn_ref, o_ref, *, nblk, sub):
    i = pl.program_id(1)
    blk = ATTN_BLOCK
    grp = N_HEADS // N_KV_HEADS
    rows = grp * blk
    qi = lax.broadcasted_iota(jnp.int32, (rows, 3 * blk), 0) % blk
    si = lax.broadcasted_iota(jnp.int32, (rows, 3 * blk), 1)
    mid = (si >= blk) & (si < 2 * blk)
    tri_prev = (si < blk) & (si >= qi)
    tri_next = (si >= 2 * blk) & (si - 2 * blk <= qi)
    hrow = lax.broadcasted_iota(jnp.int32, (rows, 1), 0) // blk
    kcat = jnp.concatenate([kp_ref[...], kc_ref[...], kn_ref[...]], axis=0)
    vcat = jnp.concatenate([vp_ref[...], vc_ref[...], vn_ref[...]], axis=0)
    for s in range(sub):
        j = sub * i + s
        valid = mid | (tri_prev & (j > 0)) | (tri_next & (j < nblk - 1))
        outs = []
        for g in range(N_KV_HEADS):
            lhs = jnp.concatenate([q_ref[s * blk:(s + 1) * blk, (grp * g + t) * LANES:(grp * g + t + 1) * LANES]
                                   for t in range(grp)], axis=0)
            sc = lax.dot_general(lhs, kcat[s * blk:(s + 3) * blk, g * LANES:(g + 1) * LANES],
                                 (((1,), (1,)), ((), ())), preferred_element_type=F32)
            sc = jnp.where(valid, sc, NEG_INF)
            sk = jnp.full((rows, 1), sink_ref[grp * g], F32)
            for t in range(1, grp):
                sk = jnp.where(hrow == t, sink_ref[grp * g + t], sk)
            m = jnp.maximum(jnp.max(sc, axis=1, keepdims=True), sk)
            e = jnp.exp(sc - m)
            den = jnp.sum(e, axis=1, keepdims=True) + jnp.exp(sk - m)
            r = _dot(e.astype(BF16), vcat[s * blk:(s + 3) * blk, g * LANES:(g + 1) * LANES]) / den
            for t in range(0, grp, 2):
                outs.append(r[t * blk:(t + 1) * blk] + pltpu.roll(r[(t + 1) * blk:(t + 2) * blk], HEAD_DIM, 1))
        o_ref[s * blk:(s + 1) * blk, :] = jnp.concatenate(outs, axis=1).astype(BF16)


def _attention(q3, k3, v3, sink):
    b, seq_len, qw = q3.shape
    blk = ATTN_BLOCK
    sub = ATTN_SUB
    nblk = seq_len // blk
    wide = k3.shape[-1]
    prev = lambda bi, i: (bi, jnp.maximum(sub * i - 1, 0), 0)
    cur = lambda bi, i: (bi, i, 0)
    nxt = lambda bi, i: (bi, jnp.minimum(sub * (i + 1), nblk - 1), 0)
    edge = lambda f: pl.BlockSpec((None, blk, wide), f)
    body = pl.BlockSpec((None, sub * blk, wide), cur)
    return pl.pallas_call(
        functools.partial(_attn_kernel, nblk=nblk, sub=sub),
        grid=(b, nblk // sub),
        in_specs=[pl.BlockSpec(memory_space=pltpu.SMEM),
                  pl.BlockSpec((None, sub * blk, qw), cur),
                  edge(prev), body, edge(nxt), edge(prev), body, edge(nxt)],
        out_specs=pl.BlockSpec((None, sub * blk, ATTN_WIDTH), cur),
        out_shape=jax.ShapeDtypeStruct((b, seq_len, ATTN_WIDTH), BF16),
        compiler_params=_cparams("parallel", "parallel"),
        name="banded_attention",
    )(sink, q3, k3, k3, k3, v3, v3, v3)


def _final_kernel(x_ref, yh_ref, ya_ref, g_ref, why_ref, wat_ref, wo_ref, nw_ref, wg_ref, wu_ref, wd_ref,
                  o_ref):
    a = _dot(yh_ref[...].astype(BF16), why_ref[...])
    b = _dot(ya_ref[...], wat_ref[...])
    merged = g_ref[:, :D_MODEL] * a + g_ref[:, D_MODEL:] * b
    x1 = x_ref[...] + _dot(merged.astype(BF16), wo_ref[...])
    ms = jnp.mean(x1 * x1, axis=-1, keepdims=True)
    f = (x1 * lax.rsqrt(ms + RMS_EPS) * nw_ref[...]).astype(BF16)
    acc = x1
    for c in range(FFN_HIDDEN // FFN_CHUNK):
        lo, hi = c * FFN_CHUNK, (c + 1) * FFN_CHUNK
        gt = _dot(f, wg_ref[:, lo:hi])
        upv = _dot(f, wu_ref[:, lo:hi])
        hid = gt * _sigmoid(gt) * upv
        acc = acc + _dot(hid.astype(BF16), wd_ref[lo:hi, :])
    o_ref[...] = acc


def _final(xf, yh, ya, g, why, wat, wo, nw, wg, wu, wd):
    n = xf.shape[0]
    tm = TM_FINAL
    row = lambda i: (i, 0)
    return pl.pallas_call(
        _final_kernel,
        grid=(n // tm,),
        in_specs=[
            pl.BlockSpec((tm, D_MODEL), row),
            pl.BlockSpec((tm, HYENA_WIDTH), row),
            pl.BlockSpec((tm, ATTN_WIDTH), row),
            pl.BlockSpec((tm, 2 * D_MODEL), row),
            _const_spec((HYENA_WIDTH, D_MODEL)),
            _const_spec((ATTN_WIDTH, D_MODEL)),
            _const_spec((D_MODEL, D_MODEL)),
            _const_spec((1, D_MODEL)),
            _const_spec((D_MODEL, FFN_HIDDEN)),
            _const_spec((D_MODEL, FFN_HIDDEN)),
            _const_spec((FFN_HIDDEN, D_MODEL)),
        ],
        out_specs=pl.BlockSpec((tm, D_MODEL), row),
        out_shape=jax.ShapeDtypeStruct((n, D_MODEL), F32),
        compiler_params=_cparams("parallel"),
        name="merge_ffn",
    )(xf, yh, ya, g, why, wat, wo, nw, wg, wu, wd)


@functools.lru_cache(maxsize=None)
def _dft_constants(n1h):
    n1_len = 2 * n1h
    m = n1_len * LANES
    k1 = np.arange(n1h, dtype=np.float64)[:, None] + 0.5
    n1 = np.arange(n1_len, dtype=np.float64)[None, :]
    gc = np.exp(-2j * np.pi * k1 * n1 / n1_len)
    g_full = np.concatenate([gc.real, gc.imag], axis=0)
    g_inv = (2.0 / m) * np.concatenate([gc.real[:, :n1h].T, gc.imag[:, :n1h].T], axis=1)
    n2 = np.arange(LANES, dtype=np.float64)
    tw = np.exp(-2j * np.pi * k1 * n2[None, :] / m)[:, :, None] * np.ones((1, 1, LANES))
    fc = np.exp(-2j * np.pi * np.outer(n2, n2) / LANES)
    fblk = np.block([[fc.real, -fc.imag], [fc.imag, fc.real]])
    fblk_inv = np.block([[fc.real, fc.imag], [-fc.imag, fc.real]])
    f32 = lambda a: np.ascontiguousarray(a, dtype=np.float32)
    return dict(g_full=f32(g_full), g_half=f32(g_full[:, :n1h]), g_inv=f32(g_inv), twr=f32(tw.real),
                twi=f32(tw.imag), fblk=f32(fblk), fblk_inv=f32(fblk_inv))


@functools.lru_cache(maxsize=None)
def _filter_features(seq_len):
    n1h = seq_len // LANES
    n2 = np.arange(LANES)[:, None]
    n1 = np.arange(n1h)[None, :]
    slot_f = (LANES * n1 + n2).reshape(-1)
    slot_b = slot_f + seq_len
    neg_lag = slot_b > seq_len
    pos_b = np.where(neg_lag, 2 * seq_len - slot_b, 0)
    bands = np.linspace(1e-4, FILTER_BANDS - 1, FILTER_BANDS)

    def feats(pos, sgn):
        pos = pos.astype(np.float64)
        ang = (2.0 * math.pi / seq_len) * pos[:, None] * bands[None, :]
        pad = np.zeros((pos.shape[0], FILT_PAD - FILTER_EMB - 1))
        return np.concatenate([(pos / (seq_len - 1))[:, None], np.cos(ang), -np.sin(ang), sgn[:, None], pad], axis=-1)

    z = np.concatenate([feats(slot_f, np.ones(slot_f.shape)), feats(pos_b, np.where(neg_lag, -1.0, 0.0))], axis=-1)
    return np.ascontiguousarray(z, dtype=np.float32)


@functools.lru_cache(maxsize=None)
def _rope_tables(seq_len):
    inv = ROPE_THETA ** (-np.arange(0, HEAD_DIM, 2, dtype=np.float64) / HEAD_DIM)
    ang = np.arange(seq_len, dtype=np.float64)[:, None] * inv[None, :]
    cos, sin = np.cos(ang), np.sin(ang)
    zero = np.zeros_like(sin)
    reps = LANES // HEAD_DIM
    f32 = lambda a: np.ascontiguousarray(np.tile(a, (1, reps)), dtype=np.float32)
    return f32(np.concatenate([cos, cos], axis=1)), f32(np.concatenate([-sin, zero], axis=1)), \
        f32(np.concatenate([zero, sin], axis=1))


def _layer(x, p):
    b, seq_len, _ = x.shape
    n = b * seq_len
    n1h = seq_len // LANES
    xf = x.reshape(n, D_MODEL)
    c = {k: jnp.asarray(v) for k, v in _dft_constants(n1h).items()}
    tj = SUBLANES if n1h >= LANES else 4 * SUBLANES
    kb = SUBLANES
    r4 = lambda a: a.reshape(b, n1h, LANES, HYENA_WIDTH)

    cos, sa, sb = (jnp.asarray(t) for t in _rope_tables(seq_len))
    x1, x2, v, q, k4, v4, g = _inproj(xf, seq_len, p["attn_norm_w"], p["w_in"], p["conv_w"], p["conv_b"],
                                      p["q_norm_w"], p["k_norm_w"], cos, sa, sb, p["ones_bd"])

    afre, afim, asum = _filter_stage_a(jnp.asarray(_filter_features(seq_len)), p["filt_w1"], p["filt_b1"],
                                       p["filt_w2"], p["filt_b2"], p["filt_w3f"], p["filt_w3b"], p["filt_freq"],
                                       p["decay"], c["g_full"], c["twr"], c["twi"], n1h, tj)
    hspec = _filter_spectrum(afre, afim, c["fblk"], asum, kb)

    are, aim = _fwd_a(r4(v), c["g_half"], c["twr"], c["twi"], tj)
    bre, bim = _mid(are, aim, c["fblk"], c["fblk_inv"], c["twr"], c["twi"], hspec, 0, kb)
    z1, are, aim = _inv_a(bre, bim, c["g_inv"], r4(x1), r4(v), p["hyena_bias"][0:1], tj,
                          nxt=(c["g_half"], c["twr"], c["twi"]))
    bre, bim = _mid(are, aim, c["fblk"], c["fblk_inv"], c["twr"], c["twi"], hspec, 1, kb)
    (yh,) = _inv_a(bre, bim, c["g_inv"], r4(x2), z1, p["hyena_bias"][1:2], tj)

    ya = _attention(q.reshape(b, seq_len, 2 * ATTN_WIDTH), k4.reshape(b, seq_len, 2 * LANES),
                    v4.reshape(b, seq_len, 2 * LANES), p["attn_sink"])

    out = _final(xf, yh.reshape(n, HYENA_WIDTH), ya.reshape(n, ATTN_WIDTH), g, p["w_hy_out"], p["w_at_out"],
                 p["w_o"], p["ffn_norm_w"], p["w_gate"], p["w_up"], p["w_down"])
    return out.reshape(b, seq_len, D_MODEL)


def _block_diag2(w):
    z = jnp.zeros_like(w)
    return jnp.concatenate([jnp.concatenate([w, z], axis=1), jnp.concatenate([z, w], axis=1)], axis=0)


def kernel(x_prompt, x_sample, attn_norm_w, w_in, hyena_conv_w, hyena_conv_b, filt_w1, filt_b1, filt_w2, filt_b2,
           filt_w3, filt_freq, hyena_bias, q_norm_w, k_norm_w, attn_sink, w_hy_out, w_at_out, w_o, ffn_norm_w,
           w_gate, w_up, w_down):
    cw = HYENA_ORDER * HYENA_WIDTH
    max_decay = math.log(DECAY_TARGET) / DECAY_FAST_PCT
    min_decay = math.log(DECAY_TARGET) / DECAY_SLOW_PCT
    deltas = jnp.abs(jnp.linspace(min_decay, max_decay, HYENA_WIDTH, dtype=F32))
    head = np.arange(ATTN_WIDTH) // HEAD_DIM
    ones_bd = jnp.asarray((head[:, None] == head[None, :]).astype(np.float32) / HEAD_DIM).astype(BF16)
    w3 = filt_w3[0].reshape(FILTER_HIDDEN, HYENA_ORDER, 2, HYENA_WIDTH).transpose(2, 0, 1, 3)
    w3 = w3.reshape(2, FILTER_HIDDEN, cw)
    w3_zero = jnp.zeros((FILTER_HIDDEN, cw), F32)
    twice = lambda a: jnp.tile(a, 2)[None, :]
    p = dict(
        attn_norm_w=attn_norm_w[0][None, :],
        w_in=w_in[0].astype(BF16),
        conv_w=hyena_conv_w[0],
        conv_b=hyena_conv_b[0][None, :],
        filt_w1=_block_diag2(jnp.pad(filt_w1[0], ((0, FILT_PAD - FILTER_EMB), (0, 0)))),
        filt_b1=twice(filt_b1[0]),
        filt_w2=_block_diag2(filt_w2[0]),
        filt_b2=twice(filt_b2[0]),
        filt_w3f=jnp.concatenate([w3[0], w3_zero], axis=0),
        filt_w3b=jnp.concatenate([w3_zero, w3[1]], axis=0),
        filt_freq=twice(filt_freq[0]),
        decay=deltas[None, :],
        hyena_bias=hyena_bias[0],
        q_norm_w=jnp.tile(q_norm_w[0], N_HEADS)[None, :],
        k_norm_w=jnp.tile(k_norm_w[0], N_KV_HEADS)[None, :],
        attn_sink=attn_sink[0],
        ones_bd=ones_bd,
        w_hy_out=w_hy_out[0].astype(BF16),
        w_at_out=w_at_out[0].astype(BF16),
        w_o=w_o[0].astype(BF16),
        ffn_norm_w=ffn_norm_w[0][None, :],
        w_gate=w_gate[0].astype(BF16),
        w_up=w_up[0].astype(BF16),
        w_down=w_down[0].astype(BF16),
    )
    return (_layer(x_prompt, p), _layer(x_sample, p))
```

```python
import functools
import math

import numpy as np
import jax
import jax.numpy as jnp
from jax import lax
from jax.experimental import pallas as pl
from jax.experimental.pallas import tpu as pltpu

F32 = jnp.float32
BF16 = jnp.bfloat16

D_MODEL = 1024
HYENA_WIDTH = 512
HYENA_ORDER = 2
FILTER_BANDS = 16
FILTER_EMB = 1 + 2 * FILTER_BANDS
FILTER_HIDDEN = 64
DECAY_FAST_PCT = 0.3
DECAY_SLOW_PCT = 1.5
DECAY_TARGET = 1e-2
DECAY_SHIFT = 0.05
N_HEADS = 8
N_KV_HEADS = 2
HEAD_DIM = 64
ATTN_WIDTH = N_HEADS * HEAD_DIM
KV_WIDTH = N_KV_HEADS * HEAD_DIM
WINDOW = 128
ROPE_THETA = 10000.0
FFN_HIDDEN = 2816
RMS_EPS = 1e-6
NEG_INF = -1e30

C_HY = 3 * HYENA_WIDTH
C_Q = C_HY + ATTN_WIDTH
C_K = C_Q + KV_WIDTH
C_V = C_K + KV_WIDTH
IN_WIDTH = C_V + 2 * D_MODEL

LANES = 128
SUBLANES = 8
ATTN_BLOCK = 128
ATTN_SUB = 4
VMEM_LIMIT = 56 * 1024 * 1024

TM_INPROJ = 512
TM_FINAL = 512
FFN_CHUNKS = ((0, 1280), (1280, FFN_HIDDEN))
_CH_SPLITS = (slice(0, HYENA_WIDTH // 2), slice(HYENA_WIDTH // 2, HYENA_WIDTH))
FILT_PAD = 64


def _cparams(*sem):
    return pltpu.CompilerParams(dimension_semantics=sem, vmem_limit_bytes=VMEM_LIMIT)


def _const_spec(shape):
    nd = len(shape)
    return pl.BlockSpec(shape, lambda *_: (0,) * nd, pipeline_mode=pl.Buffered(1))


def _dot(a, b):
    return jnp.dot(a, b, preferred_element_type=F32)


def _sigmoid(x):
    return 0.5 * jnp.tanh(0.5 * x) + 0.5


def _split(a):
    hi = a.astype(BF16)
    lo = (a - hi.astype(F32)).astype(BF16)
    return hi, lo


def _dot3(a, w):
    ah, al = _split(a)
    wh, wl = _split(w)
    return _dot(ah, wh) + _dot(al, wh) + _dot(ah, wl)


def _inproj_kernel(x_ref, xp_ref, xn_ref, nw_ref, w_ref, cw_ref, cb_ref, qw_ref, kw_ref,
                   cos_ref, sa_ref, sb_ref, ones_ref,
                   x1_ref, x2_ref, v_ref, q_ref, k4_ref, v4_ref, g_ref, pad_ref, *, tm, tiles_per_seq):
    pos = pl.program_id(0) % tiles_per_seq
    nw = nw_ref[...]

    def norm(xv):
        ms = jnp.mean(xv * xv, axis=-1, keepdims=True)
        return (xv * lax.rsqrt(ms + RMS_EPS) * nw).astype(BF16)

    h_all = norm(jnp.concatenate([x_ref[...], xp_ref[...], xn_ref[...]], axis=0))
    h = h_all[:tm]

    hy_all = _dot(h_all, w_ref[:, :C_HY])
    hy = hy_all[:tm]
    hyh = hy_all[tm:]
    prev_ok = (pos > 0).astype(F32)
    next_ok = (pos < tiles_per_seq - 1).astype(F32)
    pad_ref[0:SUBLANES] = hyh[0:SUBLANES] * prev_ok
    pad_ref[SUBLANES:SUBLANES + tm] = hy
    pad_ref[SUBLANES + tm:2 * SUBLANES + tm] = hyh[SUBLANES:] * next_ok
    up = pad_ref[SUBLANES - 1:SUBLANES - 1 + tm]
    un = pad_ref[SUBLANES + 1:SUBLANES + 1 + tm]
    cw = cw_ref[...]
    uc = cw[0:1] * up + cw[1:2] * hy + cw[2:3] * un + cb_ref[...]
    x1_ref[...] = uc[:, :HYENA_WIDTH]
    x2_ref[...] = uc[:, HYENA_WIDTH:2 * HYENA_WIDTH]
    v_ref[...] = uc[:, 2 * HYENA_WIDTH:]

    def head_norm_rope(t, wrow, ones, reps):
        ms = _dot((t * t).astype(BF16), ones)
        tn = t * lax.rsqrt(ms + RMS_EPS) * wrow
        width = t.shape[1]
        cos = jnp.concatenate([cos_ref[...]] * reps, axis=1)
        sa = jnp.concatenate([sa_ref[...]] * reps, axis=1)
        sb = jnp.concatenate([sb_ref[...]] * reps, axis=1)
        half = HEAD_DIM // 2
        return tn * cos + pltpu.roll(tn, width - half, 1) * sa + pltpu.roll(tn, half, 1) * sb

    qkv = _dot(h, w_ref[:, C_HY:C_V])
    q = qkv[:, :ATTN_WIDTH]
    q = head_norm_rope(q, qw_ref[...], ones_ref[...], ATTN_WIDTH // LANES) * (HEAD_DIM ** -0.5)
    blocks = []
    for p in range(N_HEADS // 2):
        qp = q[:, p * LANES:(p + 1) * LANES]
        blocks += [qp, pltpu.roll(qp, HEAD_DIM, 1)]
    q_ref[...] = jnp.concatenate(blocks, axis=1).astype(BF16)

    k = qkv[:, ATTN_WIDTH:ATTN_WIDTH + KV_WIDTH]
    k = head_norm_rope(k, kw_ref[...], ones_ref[:KV_WIDTH, :KV_WIDTH], 1)
    v = qkv[:, ATTN_WIDTH + KV_WIDTH:]

    lo_half = lax.broadcasted_iota(jnp.int32, (tm, KV_WIDTH), 1) < HEAD_DIM

    def spread(t):
        zero = jnp.zeros_like(t)
        return jnp.concatenate([jnp.where(lo_half, t, zero), jnp.where(lo_half, pltpu.roll(t, HEAD_DIM, 1), zero)],
                               axis=1).astype(BF16)

    k4_ref[...] = spread(k)
    v4_ref[...] = spread(v)
    g_ref[...] = _sigmoid(_dot(h, w_ref[:, C_V:]))


def _inproj(xf, seq_len, nw, w_in_bf, cw, cb, qw, kw, cos, sa, sb, ones_bd):
    n = xf.shape[0]
    tm = TM_INPROJ
    tiles_per_seq = seq_len // tm
    nblk8 = n // SUBLANES
    r8 = tm // SUBLANES
    kern = functools.partial(_inproj_kernel, tm=tm, tiles_per_seq=tiles_per_seq)
    row = lambda i: (i, 0)
    return pl.pallas_call(
        kern,
        grid=(n // tm,),
        in_specs=[
            pl.BlockSpec((tm, D_MODEL), row),
            pl.BlockSpec((SUBLANES, D_MODEL), lambda i: (jnp.maximum(i * r8 - 1, 0), 0)),
            pl.BlockSpec((SUBLANES, D_MODEL), lambda i: (jnp.minimum((i + 1) * r8, nblk8 - 1), 0)),
            _const_spec((1, D_MODEL)),
            _const_spec((D_MODEL, IN_WIDTH)),
            _const_spec((3, C_HY)),
            _const_spec((1, C_HY)),
            _const_spec((1, ATTN_WIDTH)),
            _const_spec((1, KV_WIDTH)),
            pl.BlockSpec((tm, LANES), lambda i: (i % tiles_per_seq, 0)),
            pl.BlockSpec((tm, LANES), lambda i: (i % tiles_per_seq, 0)),
            pl.BlockSpec((tm, LANES), lambda i: (i % tiles_per_seq, 0)),
            _const_spec((ATTN_WIDTH, ATTN_WIDTH)),
        ],
        out_specs=[
            pl.BlockSpec((tm, HYENA_WIDTH), row),
            pl.BlockSpec((tm, HYENA_WIDTH), row),
            pl.BlockSpec((tm, HYENA_WIDTH), row),
            pl.BlockSpec((tm, 2 * ATTN_WIDTH), row),
            pl.BlockSpec((tm, 2 * LANES), row),
            pl.BlockSpec((tm, 2 * LANES), row),
            pl.BlockSpec((tm, 2 * D_MODEL), row),
        ],
        out_shape=[
            jax.ShapeDtypeStruct((n, HYENA_WIDTH), F32),
            jax.ShapeDtypeStruct((n, HYENA_WIDTH), F32),
            jax.ShapeDtypeStruct((n, HYENA_WIDTH), F32),
            jax.ShapeDtypeStruct((n, 2 * ATTN_WIDTH), BF16),
            jax.ShapeDtypeStruct((n, 2 * LANES), BF16),
            jax.ShapeDtypeStruct((n, 2 * LANES), BF16),
            jax.ShapeDtypeStruct((n, 2 * D_MODEL), F32),
        ],
        scratch_shapes=[pltpu.VMEM((tm + 2 * SUBLANES, C_HY), F32)],
        compiler_params=_cparams("parallel"),
        name="inproj",
    )(xf, xf, xf, nw, w_in_bf, cw, cb, qw, kw, cos, sa, sb, ones_bd)


def _rows_to_tiles(r, tj):
    return r.reshape(r.shape[0], tj, r.shape[1] // tj)


def _tiles_to_rows(t):
    return t.reshape(t.shape[0], t.shape[1] * t.shape[2])


def _twiddle_tiles(r, twr_ref, twi_ref, n1h, tj):
    r3 = _rows_to_tiles(r, tj)
    ar, ai = r3[:n1h], r3[n1h:]
    reps = r3.shape[-1] // LANES
    tr = jnp.concatenate([twr_ref[...]] * reps, axis=-1)
    ti = jnp.concatenate([twi_ref[...]] * reps, axis=-1)
    return (ar * tr - ai * ti).astype(BF16), (ar * ti + ai * tr).astype(BF16)


def _store_twiddled(r, twr_ref, twi_ref, are_ref, aim_ref, sl, n1h, tj):
    re, im = _twiddle_tiles(r, twr_ref, twi_ref, n1h, tj)
    are_ref[:, :, sl] = re
    aim_ref[:, :, sl] = im


def _filt_kernel(z_ref, w1_ref, b1_ref, w2_ref, b2_ref, w3f_ref, w3b_ref, fr_ref, dl_ref, g_ref, twr_ref, twi_ref,
                 are_ref, aim_ref, s_ref, *, n1h, tj):
    i = pl.program_id(0)
    z = z_ref[...]
    fr = fr_ref[...]
    h = jnp.sin(fr * (_dot3(z, w1_ref[...]) + b1_ref[...]))
    h = jnp.sin(fr * (_dot3(h, w2_ref[...]) + b2_ref[...]))
    hb = h.astype(BF16)
    tf = z[:, 0:1]
    tb = z[:, FILT_PAD:FILT_PAD + 1]
    sgn = z[:, FILT_PAD + FILTER_EMB:FILT_PAD + FILTER_EMB + 1]
    gm = g_ref[...].astype(BF16)
    cb = HYENA_WIDTH
    parts = []
    for o in range(HYENA_ORDER):
        sl = slice(o * cb, (o + 1) * cb)
        dl = dl_ref[...]

        def taps(w_ref, t):
            return _dot(hb, w_ref[:, sl].astype(BF16)) * (jnp.exp(-t * dl) + DECAY_SHIFT)

        kf = taps(w3f_ref, tf)
        kb = taps(w3b_ref, tb) * sgn
        parts.append(jnp.sum(jnp.abs(kf), axis=0, keepdims=True) + jnp.sum(jnp.abs(kb), axis=0, keepdims=True))
        cols = [jnp.concatenate([kf[j * n1h:(j + 1) * n1h], kb[j * n1h:(j + 1) * n1h]], axis=0).astype(BF16)
                for j in range(tj)]
        re, im = _twiddle_tiles(_dot(gm, jnp.concatenate(cols, axis=1)), twr_ref, twi_ref, n1h, tj)
        are_ref[:, :, sl] = re
        aim_ref[:, :, sl] = im
    part = jnp.concatenate(parts, axis=1)

    @pl.when(i == 0)
    def _():
        s_ref[...] = part

    @pl.when(i > 0)
    def _():
        s_ref[...] += part


def _filter_stage_a(zf, w1bd, b1, w2bd, b2, w3f, w3b, fr, dl, g_full, twr, twi, n1h, tj):
    cw = HYENA_ORDER * HYENA_WIDTH
    hid2 = 2 * FILTER_HIDDEN
    a_shape = jax.ShapeDtypeStruct((n1h, LANES, cw), BF16)
    a_spec = pl.BlockSpec((n1h, tj, cw), lambda i: (0, i, 0))
    t_spec = pl.BlockSpec((n1h, tj, LANES), lambda i: (0, i, 0))
    return pl.pallas_call(
        functools.partial(_filt_kernel, n1h=n1h, tj=tj),
        grid=(LANES // tj,),
        in_specs=[
            pl.BlockSpec((tj * n1h, 2 * FILT_PAD), lambda i: (i, 0)),
            _const_spec((2 * FILT_PAD, hid2)),
            _const_spec((1, hid2)),
            _const_spec((hid2, hid2)),
            _const_spec((1, hid2)),
            _const_spec((hid2, cw)),
            _const_spec((hid2, cw)),
            _const_spec((1, hid2)),
            _const_spec((1, HYENA_WIDTH)),
            _const_spec((2 * n1h, 2 * n1h)),
            t_spec, t_spec,
        ],
        out_specs=[a_spec, a_spec, pl.BlockSpec((1, cw), lambda i: (0, 0))],
        out_shape=[a_shape, a_shape, jax.ShapeDtypeStruct((1, cw), F32)],
        compiler_params=_cparams("arbitrary"),
        name="filter_taps_stage_a",
    )(zf, w1bd, b1, w2bd, b2, w3f, w3b, fr, dl, g_full, twr, twi)


def _midf_kernel(f_ref, s_ref, are_ref, aim_ref, h_ref, *, kb):
    fm = f_ref[...].astype(BF16)
    inv = 1.0 / s_ref[...]
    for k in range(kb):
        h_ref[k] = (_dot(fm, jnp.concatenate([are_ref[k], aim_ref[k]], axis=0)) * inv).astype(BF16)


def _filter_spectrum(afre, afim, fblk, asum, kb):
    n1h, _, ctot = afre.shape
    cb = HYENA_WIDTH
    a_spec = pl.BlockSpec((kb, LANES, cb), lambda ki, ci: (ki, 0, ci))
    return pl.pallas_call(
        functools.partial(_midf_kernel, kb=kb),
        grid=(n1h // kb, ctot // cb),
        in_specs=[_const_spec((2 * LANES, 2 * LANES)), pl.BlockSpec((1, cb), lambda ki, ci: (0, ci)), a_spec, a_spec],
        out_specs=pl.BlockSpec((kb, 2 * LANES, cb), lambda ki, ci: (ki, 0, ci)),
        out_shape=jax.ShapeDtypeStruct((n1h, 2 * LANES, ctot), BF16),
        compiler_params=_cparams("parallel", "parallel"),
        name="filter_spectrum",
    )(fblk, asum, afre, afim)


def _fwda_kernel(g_ref, twr_ref, twi_ref, u_ref, are_ref, aim_ref, *, n1h, tj):
    gm = g_ref[...].astype(BF16)
    prev = None
    for sl in _CH_SPLITS:
        u = _tiles_to_rows(u_ref[:, :, sl].astype(BF16))
        if prev is not None:
            _store_twiddled(prev[1], twr_ref, twi_ref, are_ref, aim_ref, prev[0], n1h, tj)
        prev = (sl, _dot(gm, u))
    _store_twiddled(prev[1], twr_ref, twi_ref, are_ref, aim_ref, prev[0], n1h, tj)


def _fwd_a(u4, g_half, twr, twi, tj):
    b, n1h, _, cb = u4.shape
    blk = pl.BlockSpec((None, n1h, tj, cb), lambda bi, ji: (bi, 0, ji, 0))
    t_spec = pl.BlockSpec((n1h, tj, LANES), lambda bi, ji: (0, ji, 0))
    a_shape = jax.ShapeDtypeStruct(u4.shape, BF16)
    return pl.pallas_call(
        functools.partial(_fwda_kernel, n1h=n1h, tj=tj),
        grid=(b, LANES // tj),
        in_specs=[_const_spec((2 * n1h, n1h)), t_spec, t_spec, blk],
        out_specs=[blk, blk],
        out_shape=[a_shape, a_shape],
        compiler_params=_cparams("parallel", "parallel"),
        name="dft_stage_a",
    )(g_half, twr, twi, u4)


def _mid_kernel(f_ref, fi_ref, twr_ref, twi_ref, h_ref, are_ref, aim_ref, bre_ref, bim_ref, *, kb):
    fm = f_ref[...].astype(BF16)
    fim = fi_ref[...].astype(BF16)
    reps = are_ref.shape[-1] // LANES
    def fwd(k):
        return _dot(fm, jnp.concatenate([are_ref[k], aim_ref[k]], axis=0))

    def spec(k, x):
        xr, xi = x[:LANES], x[LANES:]
        hr, hi = h_ref[k, :LANES], h_ref[k, LANES:]
        return jnp.concatenate([xr * hr - xi * hi, xr * hi + xi * hr], axis=0).astype(BF16)

    def out(k, bc):
        br, bi = bc[:LANES], bc[LANES:]
        tr = jnp.concatenate([twr_ref[k]] * reps, axis=1)
        ti = jnp.concatenate([twi_ref[k]] * reps, axis=1)
        bre_ref[k] = (br * tr + bi * ti).astype(BF16)
        bim_ref[k] = (bi * tr - br * ti).astype(BF16)

    x = fwd(0)
    bc_prev = None
    for k in range(kb):
        y = spec(k, x)
        if k + 1 < kb:
            x = fwd(k + 1)
        if bc_prev is not None:
            out(k - 1, bc_prev)
        bc_prev = _dot(fim, y)
    out(kb - 1, bc_prev)


def _mid(are, aim, fblk, fblk_inv, twr, twi, hspec, order, kb):
    b, n1h, _, cb = are.shape
    a_spec = pl.BlockSpec((None, kb, LANES, cb), lambda ki, bi: (bi, ki, 0, 0))
    t_spec = pl.BlockSpec((kb, LANES, LANES), lambda ki, bi: (ki, 0, 0))
    shape = jax.ShapeDtypeStruct(are.shape, BF16)
    return pl.pallas_call(
        functools.partial(_mid_kernel, kb=kb),
        grid=(n1h // kb, b),
        in_specs=[
            _const_spec((2 * LANES, 2 * LANES)),
            _const_spec((2 * LANES, 2 * LANES)),
            t_spec, t_spec,
            pl.BlockSpec((kb, 2 * LANES, cb), lambda ki, bi: (ki, 0, order)),
            a_spec, a_spec,
        ],
        out_specs=[a_spec, a_spec],
        out_shape=[shape, shape],
        compiler_params=_cparams("parallel", "parallel"),
        name="dft_stage_c",
    )(fblk, fblk_inv, twr, twi, hspec, are, aim)


def _inva_kernel(gi_ref, bre_ref, bim_ref, xg_ref, v_ref, sb_ref, *rest, n1h, tj, fuse_next):
    gim = gi_ref[...].astype(BF16)
    if fuse_next:
        g_ref, twr_ref, twi_ref, z_ref, are_ref, aim_ref = rest
        gm = g_ref[...].astype(BF16)
    else:
        (z_ref,) = rest

    def conv(sl):
        rhs = jnp.concatenate([_tiles_to_rows(bre_ref[:, :, sl]), _tiles_to_rows(bim_ref[:, :, sl])], axis=0)
        return _dot(gim, rhs)

    def gate(sl, y):
        z = xg_ref[:, :, sl] * (_rows_to_tiles(y, tj) + sb_ref[:, sl] * v_ref[:, :, sl])
        z_ref[:, :, sl] = z
        return z

    s0, s1 = _CH_SPLITS
    y0 = conv(s0)
    y1 = conv(s1)
    z0 = gate(s0, y0)
    if fuse_next:
        r0 = _dot(gm, _tiles_to_rows(z0.astype(BF16)))
    z1 = gate(s1, y1)
    if fuse_next:
        _store_twiddled(r0, twr_ref, twi_ref, are_ref, aim_ref, s0, n1h, tj)
        r1 = _dot(gm, _tiles_to_rows(z1.astype(BF16)))
        _store_twiddled(r1, twr_ref, twi_ref, are_ref, aim_ref, s1, n1h, tj)


def _inv_a(bre, bim, g_inv, xg, v, sb_row, tj, nxt=None):
    b, n1h, _, cb = bre.shape
    blk = pl.BlockSpec((None, n1h, tj, cb), lambda bi, ji: (bi, 0, ji, 0))
    in_specs = [_const_spec((n1h, 2 * n1h)), blk, blk, blk, blk, _const_spec((1, cb))]
    args = [g_inv, bre, bim, xg, v, sb_row]
    out_specs = [blk]
    out_shape = [jax.ShapeDtypeStruct(bre.shape, F32)]
    if nxt is not None:
        t_spec = pl.BlockSpec((n1h, tj, LANES), lambda bi, ji: (0, ji, 0))
        in_specs += [_const_spec((2 * n1h, n1h)), t_spec, t_spec]
        args += list(nxt)
        out_specs += [blk, blk]
        out_shape += [jax.ShapeDtypeStruct(bre.shape, BF16)] * 2
    return pl.pallas_call(
        functools.partial(_inva_kernel, n1h=n1h, tj=tj, fuse_next=nxt is not None),
        grid=(b, LANES // tj),
        in_specs=in_specs,
        out_specs=out_specs,
        out_shape=out_shape,
        compiler_params=_cparams("parallel", "parallel"),
        name="idft_stage_a_gate",
    )(*args)


def _attn_kernel(sink_ref, q_ref, kp_ref, kc_ref, kn_ref, vp_ref, vc_ref, vn_ref, o_ref, *, nblk, sub):
    i = pl.program_id(1)
    blk = ATTN_BLOCK
    grp = N_HEADS // N_KV_HEADS
    rows = grp * blk
    qi = lax.broadcasted_iota(jnp.int32, (rows, 3 * blk), 0) % blk
    si = lax.broadcasted_iota(jnp.int32, (rows, 3 * blk), 1)
    mid = (si >= blk) & (si < 2 * blk)
    tri_prev = (si < blk) & (si >= qi)
    tri_next = (si >= 2 * blk) & (si - 2 * blk <= qi)
    hrow = lax.broadcasted_iota(jnp.int32, (rows, 1), 0) // blk
    kcat = jnp.concatenate([kp_ref[...], kc_ref[...], kn_ref[...]], axis=0)
    vcat = jnp.concatenate([vp_ref[...], vc_ref[...], vn_ref[...]], axis=0)
    for s in range(sub):
        j = sub * i + s
        valid = mid | (tri_prev & (j > 0)) | (tri_next & (j < nblk - 1))
        outs = []
        for g in range(N_KV_HEADS):
            lhs = jnp.concatenate([q_ref[s * blk:(s + 1) * blk, (grp * g + t) * LANES:(grp * g + t + 1) * LANES]
                                   for t in range(grp)], axis=0)
            sc = lax.dot_general(lhs, kcat[s * blk:(s + 3) * blk, g * LANES:(g + 1) * LANES],
                                 (((1,), (1,)), ((), ())), preferred_element_type=F32)
            sc = jnp.where(valid, sc, NEG_INF)
            sk = jnp.full((rows, 1), sink_ref[grp * g], F32)
            for t in range(1, grp):
                sk = jnp.where(hrow == t, sink_ref[grp * g + t], sk)
            m = jnp.maximum(jnp.max(sc, axis=1, keepdims=True), sk)
            e = jnp.exp(sc - m)
            den = jnp.sum(e, axis=1, keepdims=True) + jnp.exp(sk - m)
            r = _dot(e.astype(BF16), vcat[s * blk:(s + 3) * blk, g * LANES:(g + 1) * LANES]) / den
            for t in range(0, grp, 2):
                outs.append(r[t * blk:(t + 1) * blk] + pltpu.roll(r[(t + 1) * blk:(t + 2) * blk], HEAD_DIM, 1))
        o_ref[s * blk:(s + 1) * blk, :] = jnp.concatenate(outs, axis=1).astype(BF16)


def _attention(q3, k3, v3, sink):
    b, seq_len, qw = q3.shape
    blk = ATTN_BLOCK
    sub = ATTN_SUB
    nblk = seq_len // blk
    wide = k3.shape[-1]
    prev = lambda bi, i: (bi, jnp.maximum(sub * i - 1, 0), 0)
    cur = lambda bi, i: (bi, i, 0)
    nxt = lambda bi, i: (bi, jnp.minimum(sub * (i + 1), nblk - 1), 0)
    edge = lambda f: pl.BlockSpec((None, blk, wide), f)
    body = pl.BlockSpec((None, sub * blk, wide), cur)
    return pl.pallas_call(
        functools.partial(_attn_kernel, nblk=nblk, sub=sub),
        grid=(b, nblk // sub),
        in_specs=[pl.BlockSpec(memory_space=pltpu.SMEM),
                  pl.BlockSpec((None, sub * blk, qw), cur),
                  edge(prev), body, edge(nxt), edge(prev), body, edge(nxt)],
        out_specs=pl.BlockSpec((None, sub * blk, ATTN_WIDTH), cur),
        out_shape=jax.ShapeDtypeStruct((b, seq_len, ATTN_WIDTH), BF16),
        compiler_params=_cparams("parallel", "parallel"),
        name="banded_attention",
    )(sink, q3, k3, k3, k3, v3, v3, v3)


def _final_kernel(x_ref, yh_ref, ya_ref, g_ref, why_ref, wat_ref, wo_ref, nw_ref, wg_ref, wu_ref, wd_ref,
                  o_ref):
    a = _dot(yh_ref[...].astype(BF16), why_ref[...])
    b = _dot(ya_ref[...], wat_ref[...])
    merged = g_ref[:, :D_MODEL] * a + g_ref[:, D_MODEL:] * b
    x1 = x_ref[...] + _dot(merged.astype(BF16), wo_ref[...])
    ms = jnp.mean(x1 * x1, axis=-1, keepdims=True)
    f = (x1 * lax.rsqrt(ms + RMS_EPS) * nw_ref[...]).astype(BF16)
    acc = x1
    for lo, hi in FFN_CHUNKS:
        gt = _dot(f, wg_ref[:, lo:hi])
        upv = _dot(f, wu_ref[:, lo:hi])
        hid = gt * _sigmoid(gt) * upv
        acc = acc + _dot(hid.astype(BF16), wd_ref[lo:hi, :])
    o_ref[...] = acc


def _final(xf, yh, ya, g, why, wat, wo, nw, wg, wu, wd):
    n = xf.shape[0]
    tm = TM_FINAL
    row = lambda i: (i, 0)
    return pl.pallas_call(
        _final_kernel,
        grid=(n // tm,),
        in_specs=[
            pl.BlockSpec((tm, D_MODEL), row),
            pl.BlockSpec((tm, HYENA_WIDTH), row),
            pl.BlockSpec((tm, ATTN_WIDTH), row),
            pl.BlockSpec((tm, 2 * D_MODEL), row),
            _const_spec((HYENA_WIDTH, D_MODEL)),
            _const_spec((ATTN_WIDTH, D_MODEL)),
            _const_spec((D_MODEL, D_MODEL)),
            _const_spec((1, D_MODEL)),
            _const_spec((D_MODEL, FFN_HIDDEN)),
            _const_spec((D_MODEL, FFN_HIDDEN)),
            _const_spec((FFN_HIDDEN, D_MODEL)),
        ],
        out_specs=pl.BlockSpec((tm, D_MODEL), row),
        out_shape=jax.ShapeDtypeStruct((n, D_MODEL), F32),
        compiler_params=_cparams("parallel"),
        name="merge_ffn",
    )(xf, yh, ya, g, why, wat, wo, nw, wg, wu, wd)


@functools.lru_cache(maxsize=None)
def _dft_constants(n1h):
    n1_len = 2 * n1h
    m = n1_len * LANES
    k1 = np.arange(n1h, dtype=np.float64)[:, None] + 0.5
    n1 = np.arange(n1_len, dtype=np.float64)[None, :]
    gc = np.exp(-2j * np.pi * k1 * n1 / n1_len)
    g_full = np.concatenate([gc.real, gc.imag], axis=0)
    g_inv = (2.0 / m) * np.concatenate([gc.real[:, :n1h].T, gc.imag[:, :n1h].T], axis=1)
    n2 = np.arange(LANES, dtype=np.float64)
    tw = np.exp(-2j * np.pi * k1 * n2[None, :] / m)[:, :, None] * np.ones((1, 1, LANES))
    fc = np.exp(-2j * np.pi * np.outer(n2, n2) / LANES)
    fblk = np.block([[fc.real, -fc.imag], [fc.imag, fc.real]])
    fblk_inv = np.block([[fc.real, fc.imag], [-fc.imag, fc.real]])
    f32 = lambda a: np.ascontiguousarray(a, dtype=np.float32)
    return dict(g_full=f32(g_full), g_half=f32(g_full[:, :n1h]), g_inv=f32(g_inv), twr=f32(tw.real),
                twi=f32(tw.imag), fblk=f32(fblk), fblk_inv=f32(fblk_inv))


@functools.lru_cache(maxsize=None)
def _filter_features(seq_len):
    n1h = seq_len // LANES
    n2 = np.arange(LANES)[:, None]
    n1 = np.arange(n1h)[None, :]
    slot_f = (LANES * n1 + n2).reshape(-1)
    slot_b = slot_f + seq_len
    neg_lag = slot_b > seq_len
    pos_b = np.where(neg_lag, 2 * seq_len - slot_b, 0)
    bands = np.linspace(1e-4, FILTER_BANDS - 1, FILTER_BANDS)

    def feats(pos, sgn):
        pos = pos.astype(np.float64)
        ang = (2.0 * math.pi / seq_len) * pos[:, None] * bands[None, :]
        pad = np.zeros((pos.shape[0], FILT_PAD - FILTER_EMB - 1))
        return np.concatenate([(pos / (seq_len - 1))[:, None], np.cos(ang), -np.sin(ang), sgn[:, None], pad], axis=-1)

    z = np.concatenate([feats(slot_f, np.ones(slot_f.shape)), feats(pos_b, np.where(neg_lag, -1.0, 0.0))], axis=-1)
    return np.ascontiguousarray(z, dtype=np.float32)


@functools.lru_cache(maxsize=None)
def _rope_tables(seq_len):
    inv = ROPE_THETA ** (-np.arange(0, HEAD_DIM, 2, dtype=np.float64) / HEAD_DIM)
    ang = np.arange(seq_len, dtype=np.float64)[:, None] * inv[None, :]
    cos, sin = np.cos(ang), np.sin(ang)
    zero = np.zeros_like(sin)
    reps = LANES // HEAD_DIM
    f32 = lambda a: np.ascontiguousarray(np.tile(a, (1, reps)), dtype=np.float32)
    return f32(np.concatenate([cos, cos], axis=1)), f32(np.concatenate([-sin, zero], axis=1)), \
        f32(np.concatenate([zero, sin], axis=1))


def _layer(x, p):
    b, seq_len, _ = x.shape
    n = b * seq_len
    n1h = seq_len // LANES
    xf = x.reshape(n, D_MODEL)
    c = {k: jnp.asarray(v) for k, v in _dft_constants(n1h).items()}
    tj = SUBLANES if n1h >= LANES else 4 * SUBLANES
    kb = SUBLANES
    r4 = lambda a: a.reshape(b, n1h, LANES, HYENA_WIDTH)

    cos, sa, sb = (jnp.asarray(t) for t in _rope_tables(seq_len))
    x1, x2, v, q, k4, v4, g = _inproj(xf, seq_len, p["attn_norm_w"], p["w_in"], p["conv_w"], p["conv_b"],
                                      p["q_norm_w"], p["k_norm_w"], cos, sa, sb, p["ones_bd"])

    afre, afim, asum = _filter_stage_a(jnp.asarray(_filter_features(seq_len)), p["filt_w1"], p["filt_b1"],
                                       p["filt_w2"], p["filt_b2"], p["filt_w3f"], p["filt_w3b"], p["filt_freq"],
                                       p["decay"], c["g_full"], c["twr"], c["twi"], n1h, tj)
    hspec = _filter_spectrum(afre, afim, c["fblk"], asum, kb)

    are, aim = _fwd_a(r4(v), c["g_half"], c["twr"], c["twi"], tj)
    bre, bim = _mid(are, aim, c["fblk"], c["fblk_inv"], c["twr"], c["twi"], hspec, 0, kb)
    z1, are, aim = _inv_a(bre, bim, c["g_inv"], r4(x1), r4(v), p["hyena_bias"][0:1], tj,
                          nxt=(c["g_half"], c["twr"], c["twi"]))
    bre, bim = _mid(are, aim, c["fblk"], c["fblk_inv"], c["twr"], c["twi"], hspec, 1, kb)
    (yh,) = _inv_a(bre, bim, c["g_inv"], r4(x2), z1, p["hyena_bias"][1:2], tj)

    ya = _attention(q.reshape(b, seq_len, 2 * ATTN_WIDTH), k4.reshape(b, seq_len, 2 * LANES),
                    v4.reshape(b, seq_len, 2 * LANES), p["attn_sink"])

    out = _final(xf, yh.reshape(n, HYENA_WIDTH), ya.reshape(n, ATTN_WIDTH), g, p["w_hy_out"], p["w_at_out"],
                 p["w_o"], p["ffn_norm_w"], p["w_gate"], p["w_up"], p["w_down"])
    return out.reshape(b, seq_len, D_MODEL)


def _block_diag2(w):
    z = jnp.zeros_like(w)
    return jnp.concatenate([jnp.concatenate([w, z], axis=1), jnp.concatenate([z, w], axis=1)], axis=0)


def kernel(x_prompt, x_sample, attn_norm_w, w_in, hyena_conv_w, hyena_conv_b, filt_w1, filt_b1, filt_w2, filt_b2,
           filt_w3, filt_freq, hyena_bias, q_norm_w, k_norm_w, attn_sink, w_hy_out, w_at_out, w_o, ffn_norm_w,
           w_gate, w_up, w_down):
    cw = HYENA_ORDER * HYENA_WIDTH
    max_decay = math.log(DECAY_TARGET) / DECAY_FAST_PCT
    min_decay = math.log(DECAY_TARGET) / DECAY_SLOW_PCT
    deltas = jnp.abs(jnp.linspace(min_decay, max_decay, HYENA_WIDTH, dtype=F32))
    head = np.arange(ATTN_WIDTH) // HEAD_DIM
    ones_bd = jnp.asarray((head[:, None] == head[None, :]).astype(np.float32) / HEAD_DIM).astype(BF16)
    w3 = filt_w3[0].reshape(FILTER_HIDDEN, HYENA_ORDER, 2, HYENA_WIDTH).transpose(2, 0, 1, 3)
    w3 = w3.reshape(2, FILTER_HIDDEN, cw)
    w3_zero = jnp.zeros((FILTER_HIDDEN, cw), F32)
    twice = lambda a: jnp.tile(a, 2)[None, :]
    p = dict(
        attn_norm_w=attn_norm_w[0][None, :],
        w_in=w_in[0].astype(BF16),
        conv_w=hyena_conv_w[0],
        conv_b=hyena_conv_b[0][None, :],
        filt_w1=_block_diag2(jnp.pad(filt_w1[0], ((0, FILT_PAD - FILTER_EMB), (0, 0)))),
        filt_b1=twice(filt_b1[0]),
        filt_w2=_block_diag2(filt_w2[0]),
        filt_b2=twice(filt_b2[0]),
        filt_w3f=jnp.concatenate([w3[0], w3_zero], axis=0),
        filt_w3b=jnp.concatenate([w3_zero, w3[1]], axis=0),
        filt_freq=twice(filt_freq[0]),
        decay=deltas[None, :],
        hyena_bias=hyena_bias[0],
        q_norm_w=jnp.tile(q_norm_w[0], N_HEADS)[None, :],
        k_norm_w=jnp.tile(k_norm_w[0], N_KV_HEADS)[None, :],
        attn_sink=attn_sink[0],
        ones_bd=ones_bd,
        w_hy_out=w_hy_out[0].astype(BF16),
        w_at_out=w_at_out[0].astype(BF16),
        w_o=w_o[0].astype(BF16),
        ffn_norm_w=ffn_norm_w[0][None, :],
        w_gate=w_gate[0].astype(BF16),
        w_up=w_up[0].astype(BF16),
        w_down=w_down[0].astype(BF16),
    )
    return (_layer(x_prompt, p), _layer(x_sample, p))
```

```python
import functools
import math

import numpy as np
import jax
import jax.numpy as jnp
from jax import lax
from jax.experimental import pallas as pl
from jax.experimental.pallas import tpu as pltpu

F32 = jnp.float32
BF16 = jnp.bfloat16

D_MODEL = 1024
HYENA_WIDTH = 512
HYENA_ORDER = 2
FILTER_BANDS = 16
FILTER_EMB = 1 + 2 * FILTER_BANDS
FILTER_HIDDEN = 64
DECAY_FAST_PCT = 0.3
DECAY_SLOW_PCT = 1.5
DECAY_TARGET = 1e-2
DECAY_SHIFT = 0.05
N_HEADS = 8
N_KV_HEADS = 2
HEAD_DIM = 64
ATTN_WIDTH = N_HEADS * HEAD_DIM
KV_WIDTH = N_KV_HEADS * HEAD_DIM
WINDOW = 128
ROPE_THETA = 10000.0
FFN_HIDDEN = 2816
RMS_EPS = 1e-6
NEG_INF = -1e30
LOG2E = math.log2(math.e)

C_HY = 3 * HYENA_WIDTH
C_Q = C_HY + ATTN_WIDTH
C_K = C_Q + KV_WIDTH
C_V = C_K + KV_WIDTH
IN_WIDTH = C_V + 2 * D_MODEL

LANES = 128
SUBLANES = 8
ATTN_BLOCK = 128
ATTN_SUB = 4
VMEM_LIMIT = 56 * 1024 * 1024

TM_INPROJ = 512
TM_FINAL = 512
FFN_CHUNKS = ((0, 1280), (1280, FFN_HIDDEN))
_CH_SPLITS = (slice(0, HYENA_WIDTH // 2), slice(HYENA_WIDTH // 2, HYENA_WIDTH))
FILT_PAD = 64


def _cparams(*sem):
    return pltpu.CompilerParams(dimension_semantics=sem, vmem_limit_bytes=VMEM_LIMIT)


def _const_spec(shape):
    nd = len(shape)
    return pl.BlockSpec(shape, lambda *_: (0,) * nd, pipeline_mode=pl.Buffered(1))


def _dot(a, b):
    return jnp.dot(a, b, preferred_element_type=F32)


def _sigmoid(x):
    return 0.5 * jnp.tanh(0.5 * x) + 0.5


def _split(a):
    hi = a.astype(BF16)
    lo = (a - hi.astype(F32)).astype(BF16)
    return hi, lo


def _dot3(a, w):
    ah, al = _split(a)
    wh, wl = _split(w)
    return _dot(ah, wh) + _dot(al, wh) + _dot(ah, wl)


def _inproj_kernel(x_ref, xp_ref, xn_ref, nw_ref, w_ref, cw_ref, cb_ref, qw_ref, kw_ref,
                   cos_ref, sa_ref, sb_ref, ones_ref,
                   x1_ref, x2_ref, v_ref, q_ref, k4_ref, v4_ref, g_ref, pad_ref, *, tm, tiles_per_seq):
    pos = pl.program_id(0) % tiles_per_seq
    nw = nw_ref[...]

    def norm(xv):
        ms = jnp.mean(xv * xv, axis=-1, keepdims=True)
        return (xv * lax.rsqrt(ms + RMS_EPS) * nw).astype(BF16)

    h_all = norm(jnp.concatenate([x_ref[...], xp_ref[...], xn_ref[...]], axis=0))
    h = h_all[:tm]

    hy_all = _dot(h_all, w_ref[:, :C_HY])
    hy = hy_all[:tm]
    hyh = hy_all[tm:]
    prev_ok = (pos > 0).astype(F32)
    next_ok = (pos < tiles_per_seq - 1).astype(F32)
    pad_ref[0:SUBLANES] = hyh[0:SUBLANES] * prev_ok
    pad_ref[SUBLANES:SUBLANES + tm] = hy
    pad_ref[SUBLANES + tm:2 * SUBLANES + tm] = hyh[SUBLANES:] * next_ok
    up = pad_ref[SUBLANES - 1:SUBLANES - 1 + tm]
    un = pad_ref[SUBLANES + 1:SUBLANES + 1 + tm]
    cw = cw_ref[...]
    uc = cw[0:1] * up + cw[1:2] * hy + cw[2:3] * un + cb_ref[...]
    x1_ref[...] = uc[:, :HYENA_WIDTH].astype(BF16)
    x2_ref[...] = uc[:, HYENA_WIDTH:2 * HYENA_WIDTH].astype(BF16)
    v_ref[...] = uc[:, 2 * HYENA_WIDTH:].astype(BF16)

    def head_norm_rope(t, wrow, ones, reps):
        ms = _dot((t * t).astype(BF16), ones)
        tn = t * lax.rsqrt(ms + RMS_EPS) * wrow
        width = t.shape[1]
        cos = jnp.concatenate([cos_ref[...]] * reps, axis=1)
        sa = jnp.concatenate([sa_ref[...]] * reps, axis=1)
        sb = jnp.concatenate([sb_ref[...]] * reps, axis=1)
        half = HEAD_DIM // 2
        return tn * cos + pltpu.roll(tn, width - half, 1) * sa + pltpu.roll(tn, half, 1) * sb

    qkv = _dot(h, w_ref[:, C_HY:C_V])
    q = qkv[:, :ATTN_WIDTH]
    q = head_norm_rope(q, qw_ref[...], ones_ref[...], ATTN_WIDTH // LANES) * (HEAD_DIM ** -0.5 * LOG2E)
    blocks = []
    for p in range(N_HEADS // 2):
        qp = q[:, p * LANES:(p + 1) * LANES]
        blocks += [qp, pltpu.roll(qp, HEAD_DIM, 1)]
    q_ref[...] = jnp.concatenate(blocks, axis=1).astype(BF16)

    k = qkv[:, ATTN_WIDTH:ATTN_WIDTH + KV_WIDTH]
    k = head_norm_rope(k, kw_ref[...], ones_ref[:KV_WIDTH, :KV_WIDTH], 1)
    v = qkv[:, ATTN_WIDTH + KV_WIDTH:]

    lo_half = lax.broadcasted_iota(jnp.int32, (tm, KV_WIDTH), 1) < HEAD_DIM

    def spread(t):
        zero = jnp.zeros_like(t)
        return jnp.concatenate([jnp.where(lo_half, t, zero), jnp.where(lo_half, pltpu.roll(t, HEAD_DIM, 1), zero)],
                               axis=1).astype(BF16)

    k4_ref[...] = spread(k)
    v4_ref[...] = spread(v)
    g_ref[...] = _sigmoid(_dot(h, w_ref[:, C_V:]))


def _inproj(xf, seq_len, nw, w_in_bf, cw, cb, qw, kw, cos, sa, sb, ones_bd):
    n = xf.shape[0]
    tm = TM_INPROJ
    tiles_per_seq = seq_len // tm
    nblk8 = n // SUBLANES
    r8 = tm // SUBLANES
    kern = functools.partial(_inproj_kernel, tm=tm, tiles_per_seq=tiles_per_seq)
    row = lambda i: (i, 0)
    return pl.pallas_call(
        kern,
        grid=(n // tm,),
        in_specs=[
            pl.BlockSpec((tm, D_MODEL), row),
            pl.BlockSpec((SUBLANES, D_MODEL), lambda i: (jnp.maximum(i * r8 - 1, 0), 0)),
            pl.BlockSpec((SUBLANES, D_MODEL), lambda i: (jnp.minimum((i + 1) * r8, nblk8 - 1), 0)),
            _const_spec((1, D_MODEL)),
            _const_spec((D_MODEL, IN_WIDTH)),
            _const_spec((3, C_HY)),
            _const_spec((1, C_HY)),
            _const_spec((1, ATTN_WIDTH)),
            _const_spec((1, KV_WIDTH)),
            pl.BlockSpec((tm, LANES), lambda i: (i % tiles_per_seq, 0)),
            pl.BlockSpec((tm, LANES), lambda i: (i % tiles_per_seq, 0)),
            pl.BlockSpec((tm, LANES), lambda i: (i % tiles_per_seq, 0)),
            _const_spec((ATTN_WIDTH, ATTN_WIDTH)),
        ],
        out_specs=[
            pl.BlockSpec((tm, HYENA_WIDTH), row),
            pl.BlockSpec((tm, HYENA_WIDTH), row),
            pl.BlockSpec((tm, HYENA_WIDTH), row),
            pl.BlockSpec((tm, 2 * ATTN_WIDTH), row),
            pl.BlockSpec((tm, 2 * LANES), row),
            pl.BlockSpec((tm, 2 * LANES), row),
            pl.BlockSpec((tm, 2 * D_MODEL), row),
        ],
        out_shape=[
            jax.ShapeDtypeStruct((n, HYENA_WIDTH), BF16),
            jax.ShapeDtypeStruct((n, HYENA_WIDTH), BF16),
            jax.ShapeDtypeStruct((n, HYENA_WIDTH), BF16),
            jax.ShapeDtypeStruct((n, 2 * ATTN_WIDTH), BF16),
            jax.ShapeDtypeStruct((n, 2 * LANES), BF16),
            jax.ShapeDtypeStruct((n, 2 * LANES), BF16),
            jax.ShapeDtypeStruct((n, 2 * D_MODEL), F32),
        ],
        scratch_shapes=[pltpu.VMEM((tm + 2 * SUBLANES, C_HY), F32)],
        compiler_params=_cparams("parallel"),
        name="inproj",
    )(xf, xf, xf, nw, w_in_bf, cw, cb, qw, kw, cos, sa, sb, ones_bd)


def _rows_to_tiles(r, tj):
    return r.reshape(r.shape[0], tj, r.shape[1] // tj)


def _tiles_to_rows(t):
    return t.reshape(t.shape[0], t.shape[1] * t.shape[2])


def _twiddle_tiles(r, twr_ref, twi_ref, n1h, tj):
    c = r.shape[1] // tj
    re, im = [], []
    for j in range(tj):
        ar, ai = r[:n1h, j * c:(j + 1) * c], r[n1h:, j * c:(j + 1) * c]
        tr, ti = twr_ref[:, j:j + 1], twi_ref[:, j:j + 1]
        re.append(ar * tr - ai * ti)
        im.append(ar * ti + ai * tr)
    return (_rows_to_tiles(jnp.concatenate(re, axis=1), tj).astype(BF16),
            _rows_to_tiles(jnp.concatenate(im, axis=1), tj).astype(BF16))


def _store_twiddled(r, twr_ref, twi_ref, are_ref, aim_ref, sl, n1h, tj):
    re, im = _twiddle_tiles(r, twr_ref, twi_ref, n1h, tj)
    are_ref[:, :, sl] = re
    aim_ref[:, :, sl] = im


def _filt_kernel(z_ref, w1_ref, b1_ref, w2_ref, b2_ref, w3f_ref, w3b_ref, fr_ref, dl_ref, g_ref, twr_ref, twi_ref,
                 are_ref, aim_ref, s_ref, *, n1h, tj):
    i = pl.program_id(0)
    z = z_ref[...]
    fr = fr_ref[...]
    h = jnp.sin(fr * (_dot3(z, w1_ref[...]) + b1_ref[...]))
    h = jnp.sin(fr * (_dot3(h, w2_ref[...]) + b2_ref[...]))
    hb = h.astype(BF16)
    tf = z[:, 0:1]
    tb = z[:, FILT_PAD:FILT_PAD + 1]
    sgn = z[:, FILT_PAD + FILTER_EMB:FILT_PAD + FILTER_EMB + 1]
    gm = g_ref[...].astype(BF16)
    cb = HYENA_WIDTH
    parts = []
    for o in range(HYENA_ORDER):
        sl = slice(o * cb, (o + 1) * cb)
        dl = dl_ref[...]

        def taps(w_ref, t):
            return _dot(hb, w_ref[:, sl].astype(BF16)) * (jnp.exp(-t * dl) + DECAY_SHIFT)

        kf = taps(w3f_ref, tf)
        kb = taps(w3b_ref, tb) * sgn
        parts.append(jnp.sum(jnp.abs(kf), axis=0, keepdims=True) + jnp.sum(jnp.abs(kb), axis=0, keepdims=True))
        cols = [jnp.concatenate([kf[j * n1h:(j + 1) * n1h], kb[j * n1h:(j + 1) * n1h]], axis=0).astype(BF16)
                for j in range(tj)]
        re, im = _twiddle_tiles(_dot(gm, jnp.concatenate(cols, axis=1)), twr_ref, twi_ref, n1h, tj)
        are_ref[:, :, sl] = re
        aim_ref[:, :, sl] = im
    part = jnp.concatenate(parts, axis=1)

    @pl.when(i == 0)
    def _():
        s_ref[...] = part

    @pl.when(i > 0)
    def _():
        s_ref[...] += part


def _filter_stage_a(zf, w1bd, b1, w2bd, b2, w3f, w3b, fr, dl, g_full, twr, twi, n1h, tj):
    cw = HYENA_ORDER * HYENA_WIDTH
    hid2 = 2 * FILTER_HIDDEN
    a_shape = jax.ShapeDtypeStruct((n1h, LANES, cw), BF16)
    a_spec = pl.BlockSpec((n1h, tj, cw), lambda i: (0, i, 0))
    t_spec = pl.BlockSpec((None, n1h, tj), lambda i: (i, 0, 0))
    return pl.pallas_call(
        functools.partial(_filt_kernel, n1h=n1h, tj=tj),
        grid=(LANES // tj,),
        in_specs=[
            pl.BlockSpec((tj * n1h, 2 * FILT_PAD), lambda i: (i, 0)),
            _const_spec((2 * FILT_PAD, hid2)),
            _const_spec((1, hid2)),
            _const_spec((hid2, hid2)),
            _const_spec((1, hid2)),
            _const_spec((hid2, cw)),
            _const_spec((hid2, cw)),
            _const_spec((1, hid2)),
            _const_spec((1, HYENA_WIDTH)),
            _const_spec((2 * n1h, 2 * n1h)),
            t_spec, t_spec,
        ],
        out_specs=[a_spec, a_spec, pl.BlockSpec((1, cw), lambda i: (0, 0))],
        out_shape=[a_shape, a_shape, jax.ShapeDtypeStruct((1, cw), F32)],
        compiler_params=_cparams("arbitrary"),
        name="filter_taps_stage_a",
    )(zf, w1bd, b1, w2bd, b2, w3f, w3b, fr, dl, g_full, twr, twi)


def _midf_kernel(f_ref, s_ref, are_ref, aim_ref, h_ref, *, kb):
    fm = f_ref[...].astype(BF16)
    inv = 1.0 / s_ref[...]
    for k in range(kb):
        h_ref[k] = (_dot(fm, jnp.concatenate([are_ref[k], aim_ref[k]], axis=0)) * inv).astype(BF16)


def _filter_spectrum(afre, afim, fblk, asum, kb):
    n1h, _, ctot = afre.shape
    cb = HYENA_WIDTH
    a_spec = pl.BlockSpec((kb, LANES, cb), lambda ki, ci: (ki, 0, ci))
    return pl.pallas_call(
        functools.partial(_midf_kernel, kb=kb),
        grid=(n1h // kb, ctot // cb),
        in_specs=[_const_spec((2 * LANES, 2 * LANES)), pl.BlockSpec((1, cb), lambda ki, ci: (0, ci)), a_spec, a_spec],
        out_specs=pl.BlockSpec((kb, 2 * LANES, cb), lambda ki, ci: (ki, 0, ci)),
        out_shape=jax.ShapeDtypeStruct((n1h, 2 * LANES, ctot), BF16),
        compiler_params=_cparams("parallel", "parallel"),
        name="filter_spectrum",
    )(fblk, asum, afre, afim)


def _fwda_kernel(g_ref, twr_ref, twi_ref, u_ref, are_ref, aim_ref, *, n1h, tj):
    gm = g_ref[...].astype(BF16)
    prev = None
    for sl in _CH_SPLITS:
        u = _tiles_to_rows(u_ref[:, :, sl])
        if prev is not None:
            _store_twiddled(prev[1], twr_ref, twi_ref, are_ref, aim_ref, prev[0], n1h, tj)
        prev = (sl, _dot(gm, u))
    _store_twiddled(prev[1], twr_ref, twi_ref, are_ref, aim_ref, prev[0], n1h, tj)


def _fwd_a(u4, g_half, twr, twi, tj):
    b, n1h, _, cb = u4.shape
    blk = pl.BlockSpec((None, n1h, tj, cb), lambda bi, ji: (bi, 0, ji, 0))
    t_spec = pl.BlockSpec((None, n1h, tj), lambda bi, ji: (ji, 0, 0))
    a_shape = jax.ShapeDtypeStruct(u4.shape, BF16)
    return pl.pallas_call(
        functools.partial(_fwda_kernel, n1h=n1h, tj=tj),
        grid=(b, LANES // tj),
        in_specs=[_const_spec((2 * n1h, n1h)), t_spec, t_spec, blk],
        out_specs=[blk, blk],
        out_shape=[a_shape, a_shape],
        compiler_params=_cparams("parallel", "parallel"),
        name="dft_stage_a",
    )(g_half, twr, twi, u4)


def _mid_kernel(f_ref, fi_ref, twr_ref, twi_ref, h_ref, are_ref, aim_ref, bre_ref, bim_ref, *, kb):
    fm = f_ref[...].astype(BF16)
    fim = fi_ref[...].astype(BF16)
    reps = are_ref.shape[-1] // LANES
    def fwd(k):
        return _dot(fm, jnp.concatenate([are_ref[k], aim_ref[k]], axis=0))

    def spec(k, x):
        xr, xi = x[:LANES], x[LANES:]
        hr, hi = h_ref[k, :LANES], h_ref[k, LANES:]
        return jnp.concatenate([xr * hr - xi * hi, xr * hi + xi * hr], axis=0).astype(BF16)

    def out(k, bc):
        br, bi = bc[:LANES], bc[LANES:]
        tr = jnp.concatenate([twr_ref[k]] * reps, axis=1)
        ti = jnp.concatenate([twi_ref[k]] * reps, axis=1)
        bre_ref[k] = (br * tr + bi * ti).astype(BF16)
        bim_ref[k] = (bi * tr - br * ti).astype(BF16)

    x = fwd(0)
    bc_prev = None
    for k in range(kb):
        y = spec(k, x)
        if k + 1 < kb:
            x = fwd(k + 1)
        if bc_prev is not None:
            out(k - 1, bc_prev)
        bc_prev = _dot(fim, y)
    out(kb - 1, bc_prev)


def _mid(are, aim, fblk, fblk_inv, twr, twi, hspec, order, kb):
    b, n1h, _, cb = are.shape
    a_spec = pl.BlockSpec((None, kb, LANES, cb), lambda ki, bi: (bi, ki, 0, 0))
    t_spec = pl.BlockSpec((kb, LANES, LANES), lambda ki, bi: (ki, 0, 0))
    shape = jax.ShapeDtypeStruct(are.shape, BF16)
    return pl.pallas_call(
        functools.partial(_mid_kernel, kb=kb),
        grid=(n1h // kb, b),
        in_specs=[
            _const_spec((2 * LANES, 2 * LANES)),
            _const_spec((2 * LANES, 2 * LANES)),
            t_spec, t_spec,
            pl.BlockSpec((kb, 2 * LANES, cb), lambda ki, bi: (ki, 0, order)),
            a_spec, a_spec,
        ],
        out_specs=[a_spec, a_spec],
        out_shape=[shape, shape],
        compiler_params=_cparams("parallel", "parallel"),
        name="dft_stage_c",
    )(fblk, fblk_inv, twr, twi, hspec, are, aim)


def _inva_kernel(gi_ref, bre_ref, bim_ref, xg_ref, v_ref, sb_ref, *rest, n1h, tj, fuse_next):
    gim = gi_ref[...].astype(BF16)
    if fuse_next:
        g_ref, twr_ref, twi_ref, z_ref, are_ref, aim_ref = rest
        gm = g_ref[...].astype(BF16)
    else:
        (z_ref,) = rest

    def conv(sl):
        rhs = jnp.concatenate([_tiles_to_rows(bre_ref[:, :, sl]), _tiles_to_rows(bim_ref[:, :, sl])], axis=0)
        return _dot(gim, rhs)

    def gate(sl, y):
        z = xg_ref[:, :, sl].astype(F32) * (_rows_to_tiles(y, tj) + sb_ref[:, sl] * v_ref[:, :, sl].astype(F32))
        z = z.astype(BF16)
        z_ref[:, :, sl] = z
        return z

    s0, s1 = _CH_SPLITS
    y0 = conv(s0)
    y1 = conv(s1)
    z0 = gate(s0, y0)
    if fuse_next:
        r0 = _dot(gm, _tiles_to_rows(z0))
    z1 = gate(s1, y1)
    if fuse_next:
        _store_twiddled(r0, twr_ref, twi_ref, are_ref, aim_ref, s0, n1h, tj)
        r1 = _dot(gm, _tiles_to_rows(z1))
        _store_twiddled(r1, twr_ref, twi_ref, are_ref, aim_ref, s1, n1h, tj)


def _inv_a(bre, bim, g_inv, xg, v, sb_row, tj, nxt=None):
    b, n1h, _, cb = bre.shape
    blk = pl.BlockSpec((None, n1h, tj, cb), lambda bi, ji: (bi, 0, ji, 0))
    in_specs = [_const_spec((n1h, 2 * n1h)), blk, blk, blk, blk, _const_spec((1, cb))]
    args = [g_inv, bre, bim, xg, v, sb_row]
    out_specs = [blk]
    out_shape = [jax.ShapeDtypeStruct(bre.shape, BF16)]
    if nxt is not None:
        t_spec = pl.BlockSpec((None, n1h, tj), lambda bi, ji: (ji, 0, 0))
        in_specs += [_const_spec((2 * n1h, n1h)), t_spec, t_spec]
        args += list(nxt)
        out_specs += [blk, blk]
        out_shape += [jax.ShapeDtypeStruct(bre.shape, BF16)] * 2
    return pl.pallas_call(
        functools.partial(_inva_kernel, n1h=n1h, tj=tj, fuse_next=nxt is not None),
        grid=(b, LANES // tj),
        in_specs=in_specs,
        out_specs=out_specs,
        out_shape=out_shape,
        compiler_params=_cparams("parallel", "parallel"),
        name="idft_stage_a_gate",
    )(*args)


def _attn_kernel(sink_ref, q_ref, kp_ref, kc_ref, kn_ref, vp_ref, vc_ref, vn_ref, o_ref, *, nblk, sub):
    i = pl.program_id(1)
    blk = ATTN_BLOCK
    grp = N_HEADS // N_KV_HEADS
    rows = grp * blk
    qi = lax.broadcasted_iota(jnp.int32, (rows, 3 * blk), 0) % blk
    si = lax.broadcasted_iota(jnp.int32, (rows, 3 * blk), 1)
    mid = (si >= blk) & (si < 2 * blk)
    tri_prev = (si < blk) & (si >= qi)
    tri_next = (si >= 2 * blk) & (si - 2 * blk <= qi)
    hrow = lax.broadcasted_iota(jnp.int32, (rows, 1), 0) // blk
    kcat = jnp.concatenate([kp_ref[...], kc_ref[...], kn_ref[...]], axis=0)
    vcat = jnp.concatenate([vp_ref[...], vc_ref[...], vn_ref[...]], axis=0)
    for s in range(sub):
        j = sub * i + s
        valid = mid | (tri_prev & (j > 0)) | (tri_next & (j < nblk - 1))
        outs = []
        for g in range(N_KV_HEADS):
            lhs = jnp.concatenate([q_ref[s * blk:(s + 1) * blk, (grp * g + t) * LANES:(grp * g + t + 1) * LANES]
                                   for t in range(grp)], axis=0)
            sc = lax.dot_general(lhs, kcat[s * blk:(s + 3) * blk, g * LANES:(g + 1) * LANES],
                                 (((1,), (1,)), ((), ())), preferred_element_type=F32)
            sc = jnp.where(valid, sc, NEG_INF)
            sk = jnp.full((rows, 1), sink_ref[grp * g], F32)
            for t in range(1, grp):
                sk = jnp.where(hrow == t, sink_ref[grp * g + t], sk)
            sk = sk * LOG2E
            m = jnp.maximum(jnp.max(sc, axis=1, keepdims=True), sk)
            e = jnp.exp2(sc - m)
            den = jnp.sum(e, axis=1, keepdims=True) + jnp.exp2(sk - m)
            r = _dot(e.astype(BF16), vcat[s * blk:(s + 3) * blk, g * LANES:(g + 1) * LANES]) / den
            for t in range(0, grp, 2):
                outs.append(r[t * blk:(t + 1) * blk] + pltpu.roll(r[(t + 1) * blk:(t + 2) * blk], HEAD_DIM, 1))
        o_ref[s * blk:(s + 1) * blk, :] = jnp.concatenate(outs, axis=1).astype(BF16)


def _attention(q3, k3, v3, sink):
    b, seq_len, qw = q3.shape
    blk = ATTN_BLOCK
    sub = ATTN_SUB
    nblk = seq_len // blk
    wide = k3.shape[-1]
    prev = lambda bi, i: (bi, jnp.maximum(sub * i - 1, 0), 0)
    cur = lambda bi, i: (bi, i, 0)
    nxt = lambda bi, i: (bi, jnp.minimum(sub * (i + 1), nblk - 1), 0)
    edge = lambda f: pl.BlockSpec((None, blk, wide), f)
    body = pl.BlockSpec((None, sub * blk, wide), cur)
    return pl.pallas_call(
        functools.partial(_attn_kernel, nblk=nblk, sub=sub),
        grid=(b, nblk // sub),
        in_specs=[pl.BlockSpec(memory_space=pltpu.SMEM),
                  pl.BlockSpec((None, sub * blk, qw), cur),
                  edge(prev), body, edge(nxt), edge(prev), body, edge(nxt)],
        out_specs=pl.BlockSpec((None, sub * blk, ATTN_WIDTH), cur),
        out_shape=jax.ShapeDtypeStruct((b, seq_len, ATTN_WIDTH), BF16),
        compiler_params=_cparams("parallel", "parallel"),
        name="banded_attention",
    )(sink, q3, k3, k3, k3, v3, v3, v3)


def _final_kernel(x_ref, yh_ref, ya_ref, g_ref, why_ref, wat_ref, wo_ref, nw_ref, wg_ref, wu_ref, wd_ref,
                  o_ref):
    a = _dot(yh_ref[...], why_ref[...])
    b = _dot(ya_ref[...], wat_ref[...])
    merged = g_ref[:, :D_MODEL] * a + g_ref[:, D_MODEL:] * b
    x1 = x_ref[...] + _dot(merged.astype(BF16), wo_ref[...])
    ms = jnp.mean(x1 * x1, axis=-1, keepdims=True)
    f = (x1 * lax.rsqrt(ms + RMS_EPS) * nw_ref[...]).astype(BF16)
    acc = x1
    for lo, hi in FFN_CHUNKS:
        gt = _dot(f, wg_ref[:, lo:hi])
        upv = _dot(f, wu_ref[:, lo:hi])
        hid = gt * _sigmoid(gt) * upv
        acc = acc + _dot(hid.astype(BF16), wd_ref[lo:hi, :])
    o_ref[...] = acc


def _final(xf, yh, ya, g, why, wat, wo, nw, wg, wu, wd):
    n = xf.shape[0]
    tm = TM_FINAL
    row = lambda i: (i, 0)
    return pl.pallas_call(
        _final_kernel,
        grid=(n // tm,),
        in_specs=[
            pl.BlockSpec((tm, D_MODEL), row),
            pl.BlockSpec((tm, HYENA_WIDTH), row),
            pl.BlockSpec((tm, ATTN_WIDTH), row),
            pl.BlockSpec((tm, 2 * D_MODEL), row),
            _const_spec((HYENA_WIDTH, D_MODEL)),
            _const_spec((ATTN_WIDTH, D_MODEL)),
            _const_spec((D_MODEL, D_MODEL)),
            _const_spec((1, D_MODEL)),
            _const_spec((D_MODEL, FFN_HIDDEN)),
            _const_spec((D_MODEL, FFN_HIDDEN)),
            _const_spec((FFN_HIDDEN, D_MODEL)),
        ],
        out_specs=pl.BlockSpec((tm, D_MODEL), row),
        out_shape=jax.ShapeDtypeStruct((n, D_MODEL), F32),
        compiler_params=_cparams("parallel"),
        name="merge_ffn",
    )(xf, yh, ya, g, why, wat, wo, nw, wg, wu, wd)


@functools.lru_cache(maxsize=None)
def _dft_constants(n1h, tj):
    n1_len = 2 * n1h
    m = n1_len * LANES
    k1 = np.arange(n1h, dtype=np.float64)[:, None] + 0.5
    n1 = np.arange(n1_len, dtype=np.float64)[None, :]
    gc = np.exp(-2j * np.pi * k1 * n1 / n1_len)
    g_full = np.concatenate([gc.real, gc.imag], axis=0)
    g_inv = (2.0 / m) * np.concatenate([gc.real[:, :n1h].T, gc.imag[:, :n1h].T], axis=1)
    n2 = np.arange(LANES, dtype=np.float64)
    tw = np.exp(-2j * np.pi * k1 * n2[None, :] / m)[:, :, None] * np.ones((1, 1, LANES))
    fc = np.exp(-2j * np.pi * np.outer(n2, n2) / LANES)
    fblk = np.block([[fc.real, -fc.imag], [fc.imag, fc.real]])
    fblk_inv = np.block([[fc.real, fc.imag], [-fc.imag, fc.real]])
    f32 = lambda a: np.ascontiguousarray(a, dtype=np.float32)
    tiled = lambda a: a[:, :, 0].reshape(n1h, LANES // tj, tj).transpose(1, 0, 2)
    return dict(g_full=f32(g_full), g_half=f32(g_full[:, :n1h]), g_inv=f32(g_inv), twr=f32(tw.real),
                twi=f32(tw.imag), twr_a=f32(tiled(tw.real)), twi_a=f32(tiled(tw.imag)), fblk=f32(fblk),
                fblk_inv=f32(fblk_inv))


@functools.lru_cache(maxsize=None)
def _filter_features(seq_len):
    n1h = seq_len // LANES
    n2 = np.arange(LANES)[:, None]
    n1 = np.arange(n1h)[None, :]
    slot_f = (LANES * n1 + n2).reshape(-1)
    slot_b = slot_f + seq_len
    neg_lag = slot_b > seq_len
    pos_b = np.where(neg_lag, 2 * seq_len - slot_b, 0)
    bands = np.linspace(1e-4, FILTER_BANDS - 1, FILTER_BANDS)

    def feats(pos, sgn):
        pos = pos.astype(np.float64)
        ang = (2.0 * math.pi / seq_len) * pos[:, None] * bands[None, :]
        pad = np.zeros((pos.shape[0], FILT_PAD - FILTER_EMB - 1))
        return np.concatenate([(pos / (seq_len - 1))[:, None], np.cos(ang), -np.sin(ang), sgn[:, None], pad], axis=-1)

    z = np.concatenate([feats(slot_f, np.ones(slot_f.shape)), feats(pos_b, np.where(neg_lag, -1.0, 0.0))], axis=-1)
    return np.ascontiguousarray(z, dtype=np.float32)


@functools.lru_cache(maxsize=None)
def _rope_tables(seq_len):
    inv = ROPE_THETA ** (-np.arange(0, HEAD_DIM, 2, dtype=np.float64) / HEAD_DIM)
    ang = np.arange(seq_len, dtype=np.float64)[:, None] * inv[None, :]
    cos, sin = np.cos(ang), np.sin(ang)
    zero = np.zeros_like(sin)
    reps = LANES // HEAD_DIM
    f32 = lambda a: np.ascontiguousarray(np.tile(a, (1, reps)), dtype=np.float32)
    return f32(np.concatenate([cos, cos], axis=1)), f32(np.concatenate([-sin, zero], axis=1)), \
        f32(np.concatenate([zero, sin], axis=1))


def _layer(x, p):
    b, seq_len, _ = x.shape
    n = b * seq_len
    n1h = seq_len // LANES
    xf = x.reshape(n, D_MODEL)
    tj = SUBLANES if n1h >= LANES else 4 * SUBLANES
    c = {k: jnp.asarray(v) for k, v in _dft_constants(n1h, tj).items()}
    kb = SUBLANES
    r4 = lambda a: a.reshape(b, n1h, LANES, HYENA_WIDTH)

    cos, sa, sb = (jnp.asarray(t) for t in _rope_tables(seq_len))
    x1, x2, v, q, k4, v4, g = _inproj(xf, seq_len, p["attn_norm_w"], p["w_in"], p["conv_w"], p["conv_b"],
                                      p["q_norm_w"], p["k_norm_w"], cos, sa, sb, p["ones_bd"])

    afre, afim, asum = _filter_stage_a(jnp.asarray(_filter_features(seq_len)), p["filt_w1"], p["filt_b1"],
                                       p["filt_w2"], p["filt_b2"], p["filt_w3f"], p["filt_w3b"], p["filt_freq"],
                                       p["decay"], c["g_full"], c["twr_a"], c["twi_a"], n1h, tj)
    hspec = _filter_spectrum(afre, afim, c["fblk"], asum, kb)

    are, aim = _fwd_a(r4(v), c["g_half"], c["twr_a"], c["twi_a"], tj)
    bre, bim = _mid(are, aim, c["fblk"], c["fblk_inv"], c["twr"], c["twi"], hspec, 0, kb)
    z1, are, aim = _inv_a(bre, bim, c["g_inv"], r4(x1), r4(v), p["hyena_bias"][0:1], tj,
                          nxt=(c["g_half"], c["twr_a"], c["twi_a"]))
    bre, bim = _mid(are, aim, c["fblk"], c["fblk_inv"], c["twr"], c["twi"], hspec, 1, kb)
    (yh,) = _inv_a(bre, bim, c["g_inv"], r4(x2), z1, p["hyena_bias"][1:2], tj)

    ya = _attention(q.reshape(b, seq_len, 2 * ATTN_WIDTH), k4.reshape(b, seq_len, 2 * LANES),
                    v4.reshape(b, seq_len, 2 * LANES), p["attn_sink"])

    out = _final(xf, yh.reshape(n, HYENA_WIDTH), ya.reshape(n, ATTN_WIDTH), g, p["w_hy_out"], p["w_at_out"],
                 p["w_o"], p["ffn_norm_w"], p["w_gate"], p["w_up"], p["w_down"])
    return out.reshape(b, seq_len, D_MODEL)


def _block_diag2(w):
    z = jnp.zeros_like(w)
    return jnp.concatenate([jnp.concatenate([w, z], axis=1), jnp.concatenate([z, w], axis=1)], axis=0)


def kernel(x_prompt, x_sample, attn_norm_w, w_in, hyena_conv_w, hyena_conv_b, filt_w1, filt_b1, filt_w2, filt_b2,
           filt_w3, filt_freq, hyena_bias, q_norm_w, k_norm_w, attn_sink, w_hy_out, w_at_out, w_o, ffn_norm_w,
           w_gate, w_up, w_down):
    cw = HYENA_ORDER * HYENA_WIDTH
    max_decay = math.log(DECAY_TARGET) / DECAY_FAST_PCT
    min_decay = math.log(DECAY_TARGET) / DECAY_SLOW_PCT
    deltas = jnp.abs(jnp.linspace(min_decay, max_decay, HYENA_WIDTH, dtype=F32))
    head = np.arange(ATTN_WIDTH) // HEAD_DIM
    ones_bd = jnp.asarray((head[:, None] == head[None, :]).astype(np.float32) / HEAD_DIM).astype(BF16)
    w3 = filt_w3[0].reshape(FILTER_HIDDEN, HYENA_ORDER, 2, HYENA_WIDTH).transpose(2, 0, 1, 3)
    w3 = w3.reshape(2, FILTER_HIDDEN, cw)
    w3_zero = jnp.zeros((FILTER_HIDDEN, cw), F32)
    twice = lambda a: jnp.tile(a, 2)[None, :]
    p = dict(
        attn_norm_w=attn_norm_w[0][None, :],
        w_in=w_in[0].astype(BF16),
        conv_w=hyena_conv_w[0],
        conv_b=hyena_conv_b[0][None, :],
        filt_w1=_block_diag2(jnp.pad(filt_w1[0], ((0, FILT_PAD - FILTER_EMB), (0, 0)))),
        filt_b1=twice(filt_b1[0]),
        filt_w2=_block_diag2(filt_w2[0]),
        filt_b2=twice(filt_b2[0]),
        filt_w3f=jnp.concatenate([w3[0], w3_zero], axis=0),
        filt_w3b=jnp.concatenate([w3_zero, w3[1]], axis=0),
        filt_freq=twice(filt_freq[0]),
        decay=deltas[None, :],
        hyena_bias=hyena_bias[0],
        q_norm_w=jnp.tile(q_norm_w[0], N_HEADS)[None, :],
        k_norm_w=jnp.tile(k_norm_w[0], N_KV_HEADS)[None, :],
        attn_sink=attn_sink[0],
        ones_bd=ones_bd,
        w_hy_out=w_hy_out[0].astype(BF16),
        w_at_out=w_at_out[0].astype(BF16),
        w_o=w_o[0].astype(BF16),
        ffn_norm_w=ffn_norm_w[0][None, :],
        w_gate=w_gate[0].astype(BF16),
        w_up=w_up[0].astype(BF16),
        w_down=w_down[0].astype(BF16),
    )
    return (_layer(x_prompt, p), _layer(x_sample, p))
```

```python
import functools
import math

import numpy as np
import jax
import jax.numpy as jnp
from jax import lax
from jax.experimental import pallas as pl
from jax.experimental.pallas import tpu as pltpu

F32 = jnp.float32
BF16 = jnp.bfloat16

D_MODEL = 1024
HYENA_WIDTH = 512
HYENA_ORDER = 2
FILTER_BANDS = 16
FILTER_EMB = 1 + 2 * FILTER_BANDS
FILTER_HIDDEN = 64
DECAY_FAST_PCT = 0.3
DECAY_SLOW_PCT = 1.5
DECAY_TARGET = 1e-2
DECAY_SHIFT = 0.05
N_HEADS = 8
N_KV_HEADS = 2
HEAD_DIM = 64
ATTN_WIDTH = N_HEADS * HEAD_DIM
KV_WIDTH = N_KV_HEADS * HEAD_DIM
WINDOW = 128
ROPE_THETA = 10000.0
FFN_HIDDEN = 2816
RMS_EPS = 1e-6
NEG_INF = -1e30
LOG2E = math.log2(math.e)

C_HY = 3 * HYENA_WIDTH
C_Q = C_HY + ATTN_WIDTH
C_K = C_Q + KV_WIDTH
C_V = C_K + KV_WIDTH
IN_WIDTH = C_V + 2 * D_MODEL

LANES = 128
SUBLANES = 8
ATTN_BLOCK = 128
ATTN_SUB = 4
VMEM_LIMIT = 56 * 1024 * 1024

TM_INPROJ = 512
TM_FINAL = 512
FFN_CHUNKS = ((0, 1280), (1280, FFN_HIDDEN))
_CH_SPLITS = (slice(0, HYENA_WIDTH // 2), slice(HYENA_WIDTH // 2, HYENA_WIDTH))
FILT_PAD = 64


def _cparams(*sem):
    return pltpu.CompilerParams(dimension_semantics=sem, vmem_limit_bytes=VMEM_LIMIT)


def _const_spec(shape):
    nd = len(shape)
    return pl.BlockSpec(shape, lambda *_: (0,) * nd, pipeline_mode=pl.Buffered(1))


def _dot(a, b):
    return jnp.dot(a, b, preferred_element_type=F32)


def _sigmoid(x):
    return 0.5 * jnp.tanh(0.5 * x) + 0.5


def _split(a):
    hi = a.astype(BF16)
    lo = (a - hi.astype(F32)).astype(BF16)
    return hi, lo


def _dot3(a, w):
    ah, al = _split(a)
    wh, wl = _split(w)
    return _dot(ah, wh) + _dot(al, wh) + _dot(ah, wl)


def _inproj_kernel(x_ref, xp_ref, xn_ref, nw_ref, w_ref, cw_ref, cb_ref, qw_ref, kw_ref,
                   cos_ref, sa_ref, sb_ref, ones_ref,
                   x1_ref, x2_ref, v_ref, q_ref, k4_ref, v4_ref, g_ref, pad_ref, *, tm, tiles_per_seq):
    pos = pl.program_id(0) % tiles_per_seq
    nw = nw_ref[...]

    def norm(xv):
        ms = jnp.mean(xv * xv, axis=-1, keepdims=True)
        return (xv * lax.rsqrt(ms + RMS_EPS) * nw).astype(BF16)

    h_all = norm(jnp.concatenate([x_ref[...], xp_ref[...], xn_ref[...]], axis=0))
    h = h_all[:tm]

    hy_all = _dot(h_all, w_ref[:, :C_HY])
    hy = hy_all[:tm]
    hyh = hy_all[tm:]
    prev_ok = (pos > 0).astype(F32)
    next_ok = (pos < tiles_per_seq - 1).astype(F32)
    pad_ref[0:SUBLANES] = hyh[0:SUBLANES] * prev_ok
    pad_ref[SUBLANES:SUBLANES + tm] = hy
    pad_ref[SUBLANES + tm:2 * SUBLANES + tm] = hyh[SUBLANES:] * next_ok
    up = pad_ref[SUBLANES - 1:SUBLANES - 1 + tm]
    un = pad_ref[SUBLANES + 1:SUBLANES + 1 + tm]
    cw = cw_ref[...]
    uc = cw[0:1] * up + cw[1:2] * hy + cw[2:3] * un + cb_ref[...]
    x1_ref[...] = uc[:, :HYENA_WIDTH].astype(BF16)
    x2_ref[...] = uc[:, HYENA_WIDTH:2 * HYENA_WIDTH].astype(BF16)
    v_ref[...] = uc[:, 2 * HYENA_WIDTH:].astype(BF16)

    def head_norm_rope(t, wrow, ones, reps):
        ms = _dot((t * t).astype(BF16), ones)
        tn = t * lax.rsqrt(ms + RMS_EPS) * wrow
        width = t.shape[1]
        cos = jnp.concatenate([cos_ref[...]] * reps, axis=1)
        sa = jnp.concatenate([sa_ref[...]] * reps, axis=1)
        sb = jnp.concatenate([sb_ref[...]] * reps, axis=1)
        half = HEAD_DIM // 2
        return tn * cos + pltpu.roll(tn, width - half, 1) * sa + pltpu.roll(tn, half, 1) * sb

    qkv = _dot(h, w_ref[:, C_HY:C_V])
    q = qkv[:, :ATTN_WIDTH]
    q = head_norm_rope(q, qw_ref[...], ones_ref[...], ATTN_WIDTH // LANES) * (HEAD_DIM ** -0.5 * LOG2E)
    blocks = []
    for p in range(N_HEADS // 2):
        qp = q[:, p * LANES:(p + 1) * LANES]
        blocks += [qp, pltpu.roll(qp, HEAD_DIM, 1)]
    q_ref[...] = jnp.concatenate(blocks, axis=1).astype(BF16)

    k = qkv[:, ATTN_WIDTH:ATTN_WIDTH + KV_WIDTH]
    k = head_norm_rope(k, kw_ref[...], ones_ref[:KV_WIDTH, :KV_WIDTH], 1)
    v = qkv[:, ATTN_WIDTH + KV_WIDTH:]

    lo_half = lax.broadcasted_iota(jnp.int32, (tm, KV_WIDTH), 1) < HEAD_DIM

    def spread(t):
        zero = jnp.zeros_like(t)
        return jnp.concatenate([jnp.where(lo_half, t, zero), jnp.where(lo_half, pltpu.roll(t, HEAD_DIM, 1), zero)],
                               axis=1).astype(BF16)

    k4_ref[...] = spread(k)
    v4_ref[...] = spread(v)
    g_ref[...] = _sigmoid(_dot(h, w_ref[:, C_V:]))


def _inproj(xf, seq_len, nw, w_in_bf, cw, cb, qw, kw, cos, sa, sb, ones_bd):
    n = xf.shape[0]
    tm = TM_INPROJ
    tiles_per_seq = seq_len // tm
    nblk8 = n // SUBLANES
    r8 = tm // SUBLANES
    kern = functools.partial(_inproj_kernel, tm=tm, tiles_per_seq=tiles_per_seq)
    row = lambda i: (i, 0)
    return pl.pallas_call(
        kern,
        grid=(n // tm,),
        in_specs=[
            pl.BlockSpec((tm, D_MODEL), row),
            pl.BlockSpec((SUBLANES, D_MODEL), lambda i: (jnp.maximum(i * r8 - 1, 0), 0)),
            pl.BlockSpec((SUBLANES, D_MODEL), lambda i: (jnp.minimum((i + 1) * r8, nblk8 - 1), 0)),
            _const_spec((1, D_MODEL)),
            _const_spec((D_MODEL, IN_WIDTH)),
            _const_spec((3, C_HY)),
            _const_spec((1, C_HY)),
            _const_spec((1, ATTN_WIDTH)),
            _const_spec((1, KV_WIDTH)),
            pl.BlockSpec((tm, LANES), lambda i: (i % tiles_per_seq, 0)),
            pl.BlockSpec((tm, LANES), lambda i: (i % tiles_per_seq, 0)),
            pl.BlockSpec((tm, LANES), lambda i: (i % tiles_per_seq, 0)),
            _const_spec((ATTN_WIDTH, ATTN_WIDTH)),
        ],
        out_specs=[
            pl.BlockSpec((tm, HYENA_WIDTH), row),
            pl.BlockSpec((tm, HYENA_WIDTH), row),
            pl.BlockSpec((tm, HYENA_WIDTH), row),
            pl.BlockSpec((tm, 2 * ATTN_WIDTH), row),
            pl.BlockSpec((tm, 2 * LANES), row),
            pl.BlockSpec((tm, 2 * LANES), row),
            pl.BlockSpec((tm, 2 * D_MODEL), row),
        ],
        out_shape=[
            jax.ShapeDtypeStruct((n, HYENA_WIDTH), BF16),
            jax.ShapeDtypeStruct((n, HYENA_WIDTH), BF16),
            jax.ShapeDtypeStruct((n, HYENA_WIDTH), BF16),
            jax.ShapeDtypeStruct((n, 2 * ATTN_WIDTH), BF16),
            jax.ShapeDtypeStruct((n, 2 * LANES), BF16),
            jax.ShapeDtypeStruct((n, 2 * LANES), BF16),
            jax.ShapeDtypeStruct((n, 2 * D_MODEL), F32),
        ],
        scratch_shapes=[pltpu.VMEM((tm + 2 * SUBLANES, C_HY), F32)],
        compiler_params=_cparams("parallel"),
        name="inproj",
    )(xf, xf, xf, nw, w_in_bf, cw, cb, qw, kw, cos, sa, sb, ones_bd)


def _rows_to_tiles(r, tj):
    return r.reshape(r.shape[0], tj, r.shape[1] // tj)


def _tiles_to_rows(t):
    return t.reshape(t.shape[0], t.shape[1] * t.shape[2])


def _twiddle_tiles(r, twr_ref, twi_ref, n1h, tj):
    c = r.shape[1] // tj
    re, im = [], []
    for j in range(tj):
        ar, ai = r[:n1h, j * c:(j + 1) * c], r[n1h:, j * c:(j + 1) * c]
        tr, ti = twr_ref[:, j:j + 1], twi_ref[:, j:j + 1]
        re.append(ar * tr - ai * ti)
        im.append(ar * ti + ai * tr)
    return (_rows_to_tiles(jnp.concatenate(re, axis=1), tj).astype(BF16),
            _rows_to_tiles(jnp.concatenate(im, axis=1), tj).astype(BF16))


def _store_twiddled(r, twr_ref, twi_ref, are_ref, aim_ref, sl, n1h, tj):
    re, im = _twiddle_tiles(r, twr_ref, twi_ref, n1h, tj)
    are_ref[:, :, sl] = re
    aim_ref[:, :, sl] = im


def _filt_kernel(z_ref, w1_ref, b1_ref, w2_ref, b2_ref, w3f_ref, w3b_ref, fr_ref, dl_ref, g_ref, twr_ref, twi_ref,
                 are_ref, aim_ref, s_ref, *, n1h, tj):
    i = pl.program_id(0)
    z = z_ref[...]
    fr = fr_ref[...]
    h = jnp.sin(fr * (_dot3(z, w1_ref[...]) + b1_ref[...]))
    h = jnp.sin(fr * (_dot3(h, w2_ref[...]) + b2_ref[...]))
    hb = h.astype(BF16)
    tf = z[:, 0:1]
    tb = z[:, FILT_PAD:FILT_PAD + 1]
    sgn = z[:, FILT_PAD + FILTER_EMB:FILT_PAD + FILTER_EMB + 1]
    gm = g_ref[...].astype(BF16)
    cb = HYENA_WIDTH
    parts = []
    for o in range(HYENA_ORDER):
        sl = slice(o * cb, (o + 1) * cb)
        dl = dl_ref[...]

        def taps(w_ref, t):
            return _dot(hb, w_ref[:, sl].astype(BF16)) * (jnp.exp(-t * dl) + DECAY_SHIFT)

        kf = taps(w3f_ref, tf)
        kb = taps(w3b_ref, tb) * sgn
        parts.append(jnp.sum(jnp.abs(kf), axis=0, keepdims=True) + jnp.sum(jnp.abs(kb), axis=0, keepdims=True))
        cols = [jnp.concatenate([kf[j * n1h:(j + 1) * n1h], kb[j * n1h:(j + 1) * n1h]], axis=0).astype(BF16)
                for j in range(tj)]
        re, im = _twiddle_tiles(_dot(gm, jnp.concatenate(cols, axis=1)), twr_ref, twi_ref, n1h, tj)
        are_ref[o] = re
        aim_ref[o] = im
    part = jnp.concatenate(parts, axis=1)

    @pl.when(i == 0)
    def _():
        s_ref[...] = part

    @pl.when(i > 0)
    def _():
        s_ref[...] += part


def _filter_stage_a(zf, w1bd, b1, w2bd, b2, w3f, w3b, fr, dl, g_full, twr, twi, n1h, tj):
    cw = HYENA_ORDER * HYENA_WIDTH
    hid2 = 2 * FILTER_HIDDEN
    a_shape = jax.ShapeDtypeStruct((HYENA_ORDER, n1h, LANES, HYENA_WIDTH), BF16)
    a_spec = pl.BlockSpec((HYENA_ORDER, n1h, tj, HYENA_WIDTH), lambda i: (0, 0, i, 0))
    t_spec = pl.BlockSpec((None, n1h, tj), lambda i: (i, 0, 0))
    return pl.pallas_call(
        functools.partial(_filt_kernel, n1h=n1h, tj=tj),
        grid=(LANES // tj,),
        in_specs=[
            pl.BlockSpec((tj * n1h, 2 * FILT_PAD), lambda i: (i, 0)),
            _const_spec((2 * FILT_PAD, hid2)),
            _const_spec((1, hid2)),
            _const_spec((hid2, hid2)),
            _const_spec((1, hid2)),
            _const_spec((hid2, cw)),
            _const_spec((hid2, cw)),
            _const_spec((1, hid2)),
            _const_spec((1, HYENA_WIDTH)),
            _const_spec((2 * n1h, 2 * n1h)),
            t_spec, t_spec,
        ],
        out_specs=[a_spec, a_spec, pl.BlockSpec((1, cw), lambda i: (0, 0))],
        out_shape=[a_shape, a_shape, jax.ShapeDtypeStruct((1, cw), F32)],
        compiler_params=_cparams("arbitrary"),
        name="filter_taps_stage_a",
    )(zf, w1bd, b1, w2bd, b2, w3f, w3b, fr, dl, g_full, twr, twi)


def _fwda_kernel(g_ref, twr_ref, twi_ref, u_ref, are_ref, aim_ref, *, n1h, tj):
    gm = g_ref[...].astype(BF16)
    prev = None
    for sl in _CH_SPLITS:
        u = _tiles_to_rows(u_ref[:, :, sl])
        if prev is not None:
            _store_twiddled(prev[1], twr_ref, twi_ref, are_ref, aim_ref, prev[0], n1h, tj)
        prev = (sl, _dot(gm, u))
    _store_twiddled(prev[1], twr_ref, twi_ref, are_ref, aim_ref, prev[0], n1h, tj)


def _fwd_a(u4, g_half, twr, twi, tj):
    b, n1h, _, cb = u4.shape
    blk = pl.BlockSpec((None, n1h, tj, cb), lambda bi, ji: (bi, 0, ji, 0))
    t_spec = pl.BlockSpec((None, n1h, tj), lambda bi, ji: (ji, 0, 0))
    a_shape = jax.ShapeDtypeStruct(u4.shape, BF16)
    return pl.pallas_call(
        functools.partial(_fwda_kernel, n1h=n1h, tj=tj),
        grid=(b, LANES // tj),
        in_specs=[_const_spec((2 * n1h, n1h)), t_spec, t_spec, blk],
        out_specs=[blk, blk],
        out_shape=[a_shape, a_shape],
        compiler_params=_cparams("parallel", "parallel"),
        name="dft_stage_a",
    )(g_half, twr, twi, u4)


def _mid_kernel(f_ref, fi_ref, twr_ref, twi_ref, fre_ref, fim_ref, are_ref, aim_ref, bre_ref, bim_ref, *, kb):
    fm = f_ref[...].astype(BF16)
    fim = fi_ref[...].astype(BF16)
    reps = are_ref.shape[-1] // LANES

    def fwd(k):
        return (_dot(fm, jnp.concatenate([are_ref[k], aim_ref[k]], axis=0)),
                _dot(fm, jnp.concatenate([fre_ref[k], fim_ref[k]], axis=0)))

    def spec(k, xh):
        x, h = xh
        xr, xi = x[:LANES], x[LANES:]
        hr, hi = h[:LANES], h[LANES:]
        return jnp.concatenate([xr * hr - xi * hi, xr * hi + xi * hr], axis=0).astype(BF16)

    def out(k, bc):
        br, bi = bc[:LANES], bc[LANES:]
        tr = jnp.concatenate([twr_ref[k]] * reps, axis=1)
        ti = jnp.concatenate([twi_ref[k]] * reps, axis=1)
        bre_ref[k] = (br * tr + bi * ti).astype(BF16)
        bim_ref[k] = (bi * tr - br * ti).astype(BF16)

    x = fwd(0)
    bc_prev = None
    for k in range(kb):
        y = spec(k, x)
        if k + 1 < kb:
            x = fwd(k + 1)
        if bc_prev is not None:
            out(k - 1, bc_prev)
        bc_prev = _dot(fim, y)
    out(kb - 1, bc_prev)


def _mid(are, aim, fblk, fblk_inv, twr, twi, afre, afim, order, kb):
    b, n1h, _, cb = are.shape
    a_spec = pl.BlockSpec((None, kb, LANES, cb), lambda ki, bi: (bi, ki, 0, 0))
    f_spec = pl.BlockSpec((None, kb, LANES, cb), lambda ki, bi: (order, ki, 0, 0))
    t_spec = pl.BlockSpec((kb, LANES, LANES), lambda ki, bi: (ki, 0, 0))
    shape = jax.ShapeDtypeStruct(are.shape, BF16)
    return pl.pallas_call(
        functools.partial(_mid_kernel, kb=kb),
        grid=(n1h // kb, b),
        in_specs=[
            _const_spec((2 * LANES, 2 * LANES)),
            _const_spec((2 * LANES, 2 * LANES)),
            t_spec, t_spec,
            f_spec, f_spec,
            a_spec, a_spec,
        ],
        out_specs=[a_spec, a_spec],
        out_shape=[shape, shape],
        compiler_params=_cparams("parallel", "parallel"),
        name="dft_stage_c",
    )(fblk, fblk_inv, twr, twi, afre, afim, are, aim)


def _inva_kernel(gi_ref, bre_ref, bim_ref, xg_ref, v_ref, sb_ref, s_ref, *rest, n1h, tj, fuse_next):
    gim = gi_ref[...].astype(BF16)
    inv_l1 = 1.0 / s_ref[...]
    if fuse_next:
        g_ref, twr_ref, twi_ref, z_ref, are_ref, aim_ref = rest
        gm = g_ref[...].astype(BF16)
    else:
        (z_ref,) = rest

    def conv(sl):
        rhs = jnp.concatenate([_tiles_to_rows(bre_ref[:, :, sl]), _tiles_to_rows(bim_ref[:, :, sl])], axis=0)
        return _dot(gim, rhs)

    def gate(sl, y):
        z = xg_ref[:, :, sl].astype(F32) * (_rows_to_tiles(y, tj) * inv_l1[:, sl]
                                            + sb_ref[:, sl] * v_ref[:, :, sl].astype(F32))
        z = z.astype(BF16)
        z_ref[:, :, sl] = z
        return z

    s0, s1 = _CH_SPLITS
    y0 = conv(s0)
    y1 = conv(s1)
    z0 = gate(s0, y0)
    if fuse_next:
        r0 = _dot(gm, _tiles_to_rows(z0))
    z1 = gate(s1, y1)
    if fuse_next:
        _store_twiddled(r0, twr_ref, twi_ref, are_ref, aim_ref, s0, n1h, tj)
        r1 = _dot(gm, _tiles_to_rows(z1))
        _store_twiddled(r1, twr_ref, twi_ref, are_ref, aim_ref, s1, n1h, tj)


def _inv_a(bre, bim, g_inv, xg, v, sb_row, asum, order, tj, nxt=None):
    b, n1h, _, cb = bre.shape
    blk = pl.BlockSpec((None, n1h, tj, cb), lambda bi, ji: (bi, 0, ji, 0))
    in_specs = [_const_spec((n1h, 2 * n1h)), blk, blk, blk, blk, _const_spec((1, cb)),
                pl.BlockSpec((1, cb), lambda bi, ji: (0, order))]
    args = [g_inv, bre, bim, xg, v, sb_row, asum]
    out_specs = [blk]
    out_shape = [jax.ShapeDtypeStruct(bre.shape, BF16)]
    if nxt is not None:
        t_spec = pl.BlockSpec((None, n1h, tj), lambda bi, ji: (ji, 0, 0))
        in_specs += [_const_spec((2 * n1h, n1h)), t_spec, t_spec]
        args += list(nxt)
        out_specs += [blk, blk]
        out_shape += [jax.ShapeDtypeStruct(bre.shape, BF16)] * 2
    return pl.pallas_call(
        functools.partial(_inva_kernel, n1h=n1h, tj=tj, fuse_next=nxt is not None),
        grid=(b, LANES // tj),
        in_specs=in_specs,
        out_specs=out_specs,
        out_shape=out_shape,
        compiler_params=_cparams("parallel", "parallel"),
        name="idft_stage_a_gate",
    )(*args)


def _attn_kernel(sink_ref, q_ref, kp_ref, kc_ref, kn_ref, vp_ref, vc_ref, vn_ref, o_ref, *, nblk, sub):
    i = pl.program_id(1)
    blk = ATTN_BLOCK
    grp = N_HEADS // N_KV_HEADS
    rows = grp * blk
    qi = lax.broadcasted_iota(jnp.int32, (rows, 3 * blk), 0) % blk
    si = lax.broadcasted_iota(jnp.int32, (rows, 3 * blk), 1)
    mid = (si >= blk) & (si < 2 * blk)
    tri_prev = (si < blk) & (si >= qi)
    tri_next = (si >= 2 * blk) & (si - 2 * blk <= qi)
    hrow = lax.broadcasted_iota(jnp.int32, (rows, 1), 0) // blk
    kcat = jnp.concatenate([kp_ref[...], kc_ref[...], kn_ref[...]], axis=0)
    vcat = jnp.concatenate([vp_ref[...], vc_ref[...], vn_ref[...]], axis=0)
    for s in range(sub):
        j = sub * i + s
        valid = mid | (tri_prev & (j > 0)) | (tri_next & (j < nblk - 1))
        outs = []
        for g in range(N_KV_HEADS):
            lhs = jnp.concatenate([q_ref[s * blk:(s + 1) * blk, (grp * g + t) * LANES:(grp * g + t + 1) * LANES]
                                   for t in range(grp)], axis=0)
            sc = lax.dot_general(lhs, kcat[s * blk:(s + 3) * blk, g * LANES:(g + 1) * LANES],
                                 (((1,), (1,)), ((), ())), preferred_element_type=F32)
            sc = jnp.where(valid, sc, NEG_INF)
            sk = jnp.full((rows, 1), sink_ref[grp * g], F32)
            for t in range(1, grp):
                sk = jnp.where(hrow == t, sink_ref[grp * g + t], sk)
            sk = sk * LOG2E
            m = jnp.maximum(jnp.max(sc, axis=1, keepdims=True), sk)
            e = jnp.exp2(sc - m)
            den = jnp.sum(e, axis=1, keepdims=True) + jnp.exp2(sk - m)
            r = _dot(e.astype(BF16), vcat[s * blk:(s + 3) * blk, g * LANES:(g + 1) * LANES]) / den
            for t in range(0, grp, 2):
                outs.append(r[t * blk:(t + 1) * blk] + pltpu.roll(r[(t + 1) * blk:(t + 2) * blk], HEAD_DIM, 1))
        o_ref[s * blk:(s + 1) * blk, :] = jnp.concatenate(outs, axis=1).astype(BF16)


def _attention(q3, k3, v3, sink):
    b, seq_len, qw = q3.shape
    blk = ATTN_BLOCK
    sub = ATTN_SUB
    nblk = seq_len // blk
    wide = k3.shape[-1]
    prev = lambda bi, i: (bi, jnp.maximum(sub * i - 1, 0), 0)
    cur = lambda bi, i: (bi, i, 0)
    nxt = lambda bi, i: (bi, jnp.minimum(sub * (i + 1), nblk - 1), 0)
    edge = lambda f: pl.BlockSpec((None, blk, wide), f)
    body = pl.BlockSpec((None, sub * blk, wide), cur)
    return pl.pallas_call(
        functools.partial(_attn_kernel, nblk=nblk, sub=sub),
        grid=(b, nblk // sub),
        in_specs=[pl.BlockSpec(memory_space=pltpu.SMEM),
                  pl.BlockSpec((None, sub * blk, qw), cur),
                  edge(prev), body, edge(nxt), edge(prev), body, edge(nxt)],
        out_specs=pl.BlockSpec((None, sub * blk, ATTN_WIDTH), cur),
        out_shape=jax.ShapeDtypeStruct((b, seq_len, ATTN_WIDTH), BF16),
        compiler_params=_cparams("parallel", "parallel"),
        name="banded_attention",
    )(sink, q3, k3, k3, k3, v3, v3, v3)


def _final_kernel(x_ref, yh_ref, ya_ref, g_ref, why_ref, wat_ref, wo_ref, nw_ref, wg_ref, wu_ref, wd_ref,
                  o_ref):
    a = _dot(yh_ref[...], why_ref[...])
    b = _dot(ya_ref[...], wat_ref[...])
    merged = g_ref[:, :D_MODEL] * a + g_ref[:, D_MODEL:] * b
    x1 = x_ref[...] + _dot(merged.astype(BF16), wo_ref[...])
    ms = jnp.mean(x1 * x1, axis=-1, keepdims=True)
    f = (x1 * lax.rsqrt(ms + RMS_EPS) * nw_ref[...]).astype(BF16)
    acc = x1
    for lo, hi in FFN_CHUNKS:
        gt = _dot(f, wg_ref[:, lo:hi])
        upv = _dot(f, wu_ref[:, lo:hi])
        hid = gt * _sigmoid(gt) * upv
        acc = acc + _dot(hid.astype(BF16), wd_ref[lo:hi, :])
    o_ref[...] = acc


def _final(xf, yh, ya, g, why, wat, wo, nw, wg, wu, wd):
    n = xf.shape[0]
    tm = TM_FINAL
    row = lambda i: (i, 0)
    return pl.pallas_call(
        _final_kernel,
        grid=(n // tm,),
        in_specs=[
            pl.BlockSpec((tm, D_MODEL), row),
            pl.BlockSpec((tm, HYENA_WIDTH), row),
            pl.BlockSpec((tm, ATTN_WIDTH), row),
            pl.BlockSpec((tm, 2 * D_MODEL), row),
            _const_spec((HYENA_WIDTH, D_MODEL)),
            _const_spec((ATTN_WIDTH, D_MODEL)),
            _const_spec((D_MODEL, D_MODEL)),
            _const_spec((1, D_MODEL)),
            _const_spec((D_MODEL, FFN_HIDDEN)),
            _const_spec((D_MODEL, FFN_HIDDEN)),
            _const_spec((FFN_HIDDEN, D_MODEL)),
        ],
        out_specs=pl.BlockSpec((tm, D_MODEL), row),
        out_shape=jax.ShapeDtypeStruct((n, D_MODEL), F32),
        compiler_params=_cparams("parallel"),
        name="merge_ffn",
    )(xf, yh, ya, g, why, wat, wo, nw, wg, wu, wd)


@functools.lru_cache(maxsize=None)
def _dft_constants(n1h, tj):
    n1_len = 2 * n1h
    m = n1_len * LANES
    k1 = np.arange(n1h, dtype=np.float64)[:, None] + 0.5
    n1 = np.arange(n1_len, dtype=np.float64)[None, :]
    gc = np.exp(-2j * np.pi * k1 * n1 / n1_len)
    g_full = np.concatenate([gc.real, gc.imag], axis=0)
    g_inv = (2.0 / m) * np.concatenate([gc.real[:, :n1h].T, gc.imag[:, :n1h].T], axis=1)
    n2 = np.arange(LANES, dtype=np.float64)
    tw = np.exp(-2j * np.pi * k1 * n2[None, :] / m)[:, :, None] * np.ones((1, 1, LANES))
    fc = np.exp(-2j * np.pi * np.outer(n2, n2) / LANES)
    fblk = np.block([[fc.real, -fc.imag], [fc.imag, fc.real]])
    fblk_inv = np.block([[fc.real, fc.imag], [-fc.imag, fc.real]])
    f32 = lambda a: np.ascontiguousarray(a, dtype=np.float32)
    tiled = lambda a: a[:, :, 0].reshape(n1h, LANES // tj, tj).transpose(1, 0, 2)
    return dict(g_full=f32(g_full), g_half=f32(g_full[:, :n1h]), g_inv=f32(g_inv), twr=f32(tw.real),
                twi=f32(tw.imag), twr_a=f32(tiled(tw.real)), twi_a=f32(tiled(tw.imag)), fblk=f32(fblk),
                fblk_inv=f32(fblk_inv))


@functools.lru_cache(maxsize=None)
def _filter_features(seq_len):
    n1h = seq_len // LANES
    n2 = np.arange(LANES)[:, None]
    n1 = np.arange(n1h)[None, :]
    slot_f = (LANES * n1 + n2).reshape(-1)
    slot_b = slot_f + seq_len
    neg_lag = slot_b > seq_len
    pos_b = np.where(neg_lag, 2 * seq_len - slot_b, 0)
    bands = np.linspace(1e-4, FILTER_BANDS - 1, FILTER_BANDS)

    def feats(pos, sgn):
        pos = pos.astype(np.float64)
        ang = (2.0 * math.pi / seq_len) * pos[:, None] * bands[None, :]
        pad = np.zeros((pos.shape[0], FILT_PAD - FILTER_EMB - 1))
        return np.concatenate([(pos / (seq_len - 1))[:, None], np.cos(ang), -np.sin(ang), sgn[:, None], pad], axis=-1)

    z = np.concatenate([feats(slot_f, np.ones(slot_f.shape)), feats(pos_b, np.where(neg_lag, -1.0, 0.0))], axis=-1)
    return np.ascontiguousarray(z, dtype=np.float32)


@functools.lru_cache(maxsize=None)
def _rope_tables(seq_len):
    inv = ROPE_THETA ** (-np.arange(0, HEAD_DIM, 2, dtype=np.float64) / HEAD_DIM)
    ang = np.arange(seq_len, dtype=np.float64)[:, None] * inv[None, :]
    cos, sin = np.cos(ang), np.sin(ang)
    zero = np.zeros_like(sin)
    reps = LANES // HEAD_DIM
    f32 = lambda a: np.ascontiguousarray(np.tile(a, (1, reps)), dtype=np.float32)
    return f32(np.concatenate([cos, cos], axis=1)), f32(np.concatenate([-sin, zero], axis=1)), \
        f32(np.concatenate([zero, sin], axis=1))


def _layer(x, p):
    b, seq_len, _ = x.shape
    n = b * seq_len
    n1h = seq_len // LANES
    xf = x.reshape(n, D_MODEL)
    tj = SUBLANES if n1h >= LANES else 4 * SUBLANES
    c = {k: jnp.asarray(v) for k, v in _dft_constants(n1h, tj).items()}
    kb = SUBLANES
    r4 = lambda a: a.reshape(b, n1h, LANES, HYENA_WIDTH)

    cos, sa, sb = (jnp.asarray(t) for t in _rope_tables(seq_len))
    x1, x2, v, q, k4, v4, g = _inproj(xf, seq_len, p["attn_norm_w"], p["w_in"], p["conv_w"], p["conv_b"],
                                      p["q_norm_w"], p["k_norm_w"], cos, sa, sb, p["ones_bd"])

    afre, afim, asum = _filter_stage_a(jnp.asarray(_filter_features(seq_len)), p["filt_w1"], p["filt_b1"],
                                       p["filt_w2"], p["filt_b2"], p["filt_w3f"], p["filt_w3b"], p["filt_freq"],
                                       p["decay"], c["g_full"], c["twr_a"], c["twi_a"], n1h, tj)

    are, aim = _fwd_a(r4(v), c["g_half"], c["twr_a"], c["twi_a"], tj)
    bre, bim = _mid(are, aim, c["fblk"], c["fblk_inv"], c["twr"], c["twi"], afre, afim, 0, kb)
    z1, are, aim = _inv_a(bre, bim, c["g_inv"], r4(x1), r4(v), p["hyena_bias"][0:1], asum, 0, tj,
                          nxt=(c["g_half"], c["twr_a"], c["twi_a"]))
    bre, bim = _mid(are, aim, c["fblk"], c["fblk_inv"], c["twr"], c["twi"], afre, afim, 1, kb)
    (yh,) = _inv_a(bre, bim, c["g_inv"], r4(x2), z1, p["hyena_bias"][1:2], asum, 1, tj)

    ya = _attention(q.reshape(b, seq_len, 2 * ATTN_WIDTH), k4.reshape(b, seq_len, 2 * LANES),
                    v4.reshape(b, seq_len, 2 * LANES), p["attn_sink"])

    out = _final(xf, yh.reshape(n, HYENA_WIDTH), ya.reshape(n, ATTN_WIDTH), g, p["w_hy_out"], p["w_at_out"],
                 p["w_o"], p["ffn_norm_w"], p["w_gate"], p["w_up"], p["w_down"])
    return out.reshape(b, seq_len, D_MODEL)


def _block_diag2(w):
    z = jnp.zeros_like(w)
    return jnp.concatenate([jnp.concatenate([w, z], axis=1), jnp.concatenate([z, w], axis=1)], axis=0)


def kernel(x_prompt, x_sample, attn_norm_w, w_in, hyena_conv_w, hyena_conv_b, filt_w1, filt_b1, filt_w2, filt_b2,
           filt_w3, filt_freq, hyena_bias, q_norm_w, k_norm_w, attn_sink, w_hy_out, w_at_out, w_o, ffn_norm_w,
           w_gate, w_up, w_down):
    cw = HYENA_ORDER * HYENA_WIDTH
    max_decay = math.log(DECAY_TARGET) / DECAY_FAST_PCT
    min_decay = math.log(DECAY_TARGET) / DECAY_SLOW_PCT
    deltas = jnp.abs(jnp.linspace(min_decay, max_decay, HYENA_WIDTH, dtype=F32))
    head = np.arange(ATTN_WIDTH) // HEAD_DIM
    ones_bd = jnp.asarray((head[:, None] == head[None, :]).astype(np.float32) / HEAD_DIM).astype(BF16)
    w3 = filt_w3[0].reshape(FILTER_HIDDEN, HYENA_ORDER, 2, HYENA_WIDTH).transpose(2, 0, 1, 3)
    w3 = w3.reshape(2, FILTER_HIDDEN, cw)
    w3_zero = jnp.zeros((FILTER_HIDDEN, cw), F32)
    twice = lambda a: jnp.tile(a, 2)[None, :]
    p = dict(
        attn_norm_w=attn_norm_w[0][None, :],
        w_in=w_in[0].astype(BF16),
        conv_w=hyena_conv_w[0],
        conv_b=hyena_conv_b[0][None, :],
        filt_w1=_block_diag2(jnp.pad(filt_w1[0], ((0, FILT_PAD - FILTER_EMB), (0, 0)))),
        filt_b1=twice(filt_b1[0]),
        filt_w2=_block_diag2(filt_w2[0]),
        filt_b2=twice(filt_b2[0]),
        filt_w3f=jnp.concatenate([w3[0], w3_zero], axis=0),
        filt_w3b=jnp.concatenate([w3_zero, w3[1]], axis=0),
        filt_freq=twice(filt_freq[0]),
        decay=deltas[None, :],
        hyena_bias=hyena_bias[0],
        q_norm_w=jnp.tile(q_norm_w[0], N_HEADS)[None, :],
        k_norm_w=jnp.tile(k_norm_w[0], N_KV_HEADS)[None, :],
        attn_sink=attn_sink[0],
        ones_bd=ones_bd,
        w_hy_out=w_hy_out[0].astype(BF16),
        w_at_out=w_at_out[0].astype(BF16),
        w_o=w_o[0].astype(BF16),
        ffn_norm_w=ffn_norm_w[0][None, :],
        w_gate=w_gate[0].astype(BF16),
        w_up=w_up[0].astype(BF16),
        w_down=w_down[0].astype(BF16),
    )
    return (_layer(x_prompt, p), _layer(x_sample, p))
```

```python
import functools
import math

import numpy as np
import jax
import jax.numpy as jnp
from jax import lax
from jax.experimental import pallas as pl
from jax.experimental.pallas import tpu as pltpu

F32 = jnp.float32
BF16 = jnp.bfloat16

D_MODEL = 1024
HYENA_WIDTH = 512
HYENA_ORDER = 2
FILTER_BANDS = 16
FILTER_EMB = 1 + 2 * FILTER_BANDS
FILTER_HIDDEN = 64
DECAY_FAST_PCT = 0.3
DECAY_SLOW_PCT = 1.5
DECAY_TARGET = 1e-2
DECAY_SHIFT = 0.05
N_HEADS = 8
N_KV_HEADS = 2
HEAD_DIM = 64
ATTN_WIDTH = N_HEADS * HEAD_DIM
KV_WIDTH = N_KV_HEADS * HEAD_DIM
WINDOW = 128
ROPE_THETA = 10000.0
FFN_HIDDEN = 2816
RMS_EPS = 1e-6
NEG_INF = -1e30
LOG2E = math.log2(math.e)

C_HY = 3 * HYENA_WIDTH
C_Q = C_HY + ATTN_WIDTH
C_K = C_Q + KV_WIDTH
C_V = C_K + KV_WIDTH
IN_WIDTH = C_V + 2 * D_MODEL

LANES = 128
SUBLANES = 8
ATTN_BLOCK = 128
ATTN_SUB = 4
VMEM_LIMIT = 56 * 1024 * 1024

TM_INPROJ = 512
TM_FINAL = 512
FFN_CHUNKS = ((0, 1280), (1280, FFN_HIDDEN))
_CH_SPLITS = (slice(0, HYENA_WIDTH // 2), slice(HYENA_WIDTH // 2, HYENA_WIDTH))
FILT_PAD = 64


def _cparams(*sem):
    return pltpu.CompilerParams(dimension_semantics=sem, vmem_limit_bytes=VMEM_LIMIT)


def _const_spec(shape):
    nd = len(shape)
    return pl.BlockSpec(shape, lambda *_: (0,) * nd, pipeline_mode=pl.Buffered(1))


def _dot(a, b):
    return jnp.dot(a, b, preferred_element_type=F32)


def _sigmoid(x):
    return 0.5 * jnp.tanh(0.5 * x) + 0.5


def _split(a):
    hi = a.astype(BF16)
    lo = (a - hi.astype(F32)).astype(BF16)
    return hi, lo


def _dot3(a, w):
    ah, al = _split(a)
    wh, wl = _split(w)
    return _dot(ah, wh) + _dot(al, wh) + _dot(ah, wl)


def _inproj_kernel(x_ref, xp_ref, xn_ref, nw_ref, w_ref, cw_ref, cb_ref, qw_ref, kw_ref,
                   cos_ref, sa_ref, sb_ref, ones_ref,
                   x1_ref, x2_ref, v_ref, q_ref, k4_ref, v4_ref, g_ref, pad_ref, *, tm, tiles_per_seq):
    pos = pl.program_id(0) % tiles_per_seq
    nw = nw_ref[...]

    def norm(xv):
        ms = jnp.mean(xv * xv, axis=-1, keepdims=True)
        return (xv * lax.rsqrt(ms + RMS_EPS) * nw).astype(BF16)

    h_all = norm(jnp.concatenate([x_ref[...], xp_ref[...], xn_ref[...]], axis=0))
    h = h_all[:tm]

    hy_all = _dot(h_all, w_ref[:, :C_HY])
    hy = hy_all[:tm]
    hyh = hy_all[tm:]
    prev_ok = (pos > 0).astype(F32)
    next_ok = (pos < tiles_per_seq - 1).astype(F32)
    pad_ref[0:SUBLANES] = hyh[0:SUBLANES] * prev_ok
    pad_ref[SUBLANES:SUBLANES + tm] = hy
    pad_ref[SUBLANES + tm:2 * SUBLANES + tm] = hyh[SUBLANES:] * next_ok
    up = pad_ref[SUBLANES - 1:SUBLANES - 1 + tm]
    un = pad_ref[SUBLANES + 1:SUBLANES + 1 + tm]
    cw = cw_ref[...]
    uc = cw[0:1] * up + cw[1:2] * hy + cw[2:3] * un + cb_ref[...]
    x1_ref[...] = uc[:, :HYENA_WIDTH].astype(BF16)
    x2_ref[...] = uc[:, HYENA_WIDTH:2 * HYENA_WIDTH].astype(BF16)
    v_ref[...] = uc[:, 2 * HYENA_WIDTH:].astype(BF16)

    def head_norm_rope(t, wrow, ones, reps):
        ms = _dot((t * t).astype(BF16), ones)
        tn = t * lax.rsqrt(ms + RMS_EPS) * wrow
        width = t.shape[1]
        cos = jnp.concatenate([cos_ref[...]] * reps, axis=1)
        sa = jnp.concatenate([sa_ref[...]] * reps, axis=1)
        sb = jnp.concatenate([sb_ref[...]] * reps, axis=1)
        half = HEAD_DIM // 2
        return tn * cos + pltpu.roll(tn, width - half, 1) * sa + pltpu.roll(tn, half, 1) * sb

    qkv = _dot(h, w_ref[:, C_HY:C_V])
    q = qkv[:, :ATTN_WIDTH]
    q = head_norm_rope(q, qw_ref[...], ones_ref[...], ATTN_WIDTH // LANES) * (HEAD_DIM ** -0.5 * LOG2E)
    blocks = []
    for p in range(N_HEADS // 2):
        qp = q[:, p * LANES:(p + 1) * LANES]
        blocks += [qp, pltpu.roll(qp, HEAD_DIM, 1)]
    q_ref[...] = jnp.concatenate(blocks, axis=1).astype(BF16)

    k = qkv[:, ATTN_WIDTH:ATTN_WIDTH + KV_WIDTH]
    k = head_norm_rope(k, kw_ref[...], ones_ref[:KV_WIDTH, :KV_WIDTH], 1)
    v = qkv[:, ATTN_WIDTH + KV_WIDTH:]

    lo_half = lax.broadcasted_iota(jnp.int32, (tm, KV_WIDTH), 1) < HEAD_DIM

    def spread(t):
        zero = jnp.zeros_like(t)
        return jnp.concatenate([jnp.where(lo_half, t, zero), jnp.where(lo_half, pltpu.roll(t, HEAD_DIM, 1), zero)],
                               axis=1).astype(BF16)

    k4_ref[...] = spread(k)
    v4_ref[...] = spread(v)
    g_ref[...] = _sigmoid(_dot(h, w_ref[:, C_V:]))


def _inproj(xf, seq_len, nw, w_in_bf, cw, cb, qw, kw, cos, sa, sb, ones_bd):
    n = xf.shape[0]
    tm = TM_INPROJ
    tiles_per_seq = seq_len // tm
    nblk8 = n // SUBLANES
    r8 = tm // SUBLANES
    kern = functools.partial(_inproj_kernel, tm=tm, tiles_per_seq=tiles_per_seq)
    row = lambda i: (i, 0)
    return pl.pallas_call(
        kern,
        grid=(n // tm,),
        in_specs=[
            pl.BlockSpec((tm, D_MODEL), row),
            pl.BlockSpec((SUBLANES, D_MODEL), lambda i: (jnp.maximum(i * r8 - 1, 0), 0)),
            pl.BlockSpec((SUBLANES, D_MODEL), lambda i: (jnp.minimum((i + 1) * r8, nblk8 - 1), 0)),
            _const_spec((1, D_MODEL)),
            _const_spec((D_MODEL, IN_WIDTH)),
            _const_spec((3, C_HY)),
            _const_spec((1, C_HY)),
            _const_spec((1, ATTN_WIDTH)),
            _const_spec((1, KV_WIDTH)),
            pl.BlockSpec((tm, LANES), lambda i: (i % tiles_per_seq, 0)),
            pl.BlockSpec((tm, LANES), lambda i: (i % tiles_per_seq, 0)),
            pl.BlockSpec((tm, LANES), lambda i: (i % tiles_per_seq, 0)),
            _const_spec((ATTN_WIDTH, ATTN_WIDTH)),
        ],
        out_specs=[
            pl.BlockSpec((tm, HYENA_WIDTH), row),
            pl.BlockSpec((tm, HYENA_WIDTH), row),
            pl.BlockSpec((tm, HYENA_WIDTH), row),
            pl.BlockSpec((tm, 2 * ATTN_WIDTH), row),
            pl.BlockSpec((tm, 2 * LANES), row),
            pl.BlockSpec((tm, 2 * LANES), row),
            pl.BlockSpec((tm, 2 * D_MODEL), row),
        ],
        out_shape=[
            jax.ShapeDtypeStruct((n, HYENA_WIDTH), BF16),
            jax.ShapeDtypeStruct((n, HYENA_WIDTH), BF16),
            jax.ShapeDtypeStruct((n, HYENA_WIDTH), BF16),
            jax.ShapeDtypeStruct((n, 2 * ATTN_WIDTH), BF16),
            jax.ShapeDtypeStruct((n, 2 * LANES), BF16),
            jax.ShapeDtypeStruct((n, 2 * LANES), BF16),
            jax.ShapeDtypeStruct((n, 2 * D_MODEL), F32),
        ],
        scratch_shapes=[pltpu.VMEM((tm + 2 * SUBLANES, C_HY), F32)],
        compiler_params=_cparams("parallel"),
        name="inproj",
    )(xf, xf, xf, nw, w_in_bf, cw, cb, qw, kw, cos, sa, sb, ones_bd)


def _rows_to_tiles(r, tj):
    return r.reshape(r.shape[0], tj, r.shape[1] // tj)


def _tiles_to_rows(t):
    return t.reshape(t.shape[0], t.shape[1] * t.shape[2])


def _twiddle_tiles(r, twr_ref, twi_ref, n1h, tj):
    c = r.shape[1] // tj
    re, im = [], []
    for j in range(tj):
        ar, ai = r[:n1h, j * c:(j + 1) * c], r[n1h:, j * c:(j + 1) * c]
        tr, ti = twr_ref[:, j:j + 1], twi_ref[:, j:j + 1]
        re.append(ar * tr - ai * ti)
        im.append(ar * ti + ai * tr)
    return (_rows_to_tiles(jnp.concatenate(re, axis=1), tj).astype(BF16),
            _rows_to_tiles(jnp.concatenate(im, axis=1), tj).astype(BF16))


def _store_twiddled(r, twr_ref, twi_ref, are_ref, aim_ref, sl, n1h, tj):
    re, im = _twiddle_tiles(r, twr_ref, twi_ref, n1h, tj)
    are_ref[:, :, sl] = re
    aim_ref[:, :, sl] = im


def _filt_kernel(z_ref, w1_ref, b1_ref, w2_ref, b2_ref, w3f_ref, w3b_ref, fr_ref, dl_ref, g_ref, twr_ref, twi_ref,
                 are_ref, aim_ref, s_ref, *, n1h, tj):
    i = pl.program_id(0)
    z = z_ref[...]
    fr = fr_ref[...]
    h = jnp.sin(fr * (_dot3(z, w1_ref[...]) + b1_ref[...]))
    h = jnp.sin(fr * (_dot3(h, w2_ref[...]) + b2_ref[...]))
    hb = h.astype(BF16)
    tf = z[:, 0:1]
    tb = z[:, FILT_PAD:FILT_PAD + 1]
    sgn = z[:, FILT_PAD + FILTER_EMB:FILT_PAD + FILTER_EMB + 1]
    gm = g_ref[...].astype(BF16)
    cb = HYENA_WIDTH
    parts = []
    for o in range(HYENA_ORDER):
        sl = slice(o * cb, (o + 1) * cb)
        dl = dl_ref[...]

        def taps(w_ref, t):
            return _dot(hb, w_ref[:, sl].astype(BF16)) * (jnp.exp(-t * dl) + DECAY_SHIFT)

        kf = taps(w3f_ref, tf)
        kb = taps(w3b_ref, tb) * sgn
        parts.append(jnp.sum(jnp.abs(kf), axis=0, keepdims=True) + jnp.sum(jnp.abs(kb), axis=0, keepdims=True))
        cols = [jnp.concatenate([kf[j * n1h:(j + 1) * n1h], kb[j * n1h:(j + 1) * n1h]], axis=0).astype(BF16)
                for j in range(tj)]
        re, im = _twiddle_tiles(_dot(gm, jnp.concatenate(cols, axis=1)), twr_ref, twi_ref, n1h, tj)
        are_ref[o] = re
        aim_ref[o] = im
    part = jnp.concatenate(parts, axis=1)

    @pl.when(i == 0)
    def _():
        s_ref[...] = part

    @pl.when(i > 0)
    def _():
        s_ref[...] += part


def _filter_stage_a(zf, w1bd, b1, w2bd, b2, w3f, w3b, fr, dl, g_full, twr, twi, n1h, tj):
    cw = HYENA_ORDER * HYENA_WIDTH
    hid2 = 2 * FILTER_HIDDEN
    a_shape = jax.ShapeDtypeStruct((HYENA_ORDER, n1h, LANES, HYENA_WIDTH), BF16)
    a_spec = pl.BlockSpec((HYENA_ORDER, n1h, tj, HYENA_WIDTH), lambda i: (0, 0, i, 0))
    t_spec = pl.BlockSpec((None, n1h, tj), lambda i: (i, 0, 0))
    return pl.pallas_call(
        functools.partial(_filt_kernel, n1h=n1h, tj=tj),
        grid=(LANES // tj,),
        in_specs=[
            pl.BlockSpec((tj * n1h, 2 * FILT_PAD), lambda i: (i, 0)),
            _const_spec((2 * FILT_PAD, hid2)),
            _const_spec((1, hid2)),
            _const_spec((hid2, hid2)),
            _const_spec((1, hid2)),
            _const_spec((hid2, cw)),
            _const_spec((hid2, cw)),
            _const_spec((1, hid2)),
            _const_spec((1, HYENA_WIDTH)),
            _const_spec((2 * n1h, 2 * n1h)),
            t_spec, t_spec,
        ],
        out_specs=[a_spec, a_spec, pl.BlockSpec((1, cw), lambda i: (0, 0))],
        out_shape=[a_shape, a_shape, jax.ShapeDtypeStruct((1, cw), F32)],
        compiler_params=_cparams("arbitrary"),
        name="filter_taps_stage_a",
    )(zf, w1bd, b1, w2bd, b2, w3f, w3b, fr, dl, g_full, twr, twi)


def _fwda_kernel(g_ref, twr_ref, twi_ref, u_ref, are_ref, aim_ref, *, n1h, tj):
    gm = g_ref[...].astype(BF16)
    prev = None
    for sl in _CH_SPLITS:
        u = _tiles_to_rows(u_ref[:, :, sl])
        if prev is not None:
            _store_twiddled(prev[1], twr_ref, twi_ref, are_ref, aim_ref, prev[0], n1h, tj)
        prev = (sl, _dot(gm, u))
    _store_twiddled(prev[1], twr_ref, twi_ref, are_ref, aim_ref, prev[0], n1h, tj)


def _fwd_a(u4, g_half, twr, twi, tj):
    b, n1h, _, cb = u4.shape
    blk = pl.BlockSpec((None, n1h, tj, cb), lambda bi, ji: (bi, 0, ji, 0))
    t_spec = pl.BlockSpec((None, n1h, tj), lambda bi, ji: (ji, 0, 0))
    a_shape = jax.ShapeDtypeStruct(u4.shape, BF16)
    return pl.pallas_call(
        functools.partial(_fwda_kernel, n1h=n1h, tj=tj),
        grid=(b, LANES // tj),
        in_specs=[_const_spec((2 * n1h, n1h)), t_spec, t_spec, blk],
        out_specs=[blk, blk],
        out_shape=[a_shape, a_shape],
        compiler_params=_cparams("parallel", "parallel"),
        name="dft_stage_a",
    )(g_half, twr, twi, u4)


def _mid_kernel(f_ref, fi_ref, twr_ref, twi_ref, fre_ref, fim_ref, are_ref, aim_ref, bre_ref, bim_ref, *, kb):
    fm = f_ref[...].astype(BF16)
    fim = fi_ref[...].astype(BF16)
    fill = jnp.zeros((LANES - kb, LANES), F32)
    twr_t = jnp.concatenate([twr_ref[...], fill], axis=0).T
    twi_t = jnp.concatenate([twi_ref[...], fill], axis=0).T

    def fwd(k):
        return (_dot(fm, jnp.concatenate([are_ref[k], aim_ref[k]], axis=0)),
                _dot(fm, jnp.concatenate([fre_ref[k], fim_ref[k]], axis=0)))

    def spec(k, xh):
        x, h = xh
        xr, xi = x[:LANES], x[LANES:]
        hr, hi = h[:LANES], h[LANES:]
        return jnp.concatenate([xr * hr - xi * hi, xr * hi + xi * hr], axis=0).astype(BF16)

    def out(k, bc):
        br, bi = bc[:LANES], bc[LANES:]
        tr, ti = twr_t[:, k:k + 1], twi_t[:, k:k + 1]
        bre_ref[k] = (br * tr + bi * ti).astype(BF16)
        bim_ref[k] = (bi * tr - br * ti).astype(BF16)

    x = fwd(0)
    bc_prev = None
    for k in range(kb):
        y = spec(k, x)
        if k + 1 < kb:
            x = fwd(k + 1)
        if bc_prev is not None:
            out(k - 1, bc_prev)
        bc_prev = _dot(fim, y)
    out(kb - 1, bc_prev)


def _mid(are, aim, fblk, fblk_inv, twr, twi, afre, afim, order, kb):
    b, n1h, _, cb = are.shape
    a_spec = pl.BlockSpec((None, kb, LANES, cb), lambda ki, bi: (bi, ki, 0, 0))
    f_spec = pl.BlockSpec((None, kb, LANES, cb), lambda ki, bi: (order, ki, 0, 0))
    t_spec = pl.BlockSpec((kb, LANES), lambda ki, bi: (ki, 0))
    shape = jax.ShapeDtypeStruct(are.shape, BF16)
    return pl.pallas_call(
        functools.partial(_mid_kernel, kb=kb),
        grid=(n1h // kb, b),
        in_specs=[
            _const_spec((2 * LANES, 2 * LANES)),
            _const_spec((2 * LANES, 2 * LANES)),
            t_spec, t_spec,
            f_spec, f_spec,
            a_spec, a_spec,
        ],
        out_specs=[a_spec, a_spec],
        out_shape=[shape, shape],
        compiler_params=_cparams("parallel", "parallel"),
        name="dft_stage_c",
    )(fblk, fblk_inv, twr, twi, afre, afim, are, aim)


def _inva_kernel(gi_ref, bre_ref, bim_ref, xg_ref, v_ref, sb_ref, s_ref, *rest, n1h, tj, fuse_next):
    gim = gi_ref[...].astype(BF16)
    inv_l1 = 1.0 / s_ref[...]
    if fuse_next:
        g_ref, twr_ref, twi_ref, z_ref, are_ref, aim_ref = rest
        gm = g_ref[...].astype(BF16)
    else:
        (z_ref,) = rest

    def conv(sl):
        rhs = jnp.concatenate([_tiles_to_rows(bre_ref[:, :, sl]), _tiles_to_rows(bim_ref[:, :, sl])], axis=0)
        return _dot(gim, rhs)

    def gate(sl, y):
        z = xg_ref[:, :, sl].astype(F32) * (_rows_to_tiles(y, tj) * inv_l1[:, sl]
                                            + sb_ref[:, sl] * v_ref[:, :, sl].astype(F32))
        z = z.astype(BF16)
        z_ref[:, :, sl] = z
        return z

    s0, s1 = _CH_SPLITS
    y0 = conv(s0)
    y1 = conv(s1)
    z0 = gate(s0, y0)
    if fuse_next:
        r0 = _dot(gm, _tiles_to_rows(z0))
    z1 = gate(s1, y1)
    if fuse_next:
        _store_twiddled(r0, twr_ref, twi_ref, are_ref, aim_ref, s0, n1h, tj)
        r1 = _dot(gm, _tiles_to_rows(z1))
        _store_twiddled(r1, twr_ref, twi_ref, are_ref, aim_ref, s1, n1h, tj)


def _inv_a(bre, bim, g_inv, xg, v, sb_row, asum, order, tj, nxt=None):
    b, n1h, _, cb = bre.shape
    blk = pl.BlockSpec((None, n1h, tj, cb), lambda bi, ji: (bi, 0, ji, 0))
    in_specs = [_const_spec((n1h, 2 * n1h)), blk, blk, blk, blk, _const_spec((1, cb)),
                pl.BlockSpec((1, cb), lambda bi, ji: (0, order))]
    args = [g_inv, bre, bim, xg, v, sb_row, asum]
    out_specs = [blk]
    out_shape = [jax.ShapeDtypeStruct(bre.shape, BF16)]
    if nxt is not None:
        t_spec = pl.BlockSpec((None, n1h, tj), lambda bi, ji: (ji, 0, 0))
        in_specs += [_const_spec((2 * n1h, n1h)), t_spec, t_spec]
        args += list(nxt)
        out_specs += [blk, blk]
        out_shape += [jax.ShapeDtypeStruct(bre.shape, BF16)] * 2
    return pl.pallas_call(
        functools.partial(_inva_kernel, n1h=n1h, tj=tj, fuse_next=nxt is not None),
        grid=(b, LANES // tj),
        in_specs=in_specs,
        out_specs=out_specs,
        out_shape=out_shape,
        compiler_params=_cparams("parallel", "parallel"),
        name="idft_stage_a_gate",
    )(*args)


def _attn_kernel(sink_ref, q_ref, kp_ref, kc_ref, kn_ref, vp_ref, vc_ref, vn_ref, o_ref, *, nblk, sub):
    i = pl.program_id(1)
    blk = ATTN_BLOCK
    grp = N_HEADS // N_KV_HEADS
    rows = grp * blk
    qi = lax.broadcasted_iota(jnp.int32, (rows, 3 * blk), 0) % blk
    si = lax.broadcasted_iota(jnp.int32, (rows, 3 * blk), 1)
    mid = (si >= blk) & (si < 2 * blk)
    tri_prev = (si < blk) & (si >= qi)
    tri_next = (si >= 2 * blk) & (si - 2 * blk <= qi)
    hrow = lax.broadcasted_iota(jnp.int32, (rows, 1), 0) // blk
    kcat = jnp.concatenate([kp_ref[...], kc_ref[...], kn_ref[...]], axis=0)
    vcat = jnp.concatenate([vp_ref[...], vc_ref[...], vn_ref[...]], axis=0)
    for s in range(sub):
        j = sub * i + s
        valid = mid | (tri_prev & (j > 0)) | (tri_next & (j < nblk - 1))
        outs = []
        for g in range(N_KV_HEADS):
            lhs = jnp.concatenate([q_ref[s * blk:(s + 1) * blk, (grp * g + t) * LANES:(grp * g + t + 1) * LANES]
                                   for t in range(grp)], axis=0)
            sc = lax.dot_general(lhs, kcat[s * blk:(s + 3) * blk, g * LANES:(g + 1) * LANES],
                                 (((1,), (1,)), ((), ())), preferred_element_type=F32)
            sc = jnp.where(valid, sc, NEG_INF)
            sk = jnp.full((rows, 1), sink_ref[grp * g], F32)
            for t in range(1, grp):
                sk = jnp.where(hrow == t, sink_ref[grp * g + t], sk)
            sk = sk * LOG2E
            m = jnp.maximum(jnp.max(sc, axis=1, keepdims=True), sk)
            e = jnp.exp2(sc - m)
            den = jnp.sum(e, axis=1, keepdims=True) + jnp.exp2(sk - m)
            r = _dot(e.astype(BF16), vcat[s * blk:(s + 3) * blk, g * LANES:(g + 1) * LANES]) / den
            for t in range(0, grp, 2):
                outs.append(r[t * blk:(t + 1) * blk] + pltpu.roll(r[(t + 1) * blk:(t + 2) * blk], HEAD_DIM, 1))
        o_ref[s * blk:(s + 1) * blk, :] = jnp.concatenate(outs, axis=1).astype(BF16)


def _attention(q3, k3, v3, sink):
    b, seq_len, qw = q3.shape
    blk = ATTN_BLOCK
    sub = ATTN_SUB
    nblk = seq_len // blk
    wide = k3.shape[-1]
    prev = lambda bi, i: (bi, jnp.maximum(sub * i - 1, 0), 0)
    cur = lambda bi, i: (bi, i, 0)
    nxt = lambda bi, i: (bi, jnp.minimum(sub * (i + 1), nblk - 1), 0)
    edge = lambda f: pl.BlockSpec((None, blk, wide), f)
    body = pl.BlockSpec((None, sub * blk, wide), cur)
    return pl.pallas_call(
        functools.partial(_attn_kernel, nblk=nblk, sub=sub),
        grid=(b, nblk // sub),
        in_specs=[pl.BlockSpec(memory_space=pltpu.SMEM),
                  pl.BlockSpec((None, sub * blk, qw), cur),
                  edge(prev), body, edge(nxt), edge(prev), body, edge(nxt)],
        out_specs=pl.BlockSpec((None, sub * blk, ATTN_WIDTH), cur),
        out_shape=jax.ShapeDtypeStruct((b, seq_len, ATTN_WIDTH), BF16),
        compiler_params=_cparams("parallel", "parallel"),
        name="banded_attention",
    )(sink, q3, k3, k3, k3, v3, v3, v3)


def _final_kernel(x_ref, yh_ref, ya_ref, g_ref, why_ref, wat_ref, wo_ref, nw_ref, wg_ref, wu_ref, wd_ref,
                  o_ref):
    a = _dot(yh_ref[...], why_ref[...])
    b = _dot(ya_ref[...], wat_ref[...])
    merged = g_ref[:, :D_MODEL] * a + g_ref[:, D_MODEL:] * b
    x1 = x_ref[...] + _dot(merged.astype(BF16), wo_ref[...])
    ms = jnp.mean(x1 * x1, axis=-1, keepdims=True)
    f = (x1 * lax.rsqrt(ms + RMS_EPS) * nw_ref[...]).astype(BF16)
    acc = x1
    for lo, hi in FFN_CHUNKS:
        gt = _dot(f, wg_ref[:, lo:hi])
        upv = _dot(f, wu_ref[:, lo:hi])
        hid = gt * _sigmoid(gt) * upv
        acc = acc + _dot(hid.astype(BF16), wd_ref[lo:hi, :])
    o_ref[...] = acc


def _final(xf, yh, ya, g, why, wat, wo, nw, wg, wu, wd):
    n = xf.shape[0]
    tm = TM_FINAL
    row = lambda i: (i, 0)
    return pl.pallas_call(
        _final_kernel,
        grid=(n // tm,),
        in_specs=[
            pl.BlockSpec((tm, D_MODEL), row),
            pl.BlockSpec((tm, HYENA_WIDTH), row),
            pl.BlockSpec((tm, ATTN_WIDTH), row),
            pl.BlockSpec((tm, 2 * D_MODEL), row),
            _const_spec((HYENA_WIDTH, D_MODEL)),
            _const_spec((ATTN_WIDTH, D_MODEL)),
            _const_spec((D_MODEL, D_MODEL)),
            _const_spec((1, D_MODEL)),
            _const_spec((D_MODEL, FFN_HIDDEN)),
            _const_spec((D_MODEL, FFN_HIDDEN)),
            _const_spec((FFN_HIDDEN, D_MODEL)),
        ],
        out_specs=pl.BlockSpec((tm, D_MODEL), row),
        out_shape=jax.ShapeDtypeStruct((n, D_MODEL), F32),
        compiler_params=_cparams("parallel"),
        name="merge_ffn",
    )(xf, yh, ya, g, why, wat, wo, nw, wg, wu, wd)


@functools.lru_cache(maxsize=None)
def _dft_constants(n1h, tj):
    n1_len = 2 * n1h
    m = n1_len * LANES
    k1 = np.arange(n1h, dtype=np.float64)[:, None] + 0.5
    n1 = np.arange(n1_len, dtype=np.float64)[None, :]
    gc = np.exp(-2j * np.pi * k1 * n1 / n1_len)
    g_full = np.concatenate([gc.real, gc.imag], axis=0)
    g_inv = (2.0 / m) * np.concatenate([gc.real[:, :n1h].T, gc.imag[:, :n1h].T], axis=1)
    n2 = np.arange(LANES, dtype=np.float64)
    tw = np.exp(-2j * np.pi * k1 * n2[None, :] / m)
    fc = np.exp(-2j * np.pi * np.outer(n2, n2) / LANES)
    fblk = np.block([[fc.real, -fc.imag], [fc.imag, fc.real]])
    fblk_inv = np.block([[fc.real, fc.imag], [-fc.imag, fc.real]])
    f32 = lambda a: np.ascontiguousarray(a, dtype=np.float32)
    tiled = lambda a: a.reshape(n1h, LANES // tj, tj).transpose(1, 0, 2)
    return dict(g_full=f32(g_full), g_half=f32(g_full[:, :n1h]), g_inv=f32(g_inv), twr=f32(tw.real),
                twi=f32(tw.imag), twr_a=f32(tiled(tw.real)), twi_a=f32(tiled(tw.imag)), fblk=f32(fblk),
                fblk_inv=f32(fblk_inv))


@functools.lru_cache(maxsize=None)
def _filter_features(seq_len):
    n1h = seq_len // LANES
    n2 = np.arange(LANES)[:, None]
    n1 = np.arange(n1h)[None, :]
    slot_f = (LANES * n1 + n2).reshape(-1)
    slot_b = slot_f + seq_len
    neg_lag = slot_b > seq_len
    pos_b = np.where(neg_lag, 2 * seq_len - slot_b, 0)
    bands = np.linspace(1e-4, FILTER_BANDS - 1, FILTER_BANDS)

    def feats(pos, sgn):
        pos = pos.astype(np.float64)
        ang = (2.0 * math.pi / seq_len) * pos[:, None] * bands[None, :]
        pad = np.zeros((pos.shape[0], FILT_PAD - FILTER_EMB - 1))
        return np.concatenate([(pos / (seq_len - 1))[:, None], np.cos(ang), -np.sin(ang), sgn[:, None], pad], axis=-1)

    z = np.concatenate([feats(slot_f, np.ones(slot_f.shape)), feats(pos_b, np.where(neg_lag, -1.0, 0.0))], axis=-1)
    return np.ascontiguousarray(z, dtype=np.float32)


@functools.lru_cache(maxsize=None)
def _rope_tables(seq_len):
    inv = ROPE_THETA ** (-np.arange(0, HEAD_DIM, 2, dtype=np.float64) / HEAD_DIM)
    ang = np.arange(seq_len, dtype=np.float64)[:, None] * inv[None, :]
    cos, sin = np.cos(ang), np.sin(ang)
    zero = np.zeros_like(sin)
    reps = LANES // HEAD_DIM
    f32 = lambda a: np.ascontiguousarray(np.tile(a, (1, reps)), dtype=np.float32)
    return f32(np.concatenate([cos, cos], axis=1)), f32(np.concatenate([-sin, zero], axis=1)), \
        f32(np.concatenate([zero, sin], axis=1))


def _layer(x, p):
    b, seq_len, _ = x.shape
    n = b * seq_len
    n1h = seq_len // LANES
    xf = x.reshape(n, D_MODEL)
    tj = SUBLANES if n1h >= LANES else 4 * SUBLANES
    c = {k: jnp.asarray(v) for k, v in _dft_constants(n1h, tj).items()}
    kb = SUBLANES
    r4 = lambda a: a.reshape(b, n1h, LANES, HYENA_WIDTH)

    cos, sa, sb = (jnp.asarray(t) for t in _rope_tables(seq_len))
    x1, x2, v, q, k4, v4, g = _inproj(xf, seq_len, p["attn_norm_w"], p["w_in"], p["conv_w"], p["conv_b"],
                                      p["q_norm_w"], p["k_norm_w"], cos, sa, sb, p["ones_bd"])

    afre, afim, asum = _filter_stage_a(jnp.asarray(_filter_features(seq_len)), p["filt_w1"], p["filt_b1"],
                                       p["filt_w2"], p["filt_b2"], p["filt_w3f"], p["filt_w3b"], p["filt_freq"],
                                       p["decay"], c["g_full"], c["twr_a"], c["twi_a"], n1h, tj)

    are, aim = _fwd_a(r4(v), c["g_half"], c["twr_a"], c["twi_a"], tj)
    bre, bim = _mid(are, aim, c["fblk"], c["fblk_inv"], c["twr"], c["twi"], afre, afim, 0, kb)
    z1, are, aim = _inv_a(bre, bim, c["g_inv"], r4(x1), r4(v), p["hyena_bias"][0:1], asum, 0, tj,
                          nxt=(c["g_half"], c["twr_a"], c["twi_a"]))
    bre, bim = _mid(are, aim, c["fblk"], c["fblk_inv"], c["twr"], c["twi"], afre, afim, 1, kb)
    (yh,) = _inv_a(bre, bim, c["g_inv"], r4(x2), z1, p["hyena_bias"][1:2], asum, 1, tj)

    ya = _attention(q.reshape(b, seq_len, 2 * ATTN_WIDTH), k4.reshape(b, seq_len, 2 * LANES),
                    v4.reshape(b, seq_len, 2 * LANES), p["attn_sink"])

    out = _final(xf, yh.reshape(n, HYENA_WIDTH), ya.reshape(n, ATTN_WIDTH), g, p["w_hy_out"], p["w_at_out"],
                 p["w_o"], p["ffn_norm_w"], p["w_gate"], p["w_up"], p["w_down"])
    return out.reshape(b, seq_len, D_MODEL)


def _block_diag2(w):
    z = jnp.zeros_like(w)
    return jnp.concatenate([jnp.concatenate([w, z], axis=1), jnp.concatenate([z, w], axis=1)], axis=0)


def kernel(x_prompt, x_sample, attn_norm_w, w_in, hyena_conv_w, hyena_conv_b, filt_w1, filt_b1, filt_w2, filt_b2,
           filt_w3, filt_freq, hyena_bias, q_norm_w, k_norm_w, attn_sink, w_hy_out, w_at_out, w_o, ffn_norm_w,
           w_gate, w_up, w_down):
    cw = HYENA_ORDER * HYENA_WIDTH
    max_decay = math.log(DECAY_TARGET) / DECAY_FAST_PCT
    min_decay = math.log(DECAY_TARGET) / DECAY_SLOW_PCT
    deltas = jnp.abs(jnp.linspace(min_decay, max_decay, HYENA_WIDTH, dtype=F32))
    head = np.arange(ATTN_WIDTH) // HEAD_DIM
    ones_bd = jnp.asarray((head[:, None] == head[None, :]).astype(np.float32) / HEAD_DIM).astype(BF16)
    w3 = filt_w3[0].reshape(FILTER_HIDDEN, HYENA_ORDER, 2, HYENA_WIDTH).transpose(2, 0, 1, 3)
    w3 = w3.reshape(2, FILTER_HIDDEN, cw)
    w3_zero = jnp.zeros((FILTER_HIDDEN, cw), F32)
    twice = lambda a: jnp.tile(a, 2)[None, :]
    p = dict(
        attn_norm_w=attn_norm_w[0][None, :],
        w_in=w_in[0].astype(BF16),
        conv_w=hyena_conv_w[0],
        conv_b=hyena_conv_b[0][None, :],
        filt_w1=_block_diag2(jnp.pad(filt_w1[0], ((0, FILT_PAD - FILTER_EMB), (0, 0)))),
        filt_b1=twice(filt_b1[0]),
        filt_w2=_block_diag2(filt_w2[0]),
        filt_b2=twice(filt_b2[0]),
        filt_w3f=jnp.concatenate([w3[0], w3_zero], axis=0),
        filt_w3b=jnp.concatenate([w3_zero, w3[1]], axis=0),
        filt_freq=twice(filt_freq[0]),
        decay=deltas[None, :],
        hyena_bias=hyena_bias[0],
        q_norm_w=jnp.tile(q_norm_w[0], N_HEADS)[None, :],
        k_norm_w=jnp.tile(k_norm_w[0], N_KV_HEADS)[None, :],
        attn_sink=attn_sink[0],
        ones_bd=ones_bd,
        w_hy_out=w_hy_out[0].astype(BF16),
        w_at_out=w_at_out[0].astype(BF16),
        w_o=w_o[0].astype(BF16),
        ffn_norm_w=ffn_norm_w[0][None, :],
        w_gate=w_gate[0].astype(BF16),
        w_up=w_up[0].astype(BF16),
        w_down=w_down[0].astype(BF16),
    )
    return (_layer(x_prompt, p), _layer(x_sample, p))
```

```python
import functools
import math

import numpy as np
import jax
import jax.numpy as jnp
from jax import lax
from jax.experimental import pallas as pl
from jax.experimental.pallas import tpu as pltpu

F32 = jnp.float32
BF16 = jnp.bfloat16

D_MODEL = 1024
HYENA_WIDTH = 512
HYENA_ORDER = 2
FILTER_BANDS = 16
FILTER_EMB = 1 + 2 * FILTER_BANDS
FILTER_HIDDEN = 64
DECAY_FAST_PCT = 0.3
DECAY_SLOW_PCT = 1.5
DECAY_TARGET = 1e-2
DECAY_SHIFT = 0.05
N_HEADS = 8
N_KV_HEADS = 2
HEAD_DIM = 64
ATTN_WIDTH = N_HEADS * HEAD_DIM
KV_WIDTH = N_KV_HEADS * HEAD_DIM
WINDOW = 128
ROPE_THETA = 10000.0
FFN_HIDDEN = 2816
RMS_EPS = 1e-6
NEG_INF = -1e30
LOG2E = math.log2(math.e)

C_HY = 3 * HYENA_WIDTH
C_Q = C_HY + ATTN_WIDTH
C_K = C_Q + KV_WIDTH
C_V = C_K + KV_WIDTH
IN_WIDTH = C_V + 2 * D_MODEL

LANES = 128
SUBLANES = 8
ATTN_BLOCK = 128
ATTN_SUB = 4
VMEM_LIMIT = 56 * 1024 * 1024

TM_INPROJ = 512
TM_FINAL = 512
FFN_CHUNKS = ((0, 1280), (1280, FFN_HIDDEN))
_CH_SPLITS = (slice(0, HYENA_WIDTH // 2), slice(HYENA_WIDTH // 2, HYENA_WIDTH))
FILT_PAD = 64


def _cparams(*sem):
    return pltpu.CompilerParams(dimension_semantics=sem, vmem_limit_bytes=VMEM_LIMIT)


def _const_spec(shape):
    nd = len(shape)
    return pl.BlockSpec(shape, lambda *_: (0,) * nd, pipeline_mode=pl.Buffered(1))


def _dot(a, b):
    return jnp.dot(a, b, preferred_element_type=F32)


def _sigmoid(x):
    return 0.5 * jnp.tanh(0.5 * x) + 0.5


def _split(a):
    hi = a.astype(BF16)
    lo = (a - hi.astype(F32)).astype(BF16)
    return hi, lo


def _dot3(a, w):
    ah, al = _split(a)
    wh, wl = _split(w)
    return _dot(ah, wh) + _dot(al, wh) + _dot(ah, wl)


def _inproj_kernel(*refs, tm, tiles_per_seq, n_cast):
    (x_ref, xp_ref, xn_ref, nw_ref, w_ref, cw_ref, cb_ref, qw_ref, kw_ref, cos_ref, sa_ref, sb_ref,
     ones_ref) = refs[:13]
    cast_in = refs[13:13 + n_cast]
    x1_ref, x2_ref, v_ref, q_ref, k4_ref, v4_ref, g_ref = refs[13 + n_cast:20 + n_cast]
    cast_out = refs[20 + n_cast:20 + 2 * n_cast]
    pad_ref = refs[-1]
    for src_ref, dst_ref in zip(cast_in, cast_out):
        dst_ref[...] = src_ref[...].astype(BF16)
    pos = pl.program_id(0) % tiles_per_seq
    nw = nw_ref[...]

    def norm(xv):
        ms = jnp.mean(xv * xv, axis=-1, keepdims=True)
        return (xv * lax.rsqrt(ms + RMS_EPS) * nw).astype(BF16)

    h_all = norm(jnp.concatenate([x_ref[...], xp_ref[...], xn_ref[...]], axis=0))
    h = h_all[:tm]

    hy_all = _dot(h_all, w_ref[:, :C_HY])
    hy = hy_all[:tm]
    hyh = hy_all[tm:]
    prev_ok = (pos > 0).astype(F32)
    next_ok = (pos < tiles_per_seq - 1).astype(F32)
    pad_ref[0:SUBLANES] = hyh[0:SUBLANES] * prev_ok
    pad_ref[SUBLANES:SUBLANES + tm] = hy
    pad_ref[SUBLANES + tm:2 * SUBLANES + tm] = hyh[SUBLANES:] * next_ok
    up = pad_ref[SUBLANES - 1:SUBLANES - 1 + tm]
    un = pad_ref[SUBLANES + 1:SUBLANES + 1 + tm]
    cw = cw_ref[...]
    uc = cw[0:1] * up + cw[1:2] * hy + cw[2:3] * un + cb_ref[...]
    x1_ref[...] = uc[:, :HYENA_WIDTH].astype(BF16)
    x2_ref[...] = uc[:, HYENA_WIDTH:2 * HYENA_WIDTH].astype(BF16)
    v_ref[...] = uc[:, 2 * HYENA_WIDTH:].astype(BF16)

    def head_norm_rope(t, wrow, ones, reps):
        ms = _dot((t * t).astype(BF16), ones)
        tn = t * lax.rsqrt(ms + RMS_EPS) * wrow
        width = t.shape[1]
        cos = jnp.concatenate([cos_ref[...]] * reps, axis=1)
        sa = jnp.concatenate([sa_ref[...]] * reps, axis=1)
        sb = jnp.concatenate([sb_ref[...]] * reps, axis=1)
        half = HEAD_DIM // 2
        return tn * cos + pltpu.roll(tn, width - half, 1) * sa + pltpu.roll(tn, half, 1) * sb

    qkv = _dot(h, w_ref[:, C_HY:C_V])
    q = qkv[:, :ATTN_WIDTH]
    q = head_norm_rope(q, qw_ref[...], ones_ref[...], ATTN_WIDTH // LANES) * (HEAD_DIM ** -0.5 * LOG2E)
    blocks = []
    for p in range(N_HEADS // 2):
        qp = q[:, p * LANES:(p + 1) * LANES]
        blocks += [qp, pltpu.roll(qp, HEAD_DIM, 1)]
    q_ref[...] = jnp.concatenate(blocks, axis=1).astype(BF16)

    k = qkv[:, ATTN_WIDTH:ATTN_WIDTH + KV_WIDTH]
    k = head_norm_rope(k, kw_ref[...], ones_ref[:KV_WIDTH, :KV_WIDTH], 1)
    v = qkv[:, ATTN_WIDTH + KV_WIDTH:]

    lo_half = lax.broadcasted_iota(jnp.int32, (tm, KV_WIDTH), 1) < HEAD_DIM

    def spread(t):
        zero = jnp.zeros_like(t)
        return jnp.concatenate([jnp.where(lo_half, t, zero), jnp.where(lo_half, pltpu.roll(t, HEAD_DIM, 1), zero)],
                               axis=1).astype(BF16)

    k4_ref[...] = spread(k)
    v4_ref[...] = spread(v)
    g_ref[...] = _sigmoid(_dot(h, w_ref[:, C_V:]))


def _inproj(xf, seq_len, nw, w_in_bf, cw, cb, qw, kw, cos, sa, sb, ones_bd, cast=()):
    n = xf.shape[0]
    tm = TM_INPROJ
    tiles_per_seq = seq_len // tm
    nblk8 = n // SUBLANES
    r8 = tm // SUBLANES
    steps = n // tm
    kern = functools.partial(_inproj_kernel, tm=tm, tiles_per_seq=tiles_per_seq, n_cast=len(cast))
    row = lambda i: (i, 0)
    cast_specs = []
    for w in cast:
        per = 1 if (w.shape[0] // steps) % (2 * SUBLANES) == 0 else 2
        rows_blk = w.shape[0] * per // steps
        cast_specs.append(pl.BlockSpec((rows_blk, w.shape[1]), lambda i, per=per: (i // per, 0)))
    return pl.pallas_call(
        kern,
        grid=(n // tm,),
        in_specs=[
            pl.BlockSpec((tm, D_MODEL), row),
            pl.BlockSpec((SUBLANES, D_MODEL), lambda i: (jnp.maximum(i * r8 - 1, 0), 0)),
            pl.BlockSpec((SUBLANES, D_MODEL), lambda i: (jnp.minimum((i + 1) * r8, nblk8 - 1), 0)),
            _const_spec((1, D_MODEL)),
            _const_spec((D_MODEL, IN_WIDTH)),
            _const_spec((3, C_HY)),
            _const_spec((1, C_HY)),
            _const_spec((1, ATTN_WIDTH)),
            _const_spec((1, KV_WIDTH)),
            pl.BlockSpec((tm, LANES), lambda i: (i % tiles_per_seq, 0)),
            pl.BlockSpec((tm, LANES), lambda i: (i % tiles_per_seq, 0)),
            pl.BlockSpec((tm, LANES), lambda i: (i % tiles_per_seq, 0)),
            _const_spec((ATTN_WIDTH, ATTN_WIDTH)),
        ] + cast_specs,
        out_specs=[
            pl.BlockSpec((tm, HYENA_WIDTH), row),
            pl.BlockSpec((tm, HYENA_WIDTH), row),
            pl.BlockSpec((tm, HYENA_WIDTH), row),
            pl.BlockSpec((tm, 2 * ATTN_WIDTH), row),
            pl.BlockSpec((tm, 2 * LANES), row),
            pl.BlockSpec((tm, 2 * LANES), row),
            pl.BlockSpec((tm, 2 * D_MODEL), row),
        ] + cast_specs,
        out_shape=[
            jax.ShapeDtypeStruct((n, HYENA_WIDTH), BF16),
            jax.ShapeDtypeStruct((n, HYENA_WIDTH), BF16),
            jax.ShapeDtypeStruct((n, HYENA_WIDTH), BF16),
            jax.ShapeDtypeStruct((n, 2 * ATTN_WIDTH), BF16),
            jax.ShapeDtypeStruct((n, 2 * LANES), BF16),
            jax.ShapeDtypeStruct((n, 2 * LANES), BF16),
            jax.ShapeDtypeStruct((n, 2 * D_MODEL), F32),
        ] + [jax.ShapeDtypeStruct(w.shape, BF16) for w in cast],
        scratch_shapes=[pltpu.VMEM((tm + 2 * SUBLANES, C_HY), F32)],
        compiler_params=_cparams("parallel"),
        name="inproj",
    )(xf, xf, xf, nw, w_in_bf, cw, cb, qw, kw, cos, sa, sb, ones_bd, *cast)


def _rows_to_tiles(r, tj):
    return r.reshape(r.shape[0], tj, r.shape[1] // tj)


def _tiles_to_rows(t):
    return t.reshape(t.shape[0], t.shape[1] * t.shape[2])


def _twiddle_tiles(r, twr_ref, twi_ref, n1h, tj):
    c = r.shape[1] // tj
    re, im = [], []
    for j in range(tj):
        ar, ai = r[:n1h, j * c:(j + 1) * c], r[n1h:, j * c:(j + 1) * c]
        tr, ti = twr_ref[:, j:j + 1], twi_ref[:, j:j + 1]
        re.append(ar * tr - ai * ti)
        im.append(ar * ti + ai * tr)
    return (_rows_to_tiles(jnp.concatenate(re, axis=1), tj).astype(BF16),
            _rows_to_tiles(jnp.concatenate(im, axis=1), tj).astype(BF16))


def _store_twiddled(r, twr_ref, twi_ref, are_ref, aim_ref, sl, n1h, tj):
    re, im = _twiddle_tiles(r, twr_ref, twi_ref, n1h, tj)
    are_ref[:, :, sl] = re
    aim_ref[:, :, sl] = im


def _filt_kernel(z_ref, w1_ref, b1_ref, w2_ref, b2_ref, w3f_ref, w3b_ref, fr_ref, dl_ref, g_ref, twr_ref, twi_ref,
                 are_ref, aim_ref, s_ref, *, n1h, tj):
    i = pl.program_id(0)
    z = z_ref[...]
    fr = fr_ref[...]
    h = jnp.sin(fr * (_dot3(z, w1_ref[...]) + b1_ref[...]))
    h = jnp.sin(fr * (_dot3(h, w2_ref[...]) + b2_ref[...]))
    hb = h.astype(BF16)
    tf = z[:, 0:1]
    tb = z[:, FILT_PAD:FILT_PAD + 1]
    sgn = z[:, FILT_PAD + FILTER_EMB:FILT_PAD + FILTER_EMB + 1]
    gm = g_ref[...].astype(BF16)
    cb = HYENA_WIDTH
    parts = []
    for o in range(HYENA_ORDER):
        sl = slice(o * cb, (o + 1) * cb)
        dl = dl_ref[...]

        def taps(w_ref, t):
            return _dot(hb, w_ref[:, sl].astype(BF16)) * (jnp.exp(-t * dl) + DECAY_SHIFT)

        kf = taps(w3f_ref, tf)
        kb = taps(w3b_ref, tb) * sgn
        parts.append(jnp.sum(jnp.abs(kf), axis=0, keepdims=True) + jnp.sum(jnp.abs(kb), axis=0, keepdims=True))
        cols = [jnp.concatenate([kf[j * n1h:(j + 1) * n1h], kb[j * n1h:(j + 1) * n1h]], axis=0).astype(BF16)
                for j in range(tj)]
        re, im = _twiddle_tiles(_dot(gm, jnp.concatenate(cols, axis=1)), twr_ref, twi_ref, n1h, tj)
        are_ref[o] = re
        aim_ref[o] = im
    part = jnp.concatenate(parts, axis=1)

    @pl.when(i == 0)
    def _():
        s_ref[...] = part

    @pl.when(i > 0)
    def _():
        s_ref[...] += part


def _filter_stage_a(zf, w1bd, b1, w2bd, b2, w3f, w3b, fr, dl, g_full, twr, twi, n1h, tj):
    cw = HYENA_ORDER * HYENA_WIDTH
    hid2 = 2 * FILTER_HIDDEN
    a_shape = jax.ShapeDtypeStruct((HYENA_ORDER, n1h, LANES, HYENA_WIDTH), BF16)
    a_spec = pl.BlockSpec((HYENA_ORDER, n1h, tj, HYENA_WIDTH), lambda i: (0, 0, i, 0))
    t_spec = pl.BlockSpec((None, n1h, tj), lambda i: (i, 0, 0))
    return pl.pallas_call(
        functools.partial(_filt_kernel, n1h=n1h, tj=tj),
        grid=(LANES // tj,),
        in_specs=[
            pl.BlockSpec((tj * n1h, 2 * FILT_PAD), lambda i: (i, 0)),
            _const_spec((2 * FILT_PAD, hid2)),
            _const_spec((1, hid2)),
            _const_spec((hid2, hid2)),
            _const_spec((1, hid2)),
            _const_spec((hid2, cw)),
            _const_spec((hid2, cw)),
            _const_spec((1, hid2)),
            _const_spec((1, HYENA_WIDTH)),
            _const_spec((2 * n1h, 2 * n1h)),
            t_spec, t_spec,
        ],
        out_specs=[a_spec, a_spec, pl.BlockSpec((1, cw), lambda i: (0, 0))],
        out_shape=[a_shape, a_shape, jax.ShapeDtypeStruct((1, cw), F32)],
        compiler_params=_cparams("arbitrary"),
        name="filter_taps_stage_a",
    )(zf, w1bd, b1, w2bd, b2, w3f, w3b, fr, dl, g_full, twr, twi)


def _fwda_kernel(g_ref, twr_ref, twi_ref, u_ref, are_ref, aim_ref, *, n1h, tj):
    gm = g_ref[...].astype(BF16)
    prev = None
    for sl in _CH_SPLITS:
        u = _tiles_to_rows(u_ref[:, :, sl])
        if prev is not None:
            _store_twiddled(prev[1], twr_ref, twi_ref, are_ref, aim_ref, prev[0], n1h, tj)
        prev = (sl, _dot(gm, u))
    _store_twiddled(prev[1], twr_ref, twi_ref, are_ref, aim_ref, prev[0], n1h, tj)


def _fwd_a(u4, g_half, twr, twi, tj):
    b, n1h, _, cb = u4.shape
    blk = pl.BlockSpec((None, n1h, tj, cb), lambda bi, ji: (bi, 0, ji, 0))
    t_spec = pl.BlockSpec((None, n1h, tj), lambda bi, ji: (ji, 0, 0))
    a_shape = jax.ShapeDtypeStruct(u4.shape, BF16)
    return pl.pallas_call(
        functools.partial(_fwda_kernel, n1h=n1h, tj=tj),
        grid=(b, LANES // tj),
        in_specs=[_const_spec((2 * n1h, n1h)), t_spec, t_spec, blk],
        out_specs=[blk, blk],
        out_shape=[a_shape, a_shape],
        compiler_params=_cparams("parallel", "parallel"),
        name="dft_stage_a",
    )(g_half, twr, twi, u4)


def _mid_kernel(f_ref, fi_ref, twr_ref, twi_ref, fre_ref, fim_ref, are_ref, aim_ref, bre_ref, bim_ref, *, kb):
    fm = f_ref[...].astype(BF16)
    fim = fi_ref[...].astype(BF16)
    fill = jnp.zeros((LANES - kb, LANES), F32)
    twr_t = jnp.concatenate([twr_ref[...], fill], axis=0).T
    twi_t = jnp.concatenate([twi_ref[...], fill], axis=0).T

    def fwd(k):
        return (_dot(fm, jnp.concatenate([are_ref[k], aim_ref[k]], axis=0)),
                _dot(fm, jnp.concatenate([fre_ref[k], fim_ref[k]], axis=0)))

    def spec(k, xh):
        x, h = xh
        xr, xi = x[:LANES], x[LANES:]
        hr, hi = h[:LANES], h[LANES:]
        return jnp.concatenate([xr * hr - xi * hi, xr * hi + xi * hr], axis=0).astype(BF16)

    def out(k, bc):
        br, bi = bc[:LANES], bc[LANES:]
        tr, ti = twr_t[:, k:k + 1], twi_t[:, k:k + 1]
        bre_ref[k] = (br * tr + bi * ti).astype(BF16)
        bim_ref[k] = (bi * tr - br * ti).astype(BF16)

    x = fwd(0)
    bc_prev = None
    for k in range(kb):
        y = spec(k, x)
        if k + 1 < kb:
            x = fwd(k + 1)
        if bc_prev is not None:
            out(k - 1, bc_prev)
        bc_prev = _dot(fim, y)
    out(kb - 1, bc_prev)


def _mid(are, aim, fblk, fblk_inv, twr, twi, afre, afim, order, kb):
    b, n1h, _, cb = are.shape
    a_spec = pl.BlockSpec((None, kb, LANES, cb), lambda ki, bi: (bi, ki, 0, 0))
    f_spec = pl.BlockSpec((None, kb, LANES, cb), lambda ki, bi: (order, ki, 0, 0))
    t_spec = pl.BlockSpec((kb, LANES), lambda ki, bi: (ki, 0))
    shape = jax.ShapeDtypeStruct(are.shape, BF16)
    return pl.pallas_call(
        functools.partial(_mid_kernel, kb=kb),
        grid=(n1h // kb, b),
        in_specs=[
            _const_spec((2 * LANES, 2 * LANES)),
            _const_spec((2 * LANES, 2 * LANES)),
            t_spec, t_spec,
            f_spec, f_spec,
            a_spec, a_spec,
        ],
        out_specs=[a_spec, a_spec],
        out_shape=[shape, shape],
        compiler_params=_cparams("parallel", "parallel"),
        name="dft_stage_c",
    )(fblk, fblk_inv, twr, twi, afre, afim, are, aim)


def _inva_kernel(gi_ref, bre_ref, bim_ref, xg_ref, v_ref, sb_ref, s_ref, *rest, n1h, tj, fuse_next):
    gim = gi_ref[...].astype(BF16)
    inv_l1 = 1.0 / s_ref[...]
    if fuse_next:
        g_ref, twr_ref, twi_ref, z_ref, are_ref, aim_ref = rest
        gm = g_ref[...].astype(BF16)
    else:
        (z_ref,) = rest

    def conv(sl):
        rhs = jnp.concatenate([_tiles_to_rows(bre_ref[:, :, sl]), _tiles_to_rows(bim_ref[:, :, sl])], axis=0)
        return _dot(gim, rhs)

    def gate(sl, y):
        z = xg_ref[:, :, sl].astype(F32) * (_rows_to_tiles(y, tj) * inv_l1[:, sl]
                                            + sb_ref[:, sl] * v_ref[:, :, sl].astype(F32))
        z = z.astype(BF16)
        z_ref[:, :, sl] = z
        return z

    s0, s1 = _CH_SPLITS
    y0 = conv(s0)
    y1 = conv(s1)
    z0 = gate(s0, y0)
    if fuse_next:
        r0 = _dot(gm, _tiles_to_rows(z0))
    z1 = gate(s1, y1)
    if fuse_next:
        _store_twiddled(r0, twr_ref, twi_ref, are_ref, aim_ref, s0, n1h, tj)
        r1 = _dot(gm, _tiles_to_rows(z1))
        _store_twiddled(r1, twr_ref, twi_ref, are_ref, aim_ref, s1, n1h, tj)


def _inv_a(bre, bim, g_inv, xg, v, sb_row, asum, order, tj, nxt=None):
    b, n1h, _, cb = bre.shape
    blk = pl.BlockSpec((None, n1h, tj, cb), lambda bi, ji: (bi, 0, ji, 0))
    in_specs = [_const_spec((n1h, 2 * n1h)), blk, blk, blk, blk, _const_spec((1, cb)),
                pl.BlockSpec((1, cb), lambda bi, ji: (0, order))]
    args = [g_inv, bre, bim, xg, v, sb_row, asum]
    out_specs = [blk]
    out_shape = [jax.ShapeDtypeStruct(bre.shape, BF16)]
    if nxt is not None:
        t_spec = pl.BlockSpec((None, n1h, tj), lambda bi, ji: (ji, 0, 0))
        in_specs += [_const_spec((2 * n1h, n1h)), t_spec, t_spec]
        args += list(nxt)
        out_specs += [blk, blk]
        out_shape += [jax.ShapeDtypeStruct(bre.shape, BF16)] * 2
    return pl.pallas_call(
        functools.partial(_inva_kernel, n1h=n1h, tj=tj, fuse_next=nxt is not None),
        grid=(b, LANES // tj),
        in_specs=in_specs,
        out_specs=out_specs,
        out_shape=out_shape,
        compiler_params=_cparams("parallel", "parallel"),
        name="idft_stage_a_gate",
    )(*args)


def _attn_kernel(sink_ref, q_ref, kp_ref, kc_ref, kn_ref, vp_ref, vc_ref, vn_ref, o_ref, *, nblk, sub):
    i = pl.program_id(1)
    blk = ATTN_BLOCK
    grp = N_HEADS // N_KV_HEADS
    rows = grp * blk
    qi = lax.broadcasted_iota(jnp.int32, (rows, 3 * blk), 0) % blk
    si = lax.broadcasted_iota(jnp.int32, (rows, 3 * blk), 1)
    mid = (si >= blk) & (si < 2 * blk)
    tri_prev = (si < blk) & (si >= qi)
    tri_next = (si >= 2 * blk) & (si - 2 * blk <= qi)
    hrow = lax.broadcasted_iota(jnp.int32, (rows, 1), 0) // blk
    kcat = jnp.concatenate([kp_ref[...], kc_ref[...], kn_ref[...]], axis=0)
    vcat = jnp.concatenate([vp_ref[...], vc_ref[...], vn_ref[...]], axis=0)
    for s in range(sub):
        j = sub * i + s
        valid = mid | (tri_prev & (j > 0)) | (tri_next & (j < nblk - 1))
        outs = []
        for g in range(N_KV_HEADS):
            lhs = jnp.concatenate([q_ref[s * blk:(s + 1) * blk, (grp * g + t) * LANES:(grp * g + t + 1) * LANES]
                                   for t in range(grp)], axis=0)
            sc = lax.dot_general(lhs, kcat[s * blk:(s + 3) * blk, g * LANES:(g + 1) * LANES],
                                 (((1,), (1,)), ((), ())), preferred_element_type=F32)
            sc = jnp.where(valid, sc, NEG_INF)
            sk = jnp.full((rows, 1), sink_ref[grp * g], F32)
            for t in range(1, grp):
                sk = jnp.where(hrow == t, sink_ref[grp * g + t], sk)
            sk = sk * LOG2E
            m = jnp.maximum(jnp.max(sc, axis=1, keepdims=True), sk)
            e = jnp.exp2(sc - m)
            den = jnp.sum(e, axis=1, keepdims=True) + jnp.exp2(sk - m)
            r = _dot(e.astype(BF16), vcat[s * blk:(s + 3) * blk, g * LANES:(g + 1) * LANES]) / den
            for t in range(0, grp, 2):
                outs.append(r[t * blk:(t + 1) * blk] + pltpu.roll(r[(t + 1) * blk:(t + 2) * blk], HEAD_DIM, 1))
        o_ref[s * blk:(s + 1) * blk, :] = jnp.concatenate(outs, axis=1).astype(BF16)


def _attention(q3, k3, v3, sink):
    b, seq_len, qw = q3.shape
    blk = ATTN_BLOCK
    sub = ATTN_SUB
    nblk = seq_len // blk
    wide = k3.shape[-1]
    prev = lambda bi, i: (bi, jnp.maximum(sub * i - 1, 0), 0)
    cur = lambda bi, i: (bi, i, 0)
    nxt = lambda bi, i: (bi, jnp.minimum(sub * (i + 1), nblk - 1), 0)
    edge = lambda f: pl.BlockSpec((None, blk, wide), f)
    body = pl.BlockSpec((None, sub * blk, wide), cur)
    return pl.pallas_call(
        functools.partial(_attn_kernel, nblk=nblk, sub=sub),
        grid=(b, nblk // sub),
        in_specs=[pl.BlockSpec(memory_space=pltpu.SMEM),
                  pl.BlockSpec((None, sub * blk, qw), cur),
                  edge(prev), body, edge(nxt), edge(prev), body, edge(nxt)],
        out_specs=pl.BlockSpec((None, sub * blk, ATTN_WIDTH), cur),
        out_shape=jax.ShapeDtypeStruct((b, seq_len, ATTN_WIDTH), BF16),
        compiler_params=_cparams("parallel", "parallel"),
        name="banded_attention",
    )(sink, q3, k3, k3, k3, v3, v3, v3)


def _final_kernel(x_ref, yh_ref, ya_ref, g_ref, why_ref, wat_ref, wo_ref, nw_ref, wg_ref, wu_ref, wd_ref,
                  o_ref):
    a = _dot(yh_ref[...], why_ref[...])
    b = _dot(ya_ref[...], wat_ref[...])
    merged = g_ref[:, :D_MODEL] * a + g_ref[:, D_MODEL:] * b
    x1 = x_ref[...] + _dot(merged.astype(BF16), wo_ref[...])
    ms = jnp.mean(x1 * x1, axis=-1, keepdims=True)
    f = (x1 * lax.rsqrt(ms + RMS_EPS) * nw_ref[...]).astype(BF16)
    acc = x1
    for lo, hi in FFN_CHUNKS:
        gt = _dot(f, wg_ref[:, lo:hi])
        upv = _dot(f, wu_ref[:, lo:hi])
        hid = gt * _sigmoid(gt) * upv
        acc = acc + _dot(hid.astype(BF16), wd_ref[lo:hi, :])
    o_ref[...] = acc


def _final(xf, yh, ya, g, why, wat, wo, nw, wg, wu, wd):
    n = xf.shape[0]
    tm = TM_FINAL
    row = lambda i: (i, 0)
    return pl.pallas_call(
        _final_kernel,
        grid=(n // tm,),
        in_specs=[
            pl.BlockSpec((tm, D_MODEL), row),
            pl.BlockSpec((tm, HYENA_WIDTH), row),
            pl.BlockSpec((tm, ATTN_WIDTH), row),
            pl.BlockSpec((tm, 2 * D_MODEL), row),
            _const_spec((HYENA_WIDTH, D_MODEL)),
            _const_spec((ATTN_WIDTH, D_MODEL)),
            _const_spec((D_MODEL, D_MODEL)),
            _const_spec((1, D_MODEL)),
            _const_spec((D_MODEL, FFN_HIDDEN)),
            _const_spec((D_MODEL, FFN_HIDDEN)),
            _const_spec((FFN_HIDDEN, D_MODEL)),
        ],
        out_specs=pl.BlockSpec((tm, D_MODEL), row),
        out_shape=jax.ShapeDtypeStruct((n, D_MODEL), F32),
        compiler_params=_cparams("parallel"),
        name="merge_ffn",
    )(xf, yh, ya, g, why, wat, wo, nw, wg, wu, wd)


@functools.lru_cache(maxsize=None)
def _dft_constants(n1h, tj):
    n1_len = 2 * n1h
    m = n1_len * LANES
    k1 = np.arange(n1h, dtype=np.float64)[:, None] + 0.5
    n1 = np.arange(n1_len, dtype=np.float64)[None, :]
    gc = np.exp(-2j * np.pi * k1 * n1 / n1_len)
    g_full = np.concatenate([gc.real, gc.imag], axis=0)
    g_inv = (2.0 / m) * np.concatenate([gc.real[:, :n1h].T, gc.imag[:, :n1h].T], axis=1)
    n2 = np.arange(LANES, dtype=np.float64)
    tw = np.exp(-2j * np.pi * k1 * n2[None, :] / m)
    fc = np.exp(-2j * np.pi * np.outer(n2, n2) / LANES)
    fblk = np.block([[fc.real, -fc.imag], [fc.imag, fc.real]])
    fblk_inv = np.block([[fc.real, fc.imag], [-fc.imag, fc.real]])
    f32 = lambda a: np.ascontiguousarray(a, dtype=np.float32)
    tiled = lambda a: a.reshape(n1h, LANES // tj, tj).transpose(1, 0, 2)
    return dict(g_full=f32(g_full), g_half=f32(g_full[:, :n1h]), g_inv=f32(g_inv), twr=f32(tw.real),
                twi=f32(tw.imag), twr_a=f32(tiled(tw.real)), twi_a=f32(tiled(tw.imag)), fblk=f32(fblk),
                fblk_inv=f32(fblk_inv))


@functools.lru_cache(maxsize=None)
def _filter_features(seq_len):
    n1h = seq_len // LANES
    n2 = np.arange(LANES)[:, None]
    n1 = np.arange(n1h)[None, :]
    slot_f = (LANES * n1 + n2).reshape(-1)
    slot_b = slot_f + seq_len
    neg_lag = slot_b > seq_len
    pos_b = np.where(neg_lag, 2 * seq_len - slot_b, 0)
    bands = np.linspace(1e-4, FILTER_BANDS - 1, FILTER_BANDS)

    def feats(pos, sgn):
        pos = pos.astype(np.float64)
        ang = (2.0 * math.pi / seq_len) * pos[:, None] * bands[None, :]
        pad = np.zeros((pos.shape[0], FILT_PAD - FILTER_EMB - 1))
        return np.concatenate([(pos / (seq_len - 1))[:, None], np.cos(ang), -np.sin(ang), sgn[:, None], pad], axis=-1)

    z = np.concatenate([feats(slot_f, np.ones(slot_f.shape)), feats(pos_b, np.where(neg_lag, -1.0, 0.0))], axis=-1)
    return np.ascontiguousarray(z, dtype=np.float32)


@functools.lru_cache(maxsize=None)
def _rope_tables(seq_len):
    inv = ROPE_THETA ** (-np.arange(0, HEAD_DIM, 2, dtype=np.float64) / HEAD_DIM)
    ang = np.arange(seq_len, dtype=np.float64)[:, None] * inv[None, :]
    cos, sin = np.cos(ang), np.sin(ang)
    zero = np.zeros_like(sin)
    reps = LANES // HEAD_DIM
    f32 = lambda a: np.ascontiguousarray(np.tile(a, (1, reps)), dtype=np.float32)
    return f32(np.concatenate([cos, cos], axis=1)), f32(np.concatenate([-sin, zero], axis=1)), \
        f32(np.concatenate([zero, sin], axis=1))


_LATE_WEIGHTS = ("w_hy_out", "w_at_out", "w_o", "w_gate", "w_up", "w_down")


def _layer(x, p, late_bf=None):
    b, seq_len, _ = x.shape
    n = b * seq_len
    n1h = seq_len // LANES
    xf = x.reshape(n, D_MODEL)
    tj = SUBLANES if n1h >= LANES else 4 * SUBLANES
    c = {k: jnp.asarray(v) for k, v in _dft_constants(n1h, tj).items()}
    kb = SUBLANES
    r4 = lambda a: a.reshape(b, n1h, LANES, HYENA_WIDTH)

    cos, sa, sb = (jnp.asarray(t) for t in _rope_tables(seq_len))
    cast = () if late_bf is not None else tuple(p[k] for k in _LATE_WEIGHTS)
    res = _inproj(xf, seq_len, p["attn_norm_w"], p["w_in"], p["conv_w"], p["conv_b"],
                  p["q_norm_w"], p["k_norm_w"], cos, sa, sb, p["ones_bd"], cast)
    x1, x2, v, q, k4, v4, g = res[:7]
    if late_bf is None:
        late_bf = dict(zip(_LATE_WEIGHTS, res[7:]))

    afre, afim, asum = _filter_stage_a(jnp.asarray(_filter_features(seq_len)), p["filt_w1"], p["filt_b1"],
                                       p["filt_w2"], p["filt_b2"], p["filt_w3f"], p["filt_w3b"], p["filt_freq"],
                                       p["decay"], c["g_full"], c["twr_a"], c["twi_a"], n1h, tj)

    are, aim = _fwd_a(r4(v), c["g_half"], c["twr_a"], c["twi_a"], tj)
    bre, bim = _mid(are, aim, c["fblk"], c["fblk_inv"], c["twr"], c["twi"], afre, afim, 0, kb)
    z1, are, aim = _inv_a(bre, bim, c["g_inv"], r4(x1), r4(v), p["hyena_bias"][0:1], asum, 0, tj,
                          nxt=(c["g_half"], c["twr_a"], c["twi_a"]))
    bre, bim = _mid(are, aim, c["fblk"], c["fblk_inv"], c["twr"], c["twi"], afre, afim, 1, kb)
    (yh,) = _inv_a(bre, bim, c["g_inv"], r4(x2), z1, p["hyena_bias"][1:2], asum, 1, tj)

    ya = _attention(q.reshape(b, seq_len, 2 * ATTN_WIDTH), k4.reshape(b, seq_len, 2 * LANES),
                    v4.reshape(b, seq_len, 2 * LANES), p["attn_sink"])

    w = late_bf
    out = _final(xf, yh.reshape(n, HYENA_WIDTH), ya.reshape(n, ATTN_WIDTH), g, w["w_hy_out"], w["w_at_out"],
                 w["w_o"], p["ffn_norm_w"], w["w_gate"], w["w_up"], w["w_down"])
    return out.reshape(b, seq_len, D_MODEL), late_bf


def _block_diag2(w):
    z = jnp.zeros_like(w)
    return jnp.concatenate([jnp.concatenate([w, z], axis=1), jnp.concatenate([z, w], axis=1)], axis=0)


def kernel(x_prompt, x_sample, attn_norm_w, w_in, hyena_conv_w, hyena_conv_b, filt_w1, filt_b1, filt_w2, filt_b2,
           filt_w3, filt_freq, hyena_bias, q_norm_w, k_norm_w, attn_sink, w_hy_out, w_at_out, w_o, ffn_norm_w,
           w_gate, w_up, w_down):
    cw = HYENA_ORDER * HYENA_WIDTH
    max_decay = math.log(DECAY_TARGET) / DECAY_FAST_PCT
    min_decay = math.log(DECAY_TARGET) / DECAY_SLOW_PCT
    deltas = jnp.abs(jnp.linspace(min_decay, max_decay, HYENA_WIDTH, dtype=F32))
    head = np.arange(ATTN_WIDTH) // HEAD_DIM
    ones_bd = jnp.asarray((head[:, None] == head[None, :]).astype(np.float32) / HEAD_DIM).astype(BF16)
    w3 = filt_w3[0].reshape(FILTER_HIDDEN, HYENA_ORDER, 2, HYENA_WIDTH).transpose(2, 0, 1, 3)
    w3 = w3.reshape(2, FILTER_HIDDEN, cw)
    w3_zero = jnp.zeros((FILTER_HIDDEN, cw), F32)
    twice = lambda a: jnp.tile(a, 2)[None, :]
    p = dict(
        attn_norm_w=attn_norm_w[0][None, :],
        w_in=w_in[0].astype(BF16),
        conv_w=hyena_conv_w[0],
        conv_b=hyena_conv_b[0][None, :],
        filt_w1=_block_diag2(jnp.pad(filt_w1[0], ((0, FILT_PAD - FILTER_EMB), (0, 0)))),
        filt_b1=twice(filt_b1[0]),
        filt_w2=_block_diag2(filt_w2[0]),
        filt_b2=twice(filt_b2[0]),
        filt_w3f=jnp.concatenate([w3[0], w3_zero], axis=0),
        filt_w3b=jnp.concatenate([w3_zero, w3[1]], axis=0),
        filt_freq=twice(filt_freq[0]),
        decay=deltas[None, :],
        hyena_bias=hyena_bias[0],
        q_norm_w=jnp.tile(q_norm_w[0], N_HEADS)[None, :],
        k_norm_w=jnp.tile(k_norm_w[0], N_KV_HEADS)[None, :],
        attn_sink=attn_sink[0],
        ones_bd=ones_bd,
        w_hy_out=w_hy_out[0],
        w_at_out=w_at_out[0],
        w_o=w_o[0],
        ffn_norm_w=ffn_norm_w[0][None, :],
        w_gate=w_gate[0],
        w_up=w_up[0],
        w_down=w_down[0],
    )
    y_prompt, late_bf = _layer(x_prompt, p)
    y_sample, _ = _layer(x_sample, p, late_bf)
    return (y_prompt, y_sample)
```

```python
import functools
import math

import numpy as np
import jax
import jax.numpy as jnp
from jax import lax
from jax.experimental import pallas as pl
from jax.experimental.pallas import tpu as pltpu

F32 = jnp.float32
BF16 = jnp.bfloat16

D_MODEL = 1024
HYENA_WIDTH = 512
HYENA_ORDER = 2
FILTER_BANDS = 16
FILTER_EMB = 1 + 2 * FILTER_BANDS
FILTER_HIDDEN = 64
DECAY_FAST_PCT = 0.3
DECAY_SLOW_PCT = 1.5
DECAY_TARGET = 1e-2
DECAY_SHIFT = 0.05
N_HEADS = 8
N_KV_HEADS = 2
HEAD_DIM = 64
ATTN_WIDTH = N_HEADS * HEAD_DIM
KV_WIDTH = N_KV_HEADS * HEAD_DIM
WINDOW = 128
ROPE_THETA = 10000.0
FFN_HIDDEN = 2816
RMS_EPS = 1e-6
NEG_INF = -1e30
LOG2E = math.log2(math.e)

C_HY = 3 * HYENA_WIDTH
C_Q = C_HY + ATTN_WIDTH
C_K = C_Q + KV_WIDTH
C_V = C_K + KV_WIDTH
IN_WIDTH = C_V + 2 * D_MODEL

LANES = 128
SUBLANES = 8
ATTN_BLOCK = 128
ATTN_HEAD_ORDER = tuple(g * (N_HEADS // N_KV_HEADS) + t for t in range(N_HEADS // N_KV_HEADS) for g in range(N_KV_HEADS))
ATTN_SUB = 4
VMEM_LIMIT = 56 * 1024 * 1024

TM_INPROJ = 512
TM_FINAL = 512
FFN_CHUNKS = ((0, 1280), (1280, FFN_HIDDEN))
_CH_SPLITS = (slice(0, HYENA_WIDTH // 2), slice(HYENA_WIDTH // 2, HYENA_WIDTH))
FILT_PAD = 64


def _cparams(*sem):
    return pltpu.CompilerParams(dimension_semantics=sem, vmem_limit_bytes=VMEM_LIMIT)


def _const_spec(shape):
    nd = len(shape)
    return pl.BlockSpec(shape, lambda *_: (0,) * nd, pipeline_mode=pl.Buffered(1))


def _dot(a, b):
    return jnp.dot(a, b, preferred_element_type=F32)


def _sigmoid(x):
    return 0.5 * jnp.tanh(0.5 * x) + 0.5


def _split(a):
    hi = a.astype(BF16)
    lo = (a - hi.astype(F32)).astype(BF16)
    return hi, lo


def _dot3(a, w):
    ah, al = _split(a)
    wh, wl = _split(w)
    return _dot(ah, wh) + _dot(al, wh) + _dot(ah, wl)


def _inproj_kernel(*refs, tm, tiles_per_seq, n_cast):
    (x_ref, xp_ref, xn_ref, nw_ref, w_ref, cw_ref, cb_ref, qw_ref, kw_ref, cos_ref, sa_ref, sb_ref,
     ones_ref) = refs[:13]
    cast_in = refs[13:13 + n_cast]
    x1_ref, x2_ref, v_ref, q_ref, k4_ref, v4_ref, g_ref = refs[13 + n_cast:20 + n_cast]
    cast_out = refs[20 + n_cast:20 + 2 * n_cast]
    pad_ref = refs[-1]
    for src_ref, dst_ref in zip(cast_in, cast_out):
        dst_ref[...] = src_ref[...].astype(BF16)
    pos = pl.program_id(0) % tiles_per_seq
    nw = nw_ref[...]

    def norm(xv):
        ms = jnp.mean(xv * xv, axis=-1, keepdims=True)
        return (xv * lax.rsqrt(ms + RMS_EPS) * nw).astype(BF16)

    h_all = norm(jnp.concatenate([x_ref[...], xp_ref[...], xn_ref[...]], axis=0))
    h = h_all[:tm]

    hy_all = _dot(h_all, w_ref[:, :C_HY])
    hy = hy_all[:tm]
    hyh = hy_all[tm:]
    prev_ok = (pos > 0).astype(F32)
    next_ok = (pos < tiles_per_seq - 1).astype(F32)
    pad_ref[0:SUBLANES] = hyh[0:SUBLANES] * prev_ok
    pad_ref[SUBLANES:SUBLANES + tm] = hy
    pad_ref[SUBLANES + tm:2 * SUBLANES + tm] = hyh[SUBLANES:] * next_ok
    up = pad_ref[SUBLANES - 1:SUBLANES - 1 + tm]
    un = pad_ref[SUBLANES + 1:SUBLANES + 1 + tm]
    cw = cw_ref[...]
    uc = cw[0:1] * up + cw[1:2] * hy + cw[2:3] * un + cb_ref[...]
    x1_ref[...] = uc[:, :HYENA_WIDTH].astype(BF16)
    x2_ref[...] = uc[:, HYENA_WIDTH:2 * HYENA_WIDTH].astype(BF16)
    v_ref[...] = uc[:, 2 * HYENA_WIDTH:].astype(BF16)

    def head_norm_rope(t, wrow, ones, reps):
        ms = _dot((t * t).astype(BF16), ones)
        tn = t * lax.rsqrt(ms + RMS_EPS) * wrow
        width = t.shape[1]
        cos = jnp.concatenate([cos_ref[...]] * reps, axis=1)
        sa = jnp.concatenate([sa_ref[...]] * reps, axis=1)
        sb = jnp.concatenate([sb_ref[...]] * reps, axis=1)
        half = HEAD_DIM // 2
        return tn * cos + pltpu.roll(tn, width - half, 1) * sa + pltpu.roll(tn, half, 1) * sb

    qkv = _dot(h, w_ref[:, C_HY:C_V])
    q = qkv[:, :ATTN_WIDTH]
    q = head_norm_rope(q, qw_ref[...], ones_ref[...], ATTN_WIDTH // LANES) * (HEAD_DIM ** -0.5 * LOG2E)
    lead = {}
    for p in range(N_HEADS // 2):
        qp = q[:, p * LANES:(p + 1) * LANES]
        lead[2 * p], lead[2 * p + 1] = qp, pltpu.roll(qp, HEAD_DIM, 1)
    q_ref[...] = jnp.concatenate([lead[h] for h in ATTN_HEAD_ORDER], axis=1).astype(BF16)

    k = qkv[:, ATTN_WIDTH:ATTN_WIDTH + KV_WIDTH]
    k = head_norm_rope(k, kw_ref[...], ones_ref[:KV_WIDTH, :KV_WIDTH], 1)
    v = qkv[:, ATTN_WIDTH + KV_WIDTH:]

    lo_half = lax.broadcasted_iota(jnp.int32, (tm, KV_WIDTH), 1) < HEAD_DIM

    def spread(t):
        zero = jnp.zeros_like(t)
        return jnp.concatenate([jnp.where(lo_half, t, zero), jnp.where(lo_half, pltpu.roll(t, HEAD_DIM, 1), zero)],
                               axis=1).astype(BF16)

    k4_ref[...] = spread(k)
    v4_ref[...] = spread(v)
    g_ref[...] = _sigmoid(_dot(h, w_ref[:, C_V:]))


def _inproj(xf, seq_len, nw, w_in_bf, cw, cb, qw, kw, cos, sa, sb, ones_bd, cast=()):
    n = xf.shape[0]
    tm = TM_INPROJ
    tiles_per_seq = seq_len // tm
    nblk8 = n // SUBLANES
    r8 = tm // SUBLANES
    steps = n // tm
    kern = functools.partial(_inproj_kernel, tm=tm, tiles_per_seq=tiles_per_seq, n_cast=len(cast))
    row = lambda i: (i, 0)
    cast_specs = []
    for w in cast:
        per = 1 if (w.shape[0] // steps) % (2 * SUBLANES) == 0 else 2
        rows_blk = w.shape[0] * per // steps
        cast_specs.append(pl.BlockSpec((rows_blk, w.shape[1]), lambda i, per=per: (i // per, 0)))
    return pl.pallas_call(
        kern,
        grid=(n // tm,),
        in_specs=[
            pl.BlockSpec((tm, D_MODEL), row),
            pl.BlockSpec((SUBLANES, D_MODEL), lambda i: (jnp.maximum(i * r8 - 1, 0), 0)),
            pl.BlockSpec((SUBLANES, D_MODEL), lambda i: (jnp.minimum((i + 1) * r8, nblk8 - 1), 0)),
            _const_spec((1, D_MODEL)),
            _const_spec((D_MODEL, IN_WIDTH)),
            _const_spec((3, C_HY)),
            _const_spec((1, C_HY)),
            _const_spec((1, ATTN_WIDTH)),
            _const_spec((1, KV_WIDTH)),
            pl.BlockSpec((tm, LANES), lambda i: (i % tiles_per_seq, 0)),
            pl.BlockSpec((tm, LANES), lambda i: (i % tiles_per_seq, 0)),
            pl.BlockSpec((tm, LANES), lambda i: (i % tiles_per_seq, 0)),
            _const_spec((ATTN_WIDTH, ATTN_WIDTH)),
        ] + cast_specs,
        out_specs=[
            pl.BlockSpec((tm, HYENA_WIDTH), row),
            pl.BlockSpec((tm, HYENA_WIDTH), row),
            pl.BlockSpec((tm, HYENA_WIDTH), row),
            pl.BlockSpec((tm, 2 * ATTN_WIDTH), row),
            pl.BlockSpec((tm, 2 * LANES), row),
            pl.BlockSpec((tm, 2 * LANES), row),
            pl.BlockSpec((tm, 2 * D_MODEL), row),
        ] + cast_specs,
        out_shape=[
            jax.ShapeDtypeStruct((n, HYENA_WIDTH), BF16),
            jax.ShapeDtypeStruct((n, HYENA_WIDTH), BF16),
            jax.ShapeDtypeStruct((n, HYENA_WIDTH), BF16),
            jax.ShapeDtypeStruct((n, 2 * ATTN_WIDTH), BF16),
            jax.ShapeDtypeStruct((n, 2 * LANES), BF16),
            jax.ShapeDtypeStruct((n, 2 * LANES), BF16),
            jax.ShapeDtypeStruct((n, 2 * D_MODEL), F32),
        ] + [jax.ShapeDtypeStruct(w.shape, BF16) for w in cast],
        scratch_shapes=[pltpu.VMEM((tm + 2 * SUBLANES, C_HY), F32)],
        compiler_params=_cparams("parallel"),
        name="inproj",
    )(xf, xf, xf, nw, w_in_bf, cw, cb, qw, kw, cos, sa, sb, ones_bd, *cast)


def _rows_to_tiles(r, tj):
    return r.reshape(r.shape[0], tj, r.shape[1] // tj)


def _tiles_to_rows(t):
    return t.reshape(t.shape[0], t.shape[1] * t.shape[2])


def _twiddle_tiles(r, twr_ref, twi_ref, n1h, tj):
    c = r.shape[1] // tj
    re, im = [], []
    for j in range(tj):
        ar, ai = r[:n1h, j * c:(j + 1) * c], r[n1h:, j * c:(j + 1) * c]
        tr, ti = twr_ref[:, j:j + 1], twi_ref[:, j:j + 1]
        re.append(ar * tr - ai * ti)
        im.append(ar * ti + ai * tr)
    return (_rows_to_tiles(jnp.concatenate(re, axis=1), tj).astype(BF16),
            _rows_to_tiles(jnp.concatenate(im, axis=1), tj).astype(BF16))


def _store_twiddled(r, twr_ref, twi_ref, are_ref, aim_ref, sl, n1h, tj):
    re, im = _twiddle_tiles(r, twr_ref, twi_ref, n1h, tj)
    are_ref[:, :, sl] = re
    aim_ref[:, :, sl] = im


def _filt_kernel(z_ref, w1_ref, b1_ref, w2_ref, b2_ref, w3f_ref, w3b_ref, fr_ref, dl_ref, g_ref, twr_ref, twi_ref,
                 are_ref, aim_ref, s_ref, *, n1h, tj):
    i = pl.program_id(0)
    z = z_ref[...]
    fr = fr_ref[...]
    h = jnp.sin(fr * (_dot3(z, w1_ref[...]) + b1_ref[...]))
    h = jnp.sin(fr * (_dot3(h, w2_ref[...]) + b2_ref[...]))
    hb = h.astype(BF16)
    tf = z[:, 0:1]
    tb = z[:, FILT_PAD:FILT_PAD + 1]
    sgn = z[:, FILT_PAD + FILTER_EMB:FILT_PAD + FILTER_EMB + 1]
    gm = g_ref[...].astype(BF16)
    cb = HYENA_WIDTH
    parts = []
    for o in range(HYENA_ORDER):
        sl = slice(o * cb, (o + 1) * cb)
        dl = dl_ref[...]

        def taps(w_ref, t):
            return _dot(hb, w_ref[:, sl].astype(BF16)) * (jnp.exp(-t * dl) + DECAY_SHIFT)

        kf = taps(w3f_ref, tf)
        kb = taps(w3b_ref, tb) * sgn
        parts.append(jnp.sum(jnp.abs(kf), axis=0, keepdims=True) + jnp.sum(jnp.abs(kb), axis=0, keepdims=True))
        cols = [jnp.concatenate([kf[j * n1h:(j + 1) * n1h], kb[j * n1h:(j + 1) * n1h]], axis=0).astype(BF16)
                for j in range(tj)]
        re, im = _twiddle_tiles(_dot(gm, jnp.concatenate(cols, axis=1)), twr_ref, twi_ref, n1h, tj)
        are_ref[o] = re
        aim_ref[o] = im
    part = jnp.concatenate(parts, axis=1)

    @pl.when(i == 0)
    def _():
        s_ref[...] = part

    @pl.when(i > 0)
    def _():
        s_ref[...] += part


def _filter_stage_a(zf, w1bd, b1, w2bd, b2, w3f, w3b, fr, dl, g_full, twr, twi, n1h, tj):
    cw = HYENA_ORDER * HYENA_WIDTH
    hid2 = 2 * FILTER_HIDDEN
    a_shape = jax.ShapeDtypeStruct((HYENA_ORDER, n1h, LANES, HYENA_WIDTH), BF16)
    a_spec = pl.BlockSpec((HYENA_ORDER, n1h, tj, HYENA_WIDTH), lambda i: (0, 0, i, 0))
    t_spec = pl.BlockSpec((None, n1h, tj), lambda i: (i, 0, 0))
    return pl.pallas_call(
        functools.partial(_filt_kernel, n1h=n1h, tj=tj),
        grid=(LANES // tj,),
        in_specs=[
            pl.BlockSpec((tj * n1h, 2 * FILT_PAD), lambda i: (i, 0)),
            _const_spec((2 * FILT_PAD, hid2)),
            _const_spec((1, hid2)),
            _const_spec((hid2, hid2)),
            _const_spec((1, hid2)),
            _const_spec((hid2, cw)),
            _const_spec((hid2, cw)),
            _const_spec((1, hid2)),
            _const_spec((1, HYENA_WIDTH)),
            _const_spec((2 * n1h, 2 * n1h)),
            t_spec, t_spec,
        ],
        out_specs=[a_spec, a_spec, pl.BlockSpec((1, cw), lambda i: (0, 0))],
        out_shape=[a_shape, a_shape, jax.ShapeDtypeStruct((1, cw), F32)],
        compiler_params=_cparams("arbitrary"),
        name="filter_taps_stage_a",
    )(zf, w1bd, b1, w2bd, b2, w3f, w3b, fr, dl, g_full, twr, twi)


def _fwda_kernel(g_ref, twr_ref, twi_ref, u_ref, are_ref, aim_ref, *, n1h, tj):
    gm = g_ref[...].astype(BF16)
    prev = None
    for sl in _CH_SPLITS:
        u = _tiles_to_rows(u_ref[:, :, sl])
        if prev is not None:
            _store_twiddled(prev[1], twr_ref, twi_ref, are_ref, aim_ref, prev[0], n1h, tj)
        prev = (sl, _dot(gm, u))
    _store_twiddled(prev[1], twr_ref, twi_ref, are_ref, aim_ref, prev[0], n1h, tj)


def _fwd_a(u4, g_half, twr, twi, tj):
    b, n1h, _, cb = u4.shape
    blk = pl.BlockSpec((None, n1h, tj, cb), lambda bi, ji: (bi, 0, ji, 0))
    t_spec = pl.BlockSpec((None, n1h, tj), lambda bi, ji: (ji, 0, 0))
    a_shape = jax.ShapeDtypeStruct(u4.shape, BF16)
    return pl.pallas_call(
        functools.partial(_fwda_kernel, n1h=n1h, tj=tj),
        grid=(b, LANES // tj),
        in_specs=[_const_spec((2 * n1h, n1h)), t_spec, t_spec, blk],
        out_specs=[blk, blk],
        out_shape=[a_shape, a_shape],
        compiler_params=_cparams("parallel", "parallel"),
        name="dft_stage_a",
    )(g_half, twr, twi, u4)


def _mid_kernel(f_ref, fi_ref, twr_ref, twi_ref, fre_ref, fim_ref, are_ref, aim_ref, bre_ref, bim_ref, *, kb):
    fm = f_ref[...].astype(BF16)
    fim = fi_ref[...].astype(BF16)
    fill = jnp.zeros((LANES - kb, LANES), F32)
    twr_t = jnp.concatenate([twr_ref[...], fill], axis=0).T
    twi_t = jnp.concatenate([twi_ref[...], fill], axis=0).T

    def fwd(k):
        return (_dot(fm, jnp.concatenate([are_ref[k], aim_ref[k]], axis=0)),
                _dot(fm, jnp.concatenate([fre_ref[k], fim_ref[k]], axis=0)))

    def spec(k, xh):
        x, h = xh
        xr, xi = x[:LANES], x[LANES:]
        hr, hi = h[:LANES], h[LANES:]
        return jnp.concatenate([xr * hr - xi * hi, xr * hi + xi * hr], axis=0).astype(BF16)

    def out(k, bc):
        br, bi = bc[:LANES], bc[LANES:]
        tr, ti = twr_t[:, k:k + 1], twi_t[:, k:k + 1]
        bre_ref[k] = (br * tr + bi * ti).astype(BF16)
        bim_ref[k] = (bi * tr - br * ti).astype(BF16)

    x = fwd(0)
    bc_prev = None
    for k in range(kb):
        y = spec(k, x)
        if k + 1 < kb:
            x = fwd(k + 1)
        if bc_prev is not None:
            out(k - 1, bc_prev)
        bc_prev = _dot(fim, y)
    out(kb - 1, bc_prev)


def _mid(are, aim, fblk, fblk_inv, twr, twi, afre, afim, order, kb):
    b, n1h, _, cb = are.shape
    a_spec = pl.BlockSpec((None, kb, LANES, cb), lambda ki, bi: (bi, ki, 0, 0))
    f_spec = pl.BlockSpec((None, kb, LANES, cb), lambda ki, bi: (order, ki, 0, 0))
    t_spec = pl.BlockSpec((kb, LANES), lambda ki, bi: (ki, 0))
    shape = jax.ShapeDtypeStruct(are.shape, BF16)
    return pl.pallas_call(
        functools.partial(_mid_kernel, kb=kb),
        grid=(n1h // kb, b),
        in_specs=[
            _const_spec((2 * LANES, 2 * LANES)),
            _const_spec((2 * LANES, 2 * LANES)),
            t_spec, t_spec,
            f_spec, f_spec,
            a_spec, a_spec,
        ],
        out_specs=[a_spec, a_spec],
        out_shape=[shape, shape],
        compiler_params=_cparams("parallel", "parallel"),
        name="dft_stage_c",
    )(fblk, fblk_inv, twr, twi, afre, afim, are, aim)


def _inva_kernel(gi_ref, bre_ref, bim_ref, xg_ref, v_ref, sb_ref, s_ref, *rest, n1h, tj, fuse_next):
    gim = gi_ref[...].astype(BF16)
    inv_l1 = 1.0 / s_ref[...]
    if fuse_next:
        g_ref, twr_ref, twi_ref, z_ref, are_ref, aim_ref = rest
        gm = g_ref[...].astype(BF16)
    else:
        (z_ref,) = rest

    def conv(sl):
        rhs = jnp.concatenate([_tiles_to_rows(bre_ref[:, :, sl]), _tiles_to_rows(bim_ref[:, :, sl])], axis=0)
        return _dot(gim, rhs)

    def gate(sl, y):
        z = xg_ref[:, :, sl].astype(F32) * (_rows_to_tiles(y, tj) * inv_l1[:, sl]
                                            + sb_ref[:, sl] * v_ref[:, :, sl].astype(F32))
        z = z.astype(BF16)
        z_ref[:, :, sl] = z
        return z

    s0, s1 = _CH_SPLITS
    y0 = conv(s0)
    y1 = conv(s1)
    z0 = gate(s0, y0)
    if fuse_next:
        r0 = _dot(gm, _tiles_to_rows(z0))
    z1 = gate(s1, y1)
    if fuse_next:
        _store_twiddled(r0, twr_ref, twi_ref, are_ref, aim_ref, s0, n1h, tj)
        r1 = _dot(gm, _tiles_to_rows(z1))
        _store_twiddled(r1, twr_ref, twi_ref, are_ref, aim_ref, s1, n1h, tj)


def _inv_a(bre, bim, g_inv, xg, v, sb_row, asum, order, tj, nxt=None):
    b, n1h, _, cb = bre.shape
    blk = pl.BlockSpec((None, n1h, tj, cb), lambda bi, ji: (bi, 0, ji, 0))
    in_specs = [_const_spec((n1h, 2 * n1h)), blk, blk, blk, blk, _const_spec((1, cb)),
                pl.BlockSpec((1, cb), lambda bi, ji: (0, order))]
    args = [g_inv, bre, bim, xg, v, sb_row, asum]
    out_specs = [blk]
    out_shape = [jax.ShapeDtypeStruct(bre.shape, BF16)]
    if nxt is not None:
        t_spec = pl.BlockSpec((None, n1h, tj), lambda bi, ji: (ji, 0, 0))
        in_specs += [_const_spec((2 * n1h, n1h)), t_spec, t_spec]
        args += list(nxt)
        out_specs += [blk, blk]
        out_shape += [jax.ShapeDtypeStruct(bre.shape, BF16)] * 2
    return pl.pallas_call(
        functools.partial(_inva_kernel, n1h=n1h, tj=tj, fuse_next=nxt is not None),
        grid=(b, LANES // tj),
        in_specs=in_specs,
        out_specs=out_specs,
        out_shape=out_shape,
        compiler_params=_cparams("parallel", "parallel"),
        name="idft_stage_a_gate",
    )(*args)


def _attn_kernel(sink_ref, q_ref, kp_ref, kc_ref, kn_ref, vp_ref, vc_ref, vn_ref, o_ref, *, nblk, sub):
    i = pl.program_id(1)
    blk = ATTN_BLOCK
    grp = N_HEADS // N_KV_HEADS
    rows = grp * blk
    win = 3 * blk
    qi = lax.broadcasted_iota(jnp.int32, (rows, win), 0) % blk
    si = lax.broadcasted_iota(jnp.int32, (rows, win), 1)
    mid = (si >= blk) & (si < 2 * blk)
    tri_prev = (si < blk) & (si >= qi)
    tri_next = (si >= 2 * blk) & (si - 2 * blk <= qi)
    hrow = lax.broadcasted_iota(jnp.int32, (rows, 1), 0) // blk
    kcat = jnp.concatenate([kp_ref[...], kc_ref[...], kn_ref[...]], axis=0)
    vcat = jnp.concatenate([vp_ref[...], vc_ref[...], vn_ref[...]], axis=0)
    lane_k = lax.broadcasted_iota(jnp.int32, (win, N_KV_HEADS * LANES), 1)
    lane_o = lax.broadcasted_iota(jnp.int32, (rows, LANES), 1)
    zero_k = jnp.zeros((win, N_KV_HEADS * LANES), BF16)
    for s in range(sub):
        j = sub * i + s
        valid = mid | (tri_prev & (j > 0)) | (tri_next & (j < nblk - 1))
        kw = kcat[s * blk:s * blk + win]
        vw = vcat[s * blk:s * blk + win]
        kd = jnp.concatenate([jnp.where(lane_k // LANES == g, kw, zero_k) for g in range(N_KV_HEADS)], axis=0)
        vd = jnp.concatenate([vw[:, :LANES], pltpu.roll(vw[:, LANES:].astype(F32), HEAD_DIM, 1).astype(BF16)], axis=0)
        lhs = jnp.concatenate([q_ref[s * blk:(s + 1) * blk, t * N_KV_HEADS * LANES:(t + 1) * N_KV_HEADS * LANES]
                               for t in range(grp)], axis=0)
        sc_all = lax.dot_general(lhs, kd, (((1,), (1,)), ((), ())), preferred_element_type=F32)
        es, dens = [], []
        for g in range(N_KV_HEADS):
            sc = jnp.where(valid, sc_all[:, g * win:(g + 1) * win], NEG_INF)
            sk = jnp.full((rows, 1), sink_ref[grp * g], F32)
            for t in range(1, grp):
                sk = jnp.where(hrow == t, sink_ref[grp * g + t], sk)
            sk = sk * LOG2E
            m = jnp.maximum(jnp.max(sc, axis=1, keepdims=True), sk)
            e = jnp.exp2(sc - m)
            dens.append(jnp.sum(e, axis=1, keepdims=True) + jnp.exp2(sk - m))
            es.append(e.astype(BF16))
        den = jnp.where(lane_o < HEAD_DIM, dens[0], dens[1])
        r = _dot(jnp.concatenate(es, axis=1), vd) / den
        o_ref[s * blk:(s + 1) * blk, :] = jnp.concatenate(
            [r[t * blk:(t + 1) * blk] for t in range(grp)], axis=1).astype(BF16)


def _attention(q3, k3, v3, sink):
    b, seq_len, qw = q3.shape
    blk = ATTN_BLOCK
    sub = ATTN_SUB
    nblk = seq_len // blk
    wide = k3.shape[-1]
    prev = lambda bi, i: (bi, jnp.maximum(sub * i - 1, 0), 0)
    cur = lambda bi, i: (bi, i, 0)
    nxt = lambda bi, i: (bi, jnp.minimum(sub * (i + 1), nblk - 1), 0)
    edge = lambda f: pl.BlockSpec((None, blk, wide), f)
    body = pl.BlockSpec((None, sub * blk, wide), cur)
    return pl.pallas_call(
        functools.partial(_attn_kernel, nblk=nblk, sub=sub),
        grid=(b, nblk // sub),
        in_specs=[pl.BlockSpec(memory_space=pltpu.SMEM),
                  pl.BlockSpec((None, sub * blk, qw), cur),
                  edge(prev), body, edge(nxt), edge(prev), body, edge(nxt)],
        out_specs=pl.BlockSpec((None, sub * blk, ATTN_WIDTH), cur),
        out_shape=jax.ShapeDtypeStruct((b, seq_len, ATTN_WIDTH), BF16),
        compiler_params=_cparams("parallel", "parallel"),
        name="banded_attention",
    )(sink, q3, k3, k3, k3, v3, v3, v3)


def _final_kernel(x_ref, yh_ref, ya_ref, g_ref, why_ref, wat_ref, wo_ref, nw_ref, wg_ref, wu_ref, wd_ref,
                  o_ref):
    a = _dot(yh_ref[...], why_ref[...])
    b = _dot(ya_ref[...], wat_ref[...])
    merged = g_ref[:, :D_MODEL] * a + g_ref[:, D_MODEL:] * b
    x1 = x_ref[...] + _dot(merged.astype(BF16), wo_ref[...])
    ms = jnp.mean(x1 * x1, axis=-1, keepdims=True)
    f = (x1 * lax.rsqrt(ms + RMS_EPS) * nw_ref[...]).astype(BF16)
    acc = x1
    for lo, hi in FFN_CHUNKS:
        gt = _dot(f, wg_ref[:, lo:hi])
        upv = _dot(f, wu_ref[:, lo:hi])
        hid = gt * _sigmoid(gt) * upv
        acc = acc + _dot(hid.astype(BF16), wd_ref[lo:hi, :])
    o_ref[...] = acc


def _final(xf, yh, ya, g, why, wat, wo, nw, wg, wu, wd):
    n = xf.shape[0]
    tm = TM_FINAL
    row = lambda i: (i, 0)
    return pl.pallas_call(
        _final_kernel,
        grid=(n // tm,),
        in_specs=[
            pl.BlockSpec((tm, D_MODEL), row),
            pl.BlockSpec((tm, HYENA_WIDTH), row),
            pl.BlockSpec((tm, ATTN_WIDTH), row),
            pl.BlockSpec((tm, 2 * D_MODEL), row),
            _const_spec((HYENA_WIDTH, D_MODEL)),
            _const_spec((ATTN_WIDTH, D_MODEL)),
            _const_spec((D_MODEL, D_MODEL)),
            _const_spec((1, D_MODEL)),
            _const_spec((D_MODEL, FFN_HIDDEN)),
            _const_spec((D_MODEL, FFN_HIDDEN)),
            _const_spec((FFN_HIDDEN, D_MODEL)),
        ],
        out_specs=pl.BlockSpec((tm, D_MODEL), row),
        out_shape=jax.ShapeDtypeStruct((n, D_MODEL), F32),
        compiler_params=_cparams("parallel"),
        name="merge_ffn",
    )(xf, yh, ya, g, why, wat, wo, nw, wg, wu, wd)


@functools.lru_cache(maxsize=None)
def _dft_constants(n1h, tj):
    n1_len = 2 * n1h
    m = n1_len * LANES
    k1 = np.arange(n1h, dtype=np.float64)[:, None] + 0.5
    n1 = np.arange(n1_len, dtype=np.float64)[None, :]
    gc = np.exp(-2j * np.pi * k1 * n1 / n1_len)
    g_full = np.concatenate([gc.real, gc.imag], axis=0)
    g_inv = (2.0 / m) * np.concatenate([gc.real[:, :n1h].T, gc.imag[:, :n1h].T], axis=1)
    n2 = np.arange(LANES, dtype=np.float64)
    tw = np.exp(-2j * np.pi * k1 * n2[None, :] / m)
    fc = np.exp(-2j * np.pi * np.outer(n2, n2) / LANES)
    fblk = np.block([[fc.real, -fc.imag], [fc.imag, fc.real]])
    fblk_inv = np.block([[fc.real, fc.imag], [-fc.imag, fc.real]])
    f32 = lambda a: np.ascontiguousarray(a, dtype=np.float32)
    tiled = lambda a: a.reshape(n1h, LANES // tj, tj).transpose(1, 0, 2)
    return dict(g_full=f32(g_full), g_half=f32(g_full[:, :n1h]), g_inv=f32(g_inv), twr=f32(tw.real),
                twi=f32(tw.imag), twr_a=f32(tiled(tw.real)), twi_a=f32(tiled(tw.imag)), fblk=f32(fblk),
                fblk_inv=f32(fblk_inv))


@functools.lru_cache(maxsize=None)
def _filter_features(seq_len):
    n1h = seq_len // LANES
    n2 = np.arange(LANES)[:, None]
    n1 = np.arange(n1h)[None, :]
    slot_f = (LANES * n1 + n2).reshape(-1)
    slot_b = slot_f + seq_len
    neg_lag = slot_b > seq_len
    pos_b = np.where(neg_lag, 2 * seq_len - slot_b, 0)
    bands = np.linspace(1e-4, FILTER_BANDS - 1, FILTER_BANDS)

    def feats(pos, sgn):
        pos = pos.astype(np.float64)
        ang = (2.0 * math.pi / seq_len) * pos[:, None] * bands[None, :]
        pad = np.zeros((pos.shape[0], FILT_PAD - FILTER_EMB - 1))
        return np.concatenate([(pos / (seq_len - 1))[:, None], np.cos(ang), -np.sin(ang), sgn[:, None], pad], axis=-1)

    z = np.concatenate([feats(slot_f, np.ones(slot_f.shape)), feats(pos_b, np.where(neg_lag, -1.0, 0.0))], axis=-1)
    return np.ascontiguousarray(z, dtype=np.float32)


@functools.lru_cache(maxsize=None)
def _rope_tables(seq_len):
    inv = ROPE_THETA ** (-np.arange(0, HEAD_DIM, 2, dtype=np.float64) / HEAD_DIM)
    ang = np.arange(seq_len, dtype=np.float64)[:, None] * inv[None, :]
    cos, sin = np.cos(ang), np.sin(ang)
    zero = np.zeros_like(sin)
    reps = LANES // HEAD_DIM
    f32 = lambda a: np.ascontiguousarray(np.tile(a, (1, reps)), dtype=np.float32)
    return f32(np.concatenate([cos, cos], axis=1)), f32(np.concatenate([-sin, zero], axis=1)), \
        f32(np.concatenate([zero, sin], axis=1))


_LATE_WEIGHTS = ("w_hy_out", "w_at_out", "w_o", "w_gate", "w_up", "w_down")


def _layer(x, p, late_bf=None):
    b, seq_len, _ = x.shape
    n = b * seq_len
    n1h = seq_len // LANES
    xf = x.reshape(n, D_MODEL)
    tj = SUBLANES if n1h >= LANES else 4 * SUBLANES
    c = {k: jnp.asarray(v) for k, v in _dft_constants(n1h, tj).items()}
    kb = SUBLANES
    r4 = lambda a: a.reshape(b, n1h, LANES, HYENA_WIDTH)

    cos, sa, sb = (jnp.asarray(t) for t in _rope_tables(seq_len))
    cast = () if late_bf is not None else tuple(p[k] for k in _LATE_WEIGHTS)
    res = _inproj(xf, seq_len, p["attn_norm_w"], p["w_in"], p["conv_w"], p["conv_b"],
                  p["q_norm_w"], p["k_norm_w"], cos, sa, sb, p["ones_bd"], cast)
    x1, x2, v, q, k4, v4, g = res[:7]
    if late_bf is None:
        late_bf = dict(zip(_LATE_WEIGHTS, res[7:]))

    afre, afim, asum = _filter_stage_a(jnp.asarray(_filter_features(seq_len)), p["filt_w1"], p["filt_b1"],
                                       p["filt_w2"], p["filt_b2"], p["filt_w3f"], p["filt_w3b"], p["filt_freq"],
                                       p["decay"], c["g_full"], c["twr_a"], c["twi_a"], n1h, tj)

    are, aim = _fwd_a(r4(v), c["g_half"], c["twr_a"], c["twi_a"], tj)
    bre, bim = _mid(are, aim, c["fblk"], c["fblk_inv"], c["twr"], c["twi"], afre, afim, 0, kb)
    z1, are, aim = _inv_a(bre, bim, c["g_inv"], r4(x1), r4(v), p["hyena_bias"][0:1], asum, 0, tj,
                          nxt=(c["g_half"], c["twr_a"], c["twi_a"]))
    bre, bim = _mid(are, aim, c["fblk"], c["fblk_inv"], c["twr"], c["twi"], afre, afim, 1, kb)
    (yh,) = _inv_a(bre, bim, c["g_inv"], r4(x2), z1, p["hyena_bias"][1:2], asum, 1, tj)

    ya = _attention(q.reshape(b, seq_len, 2 * ATTN_WIDTH), k4.reshape(b, seq_len, 2 * LANES),
                    v4.reshape(b, seq_len, 2 * LANES), p["attn_sink"])

    w = late_bf
    out = _final(xf, yh.reshape(n, HYENA_WIDTH), ya.reshape(n, ATTN_WIDTH), g, w["w_hy_out"], w["w_at_out"],
                 w["w_o"], p["ffn_norm_w"], w["w_gate"], w["w_up"], w["w_down"])
    return out.reshape(b, seq_len, D_MODEL), late_bf


def _block_diag2(w):
    z = jnp.zeros_like(w)
    return jnp.concatenate([jnp.concatenate([w, z], axis=1), jnp.concatenate([z, w], axis=1)], axis=0)


def kernel(x_prompt, x_sample, attn_norm_w, w_in, hyena_conv_w, hyena_conv_b, filt_w1, filt_b1, filt_w2, filt_b2,
           filt_w3, filt_freq, hyena_bias, q_norm_w, k_norm_w, attn_sink, w_hy_out, w_at_out, w_o, ffn_norm_w,
           w_gate, w_up, w_down):
    cw = HYENA_ORDER * HYENA_WIDTH
    max_decay = math.log(DECAY_TARGET) / DECAY_FAST_PCT
    min_decay = math.log(DECAY_TARGET) / DECAY_SLOW_PCT
    deltas = jnp.abs(jnp.linspace(min_decay, max_decay, HYENA_WIDTH, dtype=F32))
    head = np.arange(ATTN_WIDTH) // HEAD_DIM
    ones_bd = jnp.asarray((head[:, None] == head[None, :]).astype(np.float32) / HEAD_DIM).astype(BF16)
    w3 = filt_w3[0].reshape(FILTER_HIDDEN, HYENA_ORDER, 2, HYENA_WIDTH).transpose(2, 0, 1, 3)
    w3 = w3.reshape(2, FILTER_HIDDEN, cw)
    w3_zero = jnp.zeros((FILTER_HIDDEN, cw), F32)
    twice = lambda a: jnp.tile(a, 2)[None, :]
    p = dict(
        attn_norm_w=attn_norm_w[0][None, :],
        w_in=w_in[0].astype(BF16),
        conv_w=hyena_conv_w[0],
        conv_b=hyena_conv_b[0][None, :],
        filt_w1=_block_diag2(jnp.pad(filt_w1[0], ((0, FILT_PAD - FILTER_EMB), (0, 0)))),
        filt_b1=twice(filt_b1[0]),
        filt_w2=_block_diag2(filt_w2[0]),
        filt_b2=twice(filt_b2[0]),
        filt_w3f=jnp.concatenate([w3[0], w3_zero], axis=0),
        filt_w3b=jnp.concatenate([w3_zero, w3[1]], axis=0),
        filt_freq=twice(filt_freq[0]),
        decay=deltas[None, :],
        hyena_bias=hyena_bias[0],
        q_norm_w=jnp.tile(q_norm_w[0], N_HEADS)[None, :],
        k_norm_w=jnp.tile(k_norm_w[0], N_KV_HEADS)[None, :],
        attn_sink=attn_sink[0],
        ones_bd=ones_bd,
        w_hy_out=w_hy_out[0],
        w_at_out=w_at_out[0].reshape(N_HEADS, HEAD_DIM, D_MODEL)[np.array(ATTN_HEAD_ORDER)].reshape(ATTN_WIDTH, D_MODEL),
        w_o=w_o[0],
        ffn_norm_w=ffn_norm_w[0][None, :],
        w_gate=w_gate[0],
        w_up=w_up[0],
        w_down=w_down[0],
    )
    y_prompt, late_bf = _layer(x_prompt, p)
    y_sample, _ = _layer(x_sample, p, late_bf)
    return (y_prompt, y_sample)
```

```python
import functools
import math

import numpy as np
import jax
import jax.numpy as jnp
from jax import lax
from jax.experimental import pallas as pl
from jax.experimental.pallas import tpu as pltpu

F32 = jnp.float32
BF16 = jnp.bfloat16

D_MODEL = 1024
HYENA_WIDTH = 512
HYENA_ORDER = 2
FILTER_BANDS = 16
FILTER_EMB = 1 + 2 * FILTER_BANDS
FILTER_HIDDEN = 64
DECAY_FAST_PCT = 0.3
DECAY_SLOW_PCT = 1.5
DECAY_TARGET = 1e-2
DECAY_SHIFT = 0.05
N_HEADS = 8
N_KV_HEADS = 2
HEAD_DIM = 64
ATTN_WIDTH = N_HEADS * HEAD_DIM
KV_WIDTH = N_KV_HEADS * HEAD_DIM
WINDOW = 128
ROPE_THETA = 10000.0
FFN_HIDDEN = 2816
RMS_EPS = 1e-6
NEG_INF = -1e30
LOG2E = math.log2(math.e)

C_HY = 3 * HYENA_WIDTH
C_Q = C_HY + ATTN_WIDTH
C_K = C_Q + KV_WIDTH
C_V = C_K + KV_WIDTH
IN_WIDTH = C_V + 2 * D_MODEL

LANES = 128
SUBLANES = 8
ATTN_BLOCK = WINDOW
ATTN_HEAD_ORDER = tuple(g * (N_HEADS // N_KV_HEADS) + t for t in range(N_HEADS // N_KV_HEADS) for g in range(N_KV_HEADS))
ATTN_SUB = 4
VMEM_LIMIT = 56 * 1024 * 1024

TM_INPROJ = 512
TM_FINAL = 512
FFN_CHUNKS = ((0, 1280), (1280, FFN_HIDDEN))
_CH_SPLITS = (slice(0, HYENA_WIDTH // 2), slice(HYENA_WIDTH // 2, HYENA_WIDTH))
FILT_PAD = 64


def _cparams(*sem):
    return pltpu.CompilerParams(dimension_semantics=sem, vmem_limit_bytes=VMEM_LIMIT)


def _const_spec(shape):
    nd = len(shape)
    return pl.BlockSpec(shape, lambda *_: (0,) * nd, pipeline_mode=pl.Buffered(1))


def _dot(a, b):
    return jnp.dot(a, b, preferred_element_type=F32)


def _sigmoid(x):
    return 0.5 * jnp.tanh(0.5 * x) + 0.5


def _split(a):
    hi = a.astype(BF16)
    lo = (a - hi.astype(F32)).astype(BF16)
    return hi, lo


def _dot3(a, w):
    ah, al = _split(a)
    wh, wl = _split(w)
    return _dot(ah, wh) + _dot(al, wh) + _dot(ah, wl)


def _inproj_kernel(*refs, tm, tiles_per_seq, n_cast):
    (x_ref, xp_ref, xn_ref, nw_ref, w_ref, cw_ref, cb_ref, qw_ref, kw_ref, cos_ref, sa_ref, sb_ref,
     ones_ref) = refs[:13]
    cast_in = refs[13:13 + n_cast]
    x1_ref, x2_ref, v_ref, q_ref, k4_ref, v4_ref, g_ref = refs[13 + n_cast:20 + n_cast]
    cast_out = refs[20 + n_cast:20 + 2 * n_cast]
    pad_ref = refs[-1]
    for src_ref, dst_ref in zip(cast_in, cast_out):
        dst_ref[...] = src_ref[...].astype(BF16)
    pos = pl.program_id(0) % tiles_per_seq
    nw = nw_ref[...]

    def norm(xv):
        ms = jnp.mean(xv * xv, axis=-1, keepdims=True)
        return (xv * lax.rsqrt(ms + RMS_EPS) * nw).astype(BF16)

    h_all = norm(jnp.concatenate([x_ref[...], xp_ref[...], xn_ref[...]], axis=0))
    h = h_all[:tm]

    hy_all = _dot(h_all, w_ref[:, :C_HY])
    hy = hy_all[:tm]
    hyh = hy_all[tm:]
    prev_ok = (pos > 0).astype(F32)
    next_ok = (pos < tiles_per_seq - 1).astype(F32)
    pad_ref[0:SUBLANES] = hyh[0:SUBLANES] * prev_ok
    pad_ref[SUBLANES:SUBLANES + tm] = hy
    pad_ref[SUBLANES + tm:2 * SUBLANES + tm] = hyh[SUBLANES:] * next_ok
    up = pad_ref[SUBLANES - 1:SUBLANES - 1 + tm]
    un = pad_ref[SUBLANES + 1:SUBLANES + 1 + tm]
    cw = cw_ref[...]
    uc = cw[0:1] * up + cw[1:2] * hy + cw[2:3] * un + cb_ref[...]
    x1_ref[...] = uc[:, :HYENA_WIDTH].astype(BF16)
    x2_ref[...] = uc[:, HYENA_WIDTH:2 * HYENA_WIDTH].astype(BF16)
    v_ref[...] = uc[:, 2 * HYENA_WIDTH:].astype(BF16)

    def head_norm_rope(t, wrow, ones, reps):
        ms = _dot((t * t).astype(BF16), ones)
        tn = t * lax.rsqrt(ms + RMS_EPS) * wrow
        width = t.shape[1]
        cos = jnp.concatenate([cos_ref[...]] * reps, axis=1)
        sa = jnp.concatenate([sa_ref[...]] * reps, axis=1)
        sb = jnp.concatenate([sb_ref[...]] * reps, axis=1)
        half = HEAD_DIM // 2
        return tn * cos + pltpu.roll(tn, width - half, 1) * sa + pltpu.roll(tn, half, 1) * sb

    qkv = _dot(h, w_ref[:, C_HY:C_V])
    q = qkv[:, :ATTN_WIDTH]
    q = head_norm_rope(q, qw_ref[...], ones_ref[...], ATTN_WIDTH // LANES) * (HEAD_DIM ** -0.5 * LOG2E)
    lead = {}
    for p in range(N_HEADS // 2):
        qp = q[:, p * LANES:(p + 1) * LANES]
        lead[2 * p], lead[2 * p + 1] = qp, pltpu.roll(qp, HEAD_DIM, 1)
    q_ref[...] = jnp.concatenate([lead[h] for h in ATTN_HEAD_ORDER], axis=1).astype(BF16)

    k = qkv[:, ATTN_WIDTH:ATTN_WIDTH + KV_WIDTH]
    k = head_norm_rope(k, kw_ref[...], ones_ref[:KV_WIDTH, :KV_WIDTH], 1)
    v = qkv[:, ATTN_WIDTH + KV_WIDTH:]

    lo_half = lax.broadcasted_iota(jnp.int32, (tm, KV_WIDTH), 1) < HEAD_DIM

    def spread(t):
        zero = jnp.zeros_like(t)
        return jnp.concatenate([jnp.where(lo_half, t, zero), jnp.where(lo_half, pltpu.roll(t, HEAD_DIM, 1), zero)],
                               axis=1).astype(BF16)

    k4_ref[...] = spread(k)
    v4_ref[...] = spread(v)
    g_ref[...] = _sigmoid(_dot(h, w_ref[:, C_V:]))


def _inproj(xf, seq_len, nw, w_in_bf, cw, cb, qw, kw, cos, sa, sb, ones_bd, cast=()):
    n = xf.shape[0]
    tm = TM_INPROJ
    tiles_per_seq = seq_len // tm
    nblk8 = n // SUBLANES
    r8 = tm // SUBLANES
    steps = n // tm
    kern = functools.partial(_inproj_kernel, tm=tm, tiles_per_seq=tiles_per_seq, n_cast=len(cast))
    row = lambda i: (i, 0)
    cast_specs = []
    for w in cast:
        per = 1 if (w.shape[0] // steps) % (2 * SUBLANES) == 0 else 2
        rows_blk = w.shape[0] * per // steps
        cast_specs.append(pl.BlockSpec((rows_blk, w.shape[1]), lambda i, per=per: (i // per, 0)))
    return pl.pallas_call(
        kern,
        grid=(n // tm,),
        in_specs=[
            pl.BlockSpec((tm, D_MODEL), row),
            pl.BlockSpec((SUBLANES, D_MODEL), lambda i: (jnp.maximum(i * r8 - 1, 0), 0)),
            pl.BlockSpec((SUBLANES, D_MODEL), lambda i: (jnp.minimum((i + 1) * r8, nblk8 - 1), 0)),
            _const_spec((1, D_MODEL)),
            _const_spec((D_MODEL, IN_WIDTH)),
            _const_spec((3, C_HY)),
            _const_spec((1, C_HY)),
            _const_spec((1, ATTN_WIDTH)),
            _const_spec((1, KV_WIDTH)),
            pl.BlockSpec((tm, LANES), lambda i: (i % tiles_per_seq, 0)),
            pl.BlockSpec((tm, LANES), lambda i: (i % tiles_per_seq, 0)),
            pl.BlockSpec((tm, LANES), lambda i: (i % tiles_per_seq, 0)),
            _const_spec((ATTN_WIDTH, ATTN_WIDTH)),
        ] + cast_specs,
        out_specs=[
            pl.BlockSpec((tm, HYENA_WIDTH), row),
            pl.BlockSpec((tm, HYENA_WIDTH), row),
            pl.BlockSpec((tm, HYENA_WIDTH), row),
            pl.BlockSpec((tm, 2 * ATTN_WIDTH), row),
            pl.BlockSpec((tm, 2 * LANES), row),
            pl.BlockSpec((tm, 2 * LANES), row),
            pl.BlockSpec((tm, 2 * D_MODEL), row),
        ] + cast_specs,
        out_shape=[
            jax.ShapeDtypeStruct((n, HYENA_WIDTH), BF16),
            jax.ShapeDtypeStruct((n, HYENA_WIDTH), BF16),
            jax.ShapeDtypeStruct((n, HYENA_WIDTH), BF16),
            jax.ShapeDtypeStruct((n, 2 * ATTN_WIDTH), BF16),
            jax.ShapeDtypeStruct((n, 2 * LANES), BF16),
            jax.ShapeDtypeStruct((n, 2 * LANES), BF16),
            jax.ShapeDtypeStruct((n, 2 * D_MODEL), F32),
        ] + [jax.ShapeDtypeStruct(w.shape, BF16) for w in cast],
        scratch_shapes=[pltpu.VMEM((tm + 2 * SUBLANES, C_HY), F32)],
        compiler_params=_cparams("parallel"),
        name="inproj",
    )(xf, xf, xf, nw, w_in_bf, cw, cb, qw, kw, cos, sa, sb, ones_bd, *cast)


def _rows_to_tiles(r, tj):
    return r.reshape(r.shape[0], tj, r.shape[1] // tj)


def _tiles_to_rows(t):
    return t.reshape(t.shape[0], t.shape[1] * t.shape[2])


def _twiddle_tiles(r, twr_ref, twi_ref, n1h, tj):
    c = r.shape[1] // tj
    re, im = [], []
    for j in range(tj):
        ar, ai = r[:n1h, j * c:(j + 1) * c], r[n1h:, j * c:(j + 1) * c]
        tr, ti = twr_ref[:, j:j + 1], twi_ref[:, j:j + 1]
        re.append(ar * tr - ai * ti)
        im.append(ar * ti + ai * tr)
    return (_rows_to_tiles(jnp.concatenate(re, axis=1), tj).astype(BF16),
            _rows_to_tiles(jnp.concatenate(im, axis=1), tj).astype(BF16))


def _store_twiddled(r, twr_ref, twi_ref, are_ref, aim_ref, sl, n1h, tj):
    re, im = _twiddle_tiles(r, twr_ref, twi_ref, n1h, tj)
    are_ref[:, :, sl] = re
    aim_ref[:, :, sl] = im


def _filt_kernel(z_ref, w1_ref, b1_ref, w2_ref, b2_ref, w3f_ref, w3b_ref, fr_ref, dl_ref, g_ref, twr_ref, twi_ref,
                 are_ref, aim_ref, s_ref, *, n1h, tj):
    i = pl.program_id(0)
    z = z_ref[...]
    fr = fr_ref[...]
    h = jnp.sin(fr * (_dot3(z, w1_ref[...]) + b1_ref[...]))
    h = jnp.sin(fr * (_dot3(h, w2_ref[...]) + b2_ref[...]))
    hb = h.astype(BF16)
    tf = z[:, 0:1]
    tb = z[:, FILT_PAD:FILT_PAD + 1]
    sgn = z[:, FILT_PAD + FILTER_EMB:FILT_PAD + FILTER_EMB + 1]
    gm = g_ref[...].astype(BF16)
    cb = HYENA_WIDTH
    parts = []
    for o in range(HYENA_ORDER):
        sl = slice(o * cb, (o + 1) * cb)
        dl = dl_ref[...]

        def taps(w_ref, t):
            return _dot(hb, w_ref[:, sl].astype(BF16)) * (jnp.exp(-t * dl) + DECAY_SHIFT)

        kf = taps(w3f_ref, tf)
        kb = taps(w3b_ref, tb) * sgn
        parts.append(jnp.sum(jnp.abs(kf), axis=0, keepdims=True) + jnp.sum(jnp.abs(kb), axis=0, keepdims=True))
        cols = [jnp.concatenate([kf[j * n1h:(j + 1) * n1h], kb[j * n1h:(j + 1) * n1h]], axis=0).astype(BF16)
                for j in range(tj)]
        re, im = _twiddle_tiles(_dot(gm, jnp.concatenate(cols, axis=1)), twr_ref, twi_ref, n1h, tj)
        are_ref[o] = re
        aim_ref[o] = im
    part = jnp.concatenate(parts, axis=1)

    @pl.when(i == 0)
    def _():
        s_ref[...] = part

    @pl.when(i > 0)
    def _():
        s_ref[...] += part


def _filter_stage_a(zf, w1bd, b1, w2bd, b2, w3f, w3b, fr, dl, g_full, twr, twi, n1h, tj):
    cw = HYENA_ORDER * HYENA_WIDTH
    hid2 = 2 * FILTER_HIDDEN
    a_shape = jax.ShapeDtypeStruct((HYENA_ORDER, n1h, LANES, HYENA_WIDTH), BF16)
    a_spec = pl.BlockSpec((HYENA_ORDER, n1h, tj, HYENA_WIDTH), lambda i: (0, 0, i, 0))
    t_spec = pl.BlockSpec((None, n1h, tj), lambda i: (i, 0, 0))
    return pl.pallas_call(
        functools.partial(_filt_kernel, n1h=n1h, tj=tj),
        grid=(LANES // tj,),
        in_specs=[
            pl.BlockSpec((tj * n1h, 2 * FILT_PAD), lambda i: (i, 0)),
            _const_spec((2 * FILT_PAD, hid2)),
            _const_spec((1, hid2)),
            _const_spec((hid2, hid2)),
            _const_spec((1, hid2)),
            _const_spec((hid2, cw)),
            _const_spec((hid2, cw)),
            _const_spec((1, hid2)),
            _const_spec((1, HYENA_WIDTH)),
            _const_spec((2 * n1h, 2 * n1h)),
            t_spec, t_spec,
        ],
        out_specs=[a_spec, a_spec, pl.BlockSpec((1, cw), lambda i: (0, 0))],
        out_shape=[a_shape, a_shape, jax.ShapeDtypeStruct((1, cw), F32)],
        compiler_params=_cparams("arbitrary"),
        name="filter_taps_stage_a",
    )(zf, w1bd, b1, w2bd, b2, w3f, w3b, fr, dl, g_full, twr, twi)


def _fwda_kernel(g_ref, twr_ref, twi_ref, u_ref, are_ref, aim_ref, *, n1h, tj):
    gm = g_ref[...].astype(BF16)
    prev = None
    for sl in _CH_SPLITS:
        u = _tiles_to_rows(u_ref[:, :, sl])
        if prev is not None:
            _store_twiddled(prev[1], twr_ref, twi_ref, are_ref, aim_ref, prev[0], n1h, tj)
        prev = (sl, _dot(gm, u))
    _store_twiddled(prev[1], twr_ref, twi_ref, are_ref, aim_ref, prev[0], n1h, tj)


def _fwd_a(u4, g_half, twr, twi, tj):
    b, n1h, _, cb = u4.shape
    blk = pl.BlockSpec((None, n1h, tj, cb), lambda bi, ji: (bi, 0, ji, 0))
    t_spec = pl.BlockSpec((None, n1h, tj), lambda bi, ji: (ji, 0, 0))
    a_shape = jax.ShapeDtypeStruct(u4.shape, BF16)
    return pl.pallas_call(
        functools.partial(_fwda_kernel, n1h=n1h, tj=tj),
        grid=(b, LANES // tj),
        in_specs=[_const_spec((2 * n1h, n1h)), t_spec, t_spec, blk],
        out_specs=[blk, blk],
        out_shape=[a_shape, a_shape],
        compiler_params=_cparams("parallel", "parallel"),
        name="dft_stage_a",
    )(g_half, twr, twi, u4)


def _mid_kernel(f_ref, fi_ref, twr_ref, twi_ref, fre_ref, fim_ref, are_ref, aim_ref, bre_ref, bim_ref, *scratch,
                kb):
    fm = f_ref[...].astype(BF16)
    fim = fi_ref[...].astype(BF16)
    fill = jnp.zeros((LANES - kb, LANES), F32)
    twr_t = jnp.concatenate([twr_ref[...], fill], axis=0).T
    twi_t = jnp.concatenate([twi_ref[...], fill], axis=0).T

    def filt(k):
        return _dot(fm, jnp.concatenate([fre_ref[k], fim_ref[k]], axis=0))

    if scratch:
        (h_ref,) = scratch

        @pl.when(pl.program_id(1) == 0)
        def _():
            for k in range(kb):
                h_ref[k] = filt(k)

    def fwd(k):
        return (_dot(fm, jnp.concatenate([are_ref[k], aim_ref[k]], axis=0)), h_ref[k] if scratch else filt(k))

    def spec(k, xh):
        x, h = xh
        xr, xi = x[:LANES], x[LANES:]
        hr, hi = h[:LANES], h[LANES:]
        return jnp.concatenate([xr * hr - xi * hi, xr * hi + xi * hr], axis=0).astype(BF16)

    def out(k, bc):
        br, bi = bc[:LANES], bc[LANES:]
        tr, ti = twr_t[:, k:k + 1], twi_t[:, k:k + 1]
        bre_ref[k] = (br * tr + bi * ti).astype(BF16)
        bim_ref[k] = (bi * tr - br * ti).astype(BF16)

    x = fwd(0)
    bc_prev = None
    for k in range(kb):
        y = spec(k, x)
        if k + 1 < kb:
            x = fwd(k + 1)
        if bc_prev is not None:
            out(k - 1, bc_prev)
        bc_prev = _dot(fim, y)
    out(kb - 1, bc_prev)


def _mid(are, aim, fblk, fblk_inv, twr, twi, afre, afim, order, kb):
    b, n1h, _, cb = are.shape
    a_spec = pl.BlockSpec((None, kb, LANES, cb), lambda ki, bi: (bi, ki, 0, 0))
    f_spec = pl.BlockSpec((None, kb, LANES, cb), lambda ki, bi: (order, ki, 0, 0))
    t_spec = pl.BlockSpec((kb, LANES), lambda ki, bi: (ki, 0))
    shape = jax.ShapeDtypeStruct(are.shape, BF16)
    return pl.pallas_call(
        functools.partial(_mid_kernel, kb=kb),
        grid=(n1h // kb, b),
        in_specs=[
            _const_spec((2 * LANES, 2 * LANES)),
            _const_spec((2 * LANES, 2 * LANES)),
            t_spec, t_spec,
            f_spec, f_spec,
            a_spec, a_spec,
        ],
        out_specs=[a_spec, a_spec],
        out_shape=[shape, shape],
        scratch_shapes=[pltpu.VMEM((kb, 2 * LANES, cb), F32)] if b > 1 else [],
        compiler_params=_cparams("parallel", "arbitrary"),
        name="dft_stage_c",
    )(fblk, fblk_inv, twr, twi, afre, afim, are, aim)


def _inva_kernel(gi_ref, bre_ref, bim_ref, xg_ref, v_ref, sb_ref, s_ref, *rest, n1h, tj, fuse_next):
    gim = gi_ref[...].astype(BF16)
    inv_l1 = 1.0 / s_ref[...]
    if fuse_next:
        g_ref, twr_ref, twi_ref, z_ref, are_ref, aim_ref = rest
        gm = g_ref[...].astype(BF16)
    else:
        (z_ref,) = rest

    def conv(sl):
        rhs = jnp.concatenate([_tiles_to_rows(bre_ref[:, :, sl]), _tiles_to_rows(bim_ref[:, :, sl])], axis=0)
        return _dot(gim, rhs)

    def gate(sl, y):
        z = xg_ref[:, :, sl].astype(F32) * (_rows_to_tiles(y, tj) * inv_l1[:, sl]
                                            + sb_ref[:, sl] * v_ref[:, :, sl].astype(F32))
        z = z.astype(BF16)
        z_ref[:, :, sl] = z
        return z

    s0, s1 = _CH_SPLITS
    y0 = conv(s0)
    y1 = conv(s1)
    z0 = gate(s0, y0)
    if fuse_next:
        r0 = _dot(gm, _tiles_to_rows(z0))
    z1 = gate(s1, y1)
    if fuse_next:
        _store_twiddled(r0, twr_ref, twi_ref, are_ref, aim_ref, s0, n1h, tj)
        r1 = _dot(gm, _tiles_to_rows(z1))
        _store_twiddled(r1, twr_ref, twi_ref, are_ref, aim_ref, s1, n1h, tj)


def _inv_a(bre, bim, g_inv, xg, v, sb_row, asum, order, tj, nxt=None):
    b, n1h, _, cb = bre.shape
    blk = pl.BlockSpec((None, n1h, tj, cb), lambda bi, ji: (bi, 0, ji, 0))
    in_specs = [_const_spec((n1h, 2 * n1h)), blk, blk, blk, blk, _const_spec((1, cb)),
                pl.BlockSpec((1, cb), lambda bi, ji: (0, order))]
    args = [g_inv, bre, bim, xg, v, sb_row, asum]
    out_specs = [blk]
    out_shape = [jax.ShapeDtypeStruct(bre.shape, BF16)]
    if nxt is not None:
        t_spec = pl.BlockSpec((None, n1h, tj), lambda bi, ji: (ji, 0, 0))
        in_specs += [_const_spec((2 * n1h, n1h)), t_spec, t_spec]
        args += list(nxt)
        out_specs += [blk, blk]
        out_shape += [jax.ShapeDtypeStruct(bre.shape, BF16)] * 2
    return pl.pallas_call(
        functools.partial(_inva_kernel, n1h=n1h, tj=tj, fuse_next=nxt is not None),
        grid=(b, LANES // tj),
        in_specs=in_specs,
        out_specs=out_specs,
        out_shape=out_shape,
        compiler_params=_cparams("parallel", "parallel"),
        name="idft_stage_a_gate",
    )(*args)


def _attn_kernel(sink_ref, q_ref, kp_ref, kc_ref, kn_ref, vp_ref, vc_ref, vn_ref, o_ref, *, nblk, sub):
    i = pl.program_id(1)
    blk = ATTN_BLOCK
    grp = N_HEADS // N_KV_HEADS
    rows = grp * blk
    win = 3 * blk
    qi = lax.broadcasted_iota(jnp.int32, (rows, win), 0) % blk
    si = lax.broadcasted_iota(jnp.int32, (rows, win), 1)
    mid = (si >= blk) & (si < 2 * blk)
    tri_prev = (si < blk) & (si >= qi)
    tri_next = (si >= 2 * blk) & (si - 2 * blk <= qi)
    hrow = lax.broadcasted_iota(jnp.int32, (rows, 1), 0) // blk
    kcat = jnp.concatenate([kp_ref[...], kc_ref[...], kn_ref[...]], axis=0)
    vcat = jnp.concatenate([vp_ref[...], vc_ref[...], vn_ref[...]], axis=0)
    lane_k = lax.broadcasted_iota(jnp.int32, (win, N_KV_HEADS * LANES), 1)
    lane_o = lax.broadcasted_iota(jnp.int32, (rows, LANES), 1)
    zero_k = jnp.zeros((win, N_KV_HEADS * LANES), BF16)
    for s in range(sub):
        j = sub * i + s
        valid = mid | (tri_prev & (j > 0)) | (tri_next & (j < nblk - 1))
        kw = kcat[s * blk:s * blk + win]
        vw = vcat[s * blk:s * blk + win]
        kd = jnp.concatenate([jnp.where(lane_k // LANES == g, kw, zero_k) for g in range(N_KV_HEADS)], axis=0)
        vd = jnp.concatenate([vw[:, :LANES], pltpu.roll(vw[:, LANES:].astype(F32), HEAD_DIM, 1).astype(BF16)], axis=0)
        lhs = jnp.concatenate([q_ref[s * blk:(s + 1) * blk, t * N_KV_HEADS * LANES:(t + 1) * N_KV_HEADS * LANES]
                               for t in range(grp)], axis=0)
        sc_all = lax.dot_general(lhs, kd, (((1,), (1,)), ((), ())), preferred_element_type=F32)
        es, dens = [], []
        for g in range(N_KV_HEADS):
            sc = jnp.where(valid, sc_all[:, g * win:(g + 1) * win], NEG_INF)
            sk = jnp.full((rows, 1), sink_ref[grp * g], F32)
            for t in range(1, grp):
                sk = jnp.where(hrow == t, sink_ref[grp * g + t], sk)
            sk = sk * LOG2E
            m = jnp.maximum(jnp.max(sc, axis=1, keepdims=True), sk)
            e = jnp.exp2(sc - m)
            dens.append(jnp.sum(e, axis=1, keepdims=True) + jnp.exp2(sk - m))
            es.append(e.astype(BF16))
        den = jnp.where(lane_o < HEAD_DIM, dens[0], dens[1])
        r = _dot(jnp.concatenate(es, axis=1), vd) / den
        o_ref[s * blk:(s + 1) * blk, :] = jnp.concatenate(
            [r[t * blk:(t + 1) * blk] for t in range(grp)], axis=1).astype(BF16)


def _attention(q3, k3, v3, sink):
    b, seq_len, qw = q3.shape
    blk = ATTN_BLOCK
    sub = ATTN_SUB
    nblk = seq_len // blk
    wide = k3.shape[-1]
    prev = lambda bi, i: (bi, jnp.maximum(sub * i - 1, 0), 0)
    cur = lambda bi, i: (bi, i, 0)
    nxt = lambda bi, i: (bi, jnp.minimum(sub * (i + 1), nblk - 1), 0)
    edge = lambda f: pl.BlockSpec((None, blk, wide), f)
    body = pl.BlockSpec((None, sub * blk, wide), cur)
    return pl.pallas_call(
        functools.partial(_attn_kernel, nblk=nblk, sub=sub),
        grid=(b, nblk // sub),
        in_specs=[pl.BlockSpec(memory_space=pltpu.SMEM),
                  pl.BlockSpec((None, sub * blk, qw), cur),
                  edge(prev), body, edge(nxt), edge(prev), body, edge(nxt)],
        out_specs=pl.BlockSpec((None, sub * blk, ATTN_WIDTH), cur),
        out_shape=jax.ShapeDtypeStruct((b, seq_len, ATTN_WIDTH), BF16),
        compiler_params=_cparams("parallel", "parallel"),
        name="banded_attention",
    )(sink, q3, k3, k3, k3, v3, v3, v3)


def _final_kernel(x_ref, yh_ref, ya_ref, g_ref, why_ref, wat_ref, wo_ref, nw_ref, wg_ref, wu_ref, wd_ref,
                  o_ref):
    a = _dot(yh_ref[...], why_ref[...])
    b = _dot(ya_ref[...], wat_ref[...])
    merged = g_ref[:, :D_MODEL] * a + g_ref[:, D_MODEL:] * b
    x1 = x_ref[...] + _dot(merged.astype(BF16), wo_ref[...])
    ms = jnp.mean(x1 * x1, axis=-1, keepdims=True)
    f = (x1 * lax.rsqrt(ms + RMS_EPS) * nw_ref[...]).astype(BF16)
    acc = x1
    for lo, hi in FFN_CHUNKS:
        gt = _dot(f, wg_ref[:, lo:hi])
        upv = _dot(f, wu_ref[:, lo:hi])
        hid = gt * _sigmoid(gt) * upv
        acc = acc + _dot(hid.astype(BF16), wd_ref[lo:hi, :])
    o_ref[...] = acc


def _final(xf, yh, ya, g, why, wat, wo, nw, wg, wu, wd):
    n = xf.shape[0]
    tm = TM_FINAL
    row = lambda i: (i, 0)
    return pl.pallas_call(
        _final_kernel,
        grid=(n // tm,),
        in_specs=[
            pl.BlockSpec((tm, D_MODEL), row),
            pl.BlockSpec((tm, HYENA_WIDTH), row),
            pl.BlockSpec((tm, ATTN_WIDTH), row),
            pl.BlockSpec((tm, 2 * D_MODEL), row),
            _const_spec((HYENA_WIDTH, D_MODEL)),
            _const_spec((ATTN_WIDTH, D_MODEL)),
            _const_spec((D_MODEL, D_MODEL)),
            _const_spec((1, D_MODEL)),
            _const_spec((D_MODEL, FFN_HIDDEN)),
            _const_spec((D_MODEL, FFN_HIDDEN)),
            _const_spec((FFN_HIDDEN, D_MODEL)),
        ],
        out_specs=pl.BlockSpec((tm, D_MODEL), row),
        out_shape=jax.ShapeDtypeStruct((n, D_MODEL), F32),
        compiler_params=_cparams("parallel"),
        name="merge_ffn",
    )(xf, yh, ya, g, why, wat, wo, nw, wg, wu, wd)


@functools.lru_cache(maxsize=None)
def _dft_constants(n1h, tj):
    n1_len = 2 * n1h
    m = n1_len * LANES
    k1 = np.arange(n1h, dtype=np.float64)[:, None] + 0.5
    n1 = np.arange(n1_len, dtype=np.float64)[None, :]
    gc = np.exp(-2j * np.pi * k1 * n1 / n1_len)
    g_full = np.concatenate([gc.real, gc.imag], axis=0)
    g_inv = (2.0 / m) * np.concatenate([gc.real[:, :n1h].T, gc.imag[:, :n1h].T], axis=1)
    n2 = np.arange(LANES, dtype=np.float64)
    tw = np.exp(-2j * np.pi * k1 * n2[None, :] / m)
    fc = np.exp(-2j * np.pi * np.outer(n2, n2) / LANES)
    fblk = np.block([[fc.real, -fc.imag], [fc.imag, fc.real]])
    fblk_inv = np.block([[fc.real, fc.imag], [-fc.imag, fc.real]])
    f32 = lambda a: np.ascontiguousarray(a, dtype=np.float32)
    tiled = lambda a: a.reshape(n1h, LANES // tj, tj).transpose(1, 0, 2)
    return dict(g_full=f32(g_full), g_half=f32(g_full[:, :n1h]), g_inv=f32(g_inv), twr=f32(tw.real),
                twi=f32(tw.imag), twr_a=f32(tiled(tw.real)), twi_a=f32(tiled(tw.imag)), fblk=f32(fblk),
                fblk_inv=f32(fblk_inv))


@functools.lru_cache(maxsize=None)
def _filter_features(seq_len):
    n1h = seq_len // LANES
    n2 = np.arange(LANES)[:, None]
    n1 = np.arange(n1h)[None, :]
    slot_f = (LANES * n1 + n2).reshape(-1)
    slot_b = slot_f + seq_len
    neg_lag = slot_b > seq_len
    pos_b = np.where(neg_lag, 2 * seq_len - slot_b, 0)
    bands = np.linspace(1e-4, FILTER_BANDS - 1, FILTER_BANDS)

    def feats(pos, sgn):
        pos = pos.astype(np.float64)
        ang = (2.0 * math.pi / seq_len) * pos[:, None] * bands[None, :]
        pad = np.zeros((pos.shape[0], FILT_PAD - FILTER_EMB - 1))
        return np.concatenate([(pos / (seq_len - 1))[:, None], np.cos(ang), -np.sin(ang), sgn[:, None], pad], axis=-1)

    z = np.concatenate([feats(slot_f, np.ones(slot_f.shape)), feats(pos_b, np.where(neg_lag, -1.0, 0.0))], axis=-1)
    return np.ascontiguousarray(z, dtype=np.float32)


@functools.lru_cache(maxsize=None)
def _rope_tables(seq_len):
    inv = ROPE_THETA ** (-np.arange(0, HEAD_DIM, 2, dtype=np.float64) / HEAD_DIM)
    ang = np.arange(seq_len, dtype=np.float64)[:, None] * inv[None, :]
    cos, sin = np.cos(ang), np.sin(ang)
    zero = np.zeros_like(sin)
    reps = LANES // HEAD_DIM
    f32 = lambda a: np.ascontiguousarray(np.tile(a, (1, reps)), dtype=np.float32)
    return f32(np.concatenate([cos, cos], axis=1)), f32(np.concatenate([-sin, zero], axis=1)), \
        f32(np.concatenate([zero, sin], axis=1))


_LATE_WEIGHTS = ("w_hy_out", "w_at_out", "w_o", "w_gate", "w_up", "w_down")


def _layer(x, p, late_bf=None):
    b, seq_len, _ = x.shape
    n = b * seq_len
    n1h = seq_len // LANES
    xf = x.reshape(n, D_MODEL)
    tj = SUBLANES if n1h >= LANES else 4 * SUBLANES
    c = {k: jnp.asarray(v) for k, v in _dft_constants(n1h, tj).items()}
    kb = SUBLANES
    r4 = lambda a: a.reshape(b, n1h, LANES, HYENA_WIDTH)

    cos, sa, sb = (jnp.asarray(t) for t in _rope_tables(seq_len))
    cast = () if late_bf is not None else tuple(p[k] for k in _LATE_WEIGHTS)
    res = _inproj(xf, seq_len, p["attn_norm_w"], p["w_in"], p["conv_w"], p["conv_b"],
                  p["q_norm_w"], p["k_norm_w"], cos, sa, sb, p["ones_bd"], cast)
    x1, x2, v, q, k4, v4, g = res[:7]
    if late_bf is None:
        late_bf = dict(zip(_LATE_WEIGHTS, res[7:]))

    afre, afim, asum = _filter_stage_a(jnp.asarray(_filter_features(seq_len)), p["filt_w1"], p["filt_b1"],
                                       p["filt_w2"], p["filt_b2"], p["filt_w3f"], p["filt_w3b"], p["filt_freq"],
                                       p["decay"], c["g_full"], c["twr_a"], c["twi_a"], n1h, tj)

    are, aim = _fwd_a(r4(v), c["g_half"], c["twr_a"], c["twi_a"], tj)
    bre, bim = _mid(are, aim, c["fblk"], c["fblk_inv"], c["twr"], c["twi"], afre, afim, 0, kb)
    z1, are, aim = _inv_a(bre, bim, c["g_inv"], r4(x1), r4(v), p["hyena_bias"][0:1], asum, 0, tj,
                          nxt=(c["g_half"], c["twr_a"], c["twi_a"]))
    bre, bim = _mid(are, aim, c["fblk"], c["fblk_inv"], c["twr"], c["twi"], afre, afim, 1, kb)
    (yh,) = _inv_a(bre, bim, c["g_inv"], r4(x2), z1, p["hyena_bias"][1:2], asum, 1, tj)

    ya = _attention(q.reshape(b, seq_len, 2 * ATTN_WIDTH), k4.reshape(b, seq_len, 2 * LANES),
                    v4.reshape(b, seq_len, 2 * LANES), p["attn_sink"])

    w = late_bf
    out = _final(xf, yh.reshape(n, HYENA_WIDTH), ya.reshape(n, ATTN_WIDTH), g, w["w_hy_out"], w["w_at_out"],
                 w["w_o"], p["ffn_norm_w"], w["w_gate"], w["w_up"], w["w_down"])
    return out.reshape(b, seq_len, D_MODEL), late_bf


def _block_diag2(w):
    z = jnp.zeros_like(w)
    return jnp.concatenate([jnp.concatenate([w, z], axis=1), jnp.concatenate([z, w], axis=1)], axis=0)


def kernel(x_prompt, x_sample, attn_norm_w, w_in, hyena_conv_w, hyena_conv_b, filt_w1, filt_b1, filt_w2, filt_b2,
           filt_w3, filt_freq, hyena_bias, q_norm_w, k_norm_w, attn_sink, w_hy_out, w_at_out, w_o, ffn_norm_w,
           w_gate, w_up, w_down):
    cw = HYENA_ORDER * HYENA_WIDTH
    max_decay = math.log(DECAY_TARGET) / DECAY_FAST_PCT
    min_decay = math.log(DECAY_TARGET) / DECAY_SLOW_PCT
    deltas = jnp.abs(jnp.linspace(min_decay, max_decay, HYENA_WIDTH, dtype=F32))
    head = np.arange(ATTN_WIDTH) // HEAD_DIM
    ones_bd = jnp.asarray((head[:, None] == head[None, :]).astype(np.float32) / HEAD_DIM).astype(BF16)
    w3 = filt_w3[0].reshape(FILTER_HIDDEN, HYENA_ORDER, 2, HYENA_WIDTH).transpose(2, 0, 1, 3)
    w3 = w3.reshape(2, FILTER_HIDDEN, cw)
    w3_zero = jnp.zeros((FILTER_HIDDEN, cw), F32)
    twice = lambda a: jnp.tile(a, 2)[None, :]
    p = dict(
        attn_norm_w=attn_norm_w[0][None, :],
        w_in=w_in[0].astype(BF16),
        conv_w=hyena_conv_w[0],
        conv_b=hyena_conv_b[0][None, :],
        filt_w1=_block_diag2(jnp.pad(filt_w1[0], ((0, FILT_PAD - FILTER_EMB), (0, 0)))),
        filt_b1=twice(filt_b1[0]),
        filt_w2=_block_diag2(filt_w2[0]),
        filt_b2=twice(filt_b2[0]),
        filt_w3f=jnp.concatenate([w3[0], w3_zero], axis=0),
        filt_w3b=jnp.concatenate([w3_zero, w3[1]], axis=0),
        filt_freq=twice(filt_freq[0]),
        decay=deltas[None, :],
        hyena_bias=hyena_bias[0],
        q_norm_w=jnp.tile(q_norm_w[0], N_HEADS)[None, :],
        k_norm_w=jnp.tile(k_norm_w[0], N_KV_HEADS)[None, :],
        attn_sink=attn_sink[0],
        ones_bd=ones_bd,
        w_hy_out=w_hy_out[0],
        w_at_out=w_at_out[0].reshape(N_HEADS, HEAD_DIM, D_MODEL)[np.array(ATTN_HEAD_ORDER)].reshape(ATTN_WIDTH, D_MODEL),
        w_o=w_o[0],
        ffn_norm_w=ffn_norm_w[0][None, :],
        w_gate=w_gate[0],
        w_up=w_up[0],
        w_down=w_down[0],
    )
    y_prompt, late_bf = _layer(x_prompt, p)
    y_sample, _ = _layer(x_sample, p, late_bf)
    return (y_prompt, y_sample)
```

```python
import functools
import math

import numpy as np
import jax
import jax.numpy as jnp
from jax import lax
from jax.experimental import pallas as pl
from jax.experimental.pallas import tpu as pltpu

F32 = jnp.float32
BF16 = jnp.bfloat16

D_MODEL = 1024
HYENA_WIDTH = 512
HYENA_ORDER = 2
FILTER_BANDS = 16
FILTER_EMB = 1 + 2 * FILTER_BANDS
FILTER_HIDDEN = 64
DECAY_FAST_PCT = 0.3
DECAY_SLOW_PCT = 1.5
DECAY_TARGET = 1e-2
DECAY_SHIFT = 0.05
N_HEADS = 8
N_KV_HEADS = 2
HEAD_DIM = 64
ATTN_WIDTH = N_HEADS * HEAD_DIM
KV_WIDTH = N_KV_HEADS * HEAD_DIM
WINDOW = 128
ROPE_THETA = 10000.0
FFN_HIDDEN = 2816
RMS_EPS = 1e-6
NEG_INF = -1e30
LOG2E = math.log2(math.e)

C_HY = 3 * HYENA_WIDTH
C_Q = C_HY + ATTN_WIDTH
C_K = C_Q + KV_WIDTH
C_V = C_K + KV_WIDTH
IN_WIDTH = C_V + 2 * D_MODEL

LANES = 128
SUBLANES = 8
ATTN_BLOCK = WINDOW
ATTN_HEAD_ORDER = tuple(g * (N_HEADS // N_KV_HEADS) + t for t in range(N_HEADS // N_KV_HEADS) for g in range(N_KV_HEADS))
ATTN_SUB = 4
VMEM_LIMIT = 56 * 1024 * 1024

TM_INPROJ = 512
TM_FINAL = 512
FFN_CHUNKS = ((0, 1280), (1280, FFN_HIDDEN))
_CH_SPLITS = (slice(0, HYENA_WIDTH // 2), slice(HYENA_WIDTH // 2, HYENA_WIDTH))
FILT_PAD = 64


def _cparams(*sem):
    return pltpu.CompilerParams(dimension_semantics=sem, vmem_limit_bytes=VMEM_LIMIT)


def _const_spec(shape):
    nd = len(shape)
    return pl.BlockSpec(shape, lambda *_: (0,) * nd, pipeline_mode=pl.Buffered(1))


def _dot(a, b):
    return jnp.dot(a, b, preferred_element_type=F32)


def _sigmoid(x):
    return 0.5 * jnp.tanh(0.5 * x) + 0.5


def _split(a):
    hi = a.astype(BF16)
    lo = (a - hi.astype(F32)).astype(BF16)
    return hi, lo


def _dot3(a, w):
    ah, al = _split(a)
    wh, wl = _split(w)
    return _dot(ah, wh) + _dot(al, wh) + _dot(ah, wl)


def _inproj_kernel(*refs, tm, tiles_per_seq, n_cast):
    (x_ref, xp_ref, xn_ref, nw_ref, w_ref, cw_ref, cb_ref, qw_ref, kw_ref, cos_ref, sa_ref, sb_ref,
     ones_ref) = refs[:13]
    cast_in = refs[13:13 + n_cast]
    x1_ref, x2_ref, v_ref, q_ref, k4_ref, v4_ref, g_ref = refs[13 + n_cast:20 + n_cast]
    cast_out = refs[20 + n_cast:20 + 2 * n_cast]
    pad_ref = refs[-1]
    for src_ref, dst_ref in zip(cast_in, cast_out):
        dst_ref[...] = src_ref[...].astype(BF16)
    pos = pl.program_id(0) % tiles_per_seq
    nw = nw_ref[...]

    def norm(xv):
        ms = jnp.mean(xv * xv, axis=-1, keepdims=True)
        return (xv * lax.rsqrt(ms + RMS_EPS) * nw).astype(BF16)

    h_all = norm(jnp.concatenate([x_ref[...], xp_ref[...], xn_ref[...]], axis=0))
    h = h_all[:tm]

    hy_all = _dot(h_all, w_ref[:, :C_HY])
    hy = hy_all[:tm]
    hyh = hy_all[tm:]
    prev_ok = (pos > 0).astype(F32)
    next_ok = (pos < tiles_per_seq - 1).astype(F32)
    pad_ref[0:SUBLANES] = hyh[0:SUBLANES] * prev_ok
    pad_ref[SUBLANES:SUBLANES + tm] = hy
    pad_ref[SUBLANES + tm:2 * SUBLANES + tm] = hyh[SUBLANES:] * next_ok
    up = pad_ref[SUBLANES - 1:SUBLANES - 1 + tm]
    un = pad_ref[SUBLANES + 1:SUBLANES + 1 + tm]
    cw = cw_ref[...]
    uc = cw[0:1] * up + cw[1:2] * hy + cw[2:3] * un + cb_ref[...]
    x1_ref[...] = uc[:, :HYENA_WIDTH].astype(BF16)
    x2_ref[...] = uc[:, HYENA_WIDTH:2 * HYENA_WIDTH].astype(BF16)
    v_ref[...] = uc[:, 2 * HYENA_WIDTH:].astype(BF16)

    def head_norm_rope(t, wrow, ones, reps):
        ms = _dot((t * t).astype(BF16), ones)
        tn = t * lax.rsqrt(ms + RMS_EPS) * wrow
        width = t.shape[1]
        cos = jnp.concatenate([cos_ref[...]] * reps, axis=1)
        sa = jnp.concatenate([sa_ref[...]] * reps, axis=1)
        sb = jnp.concatenate([sb_ref[...]] * reps, axis=1)
        half = HEAD_DIM // 2
        return tn * cos + pltpu.roll(tn, width - half, 1) * sa + pltpu.roll(tn, half, 1) * sb

    qkv = _dot(h, w_ref[:, C_HY:C_V])
    q = qkv[:, :ATTN_WIDTH]
    q = head_norm_rope(q, qw_ref[...], ones_ref[...], ATTN_WIDTH // LANES) * (HEAD_DIM ** -0.5 * LOG2E)
    lead = {}
    for p in range(N_HEADS // 2):
        qp = q[:, p * LANES:(p + 1) * LANES]
        lead[2 * p], lead[2 * p + 1] = qp, pltpu.roll(qp, HEAD_DIM, 1)
    q_ref[...] = jnp.concatenate([lead[h] for h in ATTN_HEAD_ORDER], axis=1).astype(BF16)

    k = qkv[:, ATTN_WIDTH:ATTN_WIDTH + KV_WIDTH]
    k = head_norm_rope(k, kw_ref[...], ones_ref[:KV_WIDTH, :KV_WIDTH], 1)
    v = qkv[:, ATTN_WIDTH + KV_WIDTH:]

    lo_half = lax.broadcasted_iota(jnp.int32, (tm, KV_WIDTH), 1) < HEAD_DIM

    def spread(t):
        zero = jnp.zeros_like(t)
        return jnp.concatenate([jnp.where(lo_half, t, zero), jnp.where(lo_half, pltpu.roll(t, HEAD_DIM, 1), zero)],
                               axis=1).astype(BF16)

    k4_ref[...] = spread(k)
    v4_ref[...] = spread(v)
    g_ref[...] = _sigmoid(_dot(h, w_ref[:, C_V:]))


def _inproj(xf, seq_len, nw, w_in_bf, cw, cb, qw, kw, cos, sa, sb, ones_bd, cast=()):
    n = xf.shape[0]
    tm = TM_INPROJ
    tiles_per_seq = seq_len // tm
    nblk8 = n // SUBLANES
    r8 = tm // SUBLANES
    steps = n // tm
    kern = functools.partial(_inproj_kernel, tm=tm, tiles_per_seq=tiles_per_seq, n_cast=len(cast))
    row = lambda i: (i, 0)
    cast_specs = []
    for w in cast:
        per = 1 if (w.shape[0] // steps) % (2 * SUBLANES) == 0 else 2
        rows_blk = w.shape[0] * per // steps
        cast_specs.append(pl.BlockSpec((rows_blk, w.shape[1]), lambda i, per=per: (i // per, 0)))
    return pl.pallas_call(
        kern,
        grid=(n // tm,),
        in_specs=[
            pl.BlockSpec((tm, D_MODEL), row),
            pl.BlockSpec((SUBLANES, D_MODEL), lambda i: (jnp.maximum(i * r8 - 1, 0), 0)),
            pl.BlockSpec((SUBLANES, D_MODEL), lambda i: (jnp.minimum((i + 1) * r8, nblk8 - 1), 0)),
            _const_spec((1, D_MODEL)),
            _const_spec((D_MODEL, IN_WIDTH)),
            _const_spec((3, C_HY)),
            _const_spec((1, C_HY)),
            _const_spec((1, ATTN_WIDTH)),
            _const_spec((1, KV_WIDTH)),
            pl.BlockSpec((tm, LANES), lambda i: (i % tiles_per_seq, 0)),
            pl.BlockSpec((tm, LANES), lambda i: (i % tiles_per_seq, 0)),
            pl.BlockSpec((tm, LANES), lambda i: (i % tiles_per_seq, 0)),
            _const_spec((ATTN_WIDTH, ATTN_WIDTH)),
        ] + cast_specs,
        out_specs=[
            pl.BlockSpec((tm, HYENA_WIDTH), row),
            pl.BlockSpec((tm, HYENA_WIDTH), row),
            pl.BlockSpec((tm, HYENA_WIDTH), row),
            pl.BlockSpec((tm, 2 * ATTN_WIDTH), row),
            pl.BlockSpec((tm, 2 * LANES), row),
            pl.BlockSpec((tm, 2 * LANES), row),
            pl.BlockSpec((tm, 2 * D_MODEL), row),
        ] + cast_specs,
        out_shape=[
            jax.ShapeDtypeStruct((n, HYENA_WIDTH), BF16),
            jax.ShapeDtypeStruct((n, HYENA_WIDTH), BF16),
            jax.ShapeDtypeStruct((n, HYENA_WIDTH), BF16),
            jax.ShapeDtypeStruct((n, 2 * ATTN_WIDTH), BF16),
            jax.ShapeDtypeStruct((n, 2 * LANES), BF16),
            jax.ShapeDtypeStruct((n, 2 * LANES), BF16),
            jax.ShapeDtypeStruct((n, 2 * D_MODEL), F32),
        ] + [jax.ShapeDtypeStruct(w.shape, BF16) for w in cast],
        scratch_shapes=[pltpu.VMEM((tm + 2 * SUBLANES, C_HY), F32)],
        compiler_params=_cparams("parallel"),
        name="inproj",
    )(xf, xf, xf, nw, w_in_bf, cw, cb, qw, kw, cos, sa, sb, ones_bd, *cast)


def _rows_to_tiles(r, tj):
    return r.reshape(r.shape[0], tj, r.shape[1] // tj)


def _tiles_to_rows(t):
    return t.reshape(t.shape[0], t.shape[1] * t.shape[2])


def _twiddle_tiles(r, twr_ref, twi_ref, n1h, tj):
    c = r.shape[1] // tj
    re, im = [], []
    for j in range(tj):
        ar, ai = r[:n1h, j * c:(j + 1) * c], r[n1h:, j * c:(j + 1) * c]
        tr, ti = twr_ref[:, j:j + 1], twi_ref[:, j:j + 1]
        re.append(ar * tr - ai * ti)
        im.append(ar * ti + ai * tr)
    return (_rows_to_tiles(jnp.concatenate(re, axis=1), tj).astype(BF16),
            _rows_to_tiles(jnp.concatenate(im, axis=1), tj).astype(BF16))


def _store_twiddled(r, twr_ref, twi_ref, are_ref, aim_ref, sl, n1h, tj):
    re, im = _twiddle_tiles(r, twr_ref, twi_ref, n1h, tj)
    are_ref[:, :, sl] = re
    aim_ref[:, :, sl] = im


def _filt_kernel(z_ref, w1_ref, b1_ref, w2_ref, b2_ref, w3f_ref, w3b_ref, fr_ref, dl_ref, g_ref, twr_ref, twi_ref,
                 are_ref, aim_ref, s_ref, *, n1h, tj):
    i = pl.program_id(0)
    z = z_ref[...]
    fr = fr_ref[...]
    h = jnp.sin(fr * (_dot3(z, w1_ref[...]) + b1_ref[...]))
    h = jnp.sin(fr * (_dot3(h, w2_ref[...]) + b2_ref[...]))
    hb = h.astype(BF16)
    tf = z[:, 0:1]
    tb = z[:, FILT_PAD:FILT_PAD + 1]
    sgn = z[:, FILT_PAD + FILTER_EMB:FILT_PAD + FILTER_EMB + 1]
    gm = g_ref[...].astype(BF16)
    cb = HYENA_WIDTH
    parts = []
    for o in range(HYENA_ORDER):
        sl = slice(o * cb, (o + 1) * cb)
        dl = dl_ref[...]

        def taps(w_ref, t):
            return _dot(hb, w_ref[:, sl].astype(BF16)) * (jnp.exp(-t * dl) + DECAY_SHIFT)

        kf = taps(w3f_ref, tf)
        kb = taps(w3b_ref, tb) * sgn
        parts.append(jnp.sum(jnp.abs(kf), axis=0, keepdims=True) + jnp.sum(jnp.abs(kb), axis=0, keepdims=True))
        cols = [jnp.concatenate([kf[j * n1h:(j + 1) * n1h], kb[j * n1h:(j + 1) * n1h]], axis=0).astype(BF16)
                for j in range(tj)]
        re, im = _twiddle_tiles(_dot(gm, jnp.concatenate(cols, axis=1)), twr_ref, twi_ref, n1h, tj)
        are_ref[o] = re
        aim_ref[o] = im
    part = jnp.concatenate(parts, axis=1)

    @pl.when(i == 0)
    def _():
        s_ref[...] = part

    @pl.when(i > 0)
    def _():
        s_ref[...] += part


def _filter_stage_a(zf, w1bd, b1, w2bd, b2, w3f, w3b, fr, dl, g_full, twr, twi, n1h, tj):
    cw = HYENA_ORDER * HYENA_WIDTH
    hid2 = 2 * FILTER_HIDDEN
    a_shape = jax.ShapeDtypeStruct((HYENA_ORDER, n1h, LANES, HYENA_WIDTH), BF16)
    a_spec = pl.BlockSpec((HYENA_ORDER, n1h, tj, HYENA_WIDTH), lambda i: (0, 0, i, 0))
    t_spec = pl.BlockSpec((None, n1h, tj), lambda i: (i, 0, 0))
    return pl.pallas_call(
        functools.partial(_filt_kernel, n1h=n1h, tj=tj),
        grid=(LANES // tj,),
        in_specs=[
            pl.BlockSpec((tj * n1h, 2 * FILT_PAD), lambda i: (i, 0)),
            _const_spec((2 * FILT_PAD, hid2)),
            _const_spec((1, hid2)),
            _const_spec((hid2, hid2)),
            _const_spec((1, hid2)),
            _const_spec((hid2, cw)),
            _const_spec((hid2, cw)),
            _const_spec((1, hid2)),
            _const_spec((1, HYENA_WIDTH)),
            _const_spec((2 * n1h, 2 * n1h)),
            t_spec, t_spec,
        ],
        out_specs=[a_spec, a_spec, pl.BlockSpec((1, cw), lambda i: (0, 0))],
        out_shape=[a_shape, a_shape, jax.ShapeDtypeStruct((1, cw), F32)],
        compiler_params=_cparams("arbitrary"),
        name="filter_taps_stage_a",
    )(zf, w1bd, b1, w2bd, b2, w3f, w3b, fr, dl, g_full, twr, twi)


def _fwda_kernel(g_ref, twr_ref, twi_ref, u_ref, are_ref, aim_ref, *, n1h, tj):
    gm = g_ref[...].astype(BF16)
    prev = None
    for sl in _CH_SPLITS:
        u = _tiles_to_rows(u_ref[:, :, sl])
        if prev is not None:
            _store_twiddled(prev[1], twr_ref, twi_ref, are_ref, aim_ref, prev[0], n1h, tj)
        prev = (sl, _dot(gm, u))
    _store_twiddled(prev[1], twr_ref, twi_ref, are_ref, aim_ref, prev[0], n1h, tj)


def _fwd_a(u4, g_half, twr, twi, tj):
    b, n1h, _, cb = u4.shape
    blk = pl.BlockSpec((None, n1h, tj, cb), lambda bi, ji: (bi, 0, ji, 0))
    t_spec = pl.BlockSpec((None, n1h, tj), lambda bi, ji: (ji, 0, 0))
    a_shape = jax.ShapeDtypeStruct(u4.shape, BF16)
    return pl.pallas_call(
        functools.partial(_fwda_kernel, n1h=n1h, tj=tj),
        grid=(b, LANES // tj),
        in_specs=[_const_spec((2 * n1h, n1h)), t_spec, t_spec, blk],
        out_specs=[blk, blk],
        out_shape=[a_shape, a_shape],
        compiler_params=_cparams("parallel", "parallel"),
        name="dft_stage_a",
    )(g_half, twr, twi, u4)


def _mid_kernel(f_ref, fi_ref, twr_ref, twi_ref, fre_ref, fim_ref, are_ref, aim_ref, bre_ref, bim_ref, *scratch,
                kb):
    fm = f_ref[...].astype(BF16)
    fim = fi_ref[...].astype(BF16)
    fill = jnp.zeros((LANES - kb, LANES), F32)
    twr_t = jnp.concatenate([twr_ref[...], fill], axis=0).T
    twi_t = jnp.concatenate([twi_ref[...], fill], axis=0).T

    def filt(k):
        return _dot(fm, jnp.concatenate([fre_ref[k], fim_ref[k]], axis=0))

    if scratch:
        (h_ref,) = scratch

        @pl.when(pl.program_id(1) == 0)
        def _():
            for k in range(kb):
                h_ref[k] = filt(k)

    def fwd(k):
        return (_dot(fm, jnp.concatenate([are_ref[k], aim_ref[k]], axis=0)), h_ref[k] if scratch else filt(k))

    def spec(k, xh):
        x, h = xh
        xr, xi = x[:LANES], x[LANES:]
        hr, hi = h[:LANES], h[LANES:]
        return jnp.concatenate([xr * hr - xi * hi, xr * hi + xi * hr], axis=0).astype(BF16)

    def out(k, bc):
        br, bi = bc[:LANES], bc[LANES:]
        tr, ti = twr_t[:, k:k + 1], twi_t[:, k:k + 1]
        bre_ref[k] = (br * tr + bi * ti).astype(BF16)
        bim_ref[k] = (bi * tr - br * ti).astype(BF16)

    x = fwd(0)
    bc_prev = None
    for k in range(kb):
        y = spec(k, x)
        if k + 1 < kb:
            x = fwd(k + 1)
        if bc_prev is not None:
            out(k - 1, bc_prev)
        bc_prev = _dot(fim, y)
    out(kb - 1, bc_prev)


def _mid(are, aim, fblk, fblk_inv, twr, twi, afre, afim, order, kb):
    b, n1h, _, cb = are.shape
    a_spec = pl.BlockSpec((None, kb, LANES, cb), lambda ki, bi: (bi, ki, 0, 0))
    f_spec = pl.BlockSpec((None, kb, LANES, cb), lambda ki, bi: (order, ki, 0, 0))
    t_spec = pl.BlockSpec((kb, LANES), lambda ki, bi: (ki, 0))
    shape = jax.ShapeDtypeStruct(are.shape, BF16)
    return pl.pallas_call(
        functools.partial(_mid_kernel, kb=kb),
        grid=(n1h // kb, b),
        in_specs=[
            _const_spec((2 * LANES, 2 * LANES)),
            _const_spec((2 * LANES, 2 * LANES)),
            t_spec, t_spec,
            f_spec, f_spec,
            a_spec, a_spec,
        ],
        out_specs=[a_spec, a_spec],
        out_shape=[shape, shape],
        scratch_shapes=[pltpu.VMEM((kb, 2 * LANES, cb), F32)] if b > 1 else [],
        compiler_params=_cparams("parallel", "arbitrary"),
        name="dft_stage_c",
    )(fblk, fblk_inv, twr, twi, afre, afim, are, aim)


def _inva_kernel(gi_ref, bre_ref, bim_ref, xg_ref, v_ref, sb_ref, s_ref, *rest, n1h, tj, fuse_next):
    gim = gi_ref[...].astype(BF16)
    inv_l1 = 1.0 / s_ref[...]
    if fuse_next:
        g_ref, twr_ref, twi_ref, z_ref, are_ref, aim_ref = rest
        gm = g_ref[...].astype(BF16)
    else:
        (z_ref,) = rest

    def conv(sl):
        rhs = jnp.concatenate([_tiles_to_rows(bre_ref[:, :, sl]), _tiles_to_rows(bim_ref[:, :, sl])], axis=0)
        return _dot(gim, rhs)

    def gate(sl, y):
        z = xg_ref[:, :, sl].astype(F32) * (_rows_to_tiles(y, tj) * inv_l1[:, sl]
                                            + sb_ref[:, sl] * v_ref[:, :, sl].astype(F32))
        z = z.astype(BF16)
        z_ref[:, :, sl] = z
        return z

    s0, s1 = _CH_SPLITS
    y0 = conv(s0)
    y1 = conv(s1)
    z0 = gate(s0, y0)
    if fuse_next:
        r0 = _dot(gm, _tiles_to_rows(z0))
    z1 = gate(s1, y1)
    if fuse_next:
        _store_twiddled(r0, twr_ref, twi_ref, are_ref, aim_ref, s0, n1h, tj)
        r1 = _dot(gm, _tiles_to_rows(z1))
        _store_twiddled(r1, twr_ref, twi_ref, are_ref, aim_ref, s1, n1h, tj)


def _inv_a(bre, bim, g_inv, xg, v, sb_row, asum, order, tj, nxt=None):
    b, n1h, _, cb = bre.shape
    blk = pl.BlockSpec((None, n1h, tj, cb), lambda bi, ji: (bi, 0, ji, 0))
    in_specs = [_const_spec((n1h, 2 * n1h)), blk, blk, blk, blk, _const_spec((1, cb)),
                pl.BlockSpec((1, cb), lambda bi, ji: (0, order))]
    args = [g_inv, bre, bim, xg, v, sb_row, asum]
    out_specs = [blk]
    out_shape = [jax.ShapeDtypeStruct(bre.shape, BF16)]
    if nxt is not None:
        t_spec = pl.BlockSpec((None, n1h, tj), lambda bi, ji: (ji, 0, 0))
        in_specs += [_const_spec((2 * n1h, n1h)), t_spec, t_spec]
        args += list(nxt)
        out_specs += [blk, blk]
        out_shape += [jax.ShapeDtypeStruct(bre.shape, BF16)] * 2
    return pl.pallas_call(
        functools.partial(_inva_kernel, n1h=n1h, tj=tj, fuse_next=nxt is not None),
        grid=(b, LANES // tj),
        in_specs=in_specs,
        out_specs=out_specs,
        out_shape=out_shape,
        compiler_params=_cparams("parallel", "parallel"),
        name="idft_stage_a_gate",
    )(*args)


def _attn_kernel(sink_ref, q_ref, kp_ref, kc_ref, kn_ref, vp_ref, vc_ref, vn_ref, o_ref, *, nblk, sub):
    i = pl.program_id(1)
    blk = ATTN_BLOCK
    grp = N_HEADS // N_KV_HEADS
    rows = grp * blk
    win = 3 * blk
    qi = lax.broadcasted_iota(jnp.int32, (rows, win), 0) % blk
    si = lax.broadcasted_iota(jnp.int32, (rows, win), 1)
    mid = (si >= blk) & (si < 2 * blk)
    tri_prev = (si < blk) & (si >= qi)
    tri_next = (si >= 2 * blk) & (si - 2 * blk <= qi)
    hrow = lax.broadcasted_iota(jnp.int32, (rows, 1), 0) // blk
    kcat = jnp.concatenate([kp_ref[...], kc_ref[...], kn_ref[...]], axis=0)
    vcat = jnp.concatenate([vp_ref[...], vc_ref[...], vn_ref[...]], axis=0)
    lane_k = lax.broadcasted_iota(jnp.int32, (win, N_KV_HEADS * LANES), 1)
    lane_o = lax.broadcasted_iota(jnp.int32, (rows, LANES), 1)
    zero_k = jnp.zeros((win, N_KV_HEADS * LANES), BF16)
    for s in range(sub):
        j = sub * i + s
        valid = mid | (tri_prev & (j > 0)) | (tri_next & (j < nblk - 1))
        kw = kcat[s * blk:s * blk + win]
        vw = vcat[s * blk:s * blk + win]
        kd = jnp.concatenate([jnp.where(lane_k // LANES == g, kw, zero_k) for g in range(N_KV_HEADS)], axis=0)
        vd = jnp.concatenate([vw[:, :LANES], pltpu.roll(vw[:, LANES:].astype(F32), HEAD_DIM, 1).astype(BF16)], axis=0)
        lhs = jnp.concatenate([q_ref[s * blk:(s + 1) * blk, t * N_KV_HEADS * LANES:(t + 1) * N_KV_HEADS * LANES]
                               for t in range(grp)], axis=0)
        sc_all = lax.dot_general(lhs, kd, (((1,), (1,)), ((), ())), preferred_element_type=F32)
        es, dens = [], []
        for g in range(N_KV_HEADS):
            sc = jnp.where(valid, sc_all[:, g * win:(g + 1) * win], NEG_INF)
            sk = jnp.full((rows, 1), sink_ref[grp * g], F32)
            for t in range(1, grp):
                sk = jnp.where(hrow == t, sink_ref[grp * g + t], sk)
            sk = sk * LOG2E
            m = jnp.maximum(jnp.max(sc, axis=1, keepdims=True), sk)
            e = jnp.exp2(sc - m)
            dens.append(jnp.sum(e, axis=1, keepdims=True) + jnp.exp2(sk - m))
            es.append(e.astype(BF16))
        den = jnp.where(lane_o < HEAD_DIM, dens[0], dens[1])
        r = _dot(jnp.concatenate(es, axis=1), vd) / den
        o_ref[s * blk:(s + 1) * blk, :] = jnp.concatenate(
            [r[t * blk:(t + 1) * blk] for t in range(grp)], axis=1).astype(BF16)


def _attention(q3, k3, v3, sink):
    b, seq_len, qw = q3.shape
    blk = ATTN_BLOCK
    sub = ATTN_SUB
    nblk = seq_len // blk
    wide = k3.shape[-1]
    prev = lambda bi, i: (bi, jnp.maximum(sub * i - 1, 0), 0)
    cur = lambda bi, i: (bi, i, 0)
    nxt = lambda bi, i: (bi, jnp.minimum(sub * (i + 1), nblk - 1), 0)
    edge = lambda f: pl.BlockSpec((None, blk, wide), f)
    body = pl.BlockSpec((None, sub * blk, wide), cur)
    return pl.pallas_call(
        functools.partial(_attn_kernel, nblk=nblk, sub=sub),
        grid=(b, nblk // sub),
        in_specs=[pl.BlockSpec(memory_space=pltpu.SMEM),
                  pl.BlockSpec((None, sub * blk, qw), cur),
                  edge(prev), body, edge(nxt), edge(prev), body, edge(nxt)],
        out_specs=pl.BlockSpec((None, sub * blk, ATTN_WIDTH), cur),
        out_shape=jax.ShapeDtypeStruct((b, seq_len, ATTN_WIDTH), BF16),
        compiler_params=_cparams("parallel", "parallel"),
        name="banded_attention",
    )(sink, q3, k3, k3, k3, v3, v3, v3)


def _final_kernel(x_ref, yh_ref, ya_ref, g_ref, why_ref, wat_ref, wo_ref, nw_ref, wg_ref, wu_ref, wd_ref,
                  o_ref):
    a = _dot(yh_ref[...], why_ref[...])
    b = _dot(ya_ref[...], wat_ref[...])
    merged = g_ref[:, :D_MODEL] * a + g_ref[:, D_MODEL:] * b
    x1 = x_ref[...] + _dot(merged.astype(BF16), wo_ref[...])
    ms = jnp.mean(x1 * x1, axis=-1, keepdims=True)
    f = (x1 * lax.rsqrt(ms + RMS_EPS) * nw_ref[...]).astype(BF16)
    acc = x1
    for lo, hi in FFN_CHUNKS:
        gt = _dot(f, wg_ref[:, lo:hi])
        upv = _dot(f, wu_ref[:, lo:hi])
        hid = gt * _sigmoid(gt) * upv
        acc = acc + _dot(hid.astype(BF16), wd_ref[lo:hi, :])
    o_ref[...] = acc


def _final(xf, yh, ya, g, why, wat, wo, nw, wg, wu, wd):
    n = xf.shape[0]
    tm = TM_FINAL
    row = lambda i: (i, 0)
    return pl.pallas_call(
        _final_kernel,
        grid=(n // tm,),
        in_specs=[
            pl.BlockSpec((tm, D_MODEL), row),
            pl.BlockSpec((tm, HYENA_WIDTH), row),
            pl.BlockSpec((tm, ATTN_WIDTH), row),
            pl.BlockSpec((tm, 2 * D_MODEL), row),
            _const_spec((HYENA_WIDTH, D_MODEL)),
            _const_spec((ATTN_WIDTH, D_MODEL)),
            _const_spec((D_MODEL, D_MODEL)),
            _const_spec((1, D_MODEL)),
            _const_spec((D_MODEL, FFN_HIDDEN)),
            _const_spec((D_MODEL, FFN_HIDDEN)),
            _const_spec((FFN_HIDDEN, D_MODEL)),
        ],
        out_specs=pl.BlockSpec((tm, D_MODEL), row),
        out_shape=jax.ShapeDtypeStruct((n, D_MODEL), F32),
        compiler_params=_cparams("parallel"),
        name="merge_ffn",
    )(xf, yh, ya, g, why, wat, wo, nw, wg, wu, wd)


@functools.lru_cache(maxsize=None)
def _dft_constants(n1h, tj):
    n1_len = 2 * n1h
    m = n1_len * LANES
    k1 = np.arange(n1h, dtype=np.float64)[:, None] + 0.5
    n1 = np.arange(n1_len, dtype=np.float64)[None, :]
    gc = np.exp(-2j * np.pi * k1 * n1 / n1_len)
    g_full = np.concatenate([gc.real, gc.imag], axis=0)
    g_inv = (2.0 / m) * np.concatenate([gc.real[:, :n1h].T, gc.imag[:, :n1h].T], axis=1)
    n2 = np.arange(LANES, dtype=np.float64)
    tw = np.exp(-2j * np.pi * k1 * n2[None, :] / m)
    fc = np.exp(-2j * np.pi * np.outer(n2, n2) / LANES)
    fblk = np.block([[fc.real, -fc.imag], [fc.imag, fc.real]])
    fblk_inv = np.block([[fc.real, fc.imag], [-fc.imag, fc.real]])
    f32 = lambda a: np.ascontiguousarray(a, dtype=np.float32)
    tiled = lambda a: a.reshape(n1h, LANES // tj, tj).transpose(1, 0, 2)
    return dict(g_full=f32(g_full), g_half=f32(g_full[:, :n1h]), g_inv=f32(g_inv), twr=f32(tw.real),
                twi=f32(tw.imag), twr_a=f32(tiled(tw.real)), twi_a=f32(tiled(tw.imag)), fblk=f32(fblk),
                fblk_inv=f32(fblk_inv))


@functools.lru_cache(maxsize=None)
def _filter_features(seq_len):
    n1h = seq_len // LANES
    n2 = np.arange(LANES)[:, None]
    n1 = np.arange(n1h)[None, :]
    slot_f = (LANES * n1 + n2).reshape(-1)
    slot_b = slot_f + seq_len
    neg_lag = slot_b > seq_len
    pos_b = np.where(neg_lag, 2 * seq_len - slot_b, 0)
    bands = np.linspace(1e-4, FILTER_BANDS - 1, FILTER_BANDS)

    def feats(pos, sgn):
        pos = pos.astype(np.float64)
        ang = (2.0 * math.pi / seq_len) * pos[:, None] * bands[None, :]
        pad = np.zeros((pos.shape[0], FILT_PAD - FILTER_EMB - 1))
        return np.concatenate([(pos / (seq_len - 1))[:, None], np.cos(ang), -np.sin(ang), sgn[:, None], pad], axis=-1)

    z = np.concatenate([feats(slot_f, np.ones(slot_f.shape)), feats(pos_b, np.where(neg_lag, -1.0, 0.0))], axis=-1)
    return np.ascontiguousarray(z, dtype=np.float32)


@functools.lru_cache(maxsize=None)
def _rope_tables(seq_len):
    inv = ROPE_THETA ** (-np.arange(0, HEAD_DIM, 2, dtype=np.float64) / HEAD_DIM)
    ang = np.arange(seq_len, dtype=np.float64)[:, None] * inv[None, :]
    cos, sin = np.cos(ang), np.sin(ang)
    zero = np.zeros_like(sin)
    reps = LANES // HEAD_DIM
    f32 = lambda a: np.ascontiguousarray(np.tile(a, (1, reps)), dtype=np.float32)
    return f32(np.concatenate([cos, cos], axis=1)), f32(np.concatenate([-sin, zero], axis=1)), \
        f32(np.concatenate([zero, sin], axis=1))


_LATE_WEIGHTS = ("w_hy_out", "w_at_out", "w_o", "w_gate", "w_up", "w_down")


def _layer(x, p, late_bf=None):
    b, seq_len, _ = x.shape
    n = b * seq_len
    n1h = seq_len // LANES
    xf = x.reshape(n, D_MODEL)
    tj = 2 * SUBLANES if n1h >= LANES else 8 * SUBLANES
    c = {k: jnp.asarray(v) for k, v in _dft_constants(n1h, tj).items()}
    kb = SUBLANES
    r4 = lambda a: a.reshape(b, n1h, LANES, HYENA_WIDTH)

    cos, sa, sb = (jnp.asarray(t) for t in _rope_tables(seq_len))
    cast = () if late_bf is not None else tuple(p[k] for k in _LATE_WEIGHTS)
    res = _inproj(xf, seq_len, p["attn_norm_w"], p["w_in"], p["conv_w"], p["conv_b"],
                  p["q_norm_w"], p["k_norm_w"], cos, sa, sb, p["ones_bd"], cast)
    x1, x2, v, q, k4, v4, g = res[:7]
    if late_bf is None:
        late_bf = dict(zip(_LATE_WEIGHTS, res[7:]))

    afre, afim, asum = _filter_stage_a(jnp.asarray(_filter_features(seq_len)), p["filt_w1"], p["filt_b1"],
                                       p["filt_w2"], p["filt_b2"], p["filt_w3f"], p["filt_w3b"], p["filt_freq"],
                                       p["decay"], c["g_full"], c["twr_a"], c["twi_a"], n1h, tj)

    are, aim = _fwd_a(r4(v), c["g_half"], c["twr_a"], c["twi_a"], tj)
    bre, bim = _mid(are, aim, c["fblk"], c["fblk_inv"], c["twr"], c["twi"], afre, afim, 0, kb)
    z1, are, aim = _inv_a(bre, bim, c["g_inv"], r4(x1), r4(v), p["hyena_bias"][0:1], asum, 0, tj,
                          nxt=(c["g_half"], c["twr_a"], c["twi_a"]))
    bre, bim = _mid(are, aim, c["fblk"], c["fblk_inv"], c["twr"], c["twi"], afre, afim, 1, kb)
    (yh,) = _inv_a(bre, bim, c["g_inv"], r4(x2), z1, p["hyena_bias"][1:2], asum, 1, tj)

    ya = _attention(q.reshape(b, seq_len, 2 * ATTN_WIDTH), k4.reshape(b, seq_len, 2 * LANES),
                    v4.reshape(b, seq_len, 2 * LANES), p["attn_sink"])

    w = late_bf
    out = _final(xf, yh.reshape(n, HYENA_WIDTH), ya.reshape(n, ATTN_WIDTH), g, w["w_hy_out"], w["w_at_out"],
                 w["w_o"], p["ffn_norm_w"], w["w_gate"], w["w_up"], w["w_down"])
    return out.reshape(b, seq_len, D_MODEL), late_bf


def _block_diag2(w):
    z = jnp.zeros_like(w)
    return jnp.concatenate([jnp.concatenate([w, z], axis=1), jnp.concatenate([z, w], axis=1)], axis=0)


def kernel(x_prompt, x_sample, attn_norm_w, w_in, hyena_conv_w, hyena_conv_b, filt_w1, filt_b1, filt_w2, filt_b2,
           filt_w3, filt_freq, hyena_bias, q_norm_w, k_norm_w, attn_sink, w_hy_out, w_at_out, w_o, ffn_norm_w,
           w_gate, w_up, w_down):
    cw = HYENA_ORDER * HYENA_WIDTH
    max_decay = math.log(DECAY_TARGET) / DECAY_FAST_PCT
    min_decay = math.log(DECAY_TARGET) / DECAY_SLOW_PCT
    deltas = jnp.abs(jnp.linspace(min_decay, max_decay, HYENA_WIDTH, dtype=F32))
    head = np.arange(ATTN_WIDTH) // HEAD_DIM
    ones_bd = jnp.asarray((head[:, None] == head[None, :]).astype(np.float32) / HEAD_DIM).astype(BF16)
    w3 = filt_w3[0].reshape(FILTER_HIDDEN, HYENA_ORDER, 2, HYENA_WIDTH).transpose(2, 0, 1, 3)
    w3 = w3.reshape(2, FILTER_HIDDEN, cw)
    w3_zero = jnp.zeros((FILTER_HIDDEN, cw), F32)
    twice = lambda a: jnp.tile(a, 2)[None, :]
    p = dict(
        attn_norm_w=attn_norm_w[0][None, :],
        w_in=w_in[0].astype(BF16),
        conv_w=hyena_conv_w[0],
        conv_b=hyena_conv_b[0][None, :],
        filt_w1=_block_diag2(jnp.pad(filt_w1[0], ((0, FILT_PAD - FILTER_EMB), (0, 0)))),
        filt_b1=twice(filt_b1[0]),
        filt_w2=_block_diag2(filt_w2[0]),
        filt_b2=twice(filt_b2[0]),
        filt_w3f=jnp.concatenate([w3[0], w3_zero], axis=0),
        filt_w3b=jnp.concatenate([w3_zero, w3[1]], axis=0),
        filt_freq=twice(filt_freq[0]),
        decay=deltas[None, :],
        hyena_bias=hyena_bias[0],
        q_norm_w=jnp.tile(q_norm_w[0], N_HEADS)[None, :],
        k_norm_w=jnp.tile(k_norm_w[0], N_KV_HEADS)[None, :],
        attn_sink=attn_sink[0],
        ones_bd=ones_bd,
        w_hy_out=w_hy_out[0],
        w_at_out=w_at_out[0].reshape(N_HEADS, HEAD_DIM, D_MODEL)[np.array(ATTN_HEAD_ORDER)].reshape(ATTN_WIDTH, D_MODEL),
        w_o=w_o[0],
        ffn_norm_w=ffn_norm_w[0][None, :],
        w_gate=w_gate[0],
        w_up=w_up[0],
        w_down=w_down[0],
    )
    y_prompt, late_bf = _layer(x_prompt, p)
    y_sample, _ = _layer(x_sample, p, late_bf)
    return (y_prompt, y_sample)
```

```python
import functools
import math

import numpy as np
import jax
import jax.numpy as jnp
from jax import lax
from jax.experimental import pallas as pl
from jax.experimental.pallas import tpu as pltpu

F32 = jnp.float32
BF16 = jnp.bfloat16

D_MODEL = 1024
HYENA_WIDTH = 512
HYENA_ORDER = 2
FILTER_BANDS = 16
FILTER_EMB = 1 + 2 * FILTER_BANDS
FILTER_HIDDEN = 64
DECAY_FAST_PCT = 0.3
DECAY_SLOW_PCT = 1.5
DECAY_TARGET = 1e-2
DECAY_SHIFT = 0.05
N_HEADS = 8
N_KV_HEADS = 2
HEAD_DIM = 64
ATTN_WIDTH = N_HEADS * HEAD_DIM
KV_WIDTH = N_KV_HEADS * HEAD_DIM
WINDOW = 128
ROPE_THETA = 10000.0
FFN_HIDDEN = 2816
RMS_EPS = 1e-6
NEG_INF = -1e30
LOG2E = math.log2(math.e)

C_HY = 3 * HYENA_WIDTH
C_Q = C_HY + ATTN_WIDTH
C_K = C_Q + KV_WIDTH
C_V = C_K + KV_WIDTH
IN_WIDTH = C_V + 2 * D_MODEL

LANES = 128
SUBLANES = 8
ATTN_BLOCK = WINDOW
ATTN_HEAD_ORDER = tuple(g * (N_HEADS // N_KV_HEADS) + t for t in range(N_HEADS // N_KV_HEADS) for g in range(N_KV_HEADS))
ATTN_SUB = 4
VMEM_LIMIT = 56 * 1024 * 1024

TM_INPROJ = 512
TM_FINAL = 512
FFN_CHUNKS = ((0, 1280), (1280, FFN_HIDDEN))
_CH_SPLITS = (slice(0, HYENA_WIDTH // 2), slice(HYENA_WIDTH // 2, HYENA_WIDTH))
STAGE_A_TILES = {False: (128, 128, 128, 128), True: (16, 8, 8, 16)}
FILT_PAD = 64


def _cparams(*sem):
    return pltpu.CompilerParams(dimension_semantics=sem, vmem_limit_bytes=VMEM_LIMIT)


def _const_spec(shape):
    nd = len(shape)
    return pl.BlockSpec(shape, lambda *_: (0,) * nd, pipeline_mode=pl.Buffered(1))


def _dot(a, b):
    return jnp.dot(a, b, preferred_element_type=F32)


def _sigmoid(x):
    return 0.5 * jnp.tanh(0.5 * x) + 0.5


def _split(a):
    hi = a.astype(BF16)
    lo = (a - hi.astype(F32)).astype(BF16)
    return hi, lo


def _dot3(a, w):
    ah, al = _split(a)
    wh, wl = _split(w)
    return _dot(ah, wh) + _dot(al, wh) + _dot(ah, wl)


def _inproj_kernel(*refs, tm, tiles_per_seq, n_cast):
    (x_ref, xp_ref, xn_ref, nw_ref, w_ref, cw_ref, cb_ref, qw_ref, kw_ref, cos_ref, sa_ref, sb_ref,
     ones_ref) = refs[:13]
    cast_in = refs[13:13 + n_cast]
    x1_ref, x2_ref, v_ref, q_ref, k4_ref, v4_ref, g_ref = refs[13 + n_cast:20 + n_cast]
    cast_out = refs[20 + n_cast:20 + 2 * n_cast]
    pad_ref = refs[-1]
    for src_ref, dst_ref in zip(cast_in, cast_out):
        dst_ref[...] = src_ref[...].astype(BF16)
    pos = pl.program_id(0) % tiles_per_seq
    nw = nw_ref[...]

    def norm(xv):
        ms = jnp.mean(xv * xv, axis=-1, keepdims=True)
        return (xv * lax.rsqrt(ms + RMS_EPS) * nw).astype(BF16)

    h_all = norm(jnp.concatenate([x_ref[...], xp_ref[...], xn_ref[...]], axis=0))
    h = h_all[:tm]

    hy_all = _dot(h_all, w_ref[:, :C_HY])
    hy = hy_all[:tm]
    hyh = hy_all[tm:]
    prev_ok = (pos > 0).astype(F32)
    next_ok = (pos < tiles_per_seq - 1).astype(F32)
    pad_ref[0:SUBLANES] = hyh[0:SUBLANES] * prev_ok
    pad_ref[SUBLANES:SUBLANES + tm] = hy
    pad_ref[SUBLANES + tm:2 * SUBLANES + tm] = hyh[SUBLANES:] * next_ok
    up = pad_ref[SUBLANES - 1:SUBLANES - 1 + tm]
    un = pad_ref[SUBLANES + 1:SUBLANES + 1 + tm]
    cw = cw_ref[...]
    uc = cw[0:1] * up + cw[1:2] * hy + cw[2:3] * un + cb_ref[...]
    x1_ref[...] = uc[:, :HYENA_WIDTH].astype(BF16)
    x2_ref[...] = uc[:, HYENA_WIDTH:2 * HYENA_WIDTH].astype(BF16)
    v_ref[...] = uc[:, 2 * HYENA_WIDTH:].astype(BF16)

    def head_norm_rope(t, wrow, ones, reps):
        ms = _dot((t * t).astype(BF16), ones)
        tn = t * lax.rsqrt(ms + RMS_EPS) * wrow
        width = t.shape[1]
        cos = jnp.concatenate([cos_ref[...]] * reps, axis=1)
        sa = jnp.concatenate([sa_ref[...]] * reps, axis=1)
        sb = jnp.concatenate([sb_ref[...]] * reps, axis=1)
        half = HEAD_DIM // 2
        return tn * cos + pltpu.roll(tn, width - half, 1) * sa + pltpu.roll(tn, half, 1) * sb

    qkv = _dot(h, w_ref[:, C_HY:C_V])
    q = qkv[:, :ATTN_WIDTH]
    q = head_norm_rope(q, qw_ref[...], ones_ref[...], ATTN_WIDTH // LANES) * (HEAD_DIM ** -0.5 * LOG2E)
    lead = {}
    for p in range(N_HEADS // 2):
        qp = q[:, p * LANES:(p + 1) * LANES]
        lead[2 * p], lead[2 * p + 1] = qp, pltpu.roll(qp, HEAD_DIM, 1)
    q_ref[...] = jnp.concatenate([lead[h] for h in ATTN_HEAD_ORDER], axis=1).astype(BF16)

    k = qkv[:, ATTN_WIDTH:ATTN_WIDTH + KV_WIDTH]
    k = head_norm_rope(k, kw_ref[...], ones_ref[:KV_WIDTH, :KV_WIDTH], 1)
    v = qkv[:, ATTN_WIDTH + KV_WIDTH:]

    lo_half = lax.broadcasted_iota(jnp.int32, (tm, KV_WIDTH), 1) < HEAD_DIM

    def spread(t):
        zero = jnp.zeros_like(t)
        return jnp.concatenate([jnp.where(lo_half, t, zero), jnp.where(lo_half, pltpu.roll(t, HEAD_DIM, 1), zero)],
                               axis=1).astype(BF16)

    k4_ref[...] = spread(k)
    v4_ref[...] = spread(v)
    g_ref[...] = _sigmoid(_dot(h, w_ref[:, C_V:]))


def _inproj(xf, seq_len, nw, w_in_bf, cw, cb, qw, kw, cos, sa, sb, ones_bd, cast=()):
    n = xf.shape[0]
    tm = TM_INPROJ
    tiles_per_seq = seq_len // tm
    nblk8 = n // SUBLANES
    r8 = tm // SUBLANES
    steps = n // tm
    kern = functools.partial(_inproj_kernel, tm=tm, tiles_per_seq=tiles_per_seq, n_cast=len(cast))
    row = lambda i: (i, 0)
    cast_specs = []
    for w in cast:
        per = 1 if (w.shape[0] // steps) % (2 * SUBLANES) == 0 else 2
        rows_blk = w.shape[0] * per // steps
        cast_specs.append(pl.BlockSpec((rows_blk, w.shape[1]), lambda i, per=per: (i // per, 0)))
    return pl.pallas_call(
        kern,
        grid=(n // tm,),
        in_specs=[
            pl.BlockSpec((tm, D_MODEL), row),
            pl.BlockSpec((SUBLANES, D_MODEL), lambda i: (jnp.maximum(i * r8 - 1, 0), 0)),
            pl.BlockSpec((SUBLANES, D_MODEL), lambda i: (jnp.minimum((i + 1) * r8, nblk8 - 1), 0)),
            _const_spec((1, D_MODEL)),
            _const_spec((D_MODEL, IN_WIDTH)),
            _const_spec((3, C_HY)),
            _const_spec((1, C_HY)),
            _const_spec((1, ATTN_WIDTH)),
            _const_spec((1, KV_WIDTH)),
            pl.BlockSpec((tm, LANES), lambda i: (i % tiles_per_seq, 0)),
            pl.BlockSpec((tm, LANES), lambda i: (i % tiles_per_seq, 0)),
            pl.BlockSpec((tm, LANES), lambda i: (i % tiles_per_seq, 0)),
            _const_spec((ATTN_WIDTH, ATTN_WIDTH)),
        ] + cast_specs,
        out_specs=[
            pl.BlockSpec((tm, HYENA_WIDTH), row),
            pl.BlockSpec((tm, HYENA_WIDTH), row),
            pl.BlockSpec((tm, HYENA_WIDTH), row),
            pl.BlockSpec((tm, 2 * ATTN_WIDTH), row),
            pl.BlockSpec((tm, 2 * LANES), row),
            pl.BlockSpec((tm, 2 * LANES), row),
            pl.BlockSpec((tm, 2 * D_MODEL), row),
        ] + cast_specs,
        out_shape=[
            jax.ShapeDtypeStruct((n, HYENA_WIDTH), BF16),
            jax.ShapeDtypeStruct((n, HYENA_WIDTH), BF16),
            jax.ShapeDtypeStruct((n, HYENA_WIDTH), BF16),
            jax.ShapeDtypeStruct((n, 2 * ATTN_WIDTH), BF16),
            jax.ShapeDtypeStruct((n, 2 * LANES), BF16),
            jax.ShapeDtypeStruct((n, 2 * LANES), BF16),
            jax.ShapeDtypeStruct((n, 2 * D_MODEL), F32),
        ] + [jax.ShapeDtypeStruct(w.shape, BF16) for w in cast],
        scratch_shapes=[pltpu.VMEM((tm + 2 * SUBLANES, C_HY), F32)],
        compiler_params=_cparams("parallel"),
        name="inproj",
    )(xf, xf, xf, nw, w_in_bf, cw, cb, qw, kw, cos, sa, sb, ones_bd, *cast)


def _rows_to_tiles(r, tj):
    return r.reshape(r.shape[0], tj, r.shape[1] // tj)


def _tiles_to_rows(t):
    return t.reshape(t.shape[0], t.shape[1] * t.shape[2])


def _twiddle_tiles(r, twr_ref, twi_ref, n1h, tj):
    c = r.shape[1] // tj
    re, im = [], []
    for j in range(tj):
        ar, ai = r[:n1h, j * c:(j + 1) * c], r[n1h:, j * c:(j + 1) * c]
        tr, ti = twr_ref[:, j:j + 1], twi_ref[:, j:j + 1]
        re.append(ar * tr - ai * ti)
        im.append(ar * ti + ai * tr)
    return (_rows_to_tiles(jnp.concatenate(re, axis=1), tj).astype(BF16),
            _rows_to_tiles(jnp.concatenate(im, axis=1), tj).astype(BF16))


def _store_twiddled(r, twr_ref, twi_ref, are_ref, aim_ref, sl, n1h, tj):
    re, im = _twiddle_tiles(r, twr_ref, twi_ref, n1h, tj)
    are_ref[:, :, sl] = re
    aim_ref[:, :, sl] = im


def _filt_kernel(z_ref, w1_ref, b1_ref, w2_ref, b2_ref, w3f_ref, w3b_ref, fr_ref, dl_ref, g_ref, twr_ref, twi_ref,
                 are_ref, aim_ref, s_ref, *, n1h, tj):
    i = pl.program_id(0)
    z = z_ref[...]
    fr = fr_ref[...]
    h = jnp.sin(fr * (_dot3(z, w1_ref[...]) + b1_ref[...]))
    h = jnp.sin(fr * (_dot3(h, w2_ref[...]) + b2_ref[...]))
    hb = h.astype(BF16)
    tf = z[:, 0:1]
    tb = z[:, FILT_PAD:FILT_PAD + 1]
    sgn = z[:, FILT_PAD + FILTER_EMB:FILT_PAD + FILTER_EMB + 1]
    gm = g_ref[...].astype(BF16)
    cb = HYENA_WIDTH
    parts = []
    for o in range(HYENA_ORDER):
        sl = slice(o * cb, (o + 1) * cb)
        dl = dl_ref[...]

        def taps(w_ref, t):
            return _dot(hb, w_ref[:, sl].astype(BF16)) * (jnp.exp(-t * dl) + DECAY_SHIFT)

        kf = taps(w3f_ref, tf)
        kb = taps(w3b_ref, tb) * sgn
        parts.append(jnp.sum(jnp.abs(kf), axis=0, keepdims=True) + jnp.sum(jnp.abs(kb), axis=0, keepdims=True))
        cols = [jnp.concatenate([kf[j * n1h:(j + 1) * n1h], kb[j * n1h:(j + 1) * n1h]], axis=0).astype(BF16)
                for j in range(tj)]
        re, im = _twiddle_tiles(_dot(gm, jnp.concatenate(cols, axis=1)), twr_ref, twi_ref, n1h, tj)
        are_ref[o] = re
        aim_ref[o] = im
    part = jnp.concatenate(parts, axis=1)

    @pl.when(i == 0)
    def _():
        s_ref[...] = part

    @pl.when(i > 0)
    def _():
        s_ref[...] += part


def _filter_stage_a(zf, w1bd, b1, w2bd, b2, w3f, w3b, fr, dl, g_full, twr, twi, n1h, tj):
    cw = HYENA_ORDER * HYENA_WIDTH
    hid2 = 2 * FILTER_HIDDEN
    a_shape = jax.ShapeDtypeStruct((HYENA_ORDER, n1h, LANES, HYENA_WIDTH), BF16)
    a_spec = pl.BlockSpec((HYENA_ORDER, n1h, tj, HYENA_WIDTH), lambda i: (0, 0, i, 0))
    t_spec = pl.BlockSpec((None, n1h, tj), lambda i: (i, 0, 0))
    return pl.pallas_call(
        functools.partial(_filt_kernel, n1h=n1h, tj=tj),
        grid=(LANES // tj,),
        in_specs=[
            pl.BlockSpec((tj * n1h, 2 * FILT_PAD), lambda i: (i, 0)),
            _const_spec((2 * FILT_PAD, hid2)),
            _const_spec((1, hid2)),
            _const_spec((hid2, hid2)),
            _const_spec((1, hid2)),
            _const_spec((hid2, cw)),
            _const_spec((hid2, cw)),
            _const_spec((1, hid2)),
            _const_spec((1, HYENA_WIDTH)),
            _const_spec((2 * n1h, 2 * n1h)),
            t_spec, t_spec,
        ],
        out_specs=[a_spec, a_spec, pl.BlockSpec((1, cw), lambda i: (0, 0))],
        out_shape=[a_shape, a_shape, jax.ShapeDtypeStruct((1, cw), F32)],
        compiler_params=_cparams("arbitrary"),
        name="filter_taps_stage_a",
    )(zf, w1bd, b1, w2bd, b2, w3f, w3b, fr, dl, g_full, twr, twi)


def _fwda_kernel(g_ref, twr_ref, twi_ref, u_ref, are_ref, aim_ref, *, n1h, tj):
    gm = g_ref[...].astype(BF16)
    prev = None
    for sl in _CH_SPLITS:
        u = _tiles_to_rows(u_ref[:, :, sl])
        if prev is not None:
            _store_twiddled(prev[1], twr_ref, twi_ref, are_ref, aim_ref, prev[0], n1h, tj)
        prev = (sl, _dot(gm, u))
    _store_twiddled(prev[1], twr_ref, twi_ref, are_ref, aim_ref, prev[0], n1h, tj)


def _fwd_a(u4, g_half, twr, twi, tj):
    b, n1h, _, cb = u4.shape
    blk = pl.BlockSpec((None, n1h, tj, cb), lambda bi, ji: (bi, 0, ji, 0))
    t_spec = pl.BlockSpec((None, n1h, tj), lambda bi, ji: (ji, 0, 0))
    a_shape = jax.ShapeDtypeStruct(u4.shape, BF16)
    return pl.pallas_call(
        functools.partial(_fwda_kernel, n1h=n1h, tj=tj),
        grid=(b, LANES // tj),
        in_specs=[_const_spec((2 * n1h, n1h)), t_spec, t_spec, blk],
        out_specs=[blk, blk],
        out_shape=[a_shape, a_shape],
        compiler_params=_cparams("parallel", "parallel"),
        name="dft_stage_a",
    )(g_half, twr, twi, u4)


def _mid_kernel(f_ref, fi_ref, twr_ref, twi_ref, fre_ref, fim_ref, are_ref, aim_ref, bre_ref, bim_ref, *scratch,
                kb):
    fm = f_ref[...].astype(BF16)
    fim = fi_ref[...].astype(BF16)
    fill = jnp.zeros((LANES - kb, LANES), F32)
    twr_t = jnp.concatenate([twr_ref[...], fill], axis=0).T
    twi_t = jnp.concatenate([twi_ref[...], fill], axis=0).T

    def filt(k):
        return _dot(fm, jnp.concatenate([fre_ref[k], fim_ref[k]], axis=0))

    if scratch:
        (h_ref,) = scratch

        @pl.when(pl.program_id(1) == 0)
        def _():
            for k in range(kb):
                h_ref[k] = filt(k)

    def fwd(k):
        return (_dot(fm, jnp.concatenate([are_ref[k], aim_ref[k]], axis=0)), h_ref[k] if scratch else filt(k))

    def spec(k, xh):
        x, h = xh
        xr, xi = x[:LANES], x[LANES:]
        hr, hi = h[:LANES], h[LANES:]
        return jnp.concatenate([xr * hr - xi * hi, xr * hi + xi * hr], axis=0).astype(BF16)

    def out(k, bc):
        br, bi = bc[:LANES], bc[LANES:]
        tr, ti = twr_t[:, k:k + 1], twi_t[:, k:k + 1]
        bre_ref[k] = (br * tr + bi * ti).astype(BF16)
        bim_ref[k] = (bi * tr - br * ti).astype(BF16)

    x = fwd(0)
    bc_prev = None
    for k in range(kb):
        y = spec(k, x)
        if k + 1 < kb:
            x = fwd(k + 1)
        if bc_prev is not None:
            out(k - 1, bc_prev)
        bc_prev = _dot(fim, y)
    out(kb - 1, bc_prev)


def _mid(are, aim, fblk, fblk_inv, twr, twi, afre, afim, order, kb):
    b, n1h, _, cb = are.shape
    a_spec = pl.BlockSpec((None, kb, LANES, cb), lambda ki, bi: (bi, ki, 0, 0))
    f_spec = pl.BlockSpec((None, kb, LANES, cb), lambda ki, bi: (order, ki, 0, 0))
    t_spec = pl.BlockSpec((kb, LANES), lambda ki, bi: (ki, 0))
    shape = jax.ShapeDtypeStruct(are.shape, BF16)
    return pl.pallas_call(
        functools.partial(_mid_kernel, kb=kb),
        grid=(n1h // kb, b),
        in_specs=[
            _const_spec((2 * LANES, 2 * LANES)),
            _const_spec((2 * LANES, 2 * LANES)),
            t_spec, t_spec,
            f_spec, f_spec,
            a_spec, a_spec,
        ],
        out_specs=[a_spec, a_spec],
        out_shape=[shape, shape],
        scratch_shapes=[pltpu.VMEM((kb, 2 * LANES, cb), F32)] if b > 1 else [],
        compiler_params=_cparams("parallel", "arbitrary"),
        name="dft_stage_c",
    )(fblk, fblk_inv, twr, twi, afre, afim, are, aim)


def _inva_kernel(gi_ref, bre_ref, bim_ref, xg_ref, v_ref, sb_ref, s_ref, *rest, n1h, tj, fuse_next):
    gim = gi_ref[...].astype(BF16)
    inv_l1 = 1.0 / s_ref[...]
    if fuse_next:
        g_ref, twr_ref, twi_ref, z_ref, are_ref, aim_ref = rest
        gm = g_ref[...].astype(BF16)
    else:
        (z_ref,) = rest

    def conv(sl):
        rhs = jnp.concatenate([_tiles_to_rows(bre_ref[:, :, sl]), _tiles_to_rows(bim_ref[:, :, sl])], axis=0)
        return _dot(gim, rhs)

    def gate(sl, y):
        z = xg_ref[:, :, sl].astype(F32) * (_rows_to_tiles(y, tj) * inv_l1[:, sl]
                                            + sb_ref[:, sl] * v_ref[:, :, sl].astype(F32))
        z = z.astype(BF16)
        z_ref[:, :, sl] = z
        return z

    s0, s1 = _CH_SPLITS
    y0 = conv(s0)
    y1 = conv(s1)
    z0 = gate(s0, y0)
    if fuse_next:
        r0 = _dot(gm, _tiles_to_rows(z0))
    z1 = gate(s1, y1)
    if fuse_next:
        _store_twiddled(r0, twr_ref, twi_ref, are_ref, aim_ref, s0, n1h, tj)
        r1 = _dot(gm, _tiles_to_rows(z1))
        _store_twiddled(r1, twr_ref, twi_ref, are_ref, aim_ref, s1, n1h, tj)


def _inv_a(bre, bim, g_inv, xg, v, sb_row, asum, order, tj, nxt=None):
    b, n1h, _, cb = bre.shape
    blk = pl.BlockSpec((None, n1h, tj, cb), lambda bi, ji: (bi, 0, ji, 0))
    in_specs = [_const_spec((n1h, 2 * n1h)), blk, blk, blk, blk, _const_spec((1, cb)),
                pl.BlockSpec((1, cb), lambda bi, ji: (0, order))]
    args = [g_inv, bre, bim, xg, v, sb_row, asum]
    out_specs = [blk]
    out_shape = [jax.ShapeDtypeStruct(bre.shape, BF16)]
    if nxt is not None:
        t_spec = pl.BlockSpec((None, n1h, tj), lambda bi, ji: (ji, 0, 0))
        in_specs += [_const_spec((2 * n1h, n1h)), t_spec, t_spec]
        args += list(nxt)
        out_specs += [blk, blk]
        out_shape += [jax.ShapeDtypeStruct(bre.shape, BF16)] * 2
    return pl.pallas_call(
        functools.partial(_inva_kernel, n1h=n1h, tj=tj, fuse_next=nxt is not None),
        grid=(b, LANES // tj),
        in_specs=in_specs,
        out_specs=out_specs,
        out_shape=out_shape,
        compiler_params=_cparams("parallel", "parallel"),
        name="idft_stage_a_gate",
    )(*args)


def _attn_kernel(sink_ref, q_ref, kp_ref, kc_ref, kn_ref, vp_ref, vc_ref, vn_ref, o_ref, *, nblk, sub):
    i = pl.program_id(1)
    blk = ATTN_BLOCK
    grp = N_HEADS // N_KV_HEADS
    rows = grp * blk
    win = 3 * blk
    qi = lax.broadcasted_iota(jnp.int32, (rows, win), 0) % blk
    si = lax.broadcasted_iota(jnp.int32, (rows, win), 1)
    mid = (si >= blk) & (si < 2 * blk)
    tri_prev = (si < blk) & (si >= qi)
    tri_next = (si >= 2 * blk) & (si - 2 * blk <= qi)
    hrow = lax.broadcasted_iota(jnp.int32, (rows, 1), 0) // blk
    kcat = jnp.concatenate([kp_ref[...], kc_ref[...], kn_ref[...]], axis=0)
    vcat = jnp.concatenate([vp_ref[...], vc_ref[...], vn_ref[...]], axis=0)
    lane_k = lax.broadcasted_iota(jnp.int32, (win, N_KV_HEADS * LANES), 1)
    lane_o = lax.broadcasted_iota(jnp.int32, (rows, LANES), 1)
    zero_k = jnp.zeros((win, N_KV_HEADS * LANES), BF16)
    for s in range(sub):
        j = sub * i + s
        valid = mid | (tri_prev & (j > 0)) | (tri_next & (j < nblk - 1))
        kw = kcat[s * blk:s * blk + win]
        vw = vcat[s * blk:s * blk + win]
        kd = jnp.concatenate([jnp.where(lane_k // LANES == g, kw, zero_k) for g in range(N_KV_HEADS)], axis=0)
        vd = jnp.concatenate([vw[:, :LANES], pltpu.roll(vw[:, LANES:].astype(F32), HEAD_DIM, 1).astype(BF16)], axis=0)
        lhs = jnp.concatenate([q_ref[s * blk:(s + 1) * blk, t * N_KV_HEADS * LANES:(t + 1) * N_KV_HEADS * LANES]
                               for t in range(grp)], axis=0)
        sc_all = lax.dot_general(lhs, kd, (((1,), (1,)), ((), ())), preferred_element_type=F32)
        es, dens = [], []
        for g in range(N_KV_HEADS):
            sc = jnp.where(valid, sc_all[:, g * win:(g + 1) * win], NEG_INF)
            sk = jnp.full((rows, 1), sink_ref[grp * g], F32)
            for t in range(1, grp):
                sk = jnp.where(hrow == t, sink_ref[grp * g + t], sk)
            sk = sk * LOG2E
            m = jnp.maximum(jnp.max(sc, axis=1, keepdims=True), sk)
            e = jnp.exp2(sc - m)
            dens.append(jnp.sum(e, axis=1, keepdims=True) + jnp.exp2(sk - m))
            es.append(e.astype(BF16))
        den = jnp.where(lane_o < HEAD_DIM, dens[0], dens[1])
        r = _dot(jnp.concatenate(es, axis=1), vd) / den
        o_ref[s * blk:(s + 1) * blk, :] = jnp.concatenate(
            [r[t * blk:(t + 1) * blk] for t in range(grp)], axis=1).astype(BF16)


def _attention(q3, k3, v3, sink):
    b, seq_len, qw = q3.shape
    blk = ATTN_BLOCK
    sub = ATTN_SUB
    nblk = seq_len // blk
    wide = k3.shape[-1]
    prev = lambda bi, i: (bi, jnp.maximum(sub * i - 1, 0), 0)
    cur = lambda bi, i: (bi, i, 0)
    nxt = lambda bi, i: (bi, jnp.minimum(sub * (i + 1), nblk - 1), 0)
    edge = lambda f: pl.BlockSpec((None, blk, wide), f)
    body = pl.BlockSpec((None, sub * blk, wide), cur)
    return pl.pallas_call(
        functools.partial(_attn_kernel, nblk=nblk, sub=sub),
        grid=(b, nblk // sub),
        in_specs=[pl.BlockSpec(memory_space=pltpu.SMEM),
                  pl.BlockSpec((None, sub * blk, qw), cur),
                  edge(prev), body, edge(nxt), edge(prev), body, edge(nxt)],
        out_specs=pl.BlockSpec((None, sub * blk, ATTN_WIDTH), cur),
        out_shape=jax.ShapeDtypeStruct((b, seq_len, ATTN_WIDTH), BF16),
        compiler_params=_cparams("parallel", "parallel"),
        name="banded_attention",
    )(sink, q3, k3, k3, k3, v3, v3, v3)


def _final_kernel(x_ref, yh_ref, ya_ref, g_ref, why_ref, wat_ref, wo_ref, nw_ref, wg_ref, wu_ref, wd_ref,
                  o_ref):
    a = _dot(yh_ref[...], why_ref[...])
    b = _dot(ya_ref[...], wat_ref[...])
    merged = g_ref[:, :D_MODEL] * a + g_ref[:, D_MODEL:] * b
    x1 = x_ref[...] + _dot(merged.astype(BF16), wo_ref[...])
    ms = jnp.mean(x1 * x1, axis=-1, keepdims=True)
    f = (x1 * lax.rsqrt(ms + RMS_EPS) * nw_ref[...]).astype(BF16)
    acc = x1
    for lo, hi in FFN_CHUNKS:
        gt = _dot(f, wg_ref[:, lo:hi])
        upv = _dot(f, wu_ref[:, lo:hi])
        hid = gt * _sigmoid(gt) * upv
        acc = acc + _dot(hid.astype(BF16), wd_ref[lo:hi, :])
    o_ref[...] = acc


def _final(xf, yh, ya, g, why, wat, wo, nw, wg, wu, wd):
    n = xf.shape[0]
    tm = TM_FINAL
    row = lambda i: (i, 0)
    return pl.pallas_call(
        _final_kernel,
        grid=(n // tm,),
        in_specs=[
            pl.BlockSpec((tm, D_MODEL), row),
            pl.BlockSpec((tm, HYENA_WIDTH), row),
            pl.BlockSpec((tm, ATTN_WIDTH), row),
            pl.BlockSpec((tm, 2 * D_MODEL), row),
            _const_spec((HYENA_WIDTH, D_MODEL)),
            _const_spec((ATTN_WIDTH, D_MODEL)),
            _const_spec((D_MODEL, D_MODEL)),
            _const_spec((1, D_MODEL)),
            _const_spec((D_MODEL, FFN_HIDDEN)),
            _const_spec((D_MODEL, FFN_HIDDEN)),
            _const_spec((FFN_HIDDEN, D_MODEL)),
        ],
        out_specs=pl.BlockSpec((tm, D_MODEL), row),
        out_shape=jax.ShapeDtypeStruct((n, D_MODEL), F32),
        compiler_params=_cparams("parallel"),
        name="merge_ffn",
    )(xf, yh, ya, g, why, wat, wo, nw, wg, wu, wd)


@functools.lru_cache(maxsize=None)
def _dft_constants(n1h, tj):
    n1_len = 2 * n1h
    m = n1_len * LANES
    k1 = np.arange(n1h, dtype=np.float64)[:, None] + 0.5
    n1 = np.arange(n1_len, dtype=np.float64)[None, :]
    gc = np.exp(-2j * np.pi * k1 * n1 / n1_len)
    g_full = np.concatenate([gc.real, gc.imag], axis=0)
    g_inv = (2.0 / m) * np.concatenate([gc.real[:, :n1h].T, gc.imag[:, :n1h].T], axis=1)
    n2 = np.arange(LANES, dtype=np.float64)
    tw = np.exp(-2j * np.pi * k1 * n2[None, :] / m)
    fc = np.exp(-2j * np.pi * np.outer(n2, n2) / LANES)
    fblk = np.block([[fc.real, -fc.imag], [fc.imag, fc.real]])
    fblk_inv = np.block([[fc.real, fc.imag], [-fc.imag, fc.real]])
    f32 = lambda a: np.ascontiguousarray(a, dtype=np.float32)
    tiled = lambda a: a.reshape(n1h, LANES // tj, tj).transpose(1, 0, 2)
    return dict(g_full=f32(g_full), g_half=f32(g_full[:, :n1h]), g_inv=f32(g_inv), twr=f32(tw.real),
                twi=f32(tw.imag), twr_a=f32(tiled(tw.real)), twi_a=f32(tiled(tw.imag)), fblk=f32(fblk),
                fblk_inv=f32(fblk_inv))


@functools.lru_cache(maxsize=None)
def _filter_features(seq_len):
    n1h = seq_len // LANES
    n2 = np.arange(LANES)[:, None]
    n1 = np.arange(n1h)[None, :]
    slot_f = (LANES * n1 + n2).reshape(-1)
    slot_b = slot_f + seq_len
    neg_lag = slot_b > seq_len
    pos_b = np.where(neg_lag, 2 * seq_len - slot_b, 0)
    bands = np.linspace(1e-4, FILTER_BANDS - 1, FILTER_BANDS)

    def feats(pos, sgn):
        pos = pos.astype(np.float64)
        ang = (2.0 * math.pi / seq_len) * pos[:, None] * bands[None, :]
        pad = np.zeros((pos.shape[0], FILT_PAD - FILTER_EMB - 1))
        return np.concatenate([(pos / (seq_len - 1))[:, None], np.cos(ang), -np.sin(ang), sgn[:, None], pad], axis=-1)

    z = np.concatenate([feats(slot_f, np.ones(slot_f.shape)), feats(pos_b, np.where(neg_lag, -1.0, 0.0))], axis=-1)
    return np.ascontiguousarray(z, dtype=np.float32)


@functools.lru_cache(maxsize=None)
def _rope_tables(seq_len):
    inv = ROPE_THETA ** (-np.arange(0, HEAD_DIM, 2, dtype=np.float64) / HEAD_DIM)
    ang = np.arange(seq_len, dtype=np.float64)[:, None] * inv[None, :]
    cos, sin = np.cos(ang), np.sin(ang)
    zero = np.zeros_like(sin)
    reps = LANES // HEAD_DIM
    f32 = lambda a: np.ascontiguousarray(np.tile(a, (1, reps)), dtype=np.float32)
    return f32(np.concatenate([cos, cos], axis=1)), f32(np.concatenate([-sin, zero], axis=1)), \
        f32(np.concatenate([zero, sin], axis=1))


_LATE_WEIGHTS = ("w_hy_out", "w_at_out", "w_o", "w_gate", "w_up", "w_down")


def _layer(x, p, late_bf=None):
    b, seq_len, _ = x.shape
    n = b * seq_len
    n1h = seq_len // LANES
    xf = x.reshape(n, D_MODEL)
    tj_filt, tj_a, tj_gate, tj_last = STAGE_A_TILES[n1h >= LANES]
    c = {k: jnp.asarray(v) for k, v in _dft_constants(n1h, tj_a).items()}
    tw_a = lambda tj: tuple(jnp.asarray(_dft_constants(n1h, tj)[k]) for k in ("twr_a", "twi_a"))
    kb = SUBLANES
    r4 = lambda a: a.reshape(b, n1h, LANES, HYENA_WIDTH)

    cos, sa, sb = (jnp.asarray(t) for t in _rope_tables(seq_len))
    cast = () if late_bf is not None else tuple(p[k] for k in _LATE_WEIGHTS)
    res = _inproj(xf, seq_len, p["attn_norm_w"], p["w_in"], p["conv_w"], p["conv_b"],
                  p["q_norm_w"], p["k_norm_w"], cos, sa, sb, p["ones_bd"], cast)
    x1, x2, v, q, k4, v4, g = res[:7]
    if late_bf is None:
        late_bf = dict(zip(_LATE_WEIGHTS, res[7:]))

    afre, afim, asum = _filter_stage_a(jnp.asarray(_filter_features(seq_len)), p["filt_w1"], p["filt_b1"],
                                       p["filt_w2"], p["filt_b2"], p["filt_w3f"], p["filt_w3b"], p["filt_freq"],
                                       p["decay"], c["g_full"], *tw_a(tj_filt), n1h, tj_filt)

    are, aim = _fwd_a(r4(v), c["g_half"], *tw_a(tj_a), tj_a)
    bre, bim = _mid(are, aim, c["fblk"], c["fblk_inv"], c["twr"], c["twi"], afre, afim, 0, kb)
    z1, are, aim = _inv_a(bre, bim, c["g_inv"], r4(x1), r4(v), p["hyena_bias"][0:1], asum, 0, tj_gate,
                          nxt=(c["g_half"], *tw_a(tj_gate)))
    bre, bim = _mid(are, aim, c["fblk"], c["fblk_inv"], c["twr"], c["twi"], afre, afim, 1, kb)
    (yh,) = _inv_a(bre, bim, c["g_inv"], r4(x2), z1, p["hyena_bias"][1:2], asum, 1, tj_last)

    ya = _attention(q.reshape(b, seq_len, 2 * ATTN_WIDTH), k4.reshape(b, seq_len, 2 * LANES),
                    v4.reshape(b, seq_len, 2 * LANES), p["attn_sink"])

    w = late_bf
    out = _final(xf, yh.reshape(n, HYENA_WIDTH), ya.reshape(n, ATTN_WIDTH), g, w["w_hy_out"], w["w_at_out"],
                 w["w_o"], p["ffn_norm_w"], w["w_gate"], w["w_up"], w["w_down"])
    return out.reshape(b, seq_len, D_MODEL), late_bf


def _block_diag2(w):
    z = jnp.zeros_like(w)
    return jnp.concatenate([jnp.concatenate([w, z], axis=1), jnp.concatenate([z, w], axis=1)], axis=0)


def kernel(x_prompt, x_sample, attn_norm_w, w_in, hyena_conv_w, hyena_conv_b, filt_w1, filt_b1, filt_w2, filt_b2,
           filt_w3, filt_freq, hyena_bias, q_norm_w, k_norm_w, attn_sink, w_hy_out, w_at_out, w_o, ffn_norm_w,
           w_gate, w_up, w_down):
    cw = HYENA_ORDER * HYENA_WIDTH
    max_decay = math.log(DECAY_TARGET) / DECAY_FAST_PCT
    min_decay = math.log(DECAY_TARGET) / DECAY_SLOW_PCT
    deltas = jnp.abs(jnp.linspace(min_decay, max_decay, HYENA_WIDTH, dtype=F32))
    head = np.arange(ATTN_WIDTH) // HEAD_DIM
    ones_bd = jnp.asarray((head[:, None] == head[None, :]).astype(np.float32) / HEAD_DIM).astype(BF16)
    w3 = filt_w3[0].reshape(FILTER_HIDDEN, HYENA_ORDER, 2, HYENA_WIDTH).transpose(2, 0, 1, 3)
    w3 = w3.reshape(2, FILTER_HIDDEN, cw)
    w3_zero = jnp.zeros((FILTER_HIDDEN, cw), F32)
    twice = lambda a: jnp.tile(a, 2)[None, :]
    p = dict(
        attn_norm_w=attn_norm_w[0][None, :],
        w_in=w_in[0].astype(BF16),
        conv_w=hyena_conv_w[0],
        conv_b=hyena_conv_b[0][None, :],
        filt_w1=_block_diag2(jnp.pad(filt_w1[0], ((0, FILT_PAD - FILTER_EMB), (0, 0)))),
        filt_b1=twice(filt_b1[0]),
        filt_w2=_block_diag2(filt_w2[0]),
        filt_b2=twice(filt_b2[0]),
        filt_w3f=jnp.concatenate([w3[0], w3_zero], axis=0),
        filt_w3b=jnp.concatenate([w3_zero, w3[1]], axis=0),
        filt_freq=twice(filt_freq[0]),
        decay=deltas[None, :],
        hyena_bias=hyena_bias[0],
        q_norm_w=jnp.tile(q_norm_w[0], N_HEADS)[None, :],
        k_norm_w=jnp.tile(k_norm_w[0], N_KV_HEADS)[None, :],
        attn_sink=attn_sink[0],
        ones_bd=ones_bd,
        w_hy_out=w_hy_out[0],
        w_at_out=w_at_out[0].reshape(N_HEADS, HEAD_DIM, D_MODEL)[np.array(ATTN_HEAD_ORDER)].reshape(ATTN_WIDTH, D_MODEL),
        w_o=w_o[0],
        ffn_norm_w=ffn_norm_w[0][None, :],
        w_gate=w_gate[0],
        w_up=w_up[0],
        w_down=w_down[0],
    )
    y_prompt, late_bf = _layer(x_prompt, p)
    y_sample, _ = _layer(x_sample, p, late_bf)
    return (y_prompt, y_sample)
```

```python
import functools
import math

import numpy as np
import jax
import jax.numpy as jnp
from jax import lax
from jax.experimental import pallas as pl
from jax.experimental.pallas import tpu as pltpu

F32 = jnp.float32
BF16 = jnp.bfloat16

D_MODEL = 1024
HYENA_WIDTH = 512
HYENA_ORDER = 2
FILTER_BANDS = 16
FILTER_EMB = 1 + 2 * FILTER_BANDS
FILTER_HIDDEN = 64
DECAY_FAST_PCT = 0.3
DECAY_SLOW_PCT = 1.5
DECAY_TARGET = 1e-2
DECAY_SHIFT = 0.05
N_HEADS = 8
N_KV_HEADS = 2
HEAD_DIM = 64
ATTN_WIDTH = N_HEADS * HEAD_DIM
KV_WIDTH = N_KV_HEADS * HEAD_DIM
WINDOW = 128
ROPE_THETA = 10000.0
FFN_HIDDEN = 2816
RMS_EPS = 1e-6
NEG_INF = -1e30
LOG2E = math.log2(math.e)

C_HY = 3 * HYENA_WIDTH
C_Q = C_HY + ATTN_WIDTH
C_K = C_Q + KV_WIDTH
C_V = C_K + KV_WIDTH
IN_WIDTH = C_V + 2 * D_MODEL

LANES = 128
SUBLANES = 8
ATTN_BLOCK = WINDOW
ATTN_HEAD_ORDER = tuple(g * (N_HEADS // N_KV_HEADS) + t for t in range(N_HEADS // N_KV_HEADS) for g in range(N_KV_HEADS))
ATTN_SUB = 8
VMEM_LIMIT = 56 * 1024 * 1024

TM_INPROJ = 512
TM_FINAL = 512
FFN_CHUNKS = ((0, 1280), (1280, FFN_HIDDEN))
_CH_SPLITS = (slice(0, HYENA_WIDTH // 2), slice(HYENA_WIDTH // 2, HYENA_WIDTH))
STAGE_A_TILES = {False: (128, 128, 128, 128), True: (16, 8, 8, 16)}
FILT_PAD = 64


def _cparams(*sem):
    return pltpu.CompilerParams(dimension_semantics=sem, vmem_limit_bytes=VMEM_LIMIT)


def _const_spec(shape):
    nd = len(shape)
    return pl.BlockSpec(shape, lambda *_: (0,) * nd, pipeline_mode=pl.Buffered(1))


def _dot(a, b):
    return jnp.dot(a, b, preferred_element_type=F32)


def _sigmoid(x):
    return 0.5 * jnp.tanh(0.5 * x) + 0.5


def _split(a):
    hi = a.astype(BF16)
    lo = (a - hi.astype(F32)).astype(BF16)
    return hi, lo


def _dot3(a, w):
    ah, al = _split(a)
    wh, wl = _split(w)
    return _dot(ah, wh) + _dot(al, wh) + _dot(ah, wl)


def _inproj_kernel(*refs, tm, tiles_per_seq, n_cast):
    (x_ref, xp_ref, xn_ref, nw_ref, w_ref, cw_ref, cb_ref, qw_ref, kw_ref, cos_ref, sa_ref, sb_ref,
     ones_ref) = refs[:13]
    cast_in = refs[13:13 + n_cast]
    x1_ref, x2_ref, v_ref, q_ref, k4_ref, v4_ref, g_ref = refs[13 + n_cast:20 + n_cast]
    cast_out = refs[20 + n_cast:20 + 2 * n_cast]
    pad_ref = refs[-1]
    for src_ref, dst_ref in zip(cast_in, cast_out):
        dst_ref[...] = src_ref[...].astype(BF16)
    pos = pl.program_id(0) % tiles_per_seq
    nw = nw_ref[...]

    def norm(xv):
        ms = jnp.mean(xv * xv, axis=-1, keepdims=True)
        return (xv * lax.rsqrt(ms + RMS_EPS) * nw).astype(BF16)

    h_all = norm(jnp.concatenate([x_ref[...], xp_ref[...], xn_ref[...]], axis=0))
    h = h_all[:tm]

    hy_all = _dot(h_all, w_ref[:, :C_HY])
    hy = hy_all[:tm]
    hyh = hy_all[tm:]
    prev_ok = (pos > 0).astype(F32)
    next_ok = (pos < tiles_per_seq - 1).astype(F32)
    pad_ref[0:SUBLANES] = hyh[0:SUBLANES] * prev_ok
    pad_ref[SUBLANES:SUBLANES + tm] = hy
    pad_ref[SUBLANES + tm:2 * SUBLANES + tm] = hyh[SUBLANES:] * next_ok
    up = pad_ref[SUBLANES - 1:SUBLANES - 1 + tm]
    un = pad_ref[SUBLANES + 1:SUBLANES + 1 + tm]
    cw = cw_ref[...]
    uc = cw[0:1] * up + cw[1:2] * hy + cw[2:3] * un + cb_ref[...]
    x1_ref[...] = uc[:, :HYENA_WIDTH].astype(BF16)
    x2_ref[...] = uc[:, HYENA_WIDTH:2 * HYENA_WIDTH].astype(BF16)
    v_ref[...] = uc[:, 2 * HYENA_WIDTH:].astype(BF16)

    def head_norm_rope(t, wrow, ones, reps):
        ms = _dot((t * t).astype(BF16), ones)
        tn = t * lax.rsqrt(ms + RMS_EPS) * wrow
        width = t.shape[1]
        cos = jnp.concatenate([cos_ref[...]] * reps, axis=1)
        sa = jnp.concatenate([sa_ref[...]] * reps, axis=1)
        sb = jnp.concatenate([sb_ref[...]] * reps, axis=1)
        half = HEAD_DIM // 2
        return tn * cos + pltpu.roll(tn, width - half, 1) * sa + pltpu.roll(tn, half, 1) * sb

    qkv = _dot(h, w_ref[:, C_HY:C_V])
    q = qkv[:, :ATTN_WIDTH]
    q = head_norm_rope(q, qw_ref[...], ones_ref[...], ATTN_WIDTH // LANES) * (HEAD_DIM ** -0.5 * LOG2E)
    lead = {}
    for p in range(N_HEADS // 2):
        qp = q[:, p * LANES:(p + 1) * LANES]
        lead[2 * p], lead[2 * p + 1] = qp, pltpu.roll(qp, HEAD_DIM, 1)
    q_ref[...] = jnp.concatenate([lead[h] for h in ATTN_HEAD_ORDER], axis=1).astype(BF16)

    k = qkv[:, ATTN_WIDTH:ATTN_WIDTH + KV_WIDTH]
    k = head_norm_rope(k, kw_ref[...], ones_ref[:KV_WIDTH, :KV_WIDTH], 1)
    v = qkv[:, ATTN_WIDTH + KV_WIDTH:]

    lo_half = lax.broadcasted_iota(jnp.int32, (tm, KV_WIDTH), 1) < HEAD_DIM

    def spread(t):
        zero = jnp.zeros_like(t)
        return jnp.concatenate([jnp.where(lo_half, t, zero), jnp.where(lo_half, pltpu.roll(t, HEAD_DIM, 1), zero)],
                               axis=1).astype(BF16)

    k4_ref[...] = spread(k)
    v4_ref[...] = spread(v)
    g_ref[...] = _sigmoid(_dot(h, w_ref[:, C_V:]))


def _inproj(xf, seq_len, nw, w_in_bf, cw, cb, qw, kw, cos, sa, sb, ones_bd, cast=()):
    n = xf.shape[0]
    tm = TM_INPROJ
    tiles_per_seq = seq_len // tm
    nblk8 = n // SUBLANES
    r8 = tm // SUBLANES
    steps = n // tm
    kern = functools.partial(_inproj_kernel, tm=tm, tiles_per_seq=tiles_per_seq, n_cast=len(cast))
    row = lambda i: (i, 0)
    cast_specs = []
    for w in cast:
        per = 1 if (w.shape[0] // steps) % (2 * SUBLANES) == 0 else 2
        rows_blk = w.shape[0] * per // steps
        cast_specs.append(pl.BlockSpec((rows_blk, w.shape[1]), lambda i, per=per: (i // per, 0)))
    return pl.pallas_call(
        kern,
        grid=(n // tm,),
        in_specs=[
            pl.BlockSpec((tm, D_MODEL), row),
            pl.BlockSpec((SUBLANES, D_MODEL), lambda i: (jnp.maximum(i * r8 - 1, 0), 0)),
            pl.BlockSpec((SUBLANES, D_MODEL), lambda i: (jnp.minimum((i + 1) * r8, nblk8 - 1), 0)),
            _const_spec((1, D_MODEL)),
            _const_spec((D_MODEL, IN_WIDTH)),
            _const_spec((3, C_HY)),
            _const_spec((1, C_HY)),
            _const_spec((1, ATTN_WIDTH)),
            _const_spec((1, KV_WIDTH)),
            pl.BlockSpec((tm, LANES), lambda i: (i % tiles_per_seq, 0)),
            pl.BlockSpec((tm, LANES), lambda i: (i % tiles_per_seq, 0)),
            pl.BlockSpec((tm, LANES), lambda i: (i % tiles_per_seq, 0)),
            _const_spec((ATTN_WIDTH, ATTN_WIDTH)),
        ] + cast_specs,
        out_specs=[
            pl.BlockSpec((tm, HYENA_WIDTH), row),
            pl.BlockSpec((tm, HYENA_WIDTH), row),
            pl.BlockSpec((tm, HYENA_WIDTH), row),
            pl.BlockSpec((tm, 2 * ATTN_WIDTH), row),
            pl.BlockSpec((tm, 2 * LANES), row),
            pl.BlockSpec((tm, 2 * LANES), row),
            pl.BlockSpec((tm, 2 * D_MODEL), row),
        ] + cast_specs,
        out_shape=[
            jax.ShapeDtypeStruct((n, HYENA_WIDTH), BF16),
            jax.ShapeDtypeStruct((n, HYENA_WIDTH), BF16),
            jax.ShapeDtypeStruct((n, HYENA_WIDTH), BF16),
            jax.ShapeDtypeStruct((n, 2 * ATTN_WIDTH), BF16),
            jax.ShapeDtypeStruct((n, 2 * LANES), BF16),
            jax.ShapeDtypeStruct((n, 2 * LANES), BF16),
            jax.ShapeDtypeStruct((n, 2 * D_MODEL), F32),
        ] + [jax.ShapeDtypeStruct(w.shape, BF16) for w in cast],
        scratch_shapes=[pltpu.VMEM((tm + 2 * SUBLANES, C_HY), F32)],
        compiler_params=_cparams("parallel"),
        name="inproj",
    )(xf, xf, xf, nw, w_in_bf, cw, cb, qw, kw, cos, sa, sb, ones_bd, *cast)


def _rows_to_tiles(r, tj):
    return r.reshape(r.shape[0], tj, r.shape[1] // tj)


def _tiles_to_rows(t):
    return t.reshape(t.shape[0], t.shape[1] * t.shape[2])


def _twiddle_tiles(r, twr_ref, twi_ref, n1h, tj):
    c = r.shape[1] // tj
    re, im = [], []
    for j in range(tj):
        ar, ai = r[:n1h, j * c:(j + 1) * c], r[n1h:, j * c:(j + 1) * c]
        tr, ti = twr_ref[:, j:j + 1], twi_ref[:, j:j + 1]
        re.append(ar * tr - ai * ti)
        im.append(ar * ti + ai * tr)
    return (_rows_to_tiles(jnp.concatenate(re, axis=1), tj).astype(BF16),
            _rows_to_tiles(jnp.concatenate(im, axis=1), tj).astype(BF16))


def _store_twiddled(r, twr_ref, twi_ref, are_ref, aim_ref, sl, n1h, tj):
    re, im = _twiddle_tiles(r, twr_ref, twi_ref, n1h, tj)
    are_ref[:, :, sl] = re
    aim_ref[:, :, sl] = im


def _filt_kernel(z_ref, w1_ref, b1_ref, w2_ref, b2_ref, w3f_ref, w3b_ref, fr_ref, dl_ref, g_ref, twr_ref, twi_ref,
                 are_ref, aim_ref, s_ref, *, n1h, tj):
    i = pl.program_id(0)
    z = z_ref[...]
    fr = fr_ref[...]
    h = jnp.sin(fr * (_dot3(z, w1_ref[...]) + b1_ref[...]))
    h = jnp.sin(fr * (_dot3(h, w2_ref[...]) + b2_ref[...]))
    hb = h.astype(BF16)
    tf = z[:, 0:1]
    tb = z[:, FILT_PAD:FILT_PAD + 1]
    sgn = z[:, FILT_PAD + FILTER_EMB:FILT_PAD + FILTER_EMB + 1]
    gm = g_ref[...].astype(BF16)
    cb = HYENA_WIDTH
    parts = []
    for o in range(HYENA_ORDER):
        sl = slice(o * cb, (o + 1) * cb)
        dl = dl_ref[...]

        def taps(w_ref, t):
            return _dot(hb, w_ref[:, sl].astype(BF16)) * (jnp.exp(-t * dl) + DECAY_SHIFT)

        kf = taps(w3f_ref, tf)
        kb = taps(w3b_ref, tb) * sgn
        parts.append(jnp.sum(jnp.abs(kf), axis=0, keepdims=True) + jnp.sum(jnp.abs(kb), axis=0, keepdims=True))
        cols = [jnp.concatenate([kf[j * n1h:(j + 1) * n1h], kb[j * n1h:(j + 1) * n1h]], axis=0).astype(BF16)
                for j in range(tj)]
        re, im = _twiddle_tiles(_dot(gm, jnp.concatenate(cols, axis=1)), twr_ref, twi_ref, n1h, tj)
        are_ref[o] = re
        aim_ref[o] = im
    part = jnp.concatenate(parts, axis=1)

    @pl.when(i == 0)
    def _():
        s_ref[...] = part

    @pl.when(i > 0)
    def _():
        s_ref[...] += part


def _filter_stage_a(zf, w1bd, b1, w2bd, b2, w3f, w3b, fr, dl, g_full, twr, twi, n1h, tj):
    cw = HYENA_ORDER * HYENA_WIDTH
    hid2 = 2 * FILTER_HIDDEN
    a_shape = jax.ShapeDtypeStruct((HYENA_ORDER, n1h, LANES, HYENA_WIDTH), BF16)
    a_spec = pl.BlockSpec((HYENA_ORDER, n1h, tj, HYENA_WIDTH), lambda i: (0, 0, i, 0))
    t_spec = pl.BlockSpec((None, n1h, tj), lambda i: (i, 0, 0))
    return pl.pallas_call(
        functools.partial(_filt_kernel, n1h=n1h, tj=tj),
        grid=(LANES // tj,),
        in_specs=[
            pl.BlockSpec((tj * n1h, 2 * FILT_PAD), lambda i: (i, 0)),
            _const_spec((2 * FILT_PAD, hid2)),
            _const_spec((1, hid2)),
            _const_spec((hid2, hid2)),
            _const_spec((1, hid2)),
            _const_spec((hid2, cw)),
            _const_spec((hid2, cw)),
            _const_spec((1, hid2)),
            _const_spec((1, HYENA_WIDTH)),
            _const_spec((2 * n1h, 2 * n1h)),
            t_spec, t_spec,
        ],
        out_specs=[a_spec, a_spec, pl.BlockSpec((1, cw), lambda i: (0, 0))],
        out_shape=[a_shape, a_shape, jax.ShapeDtypeStruct((1, cw), F32)],
        compiler_params=_cparams("arbitrary"),
        name="filter_taps_stage_a",
    )(zf, w1bd, b1, w2bd, b2, w3f, w3b, fr, dl, g_full, twr, twi)


def _fwda_kernel(g_ref, twr_ref, twi_ref, u_ref, are_ref, aim_ref, *, n1h, tj):
    gm = g_ref[...].astype(BF16)
    prev = None
    for sl in _CH_SPLITS:
        u = _tiles_to_rows(u_ref[:, :, sl])
        if prev is not None:
            _store_twiddled(prev[1], twr_ref, twi_ref, are_ref, aim_ref, prev[0], n1h, tj)
        prev = (sl, _dot(gm, u))
    _store_twiddled(prev[1], twr_ref, twi_ref, are_ref, aim_ref, prev[0], n1h, tj)


def _fwd_a(u4, g_half, twr, twi, tj):
    b, n1h, _, cb = u4.shape
    blk = pl.BlockSpec((None, n1h, tj, cb), lambda bi, ji: (bi, 0, ji, 0))
    t_spec = pl.BlockSpec((None, n1h, tj), lambda bi, ji: (ji, 0, 0))
    a_shape = jax.ShapeDtypeStruct(u4.shape, BF16)
    return pl.pallas_call(
        functools.partial(_fwda_kernel, n1h=n1h, tj=tj),
        grid=(b, LANES // tj),
        in_specs=[_const_spec((2 * n1h, n1h)), t_spec, t_spec, blk],
        out_specs=[blk, blk],
        out_shape=[a_shape, a_shape],
        compiler_params=_cparams("parallel", "parallel"),
        name="dft_stage_a",
    )(g_half, twr, twi, u4)


def _mid_kernel(f_ref, fi_ref, twr_ref, twi_ref, fre_ref, fim_ref, are_ref, aim_ref, bre_ref, bim_ref, *scratch,
                kb):
    fm = f_ref[...].astype(BF16)
    fim = fi_ref[...].astype(BF16)
    fill = jnp.zeros((LANES - kb, LANES), F32)
    twr_t = jnp.concatenate([twr_ref[...], fill], axis=0).T
    twi_t = jnp.concatenate([twi_ref[...], fill], axis=0).T

    def filt(k):
        return _dot(fm, jnp.concatenate([fre_ref[k], fim_ref[k]], axis=0))

    if scratch:
        (h_ref,) = scratch

        @pl.when(pl.program_id(1) == 0)
        def _():
            for k in range(kb):
                h_ref[k] = filt(k)

    def fwd(k):
        return (_dot(fm, jnp.concatenate([are_ref[k], aim_ref[k]], axis=0)), h_ref[k] if scratch else filt(k))

    def spec(k, xh):
        x, h = xh
        xr, xi = x[:LANES], x[LANES:]
        hr, hi = h[:LANES], h[LANES:]
        return jnp.concatenate([xr * hr - xi * hi, xr * hi + xi * hr], axis=0).astype(BF16)

    def out(k, bc):
        br, bi = bc[:LANES], bc[LANES:]
        tr, ti = twr_t[:, k:k + 1], twi_t[:, k:k + 1]
        bre_ref[k] = (br * tr + bi * ti).astype(BF16)
        bim_ref[k] = (bi * tr - br * ti).astype(BF16)

    x = fwd(0)
    bc_prev = None
    for k in range(kb):
        y = spec(k, x)
        if k + 1 < kb:
            x = fwd(k + 1)
        if bc_prev is not None:
            out(k - 1, bc_prev)
        bc_prev = _dot(fim, y)
    out(kb - 1, bc_prev)


def _mid(are, aim, fblk, fblk_inv, twr, twi, afre, afim, order, kb):
    b, n1h, _, cb = are.shape
    a_spec = pl.BlockSpec((None, kb, LANES, cb), lambda ki, bi: (bi, ki, 0, 0))
    f_spec = pl.BlockSpec((None, kb, LANES, cb), lambda ki, bi: (order, ki, 0, 0))
    t_spec = pl.BlockSpec((kb, LANES), lambda ki, bi: (ki, 0))
    shape = jax.ShapeDtypeStruct(are.shape, BF16)
    return pl.pallas_call(
        functools.partial(_mid_kernel, kb=kb),
        grid=(n1h // kb, b),
        in_specs=[
            _const_spec((2 * LANES, 2 * LANES)),
            _const_spec((2 * LANES, 2 * LANES)),
            t_spec, t_spec,
            f_spec, f_spec,
            a_spec, a_spec,
        ],
        out_specs=[a_spec, a_spec],
        out_shape=[shape, shape],
        scratch_shapes=[pltpu.VMEM((kb, 2 * LANES, cb), F32)] if b > 1 else [],
        compiler_params=_cparams("parallel", "arbitrary"),
        name="dft_stage_c",
    )(fblk, fblk_inv, twr, twi, afre, afim, are, aim)


def _inva_kernel(gi_ref, bre_ref, bim_ref, xg_ref, v_ref, sb_ref, s_ref, *rest, n1h, tj, fuse_next):
    gim = gi_ref[...].astype(BF16)
    inv_l1 = 1.0 / s_ref[...]
    if fuse_next:
        g_ref, twr_ref, twi_ref, z_ref, are_ref, aim_ref = rest
        gm = g_ref[...].astype(BF16)
    else:
        (z_ref,) = rest

    def conv(sl):
        rhs = jnp.concatenate([_tiles_to_rows(bre_ref[:, :, sl]), _tiles_to_rows(bim_ref[:, :, sl])], axis=0)
        return _dot(gim, rhs)

    def gate(sl, y):
        z = xg_ref[:, :, sl].astype(F32) * (_rows_to_tiles(y, tj) * inv_l1[:, sl]
                                            + sb_ref[:, sl] * v_ref[:, :, sl].astype(F32))
        z = z.astype(BF16)
        z_ref[:, :, sl] = z
        return z

    s0, s1 = _CH_SPLITS
    y0 = conv(s0)
    y1 = conv(s1)
    z0 = gate(s0, y0)
    if fuse_next:
        r0 = _dot(gm, _tiles_to_rows(z0))
    z1 = gate(s1, y1)
    if fuse_next:
        _store_twiddled(r0, twr_ref, twi_ref, are_ref, aim_ref, s0, n1h, tj)
        r1 = _dot(gm, _tiles_to_rows(z1))
        _store_twiddled(r1, twr_ref, twi_ref, are_ref, aim_ref, s1, n1h, tj)


def _inv_a(bre, bim, g_inv, xg, v, sb_row, asum, order, tj, nxt=None):
    b, n1h, _, cb = bre.shape
    blk = pl.BlockSpec((None, n1h, tj, cb), lambda bi, ji: (bi, 0, ji, 0))
    in_specs = [_const_spec((n1h, 2 * n1h)), blk, blk, blk, blk, _const_spec((1, cb)),
                pl.BlockSpec((1, cb), lambda bi, ji: (0, order))]
    args = [g_inv, bre, bim, xg, v, sb_row, asum]
    out_specs = [blk]
    out_shape = [jax.ShapeDtypeStruct(bre.shape, BF16)]
    if nxt is not None:
        t_spec = pl.BlockSpec((None, n1h, tj), lambda bi, ji: (ji, 0, 0))
        in_specs += [_const_spec((2 * n1h, n1h)), t_spec, t_spec]
        args += list(nxt)
        out_specs += [blk, blk]
        out_shape += [jax.ShapeDtypeStruct(bre.shape, BF16)] * 2
    return pl.pallas_call(
        functools.partial(_inva_kernel, n1h=n1h, tj=tj, fuse_next=nxt is not None),
        grid=(b, LANES // tj),
        in_specs=in_specs,
        out_specs=out_specs,
        out_shape=out_shape,
        compiler_params=_cparams("parallel", "parallel"),
        name="idft_stage_a_gate",
    )(*args)


def _attn_kernel(sink_ref, q_ref, kp_ref, kc_ref, kn_ref, vp_ref, vc_ref, vn_ref, o_ref, *, nblk, sub):
    i = pl.program_id(1)
    blk = ATTN_BLOCK
    grp = N_HEADS // N_KV_HEADS
    rows = grp * blk
    win = 3 * blk
    qi = lax.broadcasted_iota(jnp.int32, (rows, win), 0) % blk
    si = lax.broadcasted_iota(jnp.int32, (rows, win), 1)
    mid = (si >= blk) & (si < 2 * blk)
    tri_prev = (si < blk) & (si >= qi)
    tri_next = (si >= 2 * blk) & (si - 2 * blk <= qi)
    hrow = lax.broadcasted_iota(jnp.int32, (rows, 1), 0) // blk
    kcat = jnp.concatenate([kp_ref[...], kc_ref[...], kn_ref[...]], axis=0)
    vcat = jnp.concatenate([vp_ref[...], vc_ref[...], vn_ref[...]], axis=0)
    lane_k = lax.broadcasted_iota(jnp.int32, (win, N_KV_HEADS * LANES), 1)
    lane_o = lax.broadcasted_iota(jnp.int32, (rows, LANES), 1)
    zero_k = jnp.zeros((win, N_KV_HEADS * LANES), BF16)
    for s in range(sub):
        j = sub * i + s
        valid = mid | (tri_prev & (j > 0)) | (tri_next & (j < nblk - 1))
        kw = kcat[s * blk:s * blk + win]
        vw = vcat[s * blk:s * blk + win]
        kd = jnp.concatenate([jnp.where(lane_k // LANES == g, kw, zero_k) for g in range(N_KV_HEADS)], axis=0)
        vd = jnp.concatenate([vw[:, :LANES], pltpu.roll(vw[:, LANES:].astype(F32), HEAD_DIM, 1).astype(BF16)], axis=0)
        lhs = jnp.concatenate([q_ref[s * blk:(s + 1) * blk, t * N_KV_HEADS * LANES:(t + 1) * N_KV_HEADS * LANES]
                               for t in range(grp)], axis=0)
        sc_all = lax.dot_general(lhs, kd, (((1,), (1,)), ((), ())), preferred_element_type=F32)
        es, dens = [], []
        for g in range(N_KV_HEADS):
            sc = jnp.where(valid, sc_all[:, g * win:(g + 1) * win], NEG_INF)
            sk = jnp.full((rows, 1), sink_ref[grp * g], F32)
            for t in range(1, grp):
                sk = jnp.where(hrow == t, sink_ref[grp * g + t], sk)
            sk = sk * LOG2E
            m = jnp.maximum(jnp.max(sc, axis=1, keepdims=True), sk)
            e = jnp.exp2(sc - m)
            dens.append(jnp.sum(e, axis=1, keepdims=True) + jnp.exp2(sk - m))
            es.append(e.astype(BF16))
        den = jnp.where(lane_o < HEAD_DIM, dens[0], dens[1])
        r = _dot(jnp.concatenate(es, axis=1), vd) / den
        o_ref[s * blk:(s + 1) * blk, :] = jnp.concatenate(
            [r[t * blk:(t + 1) * blk] for t in range(grp)], axis=1).astype(BF16)


def _attention(q3, k3, v3, sink):
    b, seq_len, qw = q3.shape
    blk = ATTN_BLOCK
    sub = ATTN_SUB
    nblk = seq_len // blk
    wide = k3.shape[-1]
    prev = lambda bi, i: (bi, jnp.maximum(sub * i - 1, 0), 0)
    cur = lambda bi, i: (bi, i, 0)
    nxt = lambda bi, i: (bi, jnp.minimum(sub * (i + 1), nblk - 1), 0)
    edge = lambda f: pl.BlockSpec((None, blk, wide), f)
    body = pl.BlockSpec((None, sub * blk, wide), cur)
    return pl.pallas_call(
        functools.partial(_attn_kernel, nblk=nblk, sub=sub),
        grid=(b, nblk // sub),
        in_specs=[pl.BlockSpec(memory_space=pltpu.SMEM),
                  pl.BlockSpec((None, sub * blk, qw), cur),
                  edge(prev), body, edge(nxt), edge(prev), body, edge(nxt)],
        out_specs=pl.BlockSpec((None, sub * blk, ATTN_WIDTH), cur),
        out_shape=jax.ShapeDtypeStruct((b, seq_len, ATTN_WIDTH), BF16),
        compiler_params=_cparams("parallel", "parallel"),
        name="banded_attention",
    )(sink, q3, k3, k3, k3, v3, v3, v3)


def _final_kernel(x_ref, yh_ref, ya_ref, g_ref, why_ref, wat_ref, wo_ref, nw_ref, wg_ref, wu_ref, wd_ref,
                  o_ref):
    a = _dot(yh_ref[...], why_ref[...])
    b = _dot(ya_ref[...], wat_ref[...])
    merged = g_ref[:, :D_MODEL] * a + g_ref[:, D_MODEL:] * b
    x1 = x_ref[...] + _dot(merged.astype(BF16), wo_ref[...])
    ms = jnp.mean(x1 * x1, axis=-1, keepdims=True)
    f = (x1 * lax.rsqrt(ms + RMS_EPS) * nw_ref[...]).astype(BF16)
    acc = x1
    for lo, hi in FFN_CHUNKS:
        gt = _dot(f, wg_ref[:, lo:hi])
        upv = _dot(f, wu_ref[:, lo:hi])
        hid = gt * _sigmoid(gt) * upv
        acc = acc + _dot(hid.astype(BF16), wd_ref[lo:hi, :])
    o_ref[...] = acc


def _final(xf, yh, ya, g, why, wat, wo, nw, wg, wu, wd):
    n = xf.shape[0]
    tm = TM_FINAL
    row = lambda i: (i, 0)
    return pl.pallas_call(
        _final_kernel,
        grid=(n // tm,),
        in_specs=[
            pl.BlockSpec((tm, D_MODEL), row),
            pl.BlockSpec((tm, HYENA_WIDTH), row),
            pl.BlockSpec((tm, ATTN_WIDTH), row),
            pl.BlockSpec((tm, 2 * D_MODEL), row),
            _const_spec((HYENA_WIDTH, D_MODEL)),
            _const_spec((ATTN_WIDTH, D_MODEL)),
            _const_spec((D_MODEL, D_MODEL)),
            _const_spec((1, D_MODEL)),
            _const_spec((D_MODEL, FFN_HIDDEN)),
            _const_spec((D_MODEL, FFN_HIDDEN)),
            _const_spec((FFN_HIDDEN, D_MODEL)),
        ],
        out_specs=pl.BlockSpec((tm, D_MODEL), row),
        out_shape=jax.ShapeDtypeStruct((n, D_MODEL), F32),
        compiler_params=_cparams("parallel"),
        name="merge_ffn",
    )(xf, yh, ya, g, why, wat, wo, nw, wg, wu, wd)


@functools.lru_cache(maxsize=None)
def _dft_constants(n1h, tj):
    n1_len = 2 * n1h
    m = n1_len * LANES
    k1 = np.arange(n1h, dtype=np.float64)[:, None] + 0.5
    n1 = np.arange(n1_len, dtype=np.float64)[None, :]
    gc = np.exp(-2j * np.pi * k1 * n1 / n1_len)
    g_full = np.concatenate([gc.real, gc.imag], axis=0)
    g_inv = (2.0 / m) * np.concatenate([gc.real[:, :n1h].T, gc.imag[:, :n1h].T], axis=1)
    n2 = np.arange(LANES, dtype=np.float64)
    tw = np.exp(-2j * np.pi * k1 * n2[None, :] / m)
    fc = np.exp(-2j * np.pi * np.outer(n2, n2) / LANES)
    fblk = np.block([[fc.real, -fc.imag], [fc.imag, fc.real]])
    fblk_inv = np.block([[fc.real, fc.imag], [-fc.imag, fc.real]])
    f32 = lambda a: np.ascontiguousarray(a, dtype=np.float32)
    tiled = lambda a: a.reshape(n1h, LANES // tj, tj).transpose(1, 0, 2)
    return dict(g_full=f32(g_full), g_half=f32(g_full[:, :n1h]), g_inv=f32(g_inv), twr=f32(tw.real),
                twi=f32(tw.imag), twr_a=f32(tiled(tw.real)), twi_a=f32(tiled(tw.imag)), fblk=f32(fblk),
                fblk_inv=f32(fblk_inv))


@functools.lru_cache(maxsize=None)
def _filter_features(seq_len):
    n1h = seq_len // LANES
    n2 = np.arange(LANES)[:, None]
    n1 = np.arange(n1h)[None, :]
    slot_f = (LANES * n1 + n2).reshape(-1)
    slot_b = slot_f + seq_len
    neg_lag = slot_b > seq_len
    pos_b = np.where(neg_lag, 2 * seq_len - slot_b, 0)
    bands = np.linspace(1e-4, FILTER_BANDS - 1, FILTER_BANDS)

    def feats(pos, sgn):
        pos = pos.astype(np.float64)
        ang = (2.0 * math.pi / seq_len) * pos[:, None] * bands[None, :]
        pad = np.zeros((pos.shape[0], FILT_PAD - FILTER_EMB - 1))
        return np.concatenate([(pos / (seq_len - 1))[:, None], np.cos(ang), -np.sin(ang), sgn[:, None], pad], axis=-1)

    z = np.concatenate([feats(slot_f, np.ones(slot_f.shape)), feats(pos_b, np.where(neg_lag, -1.0, 0.0))], axis=-1)
    return np.ascontiguousarray(z, dtype=np.float32)


@functools.lru_cache(maxsize=None)
def _rope_tables(seq_len):
    inv = ROPE_THETA ** (-np.arange(0, HEAD_DIM, 2, dtype=np.float64) / HEAD_DIM)
    ang = np.arange(seq_len, dtype=np.float64)[:, None] * inv[None, :]
    cos, sin = np.cos(ang), np.sin(ang)
    zero = np.zeros_like(sin)
    reps = LANES // HEAD_DIM
    f32 = lambda a: np.ascontiguousarray(np.tile(a, (1, reps)), dtype=np.float32)
    return f32(np.concatenate([cos, cos], axis=1)), f32(np.concatenate([-sin, zero], axis=1)), \
        f32(np.concatenate([zero, sin], axis=1))


_LATE_WEIGHTS = ("w_hy_out", "w_at_out", "w_o", "w_gate", "w_up", "w_down")


def _layer(x, p, late_bf=None):
    b, seq_len, _ = x.shape
    n = b * seq_len
    n1h = seq_len // LANES
    xf = x.reshape(n, D_MODEL)
    tj_filt, tj_a, tj_gate, tj_last = STAGE_A_TILES[n1h >= LANES]
    c = {k: jnp.asarray(v) for k, v in _dft_constants(n1h, tj_a).items()}
    tw_a = lambda tj: tuple(jnp.asarray(_dft_constants(n1h, tj)[k]) for k in ("twr_a", "twi_a"))
    kb = 2 * SUBLANES
    r4 = lambda a: a.reshape(b, n1h, LANES, HYENA_WIDTH)

    cos, sa, sb = (jnp.asarray(t) for t in _rope_tables(seq_len))
    cast = () if late_bf is not None else tuple(p[k] for k in _LATE_WEIGHTS)
    res = _inproj(xf, seq_len, p["attn_norm_w"], p["w_in"], p["conv_w"], p["conv_b"],
                  p["q_norm_w"], p["k_norm_w"], cos, sa, sb, p["ones_bd"], cast)
    x1, x2, v, q, k4, v4, g = res[:7]
    if late_bf is None:
        late_bf = dict(zip(_LATE_WEIGHTS, res[7:]))

    afre, afim, asum = _filter_stage_a(jnp.asarray(_filter_features(seq_len)), p["filt_w1"], p["filt_b1"],
                                       p["filt_w2"], p["filt_b2"], p["filt_w3f"], p["filt_w3b"], p["filt_freq"],
                                       p["decay"], c["g_full"], *tw_a(tj_filt), n1h, tj_filt)

    are, aim = _fwd_a(r4(v), c["g_half"], *tw_a(tj_a), tj_a)
    bre, bim = _mid(are, aim, c["fblk"], c["fblk_inv"], c["twr"], c["twi"], afre, afim, 0, kb)
    z1, are, aim = _inv_a(bre, bim, c["g_inv"], r4(x1), r4(v), p["hyena_bias"][0:1], asum, 0, tj_gate,
                          nxt=(c["g_half"], *tw_a(tj_gate)))
    bre, bim = _mid(are, aim, c["fblk"], c["fblk_inv"], c["twr"], c["twi"], afre, afim, 1, kb)
    (yh,) = _inv_a(bre, bim, c["g_inv"], r4(x2), z1, p["hyena_bias"][1:2], asum, 1, tj_last)

    ya = _attention(q.reshape(b, seq_len, 2 * ATTN_WIDTH), k4.reshape(b, seq_len, 2 * LANES),
                    v4.reshape(b, seq_len, 2 * LANES), p["attn_sink"])

    w = late_bf
    out = _final(xf, yh.reshape(n, HYENA_WIDTH), ya.reshape(n, ATTN_WIDTH), g, w["w_hy_out"], w["w_at_out"],
                 w["w_o"], p["ffn_norm_w"], w["w_gate"], w["w_up"], w["w_down"])
    return out.reshape(b, seq_len, D_MODEL), late_bf


def _block_diag2(w):
    z = jnp.zeros_like(w)
    return jnp.concatenate([jnp.concatenate([w, z], axis=1), jnp.concatenate([z, w], axis=1)], axis=0)


def kernel(x_prompt, x_sample, attn_norm_w, w_in, hyena_conv_w, hyena_conv_b, filt_w1, filt_b1, filt_w2, filt_b2,
           filt_w3, filt_freq, hyena_bias, q_norm_w, k_norm_w, attn_sink, w_hy_out, w_at_out, w_o, ffn_norm_w,
           w_gate, w_up, w_down):
    cw = HYENA_ORDER * HYENA_WIDTH
    max_decay = math.log(DECAY_TARGET) / DECAY_FAST_PCT
    min_decay = math.log(DECAY_TARGET) / DECAY_SLOW_PCT
    deltas = jnp.abs(jnp.linspace(min_decay, max_decay, HYENA_WIDTH, dtype=F32))
    head = np.arange(ATTN_WIDTH) // HEAD_DIM
    ones_bd = jnp.asarray((head[:, None] == head[None, :]).astype(np.float32) / HEAD_DIM).astype(BF16)
    w3 = filt_w3[0].reshape(FILTER_HIDDEN, HYENA_ORDER, 2, HYENA_WIDTH).transpose(2, 0, 1, 3)
    w3 = w3.reshape(2, FILTER_HIDDEN, cw)
    w3_zero = jnp.zeros((FILTER_HIDDEN, cw), F32)
    twice = lambda a: jnp.tile(a, 2)[None, :]
    p = dict(
        attn_norm_w=attn_norm_w[0][None, :],
        w_in=w_in[0].astype(BF16),
        conv_w=hyena_conv_w[0],
        conv_b=hyena_conv_b[0][None, :],
        filt_w1=_block_diag2(jnp.pad(filt_w1[0], ((0, FILT_PAD - FILTER_EMB), (0, 0)))),
        filt_b1=twice(filt_b1[0]),
        filt_w2=_block_diag2(filt_w2[0]),
        filt_b2=twice(filt_b2[0]),
        filt_w3f=jnp.concatenate([w3[0], w3_zero], axis=0),
        filt_w3b=jnp.concatenate([w3_zero, w3[1]], axis=0),
        filt_freq=twice(filt_freq[0]),
        decay=deltas[None, :],
        hyena_bias=hyena_bias[0],
        q_norm_w=jnp.tile(q_norm_w[0], N_HEADS)[None, :],
        k_norm_w=jnp.tile(k_norm_w[0], N_KV_HEADS)[None, :],
        attn_sink=attn_sink[0],
        ones_bd=ones_bd,
        w_hy_out=w_hy_out[0],
        w_at_out=w_at_out[0].reshape(N_HEADS, HEAD_DIM, D_MODEL)[np.array(ATTN_HEAD_ORDER)].reshape(ATTN_WIDTH, D_MODEL),
        w_o=w_o[0],
        ffn_norm_w=ffn_norm_w[0][None, :],
        w_gate=w_gate[0],
        w_up=w_up[0],
        w_down=w_down[0],
    )
    y_prompt, late_bf = _layer(x_prompt, p)
    y_sample, _ = _layer(x_sample, p, late_bf)
    return (y_prompt, y_sample)
```

```python
import functools
import math

import numpy as np
import jax
import jax.numpy as jnp
from jax import lax
from jax.experimental import pallas as pl
from jax.experimental.pallas import tpu as pltpu

F32 = jnp.float32
BF16 = jnp.bfloat16

D_MODEL = 1024
HYENA_WIDTH = 512
HYENA_ORDER = 2
FILTER_BANDS = 16
FILTER_EMB = 1 + 2 * FILTER_BANDS
FILTER_HIDDEN = 64
DECAY_FAST_PCT = 0.3
DECAY_SLOW_PCT = 1.5
DECAY_TARGET = 1e-2
DECAY_SHIFT = 0.05
N_HEADS = 8
N_KV_HEADS = 2
HEAD_DIM = 64
ATTN_WIDTH = N_HEADS * HEAD_DIM
KV_WIDTH = N_KV_HEADS * HEAD_DIM
WINDOW = 128
ROPE_THETA = 10000.0
FFN_HIDDEN = 2816
RMS_EPS = 1e-6
NEG_INF = -1e30
LOG2E = math.log2(math.e)

C_HY = 3 * HYENA_WIDTH
C_Q = C_HY + ATTN_WIDTH
C_K = C_Q + KV_WIDTH
C_V = C_K + KV_WIDTH
IN_WIDTH = C_V + 2 * D_MODEL

LANES = 128
SUBLANES = 8
ATTN_BLOCK = WINDOW
ATTN_HEAD_ORDER = tuple(g * (N_HEADS // N_KV_HEADS) + t for t in range(N_HEADS // N_KV_HEADS) for g in range(N_KV_HEADS))
ATTN_SUB = 16
VMEM_LIMIT = 56 * 1024 * 1024

TM_INPROJ = 512
TM_FINAL = 512
FFN_CHUNKS = ((0, 1280), (1280, FFN_HIDDEN))
_CH_SPLITS = (slice(0, HYENA_WIDTH // 2), slice(HYENA_WIDTH // 2, HYENA_WIDTH))
STAGE_A_TILES = {False: (128, 128, 128, 128), True: (16, 8, 8, 16)}
FILT_PAD = 64


def _cparams(*sem):
    return pltpu.CompilerParams(dimension_semantics=sem, vmem_limit_bytes=VMEM_LIMIT)


def _const_spec(shape):
    nd = len(shape)
    return pl.BlockSpec(shape, lambda *_: (0,) * nd, pipeline_mode=pl.Buffered(1))


def _dot(a, b):
    return jnp.dot(a, b, preferred_element_type=F32)


def _sigmoid(x):
    return 0.5 * jnp.tanh(0.5 * x) + 0.5


def _split(a):
    hi = a.astype(BF16)
    lo = (a - hi.astype(F32)).astype(BF16)
    return hi, lo


def _dot3(a, w):
    ah, al = _split(a)
    wh, wl = _split(w)
    return _dot(ah, wh) + _dot(al, wh) + _dot(ah, wl)


def _inproj_kernel(*refs, tm, tiles_per_seq, n_cast):
    (x_ref, xp_ref, xn_ref, nw_ref, w_ref, cw_ref, cb_ref, qw_ref, kw_ref, cos_ref, sa_ref, sb_ref,
     ones_ref) = refs[:13]
    cast_in = refs[13:13 + n_cast]
    x1_ref, x2_ref, v_ref, q_ref, k4_ref, v4_ref, g_ref = refs[13 + n_cast:20 + n_cast]
    cast_out = refs[20 + n_cast:20 + 2 * n_cast]
    pad_ref = refs[-1]
    for src_ref, dst_ref in zip(cast_in, cast_out):
        dst_ref[...] = src_ref[...].astype(BF16)
    pos = pl.program_id(0) % tiles_per_seq
    nw = nw_ref[...]

    def norm(xv):
        ms = jnp.mean(xv * xv, axis=-1, keepdims=True)
        return (xv * lax.rsqrt(ms + RMS_EPS) * nw).astype(BF16)

    h_all = norm(jnp.concatenate([x_ref[...], xp_ref[...], xn_ref[...]], axis=0))
    h = h_all[:tm]

    hy_all = _dot(h_all, w_ref[:, :C_HY])
    hy = hy_all[:tm]
    hyh = hy_all[tm:]
    prev_ok = (pos > 0).astype(F32)
    next_ok = (pos < tiles_per_seq - 1).astype(F32)
    pad_ref[0:SUBLANES] = hyh[0:SUBLANES] * prev_ok
    pad_ref[SUBLANES:SUBLANES + tm] = hy
    pad_ref[SUBLANES + tm:2 * SUBLANES + tm] = hyh[SUBLANES:] * next_ok
    up = pad_ref[SUBLANES - 1:SUBLANES - 1 + tm]
    un = pad_ref[SUBLANES + 1:SUBLANES + 1 + tm]
    cw = cw_ref[...]
    uc = cw[0:1] * up + cw[1:2] * hy + cw[2:3] * un + cb_ref[...]
    x1_ref[...] = uc[:, :HYENA_WIDTH].astype(BF16)
    x2_ref[...] = uc[:, HYENA_WIDTH:2 * HYENA_WIDTH].astype(BF16)
    v_ref[...] = uc[:, 2 * HYENA_WIDTH:].astype(BF16)

    def head_norm_rope(t, wrow, ones, reps):
        ms = _dot((t * t).astype(BF16), ones)
        tn = t * lax.rsqrt(ms + RMS_EPS) * wrow
        width = t.shape[1]
        cos = jnp.concatenate([cos_ref[...]] * reps, axis=1)
        sa = jnp.concatenate([sa_ref[...]] * reps, axis=1)
        sb = jnp.concatenate([sb_ref[...]] * reps, axis=1)
        half = HEAD_DIM // 2
        return tn * cos + pltpu.roll(tn, width - half, 1) * sa + pltpu.roll(tn, half, 1) * sb

    qkv = _dot(h, w_ref[:, C_HY:C_V])
    q = qkv[:, :ATTN_WIDTH]
    q = head_norm_rope(q, qw_ref[...], ones_ref[...], ATTN_WIDTH // LANES) * (HEAD_DIM ** -0.5 * LOG2E)
    lead = {}
    for p in range(N_HEADS // 2):
        qp = q[:, p * LANES:(p + 1) * LANES]
        lead[2 * p], lead[2 * p + 1] = qp, pltpu.roll(qp, HEAD_DIM, 1)
    q_ref[...] = jnp.concatenate([lead[h] for h in ATTN_HEAD_ORDER], axis=1).astype(BF16)

    k = qkv[:, ATTN_WIDTH:ATTN_WIDTH + KV_WIDTH]
    k = head_norm_rope(k, kw_ref[...], ones_ref[:KV_WIDTH, :KV_WIDTH], 1)
    v = qkv[:, ATTN_WIDTH + KV_WIDTH:]

    lo_half = lax.broadcasted_iota(jnp.int32, (tm, KV_WIDTH), 1) < HEAD_DIM

    def spread(t):
        zero = jnp.zeros_like(t)
        return jnp.concatenate([jnp.where(lo_half, t, zero), jnp.where(lo_half, pltpu.roll(t, HEAD_DIM, 1), zero)],
                               axis=1).astype(BF16)

    k4_ref[...] = spread(k)
    v4_ref[...] = spread(v)
    g_ref[...] = _sigmoid(_dot(h, w_ref[:, C_V:]))


def _inproj(xf, seq_len, nw, w_in_bf, cw, cb, qw, kw, cos, sa, sb, ones_bd, cast=()):
    n = xf.shape[0]
    tm = TM_INPROJ
    tiles_per_seq = seq_len // tm
    nblk8 = n // SUBLANES
    r8 = tm // SUBLANES
    steps = n // tm
    kern = functools.partial(_inproj_kernel, tm=tm, tiles_per_seq=tiles_per_seq, n_cast=len(cast))
    row = lambda i: (i, 0)
    cast_specs = []
    for w in cast:
        per = 1 if (w.shape[0] // steps) % (2 * SUBLANES) == 0 else 2
        rows_blk = w.shape[0] * per // steps
        cast_specs.append(pl.BlockSpec((rows_blk, w.shape[1]), lambda i, per=per: (i // per, 0)))
    return pl.pallas_call(
        kern,
        grid=(n // tm,),
        in_specs=[
            pl.BlockSpec((tm, D_MODEL), row),
            pl.BlockSpec((SUBLANES, D_MODEL), lambda i: (jnp.maximum(i * r8 - 1, 0), 0)),
            pl.BlockSpec((SUBLANES, D_MODEL), lambda i: (jnp.minimum((i + 1) * r8, nblk8 - 1), 0)),
            _const_spec((1, D_MODEL)),
            _const_spec((D_MODEL, IN_WIDTH)),
            _const_spec((3, C_HY)),
            _const_spec((1, C_HY)),
            _const_spec((1, ATTN_WIDTH)),
            _const_spec((1, KV_WIDTH)),
            pl.BlockSpec((tm, LANES), lambda i: (i % tiles_per_seq, 0)),
            pl.BlockSpec((tm, LANES), lambda i: (i % tiles_per_seq, 0)),
            pl.BlockSpec((tm, LANES), lambda i: (i % tiles_per_seq, 0)),
            _const_spec((ATTN_WIDTH, ATTN_WIDTH)),
        ] + cast_specs,
        out_specs=[
            pl.BlockSpec((tm, HYENA_WIDTH), row),
            pl.BlockSpec((tm, HYENA_WIDTH), row),
            pl.BlockSpec((tm, HYENA_WIDTH), row),
            pl.BlockSpec((tm, 2 * ATTN_WIDTH), row),
            pl.BlockSpec((tm, 2 * LANES), row),
            pl.BlockSpec((tm, 2 * LANES), row),
            pl.BlockSpec((tm, 2 * D_MODEL), row),
        ] + cast_specs,
        out_shape=[
            jax.ShapeDtypeStruct((n, HYENA_WIDTH), BF16),
            jax.ShapeDtypeStruct((n, HYENA_WIDTH), BF16),
            jax.ShapeDtypeStruct((n, HYENA_WIDTH), BF16),
            jax.ShapeDtypeStruct((n, 2 * ATTN_WIDTH), BF16),
            jax.ShapeDtypeStruct((n, 2 * LANES), BF16),
            jax.ShapeDtypeStruct((n, 2 * LANES), BF16),
            jax.ShapeDtypeStruct((n, 2 * D_MODEL), F32),
        ] + [jax.ShapeDtypeStruct(w.shape, BF16) for w in cast],
        scratch_shapes=[pltpu.VMEM((tm + 2 * SUBLANES, C_HY), F32)],
        compiler_params=_cparams("parallel"),
        name="inproj",
    )(xf, xf, xf, nw, w_in_bf, cw, cb, qw, kw, cos, sa, sb, ones_bd, *cast)


def _rows_to_tiles(r, tj):
    return r.reshape(r.shape[0], tj, r.shape[1] // tj)


def _tiles_to_rows(t):
    return t.reshape(t.shape[0], t.shape[1] * t.shape[2])


def _twiddle_tiles(r, twr_ref, twi_ref, n1h, tj):
    c = r.shape[1] // tj
    re, im = [], []
    for j in range(tj):
        ar, ai = r[:n1h, j * c:(j + 1) * c], r[n1h:, j * c:(j + 1) * c]
        tr, ti = twr_ref[:, j:j + 1], twi_ref[:, j:j + 1]
        re.append(ar * tr - ai * ti)
        im.append(ar * ti + ai * tr)
    return (_rows_to_tiles(jnp.concatenate(re, axis=1), tj).astype(BF16),
            _rows_to_tiles(jnp.concatenate(im, axis=1), tj).astype(BF16))


def _store_twiddled(r, twr_ref, twi_ref, are_ref, aim_ref, sl, n1h, tj):
    re, im = _twiddle_tiles(r, twr_ref, twi_ref, n1h, tj)
    are_ref[:, :, sl] = re
    aim_ref[:, :, sl] = im


def _filt_kernel(z_ref, w1_ref, b1_ref, w2_ref, b2_ref, w3f_ref, w3b_ref, fr_ref, dl_ref, g_ref, twr_ref, twi_ref,
                 are_ref, aim_ref, s_ref, *, n1h, tj):
    i = pl.program_id(0)
    z = z_ref[...]
    fr = fr_ref[...]
    h = jnp.sin(fr * (_dot3(z, w1_ref[...]) + b1_ref[...]))
    h = jnp.sin(fr * (_dot3(h, w2_ref[...]) + b2_ref[...]))
    hb = h.astype(BF16)
    tf = z[:, 0:1]
    tb = z[:, FILT_PAD:FILT_PAD + 1]
    sgn = z[:, FILT_PAD + FILTER_EMB:FILT_PAD + FILTER_EMB + 1]
    gm = g_ref[...].astype(BF16)
    cb = HYENA_WIDTH
    parts = []
    for o in range(HYENA_ORDER):
        sl = slice(o * cb, (o + 1) * cb)
        dl = dl_ref[...]

        def taps(w_ref, t):
            return _dot(hb, w_ref[:, sl].astype(BF16)) * (jnp.exp(-t * dl) + DECAY_SHIFT)

        kf = taps(w3f_ref, tf)
        kb = taps(w3b_ref, tb) * sgn
        parts.append(jnp.sum(jnp.abs(kf), axis=0, keepdims=True) + jnp.sum(jnp.abs(kb), axis=0, keepdims=True))
        cols = [jnp.concatenate([kf[j * n1h:(j + 1) * n1h], kb[j * n1h:(j + 1) * n1h]], axis=0).astype(BF16)
                for j in range(tj)]
        re, im = _twiddle_tiles(_dot(gm, jnp.concatenate(cols, axis=1)), twr_ref, twi_ref, n1h, tj)
        are_ref[o] = re
        aim_ref[o] = im
    part = jnp.concatenate(parts, axis=1)

    @pl.when(i == 0)
    def _():
        s_ref[...] = part

    @pl.when(i > 0)
    def _():
        s_ref[...] += part


def _filter_stage_a(zf, w1bd, b1, w2bd, b2, w3f, w3b, fr, dl, g_full, twr, twi, n1h, tj):
    cw = HYENA_ORDER * HYENA_WIDTH
    hid2 = 2 * FILTER_HIDDEN
    a_shape = jax.ShapeDtypeStruct((HYENA_ORDER, n1h, LANES, HYENA_WIDTH), BF16)
    a_spec = pl.BlockSpec((HYENA_ORDER, n1h, tj, HYENA_WIDTH), lambda i: (0, 0, i, 0))
    t_spec = pl.BlockSpec((None, n1h, tj), lambda i: (i, 0, 0))
    return pl.pallas_call(
        functools.partial(_filt_kernel, n1h=n1h, tj=tj),
        grid=(LANES // tj,),
        in_specs=[
            pl.BlockSpec((tj * n1h, 2 * FILT_PAD), lambda i: (i, 0)),
            _const_spec((2 * FILT_PAD, hid2)),
            _const_spec((1, hid2)),
            _const_spec((hid2, hid2)),
            _const_spec((1, hid2)),
            _const_spec((hid2, cw)),
            _const_spec((hid2, cw)),
            _const_spec((1, hid2)),
            _const_spec((1, HYENA_WIDTH)),
            _const_spec((2 * n1h, 2 * n1h)),
            t_spec, t_spec,
        ],
        out_specs=[a_spec, a_spec, pl.BlockSpec((1, cw), lambda i: (0, 0))],
        out_shape=[a_shape, a_shape, jax.ShapeDtypeStruct((1, cw), F32)],
        compiler_params=_cparams("arbitrary"),
        name="filter_taps_stage_a",
    )(zf, w1bd, b1, w2bd, b2, w3f, w3b, fr, dl, g_full, twr, twi)


def _fwda_kernel(g_ref, twr_ref, twi_ref, u_ref, are_ref, aim_ref, *, n1h, tj):
    gm = g_ref[...].astype(BF16)
    prev = None
    for sl in _CH_SPLITS:
        u = _tiles_to_rows(u_ref[:, :, sl])
        if prev is not None:
            _store_twiddled(prev[1], twr_ref, twi_ref, are_ref, aim_ref, prev[0], n1h, tj)
        prev = (sl, _dot(gm, u))
    _store_twiddled(prev[1], twr_ref, twi_ref, are_ref, aim_ref, prev[0], n1h, tj)


def _fwd_a(u4, g_half, twr, twi, tj):
    b, n1h, _, cb = u4.shape
    blk = pl.BlockSpec((None, n1h, tj, cb), lambda bi, ji: (bi, 0, ji, 0))
    t_spec = pl.BlockSpec((None, n1h, tj), lambda bi, ji: (ji, 0, 0))
    a_shape = jax.ShapeDtypeStruct(u4.shape, BF16)
    return pl.pallas_call(
        functools.partial(_fwda_kernel, n1h=n1h, tj=tj),
        grid=(b, LANES // tj),
        in_specs=[_const_spec((2 * n1h, n1h)), t_spec, t_spec, blk],
        out_specs=[blk, blk],
        out_shape=[a_shape, a_shape],
        compiler_params=_cparams("parallel", "parallel"),
        name="dft_stage_a",
    )(g_half, twr, twi, u4)


def _mid_kernel(f_ref, fi_ref, twr_ref, twi_ref, fre_ref, fim_ref, are_ref, aim_ref, bre_ref, bim_ref, *scratch,
                kb):
    fm = f_ref[...].astype(BF16)
    fim = fi_ref[...].astype(BF16)
    fill = jnp.zeros((LANES - kb, LANES), F32)
    twr_t = jnp.concatenate([twr_ref[...], fill], axis=0).T
    twi_t = jnp.concatenate([twi_ref[...], fill], axis=0).T

    def filt(k):
        return _dot(fm, jnp.concatenate([fre_ref[k], fim_ref[k]], axis=0))

    if scratch:
        (h_ref,) = scratch

        @pl.when(pl.program_id(1) == 0)
        def _():
            for k in range(kb):
                h_ref[k] = filt(k)

    def fwd(k):
        return (_dot(fm, jnp.concatenate([are_ref[k], aim_ref[k]], axis=0)), h_ref[k] if scratch else filt(k))

    def spec(k, xh):
        x, h = xh
        xr, xi = x[:LANES], x[LANES:]
        hr, hi = h[:LANES], h[LANES:]
        return jnp.concatenate([xr * hr - xi * hi, xr * hi + xi * hr], axis=0).astype(BF16)

    def out(k, bc):
        br, bi = bc[:LANES], bc[LANES:]
        tr, ti = twr_t[:, k:k + 1], twi_t[:, k:k + 1]
        bre_ref[k] = (br * tr + bi * ti).astype(BF16)
        bim_ref[k] = (bi * tr - br * ti).astype(BF16)

    x = fwd(0)
    bc_prev = None
    for k in range(kb):
        y = spec(k, x)
        if k + 1 < kb:
            x = fwd(k + 1)
        if bc_prev is not None:
            out(k - 1, bc_prev)
        bc_prev = _dot(fim, y)
    out(kb - 1, bc_prev)


def _mid(are, aim, fblk, fblk_inv, twr, twi, afre, afim, order, kb):
    b, n1h, _, cb = are.shape
    a_spec = pl.BlockSpec((None, kb, LANES, cb), lambda ki, bi: (bi, ki, 0, 0))
    f_spec = pl.BlockSpec((None, kb, LANES, cb), lambda ki, bi: (order, ki, 0, 0))
    t_spec = pl.BlockSpec((kb, LANES), lambda ki, bi: (ki, 0))
    shape = jax.ShapeDtypeStruct(are.shape, BF16)
    return pl.pallas_call(
        functools.partial(_mid_kernel, kb=kb),
        grid=(n1h // kb, b),
        in_specs=[
            _const_spec((2 * LANES, 2 * LANES)),
            _const_spec((2 * LANES, 2 * LANES)),
            t_spec, t_spec,
            f_spec, f_spec,
            a_spec, a_spec,
        ],
        out_specs=[a_spec, a_spec],
        out_shape=[shape, shape],
        scratch_shapes=[pltpu.VMEM((kb, 2 * LANES, cb), F32)] if b > 1 else [],
        compiler_params=_cparams("parallel", "arbitrary"),
        name="dft_stage_c",
    )(fblk, fblk_inv, twr, twi, afre, afim, are, aim)


def _inva_kernel(gi_ref, bre_ref, bim_ref, xg_ref, v_ref, sb_ref, s_ref, *rest, n1h, tj, fuse_next):
    gim = gi_ref[...].astype(BF16)
    inv_l1 = 1.0 / s_ref[...]
    if fuse_next:
        g_ref, twr_ref, twi_ref, z_ref, are_ref, aim_ref = rest
        gm = g_ref[...].astype(BF16)
    else:
        (z_ref,) = rest

    def conv(sl):
        rhs = jnp.concatenate([_tiles_to_rows(bre_ref[:, :, sl]), _tiles_to_rows(bim_ref[:, :, sl])], axis=0)
        return _dot(gim, rhs)

    def gate(sl, y):
        z = xg_ref[:, :, sl].astype(F32) * (_rows_to_tiles(y, tj) * inv_l1[:, sl]
                                            + sb_ref[:, sl] * v_ref[:, :, sl].astype(F32))
        z = z.astype(BF16)
        z_ref[:, :, sl] = z
        return z

    s0, s1 = _CH_SPLITS
    y0 = conv(s0)
    y1 = conv(s1)
    z0 = gate(s0, y0)
    if fuse_next:
        r0 = _dot(gm, _tiles_to_rows(z0))
    z1 = gate(s1, y1)
    if fuse_next:
        _store_twiddled(r0, twr_ref, twi_ref, are_ref, aim_ref, s0, n1h, tj)
        r1 = _dot(gm, _tiles_to_rows(z1))
        _store_twiddled(r1, twr_ref, twi_ref, are_ref, aim_ref, s1, n1h, tj)


def _inv_a(bre, bim, g_inv, xg, v, sb_row, asum, order, tj, nxt=None):
    b, n1h, _, cb = bre.shape
    blk = pl.BlockSpec((None, n1h, tj, cb), lambda bi, ji: (bi, 0, ji, 0))
    in_specs = [_const_spec((n1h, 2 * n1h)), blk, blk, blk, blk, _const_spec((1, cb)),
                pl.BlockSpec((1, cb), lambda bi, ji: (0, order))]
    args = [g_inv, bre, bim, xg, v, sb_row, asum]
    out_specs = [blk]
    out_shape = [jax.ShapeDtypeStruct(bre.shape, BF16)]
    if nxt is not None:
        t_spec = pl.BlockSpec((None, n1h, tj), lambda bi, ji: (ji, 0, 0))
        in_specs += [_const_spec((2 * n1h, n1h)), t_spec, t_spec]
        args += list(nxt)
        out_specs += [blk, blk]
        out_shape += [jax.ShapeDtypeStruct(bre.shape, BF16)] * 2
    return pl.pallas_call(
        functools.partial(_inva_kernel, n1h=n1h, tj=tj, fuse_next=nxt is not None),
        grid=(b, LANES // tj),
        in_specs=in_specs,
        out_specs=out_specs,
        out_shape=out_shape,
        compiler_params=_cparams("parallel", "parallel"),
        name="idft_stage_a_gate",
    )(*args)


def _attn_kernel(sink_ref, q_ref, kp_ref, kc_ref, kn_ref, vp_ref, vc_ref, vn_ref, o_ref, *, nblk, sub):
    i = pl.program_id(1)
    blk = ATTN_BLOCK
    grp = N_HEADS // N_KV_HEADS
    rows = grp * blk
    win = 3 * blk
    qi = lax.broadcasted_iota(jnp.int32, (rows, win), 0) % blk
    si = lax.broadcasted_iota(jnp.int32, (rows, win), 1)
    mid = (si >= blk) & (si < 2 * blk)
    tri_prev = (si < blk) & (si >= qi)
    tri_next = (si >= 2 * blk) & (si - 2 * blk <= qi)
    hrow = lax.broadcasted_iota(jnp.int32, (rows, 1), 0) // blk
    kcat = jnp.concatenate([kp_ref[...], kc_ref[...], kn_ref[...]], axis=0)
    vcat = jnp.concatenate([vp_ref[...], vc_ref[...], vn_ref[...]], axis=0)
    lane_k = lax.broadcasted_iota(jnp.int32, (win, N_KV_HEADS * LANES), 1)
    lane_o = lax.broadcasted_iota(jnp.int32, (rows, LANES), 1)
    zero_k = jnp.zeros((win, N_KV_HEADS * LANES), BF16)
    for s in range(sub):
        j = sub * i + s
        valid = mid | (tri_prev & (j > 0)) | (tri_next & (j < nblk - 1))
        kw = kcat[s * blk:s * blk + win]
        vw = vcat[s * blk:s * blk + win]
        kd = jnp.concatenate([jnp.where(lane_k // LANES == g, kw, zero_k) for g in range(N_KV_HEADS)], axis=0)
        vd = jnp.concatenate([vw[:, :LANES], pltpu.roll(vw[:, LANES:].astype(F32), HEAD_DIM, 1).astype(BF16)], axis=0)
        lhs = jnp.concatenate([q_ref[s * blk:(s + 1) * blk, t * N_KV_HEADS * LANES:(t + 1) * N_KV_HEADS * LANES]
                               for t in range(grp)], axis=0)
        sc_all = lax.dot_general(lhs, kd, (((1,), (1,)), ((), ())), preferred_element_type=F32)
        es, dens = [], []
        for g in range(N_KV_HEADS):
            sc = jnp.where(valid, sc_all[:, g * win:(g + 1) * win], NEG_INF)
            sk = jnp.full((rows, 1), sink_ref[grp * g], F32)
            for t in range(1, grp):
                sk = jnp.where(hrow == t, sink_ref[grp * g + t], sk)
            sk = sk * LOG2E
            m = jnp.maximum(jnp.max(sc, axis=1, keepdims=True), sk)
            e = jnp.exp2(sc - m)
            dens.append(jnp.sum(e, axis=1, keepdims=True) + jnp.exp2(sk - m))
            es.append(e.astype(BF16))
        den = jnp.where(lane_o < HEAD_DIM, dens[0], dens[1])
        r = _dot(jnp.concatenate(es, axis=1), vd) / den
        o_ref[s * blk:(s + 1) * blk, :] = jnp.concatenate(
            [r[t * blk:(t + 1) * blk] for t in range(grp)], axis=1).astype(BF16)


def _attention(q3, k3, v3, sink):
    b, seq_len, qw = q3.shape
    blk = ATTN_BLOCK
    sub = ATTN_SUB
    nblk = seq_len // blk
    wide = k3.shape[-1]
    prev = lambda bi, i: (bi, jnp.maximum(sub * i - 1, 0), 0)
    cur = lambda bi, i: (bi, i, 0)
    nxt = lambda bi, i: (bi, jnp.minimum(sub * (i + 1), nblk - 1), 0)
    edge = lambda f: pl.BlockSpec((None, blk, wide), f)
    body = pl.BlockSpec((None, sub * blk, wide), cur)
    return pl.pallas_call(
        functools.partial(_attn_kernel, nblk=nblk, sub=sub),
        grid=(b, nblk // sub),
        in_specs=[pl.BlockSpec(memory_space=pltpu.SMEM),
                  pl.BlockSpec((None, sub * blk, qw), cur),
                  edge(prev), body, edge(nxt), edge(prev), body, edge(nxt)],
        out_specs=pl.BlockSpec((None, sub * blk, ATTN_WIDTH), cur),
        out_shape=jax.ShapeDtypeStruct((b, seq_len, ATTN_WIDTH), BF16),
        compiler_params=_cparams("parallel", "parallel"),
        name="banded_attention",
    )(sink, q3, k3, k3, k3, v3, v3, v3)


def _final_kernel(x_ref, yh_ref, ya_ref, g_ref, why_ref, wat_ref, wo_ref, nw_ref, wg_ref, wu_ref, wd_ref,
                  o_ref):
    a = _dot(yh_ref[...], why_ref[...])
    b = _dot(ya_ref[...], wat_ref[...])
    merged = g_ref[:, :D_MODEL] * a + g_ref[:, D_MODEL:] * b
    x1 = x_ref[...] + _dot(merged.astype(BF16), wo_ref[...])
    ms = jnp.mean(x1 * x1, axis=-1, keepdims=True)
    f = (x1 * lax.rsqrt(ms + RMS_EPS) * nw_ref[...]).astype(BF16)
    acc = x1
    for lo, hi in FFN_CHUNKS:
        gt = _dot(f, wg_ref[:, lo:hi])
        upv = _dot(f, wu_ref[:, lo:hi])
        hid = gt * _sigmoid(gt) * upv
        acc = acc + _dot(hid.astype(BF16), wd_ref[lo:hi, :])
    o_ref[...] = acc


def _final(xf, yh, ya, g, why, wat, wo, nw, wg, wu, wd):
    n = xf.shape[0]
    tm = TM_FINAL
    row = lambda i: (i, 0)
    return pl.pallas_call(
        _final_kernel,
        grid=(n // tm,),
        in_specs=[
            pl.BlockSpec((tm, D_MODEL), row),
            pl.BlockSpec((tm, HYENA_WIDTH), row),
            pl.BlockSpec((tm, ATTN_WIDTH), row),
            pl.BlockSpec((tm, 2 * D_MODEL), row),
            _const_spec((HYENA_WIDTH, D_MODEL)),
            _const_spec((ATTN_WIDTH, D_MODEL)),
            _const_spec((D_MODEL, D_MODEL)),
            _const_spec((1, D_MODEL)),
            _const_spec((D_MODEL, FFN_HIDDEN)),
            _const_spec((D_MODEL, FFN_HIDDEN)),
            _const_spec((FFN_HIDDEN, D_MODEL)),
        ],
        out_specs=pl.BlockSpec((tm, D_MODEL), row),
        out_shape=jax.ShapeDtypeStruct((n, D_MODEL), F32),
        compiler_params=_cparams("parallel"),
        name="merge_ffn",
    )(xf, yh, ya, g, why, wat, wo, nw, wg, wu, wd)


@functools.lru_cache(maxsize=None)
def _dft_constants(n1h, tj):
    n1_len = 2 * n1h
    m = n1_len * LANES
    k1 = np.arange(n1h, dtype=np.float64)[:, None] + 0.5
    n1 = np.arange(n1_len, dtype=np.float64)[None, :]
    gc = np.exp(-2j * np.pi * k1 * n1 / n1_len)
    g_full = np.concatenate([gc.real, gc.imag], axis=0)
    g_inv = (2.0 / m) * np.concatenate([gc.real[:, :n1h].T, gc.imag[:, :n1h].T], axis=1)
    n2 = np.arange(LANES, dtype=np.float64)
    tw = np.exp(-2j * np.pi * k1 * n2[None, :] / m)
    fc = np.exp(-2j * np.pi * np.outer(n2, n2) / LANES)
    fblk = np.block([[fc.real, -fc.imag], [fc.imag, fc.real]])
    fblk_inv = np.block([[fc.real, fc.imag], [-fc.imag, fc.real]])
    f32 = lambda a: np.ascontiguousarray(a, dtype=np.float32)
    tiled = lambda a: a.reshape(n1h, LANES // tj, tj).transpose(1, 0, 2)
    return dict(g_full=f32(g_full), g_half=f32(g_full[:, :n1h]), g_inv=f32(g_inv), twr=f32(tw.real),
                twi=f32(tw.imag), twr_a=f32(tiled(tw.real)), twi_a=f32(tiled(tw.imag)), fblk=f32(fblk),
                fblk_inv=f32(fblk_inv))


@functools.lru_cache(maxsize=None)
def _filter_features(seq_len):
    n1h = seq_len // LANES
    n2 = np.arange(LANES)[:, None]
    n1 = np.arange(n1h)[None, :]
    slot_f = (LANES * n1 + n2).reshape(-1)
    slot_b = slot_f + seq_len
    neg_lag = slot_b > seq_len
    pos_b = np.where(neg_lag, 2 * seq_len - slot_b, 0)
    bands = np.linspace(1e-4, FILTER_BANDS - 1, FILTER_BANDS)

    def feats(pos, sgn):
        pos = pos.astype(np.float64)
        ang = (2.0 * math.pi / seq_len) * pos[:, None] * bands[None, :]
        pad = np.zeros((pos.shape[0], FILT_PAD - FILTER_EMB - 1))
        return np.concatenate([(pos / (seq_len - 1))[:, None], np.cos(ang), -np.sin(ang), sgn[:, None], pad], axis=-1)

    z = np.concatenate([feats(slot_f, np.ones(slot_f.shape)), feats(pos_b, np.where(neg_lag, -1.0, 0.0))], axis=-1)
    return np.ascontiguousarray(z, dtype=np.float32)


@functools.lru_cache(maxsize=None)
def _rope_tables(seq_len):
    inv = ROPE_THETA ** (-np.arange(0, HEAD_DIM, 2, dtype=np.float64) / HEAD_DIM)
    ang = np.arange(seq_len, dtype=np.float64)[:, None] * inv[None, :]
    cos, sin = np.cos(ang), np.sin(ang)
    zero = np.zeros_like(sin)
    reps = LANES // HEAD_DIM
    f32 = lambda a: np.ascontiguousarray(np.tile(a, (1, reps)), dtype=np.float32)
    return f32(np.concatenate([cos, cos], axis=1)), f32(np.concatenate([-sin, zero], axis=1)), \
        f32(np.concatenate([zero, sin], axis=1))


_LATE_WEIGHTS = ("w_hy_out", "w_at_out", "w_o", "w_gate", "w_up", "w_down")


def _layer(x, p, late_bf=None):
    b, seq_len, _ = x.shape
    n = b * seq_len
    n1h = seq_len // LANES
    xf = x.reshape(n, D_MODEL)
    tj_filt, tj_a, tj_gate, tj_last = STAGE_A_TILES[n1h >= LANES]
    c = {k: jnp.asarray(v) for k, v in _dft_constants(n1h, tj_a).items()}
    tw_a = lambda tj: tuple(jnp.asarray(_dft_constants(n1h, tj)[k]) for k in ("twr_a", "twi_a"))
    kb = 2 * SUBLANES
    r4 = lambda a: a.reshape(b, n1h, LANES, HYENA_WIDTH)

    cos, sa, sb = (jnp.asarray(t) for t in _rope_tables(seq_len))
    cast = () if late_bf is not None else tuple(p[k] for k in _LATE_WEIGHTS)
    res = _inproj(xf, seq_len, p["attn_norm_w"], p["w_in"], p["conv_w"], p["conv_b"],
                  p["q_norm_w"], p["k_norm_w"], cos, sa, sb, p["ones_bd"], cast)
    x1, x2, v, q, k4, v4, g = res[:7]
    if late_bf is None:
        late_bf = dict(zip(_LATE_WEIGHTS, res[7:]))

    afre, afim, asum = _filter_stage_a(jnp.asarray(_filter_features(seq_len)), p["filt_w1"], p["filt_b1"],
                                       p["filt_w2"], p["filt_b2"], p["filt_w3f"], p["filt_w3b"], p["filt_freq"],
                                       p["decay"], c["g_full"], *tw_a(tj_filt), n1h, tj_filt)

    are, aim = _fwd_a(r4(v), c["g_half"], *tw_a(tj_a), tj_a)
    bre, bim = _mid(are, aim, c["fblk"], c["fblk_inv"], c["twr"], c["twi"], afre, afim, 0, kb)
    z1, are, aim = _inv_a(bre, bim, c["g_inv"], r4(x1), r4(v), p["hyena_bias"][0:1], asum, 0, tj_gate,
                          nxt=(c["g_half"], *tw_a(tj_gate)))
    bre, bim = _mid(are, aim, c["fblk"], c["fblk_inv"], c["twr"], c["twi"], afre, afim, 1, kb)
    (yh,) = _inv_a(bre, bim, c["g_inv"], r4(x2), z1, p["hyena_bias"][1:2], asum, 1, tj_last)

    ya = _attention(q.reshape(b, seq_len, 2 * ATTN_WIDTH), k4.reshape(b, seq_len, 2 * LANES),
                    v4.reshape(b, seq_len, 2 * LANES), p["attn_sink"])

    w = late_bf
    out = _final(xf, yh.reshape(n, HYENA_WIDTH), ya.reshape(n, ATTN_WIDTH), g, w["w_hy_out"], w["w_at_out"],
                 w["w_o"], p["ffn_norm_w"], w["w_gate"], w["w_up"], w["w_down"])
    return out.reshape(b, seq_len, D_MODEL), late_bf


def _block_diag2(w):
    z = jnp.zeros_like(w)
    return jnp.concatenate([jnp.concatenate([w, z], axis=1), jnp.concatenate([z, w], axis=1)], axis=0)


def kernel(x_prompt, x_sample, attn_norm_w, w_in, hyena_conv_w, hyena_conv_b, filt_w1, filt_b1, filt_w2, filt_b2,
           filt_w3, filt_freq, hyena_bias, q_norm_w, k_norm_w, attn_sink, w_hy_out, w_at_out, w_o, ffn_norm_w,
           w_gate, w_up, w_down):
    cw = HYENA_ORDER * HYENA_WIDTH
    max_decay = math.log(DECAY_TARGET) / DECAY_FAST_PCT
    min_decay = math.log(DECAY_TARGET) / DECAY_SLOW_PCT
    deltas = jnp.abs(jnp.linspace(min_decay, max_decay, HYENA_WIDTH, dtype=F32))
    head = np.arange(ATTN_WIDTH) // HEAD_DIM
    ones_bd = jnp.asarray((head[:, None] == head[None, :]).astype(np.float32) / HEAD_DIM).astype(BF16)
    w3 = filt_w3[0].reshape(FILTER_HIDDEN, HYENA_ORDER, 2, HYENA_WIDTH).transpose(2, 0, 1, 3)
    w3 = w3.reshape(2, FILTER_HIDDEN, cw)
    w3_zero = jnp.zeros((FILTER_HIDDEN, cw), F32)
    twice = lambda a: jnp.tile(a, 2)[None, :]
    p = dict(
        attn_norm_w=attn_norm_w[0][None, :],
        w_in=w_in[0].astype(BF16),
        conv_w=hyena_conv_w[0],
        conv_b=hyena_conv_b[0][None, :],
        filt_w1=_block_diag2(jnp.pad(filt_w1[0], ((0, FILT_PAD - FILTER_EMB), (0, 0)))),
        filt_b1=twice(filt_b1[0]),
        filt_w2=_block_diag2(filt_w2[0]),
        filt_b2=twice(filt_b2[0]),
        filt_w3f=jnp.concatenate([w3[0], w3_zero], axis=0),
        filt_w3b=jnp.concatenate([w3_zero, w3[1]], axis=0),
        filt_freq=twice(filt_freq[0]),
        decay=deltas[None, :],
        hyena_bias=hyena_bias[0],
        q_norm_w=jnp.tile(q_norm_w[0], N_HEADS)[None, :],
        k_norm_w=jnp.tile(k_norm_w[0], N_KV_HEADS)[None, :],
        attn_sink=attn_sink[0],
        ones_bd=ones_bd,
        w_hy_out=w_hy_out[0],
        w_at_out=w_at_out[0].reshape(N_HEADS, HEAD_DIM, D_MODEL)[np.array(ATTN_HEAD_ORDER)].reshape(ATTN_WIDTH, D_MODEL),
        w_o=w_o[0],
        ffn_norm_w=ffn_norm_w[0][None, :],
        w_gate=w_gate[0],
        w_up=w_up[0],
        w_down=w_down[0],
    )
    y_prompt, late_bf = _layer(x_prompt, p)
    y_sample, _ = _layer(x_sample, p, late_bf)
    return (y_prompt, y_sample)
```

```python
import functools
import math

import numpy as np
import jax
import jax.numpy as jnp
from jax import lax
from jax.experimental import pallas as pl
from jax.experimental.pallas import tpu as pltpu

F32 = jnp.float32
BF16 = jnp.bfloat16

D_MODEL = 1024
HYENA_WIDTH = 512
HYENA_ORDER = 2
FILTER_BANDS = 16
FILTER_EMB = 1 + 2 * FILTER_BANDS
FILTER_HIDDEN = 64
DECAY_FAST_PCT = 0.3
DECAY_SLOW_PCT = 1.5
DECAY_TARGET = 1e-2
DECAY_SHIFT = 0.05
N_HEADS = 8
N_KV_HEADS = 2
HEAD_DIM = 64
ATTN_WIDTH = N_HEADS * HEAD_DIM
KV_WIDTH = N_KV_HEADS * HEAD_DIM
WINDOW = 128
ROPE_THETA = 10000.0
FFN_HIDDEN = 2816
RMS_EPS = 1e-6
NEG_INF = -1e30
LOG2E = math.log2(math.e)

C_HY = 3 * HYENA_WIDTH
C_Q = C_HY + ATTN_WIDTH
C_K = C_Q + KV_WIDTH
C_V = C_K + KV_WIDTH
IN_WIDTH = C_V + 2 * D_MODEL

LANES = 128
SUBLANES = 8
ATTN_BLOCK = WINDOW
ATTN_HEAD_ORDER = tuple(g * (N_HEADS // N_KV_HEADS) + t for t in range(N_HEADS // N_KV_HEADS) for g in range(N_KV_HEADS))
ATTN_SUB = 16
VMEM_LIMIT = 56 * 1024 * 1024

TM_INPROJ = 512
TM_FINAL = 512
FFN_CHUNKS = ((0, 1280), (1280, FFN_HIDDEN))
_CH_SPLITS = (slice(0, HYENA_WIDTH // 2), slice(HYENA_WIDTH // 2, HYENA_WIDTH))
STAGE_A_TILES = {False: (128, 128, 128, 128), True: (16, 8, 8, 16)}
FILT_PAD = 64


def _cparams(*sem):
    return pltpu.CompilerParams(dimension_semantics=sem, vmem_limit_bytes=VMEM_LIMIT)


def _const_spec(shape):
    nd = len(shape)
    return pl.BlockSpec(shape, lambda *_: (0,) * nd, pipeline_mode=pl.Buffered(1))


def _dot(a, b):
    return jnp.dot(a, b, preferred_element_type=F32)


def _sigmoid(x):
    return 0.5 * jnp.tanh(0.5 * x) + 0.5


def _split(a):
    hi = a.astype(BF16)
    lo = (a - hi.astype(F32)).astype(BF16)
    return hi, lo


def _dot3(a, w):
    ah, al = _split(a)
    wh, wl = _split(w)
    return _dot(ah, wh) + _dot(al, wh) + _dot(ah, wl)


def _inproj_kernel(*refs, tm, tiles_per_seq, n_cast):
    (x_ref, xp_ref, xn_ref, nw_ref, w_ref, cw_ref, cb_ref, qw_ref, kw_ref, cos_ref, sa_ref, sb_ref,
     ones_ref) = refs[:13]
    cast_in = refs[13:13 + n_cast]
    x1_ref, x2_ref, v_ref, q_ref, k4_ref, v4_ref, g_ref = refs[13 + n_cast:20 + n_cast]
    cast_out = refs[20 + n_cast:20 + 2 * n_cast]
    pad_ref = refs[-1]
    for src_ref, dst_ref in zip(cast_in, cast_out):
        dst_ref[...] = src_ref[...].astype(BF16)
    pos = pl.program_id(0) % tiles_per_seq
    nw = nw_ref[...]

    def norm(xv):
        ms = jnp.mean(xv * xv, axis=-1, keepdims=True)
        return (xv * lax.rsqrt(ms + RMS_EPS) * nw).astype(BF16)

    h_all = norm(jnp.concatenate([x_ref[...], xp_ref[...], xn_ref[...]], axis=0))
    h = h_all[:tm]

    hy_all = _dot(h_all, w_ref[:, :C_HY])
    hy = hy_all[:tm]
    hyh = hy_all[tm:]
    prev_ok = (pos > 0).astype(F32)
    next_ok = (pos < tiles_per_seq - 1).astype(F32)
    pad_ref[0:SUBLANES] = hyh[0:SUBLANES] * prev_ok
    pad_ref[SUBLANES:SUBLANES + tm] = hy
    pad_ref[SUBLANES + tm:2 * SUBLANES + tm] = hyh[SUBLANES:] * next_ok
    up = pad_ref[SUBLANES - 1:SUBLANES - 1 + tm]
    un = pad_ref[SUBLANES + 1:SUBLANES + 1 + tm]
    cw = cw_ref[...]
    uc = cw[0:1] * up + cw[1:2] * hy + cw[2:3] * un + cb_ref[...]
    x1_ref[...] = uc[:, :HYENA_WIDTH].astype(BF16)
    x2_ref[...] = uc[:, HYENA_WIDTH:2 * HYENA_WIDTH].astype(BF16)
    v_ref[...] = uc[:, 2 * HYENA_WIDTH:].astype(BF16)

    def head_norm_rope(t, wrow, ones, reps):
        ms = _dot((t * t).astype(BF16), ones)
        tn = t * lax.rsqrt(ms + RMS_EPS) * wrow
        width = t.shape[1]
        cos = jnp.concatenate([cos_ref[...]] * reps, axis=1)
        sa = jnp.concatenate([sa_ref[...]] * reps, axis=1)
        sb = jnp.concatenate([sb_ref[...]] * reps, axis=1)
        half = HEAD_DIM // 2
        return tn * cos + pltpu.roll(tn, width - half, 1) * sa + pltpu.roll(tn, half, 1) * sb

    qkv = _dot(h, w_ref[:, C_HY:C_V])
    q = qkv[:, :ATTN_WIDTH]
    q = head_norm_rope(q, qw_ref[...], ones_ref[...], ATTN_WIDTH // LANES) * (HEAD_DIM ** -0.5 * LOG2E)
    lead = {}
    for p in range(N_HEADS // 2):
        qp = q[:, p * LANES:(p + 1) * LANES]
        lead[2 * p], lead[2 * p + 1] = qp, pltpu.roll(qp, HEAD_DIM, 1)
    q_ref[...] = jnp.concatenate([lead[h] for h in ATTN_HEAD_ORDER], axis=1).astype(BF16)

    k = qkv[:, ATTN_WIDTH:ATTN_WIDTH + KV_WIDTH]
    k = head_norm_rope(k, kw_ref[...], ones_ref[:KV_WIDTH, :KV_WIDTH], 1)
    v = qkv[:, ATTN_WIDTH + KV_WIDTH:]

    lo_half = lax.broadcasted_iota(jnp.int32, (tm, KV_WIDTH), 1) < HEAD_DIM

    def spread(t):
        zero = jnp.zeros_like(t)
        return jnp.concatenate([jnp.where(lo_half, t, zero), jnp.where(lo_half, pltpu.roll(t, HEAD_DIM, 1), zero)],
                               axis=1).astype(BF16)

    k4_ref[...] = spread(k)
    v4_ref[...] = spread(v)
    g_ref[...] = _sigmoid(_dot(h, w_ref[:, C_V:]))


def _inproj(xf, seq_len, nw, w_in_bf, cw, cb, qw, kw, cos, sa, sb, ones_bd, cast=()):
    n = xf.shape[0]
    tm = TM_INPROJ
    tiles_per_seq = seq_len // tm
    nblk8 = n // SUBLANES
    r8 = tm // SUBLANES
    steps = n // tm
    kern = functools.partial(_inproj_kernel, tm=tm, tiles_per_seq=tiles_per_seq, n_cast=len(cast))
    row = lambda i: (i, 0)
    cast_specs = []
    for w in cast:
        per = 1 if (w.shape[0] // steps) % (2 * SUBLANES) == 0 else 2
        rows_blk = w.shape[0] * per // steps
        cast_specs.append(pl.BlockSpec((rows_blk, w.shape[1]), lambda i, per=per: (i // per, 0)))
    return pl.pallas_call(
        kern,
        grid=(n // tm,),
        in_specs=[
            pl.BlockSpec((tm, D_MODEL), row),
            pl.BlockSpec((SUBLANES, D_MODEL), lambda i: (jnp.maximum(i * r8 - 1, 0), 0)),
            pl.BlockSpec((SUBLANES, D_MODEL), lambda i: (jnp.minimum((i + 1) * r8, nblk8 - 1), 0)),
            _const_spec((1, D_MODEL)),
            _const_spec((D_MODEL, IN_WIDTH)),
            _const_spec((3, C_HY)),
            _const_spec((1, C_HY)),
            _const_spec((1, ATTN_WIDTH)),
            _const_spec((1, KV_WIDTH)),
            pl.BlockSpec((tm, LANES), lambda i: (i % tiles_per_seq, 0)),
            pl.BlockSpec((tm, LANES), lambda i: (i % tiles_per_seq, 0)),
            pl.BlockSpec((tm, LANES), lambda i: (i % tiles_per_seq, 0)),
            _const_spec((ATTN_WIDTH, ATTN_WIDTH)),
        ] + cast_specs,
        out_specs=[
            pl.BlockSpec((tm, HYENA_WIDTH), row),
            pl.BlockSpec((tm, HYENA_WIDTH), row),
            pl.BlockSpec((tm, HYENA_WIDTH), row),
            pl.BlockSpec((tm, 2 * ATTN_WIDTH), row),
            pl.BlockSpec((tm, 2 * LANES), row),
            pl.BlockSpec((tm, 2 * LANES), row),
            pl.BlockSpec((tm, 2 * D_MODEL), row),
        ] + cast_specs,
        out_shape=[
            jax.ShapeDtypeStruct((n, HYENA_WIDTH), BF16),
            jax.ShapeDtypeStruct((n, HYENA_WIDTH), BF16),
            jax.ShapeDtypeStruct((n, HYENA_WIDTH), BF16),
            jax.ShapeDtypeStruct((n, 2 * ATTN_WIDTH), BF16),
            jax.ShapeDtypeStruct((n, 2 * LANES), BF16),
            jax.ShapeDtypeStruct((n, 2 * LANES), BF16),
            jax.ShapeDtypeStruct((n, 2 * D_MODEL), F32),
        ] + [jax.ShapeDtypeStruct(w.shape, BF16) for w in cast],
        scratch_shapes=[pltpu.VMEM((tm + 2 * SUBLANES, C_HY), F32)],
        compiler_params=_cparams("parallel"),
        name="inproj",
    )(xf, xf, xf, nw, w_in_bf, cw, cb, qw, kw, cos, sa, sb, ones_bd, *cast)


def _rows_to_tiles(r, tj):
    return r.reshape(r.shape[0], tj, r.shape[1] // tj)


def _tiles_to_rows(t):
    return t.reshape(t.shape[0], t.shape[1] * t.shape[2])


def _twiddle_tiles(r, twr_ref, twi_ref, n1h, tj):
    c = r.shape[1] // tj
    re, im = [], []
    for j in range(tj):
        ar, ai = r[:n1h, j * c:(j + 1) * c], r[n1h:, j * c:(j + 1) * c]
        tr, ti = twr_ref[:, j:j + 1], twi_ref[:, j:j + 1]
        re.append(ar * tr - ai * ti)
        im.append(ar * ti + ai * tr)
    return (_rows_to_tiles(jnp.concatenate(re, axis=1), tj).astype(BF16),
            _rows_to_tiles(jnp.concatenate(im, axis=1), tj).astype(BF16))


def _store_twiddled(r, twr_ref, twi_ref, are_ref, aim_ref, sl, n1h, tj):
    re, im = _twiddle_tiles(r, twr_ref, twi_ref, n1h, tj)
    are_ref[:, :, sl] = re
    aim_ref[:, :, sl] = im


def _filt_kernel(z_ref, w1_ref, b1_ref, w2_ref, b2_ref, w3f_ref, w3b_ref, fr_ref, dl_ref, g_ref, twr_ref, twi_ref,
                 are_ref, aim_ref, s_ref, *, n1h, tj):
    i = pl.program_id(0)
    z = z_ref[...]
    fr = fr_ref[...]
    h = jnp.sin(fr * (_dot3(z, w1_ref[...]) + b1_ref[...]))
    h = jnp.sin(fr * (_dot3(h, w2_ref[...]) + b2_ref[...]))
    hb = h.astype(BF16)
    tf = z[:, 0:1]
    tb = z[:, FILT_PAD:FILT_PAD + 1]
    sgn = z[:, FILT_PAD + FILTER_EMB:FILT_PAD + FILTER_EMB + 1]
    gm = g_ref[...].astype(BF16)
    cb = HYENA_WIDTH
    dl = dl_ref[...]

    def window(t):
        base = jnp.exp(-t[:n1h] * dl)
        return jnp.concatenate(
            [base * jnp.exp(-(t[j * n1h + 1:j * n1h + 2] - t[1:2]) * dl) + DECAY_SHIFT for j in range(tj)], axis=0)

    win_f = window(tf)
    win_b = window(tb) * sgn
    parts = []
    for o in range(HYENA_ORDER):
        sl = slice(o * cb, (o + 1) * cb)
        kf = _dot(hb, w3f_ref[:, sl].astype(BF16)) * win_f
        kb = _dot(hb, w3b_ref[:, sl].astype(BF16)) * win_b
        parts.append(jnp.sum(jnp.abs(kf), axis=0, keepdims=True) + jnp.sum(jnp.abs(kb), axis=0, keepdims=True))
        cols = [jnp.concatenate([kf[j * n1h:(j + 1) * n1h], kb[j * n1h:(j + 1) * n1h]], axis=0).astype(BF16)
                for j in range(tj)]
        re, im = _twiddle_tiles(_dot(gm, jnp.concatenate(cols, axis=1)), twr_ref, twi_ref, n1h, tj)
        are_ref[o] = re
        aim_ref[o] = im
    part = jnp.concatenate(parts, axis=1)

    @pl.when(i == 0)
    def _():
        s_ref[...] = part

    @pl.when(i > 0)
    def _():
        s_ref[...] += part


def _filter_stage_a(zf, w1bd, b1, w2bd, b2, w3f, w3b, fr, dl, g_full, twr, twi, n1h, tj):
    cw = HYENA_ORDER * HYENA_WIDTH
    hid2 = 2 * FILTER_HIDDEN
    a_shape = jax.ShapeDtypeStruct((HYENA_ORDER, n1h, LANES, HYENA_WIDTH), BF16)
    a_spec = pl.BlockSpec((HYENA_ORDER, n1h, tj, HYENA_WIDTH), lambda i: (0, 0, i, 0))
    t_spec = pl.BlockSpec((None, n1h, tj), lambda i: (i, 0, 0))
    return pl.pallas_call(
        functools.partial(_filt_kernel, n1h=n1h, tj=tj),
        grid=(LANES // tj,),
        in_specs=[
            pl.BlockSpec((tj * n1h, 2 * FILT_PAD), lambda i: (i, 0)),
            _const_spec((2 * FILT_PAD, hid2)),
            _const_spec((1, hid2)),
            _const_spec((hid2, hid2)),
            _const_spec((1, hid2)),
            _const_spec((hid2, cw)),
            _const_spec((hid2, cw)),
            _const_spec((1, hid2)),
            _const_spec((1, HYENA_WIDTH)),
            _const_spec((2 * n1h, 2 * n1h)),
            t_spec, t_spec,
        ],
        out_specs=[a_spec, a_spec, pl.BlockSpec((1, cw), lambda i: (0, 0))],
        out_shape=[a_shape, a_shape, jax.ShapeDtypeStruct((1, cw), F32)],
        compiler_params=_cparams("arbitrary"),
        name="filter_taps_stage_a",
    )(zf, w1bd, b1, w2bd, b2, w3f, w3b, fr, dl, g_full, twr, twi)


def _fwda_kernel(g_ref, twr_ref, twi_ref, u_ref, are_ref, aim_ref, *, n1h, tj):
    gm = g_ref[...].astype(BF16)
    prev = None
    for sl in _CH_SPLITS:
        u = _tiles_to_rows(u_ref[:, :, sl])
        if prev is not None:
            _store_twiddled(prev[1], twr_ref, twi_ref, are_ref, aim_ref, prev[0], n1h, tj)
        prev = (sl, _dot(gm, u))
    _store_twiddled(prev[1], twr_ref, twi_ref, are_ref, aim_ref, prev[0], n1h, tj)


def _fwd_a(u4, g_half, twr, twi, tj):
    b, n1h, _, cb = u4.shape
    blk = pl.BlockSpec((None, n1h, tj, cb), lambda bi, ji: (bi, 0, ji, 0))
    t_spec = pl.BlockSpec((None, n1h, tj), lambda bi, ji: (ji, 0, 0))
    a_shape = jax.ShapeDtypeStruct(u4.shape, BF16)
    return pl.pallas_call(
        functools.partial(_fwda_kernel, n1h=n1h, tj=tj),
        grid=(b, LANES // tj),
        in_specs=[_const_spec((2 * n1h, n1h)), t_spec, t_spec, blk],
        out_specs=[blk, blk],
        out_shape=[a_shape, a_shape],
        compiler_params=_cparams("parallel", "parallel"),
        name="dft_stage_a",
    )(g_half, twr, twi, u4)


def _mid_kernel(f_ref, fi_ref, twr_ref, twi_ref, fre_ref, fim_ref, are_ref, aim_ref, bre_ref, bim_ref, *scratch,
                kb):
    fm = f_ref[...].astype(BF16)
    fim = fi_ref[...].astype(BF16)
    fill = jnp.zeros((LANES - kb, LANES), F32)
    twr_t = jnp.concatenate([twr_ref[...], fill], axis=0).T
    twi_t = jnp.concatenate([twi_ref[...], fill], axis=0).T

    def filt(k):
        return _dot(fm, jnp.concatenate([fre_ref[k], fim_ref[k]], axis=0))

    if scratch:
        (h_ref,) = scratch

        @pl.when(pl.program_id(1) == 0)
        def _():
            for k in range(kb):
                h_ref[k] = filt(k)

    def fwd(k):
        return (_dot(fm, jnp.concatenate([are_ref[k], aim_ref[k]], axis=0)), h_ref[k] if scratch else filt(k))

    def spec(k, xh):
        x, h = xh
        xr, xi = x[:LANES], x[LANES:]
        hr, hi = h[:LANES], h[LANES:]
        return jnp.concatenate([xr * hr - xi * hi, xr * hi + xi * hr], axis=0).astype(BF16)

    def out(k, bc):
        br, bi = bc[:LANES], bc[LANES:]
        tr, ti = twr_t[:, k:k + 1], twi_t[:, k:k + 1]
        bre_ref[k] = (br * tr + bi * ti).astype(BF16)
        bim_ref[k] = (bi * tr - br * ti).astype(BF16)

    x = fwd(0)
    bc_prev = None
    for k in range(kb):
        y = spec(k, x)
        if k + 1 < kb:
            x = fwd(k + 1)
        if bc_prev is not None:
            out(k - 1, bc_prev)
        bc_prev = _dot(fim, y)
    out(kb - 1, bc_prev)


def _mid(are, aim, fblk, fblk_inv, twr, twi, afre, afim, order, kb):
    b, n1h, _, cb = are.shape
    a_spec = pl.BlockSpec((None, kb, LANES, cb), lambda ki, bi: (bi, ki, 0, 0))
    f_spec = pl.BlockSpec((None, kb, LANES, cb), lambda ki, bi: (order, ki, 0, 0))
    t_spec = pl.BlockSpec((kb, LANES), lambda ki, bi: (ki, 0))
    shape = jax.ShapeDtypeStruct(are.shape, BF16)
    return pl.pallas_call(
        functools.partial(_mid_kernel, kb=kb),
        grid=(n1h // kb, b),
        in_specs=[
            _const_spec((2 * LANES, 2 * LANES)),
            _const_spec((2 * LANES, 2 * LANES)),
            t_spec, t_spec,
            f_spec, f_spec,
            a_spec, a_spec,
        ],
        out_specs=[a_spec, a_spec],
        out_shape=[shape, shape],
        scratch_shapes=[pltpu.VMEM((kb, 2 * LANES, cb), F32)] if b > 1 else [],
        compiler_params=_cparams("parallel", "arbitrary"),
        name="dft_stage_c",
    )(fblk, fblk_inv, twr, twi, afre, afim, are, aim)


def _inva_kernel(gi_ref, bre_ref, bim_ref, xg_ref, v_ref, sb_ref, s_ref, *rest, n1h, tj, fuse_next):
    gim = gi_ref[...].astype(BF16)
    inv_l1 = 1.0 / s_ref[...]
    if fuse_next:
        g_ref, twr_ref, twi_ref, z_ref, are_ref, aim_ref = rest
        gm = g_ref[...].astype(BF16)
    else:
        (z_ref,) = rest

    def conv(sl):
        rhs = jnp.concatenate([_tiles_to_rows(bre_ref[:, :, sl]), _tiles_to_rows(bim_ref[:, :, sl])], axis=0)
        return _dot(gim, rhs)

    def gate(sl, y):
        z = xg_ref[:, :, sl].astype(F32) * (_rows_to_tiles(y, tj) * inv_l1[:, sl]
                                            + sb_ref[:, sl] * v_ref[:, :, sl].astype(F32))
        z = z.astype(BF16)
        z_ref[:, :, sl] = z
        return z

    s0, s1 = _CH_SPLITS
    y0 = conv(s0)
    y1 = conv(s1)
    z0 = gate(s0, y0)
    if fuse_next:
        r0 = _dot(gm, _tiles_to_rows(z0))
    z1 = gate(s1, y1)
    if fuse_next:
        _store_twiddled(r0, twr_ref, twi_ref, are_ref, aim_ref, s0, n1h, tj)
        r1 = _dot(gm, _tiles_to_rows(z1))
        _store_twiddled(r1, twr_ref, twi_ref, are_ref, aim_ref, s1, n1h, tj)


def _inv_a(bre, bim, g_inv, xg, v, sb_row, asum, order, tj, nxt=None):
    b, n1h, _, cb = bre.shape
    blk = pl.BlockSpec((None, n1h, tj, cb), lambda bi, ji: (bi, 0, ji, 0))
    in_specs = [_const_spec((n1h, 2 * n1h)), blk, blk, blk, blk, _const_spec((1, cb)),
                pl.BlockSpec((1, cb), lambda bi, ji: (0, order))]
    args = [g_inv, bre, bim, xg, v, sb_row, asum]
    out_specs = [blk]
    out_shape = [jax.ShapeDtypeStruct(bre.shape, BF16)]
    if nxt is not None:
        t_spec = pl.BlockSpec((None, n1h, tj), lambda bi, ji: (ji, 0, 0))
        in_specs += [_const_spec((2 * n1h, n1h)), t_spec, t_spec]
        args += list(nxt)
        out_specs += [blk, blk]
        out_shape += [jax.ShapeDtypeStruct(bre.shape, BF16)] * 2
    return pl.pallas_call(
        functools.partial(_inva_kernel, n1h=n1h, tj=tj, fuse_next=nxt is not None),
        grid=(b, LANES // tj),
        in_specs=in_specs,
        out_specs=out_specs,
        out_shape=out_shape,
        compiler_params=_cparams("parallel", "parallel"),
        name="idft_stage_a_gate",
    )(*args)


def _attn_kernel(sink_ref, q_ref, kp_ref, kc_ref, kn_ref, vp_ref, vc_ref, vn_ref, o_ref, *, nblk, sub):
    i = pl.program_id(1)
    blk = ATTN_BLOCK
    grp = N_HEADS // N_KV_HEADS
    rows = grp * blk
    win = 3 * blk
    qi = lax.broadcasted_iota(jnp.int32, (rows, win), 0) % blk
    si = lax.broadcasted_iota(jnp.int32, (rows, win), 1)
    mid = (si >= blk) & (si < 2 * blk)
    tri_prev = (si < blk) & (si >= qi)
    tri_next = (si >= 2 * blk) & (si - 2 * blk <= qi)
    hrow = lax.broadcasted_iota(jnp.int32, (rows, 1), 0) // blk
    kcat = jnp.concatenate([kp_ref[...], kc_ref[...], kn_ref[...]], axis=0)
    vcat = jnp.concatenate([vp_ref[...], vc_ref[...], vn_ref[...]], axis=0)
    lane_k = lax.broadcasted_iota(jnp.int32, (win, N_KV_HEADS * LANES), 1)
    lane_o = lax.broadcasted_iota(jnp.int32, (rows, LANES), 1)
    zero_k = jnp.zeros((win, N_KV_HEADS * LANES), BF16)
    for s in range(sub):
        j = sub * i + s
        valid = mid | (tri_prev & (j > 0)) | (tri_next & (j < nblk - 1))
        kw = kcat[s * blk:s * blk + win]
        vw = vcat[s * blk:s * blk + win]
        kd = jnp.concatenate([jnp.where(lane_k // LANES == g, kw, zero_k) for g in range(N_KV_HEADS)], axis=0)
        vd = jnp.concatenate([vw[:, :LANES], pltpu.roll(vw[:, LANES:].astype(F32), HEAD_DIM, 1).astype(BF16)], axis=0)
        lhs = jnp.concatenate([q_ref[s * blk:(s + 1) * blk, t * N_KV_HEADS * LANES:(t + 1) * N_KV_HEADS * LANES]
                               for t in range(grp)], axis=0)
        sc_all = lax.dot_general(lhs, kd, (((1,), (1,)), ((), ())), preferred_element_type=F32)
        es, dens = [], []
        for g in range(N_KV_HEADS):
            sc = jnp.where(valid, sc_all[:, g * win:(g + 1) * win], NEG_INF)
            sk = jnp.full((rows, 1), sink_ref[grp * g], F32)
            for t in range(1, grp):
                sk = jnp.where(hrow == t, sink_ref[grp * g + t], sk)
            sk = sk * LOG2E
            m = jnp.maximum(jnp.max(sc, axis=1, keepdims=True), sk)
            e = jnp.exp2(sc - m)
            dens.append(jnp.sum(e, axis=1, keepdims=True) + jnp.exp2(sk - m))
            es.append(e.astype(BF16))
        den = jnp.where(lane_o < HEAD_DIM, dens[0], dens[1])
        r = _dot(jnp.concatenate(es, axis=1), vd) / den
        o_ref[s * blk:(s + 1) * blk, :] = jnp.concatenate(
            [r[t * blk:(t + 1) * blk] for t in range(grp)], axis=1).astype(BF16)


def _attention(q3, k3, v3, sink):
    b, seq_len, qw = q3.shape
    blk = ATTN_BLOCK
    sub = ATTN_SUB
    nblk = seq_len // blk
    wide = k3.shape[-1]
    prev = lambda bi, i: (bi, jnp.maximum(sub * i - 1, 0), 0)
    cur = lambda bi, i: (bi, i, 0)
    nxt = lambda bi, i: (bi, jnp.minimum(sub * (i + 1), nblk - 1), 0)
    edge = lambda f: pl.BlockSpec((None, blk, wide), f)
    body = pl.BlockSpec((None, sub * blk, wide), cur)
    return pl.pallas_call(
        functools.partial(_attn_kernel, nblk=nblk, sub=sub),
        grid=(b, nblk // sub),
        in_specs=[pl.BlockSpec(memory_space=pltpu.SMEM),
                  pl.BlockSpec((None, sub * blk, qw), cur),
                  edge(prev), body, edge(nxt), edge(prev), body, edge(nxt)],
        out_specs=pl.BlockSpec((None, sub * blk, ATTN_WIDTH), cur),
        out_shape=jax.ShapeDtypeStruct((b, seq_len, ATTN_WIDTH), BF16),
        compiler_params=_cparams("parallel", "parallel"),
        name="banded_attention",
    )(sink, q3, k3, k3, k3, v3, v3, v3)


def _final_kernel(x_ref, yh_ref, ya_ref, g_ref, why_ref, wat_ref, wo_ref, nw_ref, wg_ref, wu_ref, wd_ref,
                  o_ref):
    a = _dot(yh_ref[...], why_ref[...])
    b = _dot(ya_ref[...], wat_ref[...])
    merged = g_ref[:, :D_MODEL] * a + g_ref[:, D_MODEL:] * b
    x1 = x_ref[...] + _dot(merged.astype(BF16), wo_ref[...])
    ms = jnp.mean(x1 * x1, axis=-1, keepdims=True)
    f = (x1 * lax.rsqrt(ms + RMS_EPS) * nw_ref[...]).astype(BF16)
    acc = x1
    for lo, hi in FFN_CHUNKS:
        gt = _dot(f, wg_ref[:, lo:hi])
        upv = _dot(f, wu_ref[:, lo:hi])
        hid = gt * _sigmoid(gt) * upv
        acc = acc + _dot(hid.astype(BF16), wd_ref[lo:hi, :])
    o_ref[...] = acc


def _final(xf, yh, ya, g, why, wat, wo, nw, wg, wu, wd):
    n = xf.shape[0]
    tm = TM_FINAL
    row = lambda i: (i, 0)
    return pl.pallas_call(
        _final_kernel,
        grid=(n // tm,),
        in_specs=[
            pl.BlockSpec((tm, D_MODEL), row),
            pl.BlockSpec((tm, HYENA_WIDTH), row),
            pl.BlockSpec((tm, ATTN_WIDTH), row),
            pl.BlockSpec((tm, 2 * D_MODEL), row),
            _const_spec((HYENA_WIDTH, D_MODEL)),
            _const_spec((ATTN_WIDTH, D_MODEL)),
            _const_spec((D_MODEL, D_MODEL)),
            _const_spec((1, D_MODEL)),
            _const_spec((D_MODEL, FFN_HIDDEN)),
            _const_spec((D_MODEL, FFN_HIDDEN)),
            _const_spec((FFN_HIDDEN, D_MODEL)),
        ],
        out_specs=pl.BlockSpec((tm, D_MODEL), row),
        out_shape=jax.ShapeDtypeStruct((n, D_MODEL), F32),
        compiler_params=_cparams("parallel"),
        name="merge_ffn",
    )(xf, yh, ya, g, why, wat, wo, nw, wg, wu, wd)


@functools.lru_cache(maxsize=None)
def _dft_constants(n1h, tj):
    n1_len = 2 * n1h
    m = n1_len * LANES
    k1 = np.arange(n1h, dtype=np.float64)[:, None] + 0.5
    n1 = np.arange(n1_len, dtype=np.float64)[None, :]
    gc = np.exp(-2j * np.pi * k1 * n1 / n1_len)
    g_full = np.concatenate([gc.real, gc.imag], axis=0)
    g_inv = (2.0 / m) * np.concatenate([gc.real[:, :n1h].T, gc.imag[:, :n1h].T], axis=1)
    n2 = np.arange(LANES, dtype=np.float64)
    tw = np.exp(-2j * np.pi * k1 * n2[None, :] / m)
    fc = np.exp(-2j * np.pi * np.outer(n2, n2) / LANES)
    fblk = np.block([[fc.real, -fc.imag], [fc.imag, fc.real]])
    fblk_inv = np.block([[fc.real, fc.imag], [-fc.imag, fc.real]])
    f32 = lambda a: np.ascontiguousarray(a, dtype=np.float32)
    tiled = lambda a: a.reshape(n1h, LANES // tj, tj).transpose(1, 0, 2)
    return dict(g_full=f32(g_full), g_half=f32(g_full[:, :n1h]), g_inv=f32(g_inv), twr=f32(tw.real),
                twi=f32(tw.imag), twr_a=f32(tiled(tw.real)), twi_a=f32(tiled(tw.imag)), fblk=f32(fblk),
                fblk_inv=f32(fblk_inv))


@functools.lru_cache(maxsize=None)
def _filter_features(seq_len):
    n1h = seq_len // LANES
    n2 = np.arange(LANES)[:, None]
    n1 = np.arange(n1h)[None, :]
    slot_f = (LANES * n1 + n2).reshape(-1)
    slot_b = slot_f + seq_len
    neg_lag = slot_b > seq_len
    pos_b = 2 * seq_len - slot_b
    bands = np.linspace(1e-4, FILTER_BANDS - 1, FILTER_BANDS)

    def feats(pos, sgn):
        pos = pos.astype(np.float64)
        ang = (2.0 * math.pi / seq_len) * pos[:, None] * bands[None, :]
        pad = np.zeros((pos.shape[0], FILT_PAD - FILTER_EMB - 1))
        return np.concatenate([(pos / (seq_len - 1))[:, None], np.cos(ang), -np.sin(ang), sgn[:, None], pad], axis=-1)

    z = np.concatenate([feats(slot_f, np.ones(slot_f.shape)), feats(pos_b, np.where(neg_lag, -1.0, 0.0))], axis=-1)
    return np.ascontiguousarray(z, dtype=np.float32)


@functools.lru_cache(maxsize=None)
def _rope_tables(seq_len):
    inv = ROPE_THETA ** (-np.arange(0, HEAD_DIM, 2, dtype=np.float64) / HEAD_DIM)
    ang = np.arange(seq_len, dtype=np.float64)[:, None] * inv[None, :]
    cos, sin = np.cos(ang), np.sin(ang)
    zero = np.zeros_like(sin)
    reps = LANES // HEAD_DIM
    f32 = lambda a: np.ascontiguousarray(np.tile(a, (1, reps)), dtype=np.float32)
    return f32(np.concatenate([cos, cos], axis=1)), f32(np.concatenate([-sin, zero], axis=1)), \
        f32(np.concatenate([zero, sin], axis=1))


_LATE_WEIGHTS = ("w_hy_out", "w_at_out", "w_o", "w_gate", "w_up", "w_down")


def _layer(x, p, late_bf=None):
    b, seq_len, _ = x.shape
    n = b * seq_len
    n1h = seq_len // LANES
    xf = x.reshape(n, D_MODEL)
    tj_filt, tj_a, tj_gate, tj_last = STAGE_A_TILES[n1h >= LANES]
    c = {k: jnp.asarray(v) for k, v in _dft_constants(n1h, tj_a).items()}
    tw_a = lambda tj: tuple(jnp.asarray(_dft_constants(n1h, tj)[k]) for k in ("twr_a", "twi_a"))
    kb = 2 * SUBLANES
    r4 = lambda a: a.reshape(b, n1h, LANES, HYENA_WIDTH)

    cos, sa, sb = (jnp.asarray(t) for t in _rope_tables(seq_len))
    cast = () if late_bf is not None else tuple(p[k] for k in _LATE_WEIGHTS)
    res = _inproj(xf, seq_len, p["attn_norm_w"], p["w_in"], p["conv_w"], p["conv_b"],
                  p["q_norm_w"], p["k_norm_w"], cos, sa, sb, p["ones_bd"], cast)
    x1, x2, v, q, k4, v4, g = res[:7]
    if late_bf is None:
        late_bf = dict(zip(_LATE_WEIGHTS, res[7:]))

    afre, afim, asum = _filter_stage_a(jnp.asarray(_filter_features(seq_len)), p["filt_w1"], p["filt_b1"],
                                       p["filt_w2"], p["filt_b2"], p["filt_w3f"], p["filt_w3b"], p["filt_freq"],
                                       p["decay"], c["g_full"], *tw_a(tj_filt), n1h, tj_filt)

    are, aim = _fwd_a(r4(v), c["g_half"], *tw_a(tj_a), tj_a)
    bre, bim = _mid(are, aim, c["fblk"], c["fblk_inv"], c["twr"], c["twi"], afre, afim, 0, kb)
    z1, are, aim = _inv_a(bre, bim, c["g_inv"], r4(x1), r4(v), p["hyena_bias"][0:1], asum, 0, tj_gate,
                          nxt=(c["g_half"], *tw_a(tj_gate)))
    bre, bim = _mid(are, aim, c["fblk"], c["fblk_inv"], c["twr"], c["twi"], afre, afim, 1, kb)
    (yh,) = _inv_a(bre, bim, c["g_inv"], r4(x2), z1, p["hyena_bias"][1:2], asum, 1, tj_last)

    ya = _attention(q.reshape(b, seq_len, 2 * ATTN_WIDTH), k4.reshape(b, seq_len, 2 * LANES),
                    v4.reshape(b, seq_len, 2 * LANES), p["attn_sink"])

    w = late_bf
    out = _final(xf, yh.reshape(n, HYENA_WIDTH), ya.reshape(n, ATTN_WIDTH), g, w["w_hy_out"], w["w_at_out"],
                 w["w_o"], p["ffn_norm_w"], w["w_gate"], w["w_up"], w["w_down"])
    return out.reshape(b, seq_len, D_MODEL), late_bf


def _block_diag2(w):
    z = jnp.zeros_like(w)
    return jnp.concatenate([jnp.concatenate([w, z], axis=1), jnp.concatenate([z, w], axis=1)], axis=0)


def kernel(x_prompt, x_sample, attn_norm_w, w_in, hyena_conv_w, hyena_conv_b, filt_w1, filt_b1, filt_w2, filt_b2,
           filt_w3, filt_freq, hyena_bias, q_norm_w, k_norm_w, attn_sink, w_hy_out, w_at_out, w_o, ffn_norm_w,
           w_gate, w_up, w_down):
    cw = HYENA_ORDER * HYENA_WIDTH
    max_decay = math.log(DECAY_TARGET) / DECAY_FAST_PCT
    min_decay = math.log(DECAY_TARGET) / DECAY_SLOW_PCT
    deltas = jnp.abs(jnp.linspace(min_decay, max_decay, HYENA_WIDTH, dtype=F32))
    head = np.arange(ATTN_WIDTH) // HEAD_DIM
    ones_bd = jnp.asarray((head[:, None] == head[None, :]).astype(np.float32) / HEAD_DIM).astype(BF16)
    w3 = filt_w3[0].reshape(FILTER_HIDDEN, HYENA_ORDER, 2, HYENA_WIDTH).transpose(2, 0, 1, 3)
    w3 = w3.reshape(2, FILTER_HIDDEN, cw)
    w3_zero = jnp.zeros((FILTER_HIDDEN, cw), F32)
    twice = lambda a: jnp.tile(a, 2)[None, :]
    p = dict(
        attn_norm_w=attn_norm_w[0][None, :],
        w_in=w_in[0].astype(BF16),
        conv_w=hyena_conv_w[0],
        conv_b=hyena_conv_b[0][None, :],
        filt_w1=_block_diag2(jnp.pad(filt_w1[0], ((0, FILT_PAD - FILTER_EMB), (0, 0)))),
        filt_b1=twice(filt_b1[0]),
        filt_w2=_block_diag2(filt_w2[0]),
        filt_b2=twice(filt_b2[0]),
        filt_w3f=jnp.concatenate([w3[0], w3_zero], axis=0),
        filt_w3b=jnp.concatenate([w3_zero, w3[1]], axis=0),
        filt_freq=twice(filt_freq[0]),
        decay=deltas[None, :],
        hyena_bias=hyena_bias[0],
        q_norm_w=jnp.tile(q_norm_w[0], N_HEADS)[None, :],
        k_norm_w=jnp.tile(k_norm_w[0], N_KV_HEADS)[None, :],
        attn_sink=attn_sink[0],
        ones_bd=ones_bd,
        w_hy_out=w_hy_out[0],
        w_at_out=w_at_out[0].reshape(N_HEADS, HEAD_DIM, D_MODEL)[np.array(ATTN_HEAD_ORDER)].reshape(ATTN_WIDTH, D_MODEL),
        w_o=w_o[0],
        ffn_norm_w=ffn_norm_w[0][None, :],
        w_gate=w_gate[0],
        w_up=w_up[0],
        w_down=w_down[0],
    )
    y_prompt, late_bf = _layer(x_prompt, p)
    y_sample, _ = _layer(x_sample, p, late_bf)
    return (y_prompt, y_sample)
```
